```python
import math
import jax
import jax.numpy as jnp
from jax import lax
import numpy as np

D_MODEL = 2048
BATCH = 4
SEQ = 4096
DEPTH = 2

GRID_W = 64
CTX_LEN = 256
HEAD_DIM = 128
ROPE_THETA = 10000.0
Q_BLOCK = 128
GQA_HEADS = 6
GQA_KV_HEADS = 2
GQA_GROUP = GQA_HEADS // GQA_KV_HEADS
MLA_HEADS = 5
MLA_Q_RANK = 512
MLA_KV_RANK = 256
MLA_NOPE_DIM = 128
MLA_ROPE_DIM = 64
MLA_V_DIM = 128
MLA_QK_DIM = MLA_NOPE_DIM + MLA_ROPE_DIM
NA_HEADS = 5
NA_KH = 8
NA_KW = 16
N_BRANCHES = 3
IN_SIZES = (GQA_HEADS * HEAD_DIM, GQA_KV_HEADS * HEAD_DIM, GQA_KV_HEADS * HEAD_DIM,
            MLA_Q_RANK, MLA_KV_RANK, MLA_ROPE_DIM,
            NA_HEADS * HEAD_DIM, NA_HEADS * HEAD_DIM, NA_HEADS * HEAD_DIM,
            N_BRANCHES * D_MODEL)
IN_WIDTH = sum(IN_SIZES)
N_GROUPS = 8
EXPERTS_PER_GROUP = 8
N_EXPERTS = N_GROUPS * EXPERTS_PER_GROUP
TOP_K = 2
D_EXPERT = 512
MOE_BLOCK = 128
LN_EPS = 1e-6
RMS_EPS = 1e-6
DEEPNORM_ALPHA = (2 * DEPTH) ** 0.25
DEEPNORM_BETA = (8 * DEPTH) ** -0.25

kernel_name = "hybrid_gqa_mla_natten_hmoe_diffusion_trunk"


def layer_norm(x, g, b):
    xf = x.astype(jnp.float32)
    mu = jnp.mean(xf, axis=-1, keepdims=True)
    var = jnp.mean(jnp.square(xf - mu), axis=-1, keepdims=True)
    return ((xf - mu) * lax.rsqrt(var + LN_EPS) * g.astype(jnp.float32) + b.astype(jnp.float32)).astype(x.dtype)


def rms_norm(x, g):
    xf = x.astype(jnp.float32)
    y = xf * lax.rsqrt(jnp.mean(jnp.square(xf), axis=-1, keepdims=True) + RMS_EPS)
    return (y * g.astype(jnp.float32)).astype(x.dtype)


def rope_tables(seq_len, dim):
    t = jnp.arange(seq_len, dtype=jnp.int32)
    row = (t // GRID_W).astype(jnp.float32)
    col = (t % GRID_W).astype(jnp.float32)
    quarter = dim // 4
    inv_freq = ROPE_THETA ** (-jnp.arange(quarter, dtype=jnp.float32) / quarter)
    ang_r = row[:, None] * inv_freq
    ang_c = col[:, None] * inv_freq
    ang = jnp.concatenate([ang_r, ang_r, ang_c, ang_c], axis=-1)
    return jnp.cos(ang), jnp.sin(ang)


def apply_rope(x, cos, sin):
    d = x.shape[-1]
    xr = x.reshape(x.shape[:-1] + (2, 2, d // 4))
    rot = jnp.stack([-xr[..., 1, :], xr[..., 0, :]], axis=-2).reshape(x.shape)
    return x * cos[:, None, :].astype(x.dtype) + rot * sin[:, None, :].astype(x.dtype)


def ada_modulation(cond, w_ada, b_ada):
    m = jax.nn.silu(cond) @ w_ada + b_ada
    return jnp.split(m, 6, axis=-1)


def modulate(x, shift, scale):
    return x * (1.0 + scale) + shift


def to_heads(t):
    return t.transpose(0, 2, 1, 3)


def heads_to_tokens(o):
    bsz, k, g, n, dv = o.shape
    return o.transpose(0, 3, 1, 2, 4).reshape(bsz, n, k * g * dv)


def project_mixers(u, w_in, gqa_qn, gqa_kn, mla_qn, mla_kvn, w_uq, w_ukv, rope):
    bsz, n, _ = u.shape
    split_pts = np.cumsum(IN_SIZES)[:-1].tolist()
    aq, ak, av, bq, bkv, bkr, cq, ck, cv, gate_logits = jnp.split(u @ w_in, split_pts, axis=-1)
    aq = rms_norm(aq.reshape(bsz, n, GQA_HEADS, HEAD_DIM), gqa_qn)
    ak = rms_norm(ak.reshape(bsz, n, GQA_KV_HEADS, HEAD_DIM), gqa_kn)
    av = av.reshape(bsz, n, GQA_KV_HEADS, HEAD_DIM)
    bq = (rms_norm(bq, mla_qn) @ w_uq).reshape(bsz, n, MLA_HEADS, MLA_QK_DIM)
    bkv = (rms_norm(bkv, mla_kvn) @ w_ukv).reshape(bsz, n, MLA_HEADS, MLA_NOPE_DIM + MLA_V_DIM)
    bq_nope, bq_rope = jnp.split(bq, [MLA_NOPE_DIM], axis=-1)
    bk_nope, bv = jnp.split(bkv, [MLA_NOPE_DIM], axis=-1)
    bkr = bkr[:, :, None, :]
    if rope is not None:
        (cos_a, sin_a), (cos_b, sin_b) = rope
        aq = apply_rope(aq, cos_a, sin_a)
        ak = apply_rope(ak, cos_a, sin_a)
        bq_rope = apply_rope(bq_rope, cos_b, sin_b)
        bkr = apply_rope(bkr, cos_b, sin_b)
    bq = jnp.concatenate([bq_nope, bq_rope], axis=-1)
    bk = jnp.concatenate([bk_nope, jnp.broadcast_to(bkr, (bsz, n, MLA_HEADS, MLA_ROPE_DIM))], axis=-1)
    cq = cq.reshape(bsz, n, NA_HEADS, HEAD_DIM)
    ck = ck.reshape(bsz, n, NA_HEADS, HEAD_DIM)
    cv = cv.reshape(bsz, n, NA_HEADS, HEAD_DIM)
    qa = aq.reshape(bsz, n, GQA_KV_HEADS, GQA_GROUP, HEAD_DIM).transpose(0, 2, 3, 1, 4)
    return (qa, to_heads(ak), to_heads(av),
            to_heads(bq)[:, :, None], to_heads(bk), to_heads(bv),
            to_heads(cq), to_heads(ck), to_heads(cv), gate_logits)


def dense_attention(q, k, v, scale):
    s = jnp.einsum("bkgqd,bksd->bkgqs", q, k, preferred_element_type=jnp.float32) * scale
    p = jax.nn.softmax(s, axis=-1).astype(v.dtype)
    return jnp.einsum("bkgqs,bksd->bkgqd", p, v)


def blocked_attention(q, k, v, scale):
    bsz, kh, g, s, dq = q.shape
    nb = s // Q_BLOCK
    qb = jnp.moveaxis(q.reshape(bsz, kh, g, nb, Q_BLOCK, dq), 3, 0)
    out = lax.map(lambda qi: dense_attention(qi, k, v, scale), qb)
    return jnp.moveaxis(out, 0, 3).reshape(bsz, kh, g, s, v.shape[-1])


def neighbourhood_attention(q, k, v, k_ctx, v_ctx, rpb, scale):
    bsz, h, s, d = q.shape
    rows = s // GRID_W
    kh = min(NA_KH, rows)
    qg = q.reshape(bsz, h, rows, GRID_W, d)
    kg = k.reshape(bsz, h, rows, GRID_W, d)
    vg = v.reshape(bsz, h, rows, GRID_W, d)
    col = jnp.arange(GRID_W, dtype=jnp.int32)
    col_start = jnp.clip(col - NA_KW // 2, 0, GRID_W - NA_KW)
    col_idx = col_start[:, None] + jnp.arange(NA_KW, dtype=jnp.int32)[None, :]
    col_off = col_idx - col[:, None] + (NA_KW - 1)
    n_win = kh * NA_KW

    def row_step(r):
        r_start = jnp.clip(r - kh // 2, 0, rows - kh)
        k_band = lax.dynamic_slice_in_dim(kg, r_start, kh, axis=2)
        v_band = lax.dynamic_slice_in_dim(vg, r_start, kh, axis=2)
        k_win = k_band[:, :, :, col_idx]
        v_win = v_band[:, :, :, col_idx]
        q_row = lax.dynamic_index_in_dim(qg, r, axis=2, keepdims=False)
        row_off = r_start + jnp.arange(kh, dtype=jnp.int32) - r + (NA_KH - 1)
        bias = rpb[:, row_off[None, :, None], col_off[:, None, :]].astype(jnp.float32)
        s_win = jnp.einsum("bhwd,bhiwjd->bhwij", q_row, k_win, preferred_element_type=jnp.float32) * scale + bias
        s_ctx = jnp.einsum("bhwd,bhcd->bhwc", q_row, k_ctx, preferred_element_type=jnp.float32) * scale
        p = jax.nn.softmax(jnp.concatenate([s_win.reshape(bsz, h, GRID_W, n_win), s_ctx], axis=-1), axis=-1).astype(v.dtype)
        p_win = p[..., :n_win].reshape(bsz, h, GRID_W, kh, NA_KW)
        return (jnp.einsum("bhwij,bhiwjd->bhwd", p_win, v_win)
                + jnp.einsum("bhwc,bhcd->bhwd", p[..., n_win:], v_ctx))

    out = lax.map(row_step, jnp.arange(rows, dtype=jnp.int32))
    return out.transpose(1, 0, 3, 2, 4).reshape(bsz, s, h * d)


def merge_branches(o_a, o_b, o_c, gate_logits, w_ba, w_bb, w_bc, w_out):
    g = jax.nn.sigmoid(gate_logits.astype(jnp.float32)).astype(o_a.dtype)
    g_a, g_b, g_c = jnp.split(g, N_BRANCHES, axis=-1)
    return (g_a * (o_a @ w_ba) + g_b * (o_b @ w_bb) + g_c * (o_c @ w_bc)) @ w_out


def routed_experts(u, expert_id, gates, w_gate, w_up, w_down):
    n, d = u.shape
    n_assign = n * TOP_K
    flat_e = expert_id.reshape(-1)
    flat_tok = jnp.arange(n_assign, dtype=jnp.int32) // TOP_K
    flat_w = gates.reshape(-1)
    order = jnp.argsort(flat_e)
    se, stok, sw = flat_e[order], flat_tok[order], flat_w[order]
    counts = jnp.bincount(flat_e, length=N_EXPERTS).astype(jnp.int32)
    starts = jnp.cumsum(counts) - counts
    padded = ((counts + MOE_BLOCK - 1) // MOE_BLOCK) * MOE_BLOCK
    pends = jnp.cumsum(padded)
    pstarts = pends - padded
    dest = pstarts[se] + jnp.arange(n_assign, dtype=jnp.int32) - starts[se]
    n_blocks = -(-n_assign // MOE_BLOCK) + N_EXPERTS
    n_slots = n_blocks * MOE_BLOCK
    slot_tok = jnp.full((n_slots,), n, jnp.int32).at[dest].set(stok)
    slot_w = jnp.zeros((n_slots,), u.dtype).at[dest].set(sw)
    block_start = jnp.arange(n_blocks, dtype=jnp.int32) * MOE_BLOCK
    block_e = jnp.clip(jnp.searchsorted(pends, block_start, side="right"), 0, N_EXPERTS - 1)
    u_pad = jnp.concatenate([u, jnp.zeros((1, d), u.dtype)], axis=0)

    def block_step(args):
        tok, e = args
        xb = u_pad[tok]
        hb = jax.nn.silu(xb @ w_gate[e]) * (xb @ w_up[e])
        return hb @ w_down[e]

    y = lax.map(block_step, (slot_tok.reshape(n_blocks, MOE_BLOCK), block_e)).reshape(n_slots, d)
    return jnp.zeros((n + 1, d), u.dtype).at[slot_tok].add(y * slot_w[:, None])[:n]


def hierarchical_moe(u, w_group, b_group, w_route, b_route, w_gate, w_up, w_down):
    n = u.shape[0]
    group_logits = jnp.matmul(u, w_group, preferred_element_type=jnp.float32) + b_group.astype(jnp.float32)
    group_prob = jax.nn.softmax(group_logits, axis=-1)
    group_sel = jnp.argmax(group_logits, axis=-1).astype(jnp.int32)
    group_gate = jnp.take_along_axis(group_prob, group_sel[:, None], axis=-1)
    exp_logits = (jnp.matmul(u, w_route, preferred_element_type=jnp.float32)
                  + b_route.astype(jnp.float32)).reshape(n, N_GROUPS, EXPERTS_PER_GROUP)
    exp_logits = jnp.take_along_axis(exp_logits, group_sel[:, None, None], axis=1)[:, 0]
    top_logits, top_idx = lax.top_k(exp_logits, TOP_K)
    gates = group_gate * jax.nn.softmax(top_logits, axis=-1)
    expert_id = group_sel[:, None] * EXPERTS_PER_GROUP + top_idx.astype(jnp.int32)
    return routed_experts(u, expert_id, gates.astype(u.dtype), w_gate, w_up, w_down)


def setup_inputs(seed: int = 0) -> dict:
    key = jax.random.key(seed)
    ks = iter(jax.random.split(key, 32))
    L, D = DEPTH, D_MODEL

    def nrm(shape, scale):
        return jax.random.normal(next(ks), shape, jnp.float32) * scale

    return {
        "x": nrm((BATCH, SEQ, D), 1.0),
        "c": nrm((BATCH, D), 1.0),
        "ctx": nrm((BATCH, CTX_LEN, D), 1.0),
        "c_ctx": nrm((D,), 1.0),
        "w_ada": nrm((L, D, 6 * D), 0.5 * D ** -0.5),
        "b_ada": nrm((L, 6 * D), 0.02),
        "w_in": nrm((L, D, IN_WIDTH), D ** -0.5),
        "gqa_q_norm": 1.0 + nrm((L, HEAD_DIM), 0.02),
        "gqa_k_norm": 1.0 + nrm((L, HEAD_DIM), 0.02),
        "mla_q_norm": 1.0 + nrm((L, MLA_Q_RANK), 0.02),
        "mla_kv_norm": 1.0 + nrm((L, MLA_KV_RANK), 0.02),
        "mla_w_uq": nrm((L, MLA_Q_RANK, MLA_HEADS * MLA_QK_DIM), MLA_Q_RANK ** -0.5),
        "mla_w_ukv": nrm((L, MLA_KV_RANK, MLA_HEADS * (MLA_NOPE_DIM + MLA_V_DIM)), MLA_KV_RANK ** -0.5),
        "na_rpb": nrm((L, NA_HEADS, 2 * NA_KH - 1, 2 * NA_KW - 1), 0.1),
        "w_branch_a": nrm((L, GQA_HEADS * HEAD_DIM, D), (GQA_HEADS * HEAD_DIM) ** -0.5),
        "w_branch_b": nrm((L, MLA_HEADS * MLA_V_DIM, D), (MLA_HEADS * MLA_V_DIM) ** -0.5),
        "w_branch_c": nrm((L, NA_HEADS * HEAD_DIM, D), (NA_HEADS * HEAD_DIM) ** -0.5),
        "w_out": nrm((L, D, D), DEEPNORM_BETA * D ** -0.5),
        "ln1_g": 1.0 + nrm((L, D), 0.02),
        "ln1_b": nrm((L, D), 0.02),
        "w_router_group": nrm((L, D, N_GROUPS), D ** -0.5),
        "b_router_group": nrm((L, N_GROUPS), 0.01),
        "w_router_expert": nrm((L, D, N_EXPERTS), D ** -0.5),
        "b_router_expert": nrm((L, N_EXPERTS), 0.01),
        "w_expert_gate": nrm((L, N_EXPERTS, D, D_EXPERT), D ** -0.5),
        "w_expert_up": nrm((L, N_EXPERTS, D, D_EXPERT), D ** -0.5),
        "w_expert_down": nrm((L, N_EXPERTS, D_EXPERT, D), DEEPNORM_BETA * D_EXPERT ** -0.5),
        "ln2_g": 1.0 + nrm((L, D), 0.02),
        "ln2_b": nrm((L, D), 0.02),
    }


def reference(x, c, ctx, c_ctx, w_ada, b_ada, w_in, gqa_q_norm, gqa_k_norm, mla_q_norm, mla_kv_norm,
              mla_w_uq, mla_w_ukv, na_rpb, w_branch_a, w_branch_b, w_branch_c, w_out, ln1_g, ln1_b,
              w_router_group, b_router_group, w_router_expert, b_router_expert,
              w_expert_gate, w_expert_up, w_expert_down, ln2_g, ln2_b):
    bsz, seq, dm = x.shape
    rope = (rope_tables(seq, HEAD_DIM), rope_tables(seq, MLA_ROPE_DIM))
    scale_a = 1.0 / math.sqrt(HEAD_DIM)
    scale_b = 1.0 / math.sqrt(MLA_QK_DIM)
    scale_c = 1.0 / math.sqrt(HEAD_DIM)
    for i in range(DEPTH):
        last = i == DEPTH - 1
        sh_m, sc_m, g_m, sh_f, sc_f, g_f = ada_modulation(c, w_ada[i], b_ada[i])
        csh_m, csc_m, cg_m, csh_f, csc_f, cg_f = ada_modulation(c_ctx, w_ada[i], b_ada[i])

        fx = project_mixers(modulate(x, sh_m[:, None], sc_m[:, None]), w_in[i], gqa_q_norm[i], gqa_k_norm[i],
                            mla_q_norm[i], mla_kv_norm[i], mla_w_uq[i], mla_w_ukv[i], rope)
        fc = project_mixers(modulate(ctx, csh_m, csc_m), w_in[i], gqa_q_norm[i], gqa_k_norm[i],
                            mla_q_norm[i], mla_kv_norm[i], mla_w_uq[i], mla_w_ukv[i], None)
        qa, ka, va, qb, kb, vb, qc, kc, vc, gx = fx
        qa_c, ka_c, va_c, qb_c, kb_c, vb_c, qc_c, kc_c, vc_c, gc = fc
        o_a = heads_to_tokens(blocked_attention(qa, jnp.concatenate([ka, ka_c], axis=2),
                                                jnp.concatenate([va, va_c], axis=2), scale_a))
        o_b = heads_to_tokens(blocked_attention(qb, jnp.concatenate([kb, kb_c], axis=2),
                                                jnp.concatenate([vb, vb_c], axis=2), scale_b))
        o_c = neighbourhood_attention(qc, kc, vc, kc_c, vc_c, na_rpb[i], scale_c)
        y = merge_branches(o_a, o_b, o_c, gx, w_branch_a[i], w_branch_b[i], w_branch_c[i], w_out[i])
        x = layer_norm(DEEPNORM_ALPHA * x + g_m[:, None] * y, ln1_g[i], ln1_b[i])
        if not last:
            oc_a = heads_to_tokens(dense_attention(qa_c, ka_c, va_c, scale_a))
            oc_b = heads_to_tokens(dense_attention(qb_c, kb_c, vb_c, scale_b))
            oc_c = heads_to_tokens(dense_attention(qc_c[:, :, None], kc_c, vc_c, scale_c))
            yc = merge_branches(oc_a, oc_b, oc_c, gc, w_branch_a[i], w_branch_b[i], w_branch_c[i], w_out[i])
            ctx = layer_norm(DEEPNORM_ALPHA * ctx + cg_m * yc, ln1_g[i], ln1_b[i])

        ux = modulate(x, sh_f[:, None], sc_f[:, None]).reshape(-1, dm)
        if last:
            mx = hierarchical_moe(ux, w_router_group[i], b_router_group[i], w_router_expert[i], b_router_expert[i],
                                  w_expert_gate[i], w_expert_up[i], w_expert_down[i])
            x = layer_norm(DEEPNORM_ALPHA * x + g_f[:, None] * mx.reshape(bsz, seq, dm), ln2_g[i], ln2_b[i])
        else:
            uc = modulate(ctx, csh_f, csc_f).reshape(-1, dm)
            n_x = ux.shape[0]
            m_all = hierarchical_moe(jnp.concatenate([ux, uc], axis=0), w_router_group[i], b_router_group[i],
                                     w_router_expert[i], b_router_expert[i],
                                     w_expert_gate[i], w_expert_up[i], w_expert_down[i])
            x = layer_norm(DEEPNORM_ALPHA * x + g_f[:, None] * m_all[:n_x].reshape(bsz, seq, dm), ln2_g[i], ln2_b[i])
            ctx = layer_norm(DEEPNORM_ALPHA * ctx + cg_f * m_all[n_x:].reshape(ctx.shape), ln2_g[i], ln2_b[i])
    return x
```

```python
import functools
import math

import jax
import jax.numpy as jnp
import numpy as np
from jax import lax
from jax.experimental import pallas as pl
from jax.experimental.pallas import tpu as pltpu

F32 = jnp.float32
BF16 = jnp.bfloat16

HEAD_DIM = 128
GRID_W = 64
ROPE_THETA = 10000.0
GQA_HEADS, GQA_KV_HEADS = 6, 2
GQA_GROUP = GQA_HEADS // GQA_KV_HEADS
MLA_HEADS, MLA_Q_RANK, MLA_KV_RANK = 5, 512, 256
MLA_NOPE, MLA_ROPE, MLA_V = 128, 64, 128
MLA_QK = MLA_NOPE + MLA_ROPE
MLA_PAD = 256
NA_HEADS, NA_KH, NA_KW = 5, 8, 16
N_GROUPS, EXPERTS_PER_GROUP, TOP_K, D_EXPERT = 8, 8, 2, 512
N_EXPERTS = N_GROUPS * EXPERTS_PER_GROUP
LN_EPS = 1e-6
RMS_EPS = 1e-6
NEG_BIG = -1e30

LANES = 128
VMEM_LIMIT = 56 * 1024 * 1024
PROJ_TM, PROJ_TN = 1024, 1024
ROW_TM = 512
MM_TM = 256
ATTN_TQ = 512
ATTN_KC = 1024
NA_ROWS = 8
NA_BAND = 16
MOE_BM = 256
ADA_TN = 512

COL_AQ, COL_AK = 0, 768
COL_BQ, COL_BKV, COL_BKR = 1024, 1536, 1792
COL_AV, COL_CQ, COL_CK, COL_CV = 1920, 2176, 2816, 3456
COL_GATE = 4096
PROJ_W = 10240


def _cparams(sem):
    return pltpu.CompilerParams(dimension_semantics=sem, vmem_limit_bytes=VMEM_LIMIT)


def _dot(a, b):
    return jnp.dot(a, b, preferred_element_type=F32)


def _dot_nt(a, b):
    return lax.dot_general(a, b, (((1,), (1,)), ((), ())), preferred_element_type=F32)


def _split_bf16(a):
    hi = a.astype(BF16)
    lo = (a - hi.astype(F32)).astype(BF16)
    return hi, lo


def _ada_kernel(c_ref, w_ref, b_ref, o_ref):
    a = c_ref[...]
    a = a * jax.nn.sigmoid(a)
    a_hi, a_lo = _split_bf16(a)
    w_hi, w_lo = _split_bf16(w_ref[0])
    acc = _dot(a_hi, w_hi) + _dot(a_lo, w_hi) + _dot(a_hi, w_lo)
    o_ref[0] = acc + b_ref[0]


def _ada_call(cc, w_ada, b_ada):
    depth, d, n = w_ada.shape
    rows = cc.shape[0]
    return pl.pallas_call(
        _ada_kernel,
        grid=(depth, n // ADA_TN),
        in_specs=[pl.BlockSpec((rows, d), lambda l, j: (0, 0)),
                  pl.BlockSpec((1, d, ADA_TN), lambda l, j: (l, 0, j)),
                  pl.BlockSpec((1, 1, ADA_TN), lambda l, j: (l, 0, j))],
        out_specs=pl.BlockSpec((1, rows, ADA_TN), lambda l, j: (l, 0, j)),
        out_shape=jax.ShapeDtypeStruct((depth, rows, n), F32),
        compiler_params=_cparams(("parallel", "parallel")),
        name="ada_modulation",
    )(cc, w_ada, b_ada.reshape(depth, 1, n))


def _proj_kernel(x_ref, mod_ref, w_ref, o_ref, u_scr):
    @pl.when(pl.program_id(1) == 0)
    def _():
        shift = mod_ref[0, 0:1, :]
        scale = mod_ref[0, 1:2, :]
        u_scr[...] = (x_ref[...] * (1.0 + scale) + shift).astype(BF16)

    o_ref[...] = _dot(u_scr[...], w_ref[...]).astype(BF16)


def _proj_call(xs, mod, w_in_p, mod_row):
    t, d = xs.shape
    return pl.pallas_call(
        _proj_kernel,
        grid=(t // PROJ_TM, PROJ_W // PROJ_TN),
        in_specs=[pl.BlockSpec((PROJ_TM, d), lambda i, j: (i, 0)),
                  pl.BlockSpec((1, 6, d), lambda i, j: (mod_row(i, PROJ_TM), 0, 0)),
                  pl.BlockSpec((d, PROJ_TN), lambda i, j: (0, j))],
        out_specs=pl.BlockSpec((PROJ_TM, PROJ_TN), lambda i, j: (i, j)),
        out_shape=jax.ShapeDtypeStruct((t, PROJ_W), BF16),
        scratch_shapes=[pltpu.VMEM((PROJ_TM, d), BF16)],
        compiler_params=_cparams(("parallel", "arbitrary")),
        name="input_projection",
    )(xs, mod, w_in_p)


def _rope128(y, cos, sin_signed, first_quarter):
    rot = jnp.where(first_quarter, pltpu.roll(y, 96, 1), pltpu.roll(y, 32, 1))
    return y * cos + rot * sin_signed


def _prep_a_kernel(p_ref, cos_ref, sin_ref, gq_ref, gk_ref, q_ref, k_ref, *, scale):
    cos = cos_ref[...]
    sin_signed = sin_ref[...]
    lane = lax.broadcasted_iota(jnp.int32, cos.shape, 1)
    first_quarter = (lane & 32) == 0
    for h in range(GQA_HEADS + GQA_KV_HEADS):
        xh = p_ref[:, h * HEAD_DIM:(h + 1) * HEAD_DIM].astype(F32)
        r = lax.rsqrt(jnp.mean(xh * xh, axis=1, keepdims=True) + RMS_EPS)
        if h < GQA_HEADS:
            y = _rope128(xh * r * gq_ref[...], cos, sin_signed, first_quarter) * scale
            q_ref[:, h * HEAD_DIM:(h + 1) * HEAD_DIM] = y.astype(BF16)
        else:
            hk = h - GQA_HEADS
            y = _rope128(xh * r * gk_ref[...], cos, sin_signed, first_quarter)
            k_ref[:, hk * HEAD_DIM:(hk + 1) * HEAD_DIM] = y.astype(BF16)


def _prep_a_call(p, cos_t, sin_t, gq, gk, rope_blk):
    t = p.shape[0]
    tm = ROW_TM
    wq, wk = GQA_HEADS * HEAD_DIM, GQA_KV_HEADS * HEAD_DIM
    return pl.pallas_call(
        functools.partial(_prep_a_kernel, scale=1.0 / math.sqrt(HEAD_DIM)),
        grid=(t // tm,),
        in_specs=[pl.BlockSpec((tm, wq + wk), lambda i: (i, 0)),
                  pl.BlockSpec((tm, HEAD_DIM), lambda i: (rope_blk(i, tm), 0)),
                  pl.BlockSpec((tm, HEAD_DIM), lambda i: (rope_blk(i, tm), 0)),
                  pl.BlockSpec((1, HEAD_DIM), lambda i: (0, 0)),
                  pl.BlockSpec((1, HEAD_DIM), lambda i: (0, 0))],
        out_specs=[pl.BlockSpec((tm, wq), lambda i: (i, 0)),
                   pl.BlockSpec((tm, wk), lambda i: (i, 0))],
        out_shape=[jax.ShapeDtypeStruct((t, wq), BF16), jax.ShapeDtypeStruct((t, wk), BF16)],
        compiler_params=_cparams(("parallel",)),
        name="gqa_qk_prep",
    )(p, cos_t, sin_t, gq, gk)


def _prep_b_kernel(ql_ref, kvl_ref, kr_ref, cos_ref, sin_ref, gq_ref, gkv_ref, wq_ref, wkv_ref,
                   q_ref, k_ref, v_ref, *, scale):
    nh, hp = MLA_HEADS, MLA_PAD
    wn = nh * MLA_NOPE
    cos = cos_ref[...]
    sin = sin_ref[...]
    ql = ql_ref[...].astype(F32)
    ql = ql * lax.rsqrt(jnp.mean(ql * ql, axis=1, keepdims=True) + RMS_EPS) * gq_ref[...]
    qf = _dot(ql.astype(BF16), wq_ref[...])
    kvl = kvl_ref[...].astype(F32)
    kvl = kvl * lax.rsqrt(jnp.mean(kvl * kvl, axis=1, keepdims=True) + RMS_EPS) * gkv_ref[...]
    kvf = _dot(kvl.astype(BF16), wkv_ref[...])
    lane = lax.broadcasted_iota(jnp.int32, cos.shape, 1)
    low = lane < MLA_ROPE
    tk = kr_ref[...].astype(F32) * jnp.where(low, cos, sin)
    kr = jnp.where(low, tk + pltpu.roll(tk, MLA_ROPE, 1), 0.0).astype(BF16)
    for h in range(nh):
        q_ref[:, h * hp:h * hp + LANES] = (qf[:, h * LANES:(h + 1) * LANES] * scale).astype(BF16)
        qr = qf[:, wn + h * LANES:wn + (h + 1) * LANES] * cos + qf[:, 2 * wn + h * LANES:2 * wn + (h + 1) * LANES] * sin
        q_ref[:, h * hp + LANES:(h + 1) * hp] = (qr * scale).astype(BF16)
        k_ref[:, h * hp:h * hp + LANES] = kvf[:, h * LANES:(h + 1) * LANES].astype(BF16)
        k_ref[:, h * hp + LANES:(h + 1) * hp] = kr
    v_ref[...] = kvf[:, wn:].astype(BF16)


def _prep_b_call(p, cos_t, sin_t, gq, gkv, wq, wkv, rope_blk):
    t = p.shape[0]
    tm = ROW_TM
    nh = MLA_HEADS
    return pl.pallas_call(
        functools.partial(_prep_b_kernel, scale=1.0 / math.sqrt(MLA_QK)),
        grid=(t // tm,),
        in_specs=[pl.BlockSpec((tm, MLA_Q_RANK), lambda i: (i, COL_BQ // MLA_Q_RANK)),
                  pl.BlockSpec((tm, MLA_KV_RANK), lambda i: (i, COL_BKV // MLA_KV_RANK)),
                  pl.BlockSpec((tm, LANES), lambda i: (i, COL_BKR // LANES)),
                  pl.BlockSpec((tm, LANES), lambda i: (rope_blk(i, tm), 0)),
                  pl.BlockSpec((tm, LANES), lambda i: (rope_blk(i, tm), 0)),
                  pl.BlockSpec((1, MLA_Q_RANK), lambda i: (0, 0)),
                  pl.BlockSpec((1, MLA_KV_RANK), lambda i: (0, 0)),
                  pl.BlockSpec(wq.shape, lambda i: (0, 0)),
                  pl.BlockSpec(wkv.shape, lambda i: (0, 0))],
        out_specs=[pl.BlockSpec((tm, nh * MLA_PAD), lambda i: (i, 0)),
                   pl.BlockSpec((tm, nh * MLA_PAD), lambda i: (i, 0)),
                   pl.BlockSpec((tm, nh * MLA_V), lambda i: (i, 0))],
        out_shape=[jax.ShapeDtypeStruct((t, nh * MLA_PAD), BF16),
                   jax.ShapeDtypeStruct((t, nh * MLA_PAD), BF16),
                   jax.ShapeDtypeStruct((t, nh * MLA_V), BF16)],
        compiler_params=_cparams(("parallel",)),
        name="mla_prep",
    )(p, p, p, cos_t, sin_t, gq, gkv, wq, wkv)


def _online_attend(q, chunks):
    m = z = acc = None
    for k, v, bias in chunks:
        s = _dot_nt(q, k)
        if bias is not None:
            s = s + bias
        cm = jnp.max(s, axis=1, keepdims=True)
        if m is None:
            m = cm
            p = jnp.exp(s - m)
            z = jnp.sum(p, axis=1, keepdims=True)
            acc = _dot(p.astype(BF16), v)
        else:
            m_new = jnp.maximum(m, cm)
            corr = jnp.exp(m - m_new)
            p = jnp.exp(s - m_new)
            z = z * corr + jnp.sum(p, axis=1, keepdims=True)
            acc = acc * corr + _dot(p.astype(BF16), v)
            m = m_new
    return acc / z


def _dense_attn_kernel(q_ref, kl_ref, vl_ref, kc_ref, vc_ref, o_ref):
    n_lat = kl_ref.shape[0]
    chunks = [(kl_ref[c:c + ATTN_KC, :], vl_ref[c:c + ATTN_KC, :], None) for c in range(0, n_lat, ATTN_KC)]
    chunks.append((kc_ref[...], vc_ref[...], None))
    o_ref[...] = _online_attend(q_ref[...], chunks).astype(o_ref.dtype)


def _dense_attn_call(q, k, v, *, n_heads, dq, q_col, k_col, v_col, bsz, seq, n_ctx):
    t = q.shape[0]
    tq = ATTN_TQ
    tpb = seq // tq
    ctx0 = bsz * seq // n_ctx
    dv = HEAD_DIM
    return pl.pallas_call(
        _dense_attn_kernel,
        grid=(bsz, n_heads, tpb),
        in_specs=[pl.BlockSpec((tq, dq), lambda b, h, i: (b * tpb + i, q_col(h))),
                  pl.BlockSpec((seq, dq), lambda b, h, i: (b, k_col(h))),
                  pl.BlockSpec((seq, dv), lambda b, h, i: (b, v_col(h))),
                  pl.BlockSpec((n_ctx, dq), lambda b, h, i: (ctx0 + b, k_col(h))),
                  pl.BlockSpec((n_ctx, dv), lambda b, h, i: (ctx0 + b, v_col(h)))],
        out_specs=pl.BlockSpec((tq, dv), lambda b, h, i: (b * tpb + i, h)),
        out_shape=jax.ShapeDtypeStruct((t, n_heads * dv), BF16),
        compiler_params=_cparams(("parallel", "parallel", "arbitrary")),
        name="dense_attention",
    )(q, k, v, k, v)


def _ctx_attn_kernel(q_ref, kc_ref, vc_ref, o_in_ref, o_ref):
    del o_in_ref
    o_ref[...] = _online_attend(q_ref[...], [(kc_ref[...], vc_ref[...], None)]).astype(o_ref.dtype)


def _ctx_attn_call(q, k, v, o, *, n_heads, dq, q_col, k_col, v_col, bsz, seq, n_ctx):
    ctx0 = bsz * seq // n_ctx
    dv = HEAD_DIM
    return pl.pallas_call(
        _ctx_attn_kernel,
        grid=(bsz, n_heads),
        in_specs=[pl.BlockSpec((n_ctx, dq), lambda b, h: (ctx0 + b, q_col(h))),
                  pl.BlockSpec((n_ctx, dq), lambda b, h: (ctx0 + b, k_col(h))),
                  pl.BlockSpec((n_ctx, dv), lambda b, h: (ctx0 + b, v_col(h))),
                  pl.BlockSpec(memory_space=pl.ANY)],
        out_specs=pl.BlockSpec((n_ctx, dv), lambda b, h: (ctx0 + b, h)),
        out_shape=jax.ShapeDtypeStruct(o.shape, o.dtype),
        input_output_aliases={3: 0},
        compiler_params=_cparams(("parallel", "parallel")),
        name="context_attention",
    )(q, k, v, o)


def _na_kernel(q_ref, k_ref, v_ref, kc_ref, vc_ref, bias_ref, o_ref, *, rows_total):
    i = pl.program_id(2)
    band = NA_BAND * GRID_W
    start_blk = jnp.clip(2 * i - 1, 0, (rows_total - NA_BAND) // 4)
    start = pl.multiple_of(start_blk * (4 * GRID_W), 4 * GRID_W)
    kb = k_ref[pl.ds(start, band), :]
    vb = v_ref[pl.ds(start, band), :]
    chunks = [(kb, vb, bias_ref[0, 0]), (kc_ref[...], vc_ref[...], None)]
    o_ref[...] = _online_attend(q_ref[...], chunks).astype(o_ref.dtype)


def _na_call(p, bias_tab, *, bsz, seq, n_ctx):
    t = p.shape[0]
    tq = NA_ROWS * GRID_W
    tpb = seq // tq
    rows_total = seq // GRID_W
    ctx0 = bsz * seq // n_ctx
    d = HEAD_DIM
    qc, kc, vc = COL_CQ // d, COL_CK // d, COL_CV // d

    def variant(i):
        return jnp.where(i == 0, 0, jnp.where(i == tpb - 1, 2, 1))

    return pl.pallas_call(
        functools.partial(_na_kernel, rows_total=rows_total),
        grid=(bsz, NA_HEADS, tpb),
        in_specs=[pl.BlockSpec((tq, d), lambda b, h, i: (b * tpb + i, qc + h)),
                  pl.BlockSpec((seq, d), lambda b, h, i: (b, kc + h)),
                  pl.BlockSpec((seq, d), lambda b, h, i: (b, vc + h)),
                  pl.BlockSpec((n_ctx, d), lambda b, h, i: (ctx0 + b, kc + h)),
                  pl.BlockSpec((n_ctx, d), lambda b, h, i: (ctx0 + b, vc + h)),
                  pl.BlockSpec((1, 1, tq, NA_BAND * GRID_W), lambda b, h, i: (h, variant(i), 0, 0))],
        out_specs=pl.BlockSpec((tq, d), lambda b, h, i: (b * tpb + i, h)),
        out_shape=jax.ShapeDtypeStruct((t, NA_HEADS * d), BF16),
        compiler_params=_cparams(("parallel", "parallel", "arbitrary")),
        name="neighbourhood_attention",
    )(p, p, p, p, p, bias_tab)


def _branch_kernel(oa_ref, ob_ref, oc_ref, ga_ref, gb_ref, gc_ref, wa_ref, wb_ref, wc_ref, o_ref):
    acc = jax.nn.sigmoid(ga_ref[...].astype(F32)) * _dot(oa_ref[...], wa_ref[...])
    acc = acc + jax.nn.sigmoid(gb_ref[...].astype(F32)) * _dot(ob_ref[...], wb_ref[...])
    acc = acc + jax.nn.sigmoid(gc_ref[...].astype(F32)) * _dot(oc_ref[...], wc_ref[...])
    o_ref[...] = acc.astype(BF16)


def _branch_call(o_a, o_b, o_c, p, wa, wb, wc, n_rows):
    d = wa.shape[1]
    tm = MM_TM
    g0 = COL_GATE // d
    return pl.pallas_call(
        _branch_kernel,
        grid=(n_rows // tm,),
        in_specs=[pl.BlockSpec((tm, o_a.shape[1]), lambda i: (i, 0)),
                  pl.BlockSpec((tm, o_b.shape[1]), lambda i: (i, 0)),
                  pl.BlockSpec((tm, o_c.shape[1]), lambda i: (i, 0)),
                  pl.BlockSpec((tm, d), lambda i: (i, g0)),
                  pl.BlockSpec((tm, d), lambda i: (i, g0 + 1)),
                  pl.BlockSpec((tm, d), lambda i: (i, g0 + 2)),
                  pl.BlockSpec(wa.shape, lambda i: (0, 0)),
                  pl.BlockSpec(wb.shape, lambda i: (0, 0)),
                  pl.BlockSpec(wc.shape, lambda i: (0, 0))],
        out_specs=pl.BlockSpec((tm, d), lambda i: (i, 0)),
        out_shape=jax.ShapeDtypeStruct((n_rows, d), BF16),
        compiler_params=_cparams(("parallel",)),
        name="branch_merge",
    )(o_a, o_b, o_c, p, p, p, wa, wb, wc)


def _layer_norm(z, g, b):
    mu = jnp.mean(z, axis=1, keepdims=True)
    zc = z - mu
    var = jnp.mean(zc * zc, axis=1, keepdims=True)
    return zc * lax.rsqrt(var + LN_EPS) * g + b


def _route(logits):
    lane = lax.broadcasted_iota(jnp.int32, logits.shape, 1)
    lane_f = lane.astype(F32)
    is_g = lane < N_GROUPS
    gl = jnp.where(is_g, logits, NEG_BIG)
    mg = jnp.max(gl, axis=1, keepdims=True)
    gsel = jnp.min(jnp.where(gl == mg, lane_f, float(LANES)), axis=1, keepdims=True)
    zg = jnp.sum(jnp.where(is_g, jnp.exp(gl - mg), 0.0), axis=1, keepdims=True)
    lo = N_GROUPS + gsel * EXPERTS_PER_GROUP
    is_e = (lane_f >= lo) & (lane_f < lo + EXPERTS_PER_GROUP)
    el = jnp.where(is_e, logits, NEG_BIG)
    t1 = jnp.max(el, axis=1, keepdims=True)
    i1 = jnp.min(jnp.where(el == t1, lane_f, float(LANES)), axis=1, keepdims=True)
    el2 = jnp.where(lane_f == i1, NEG_BIG, el)
    t2 = jnp.max(el2, axis=1, keepdims=True)
    i2 = jnp.min(jnp.where(el2 == t2, lane_f, float(LANES)), axis=1, keepdims=True)
    dd = jnp.exp(t2 - t1)
    g1 = 1.0 / (zg * (1.0 + dd))
    g2 = g1 * dd
    slab = jnp.where(lane == 0, i1 - N_GROUPS,
                     jnp.where(lane == 1, i2 - N_GROUPS,
                               jnp.where(lane == 2, g1, jnp.where(lane == 3, g2, 0.0))))
    return slab


def _out_kernel(m_ref, x_ref, mod_ref, w_ref, lg_ref, lb_ref, wrh_ref, wrl_ref, br_ref,
                xo_ref, u_ref, r_ref, *, alpha):
    y = _dot(m_ref[...], w_ref[...])
    gate_m = mod_ref[0, 2:3, :]
    shift_f = mod_ref[0, 3:4, :]
    scale_f = mod_ref[0, 4:5, :]
    xn = _layer_norm(alpha * x_ref[...] + gate_m * y, lg_ref[...], lb_ref[...])
    xo_ref[...] = xn
    u = xn * (1.0 + scale_f) + shift_f
    u_hi, u_lo = _split_bf16(u)
    u_ref[...] = u_hi
    logits = _dot(u_hi, wrh_ref[...]) + _dot(u_lo, wrh_ref[...]) + _dot(u_hi, wrl_ref[...]) + br_ref[...]
    r_ref[...] = _route(logits)


def _out_call(mrg, xs, mod, w_out, ln_g, ln_b, wr_hi, wr_lo, br, mod_row, n_rows, alpha):
    d = xs.shape[1]
    tm = MM_TM
    return pl.pallas_call(
        functools.partial(_out_kernel, alpha=alpha),
        grid=(n_rows // tm,),
        in_specs=[pl.BlockSpec((tm, d), lambda i: (i, 0)),
                  pl.BlockSpec((tm, d), lambda i: (i, 0)),
                  pl.BlockSpec((1, 6, d), lambda i: (mod_row(i, tm), 0, 0)),
                  pl.BlockSpec((d, d), lambda i: (0, 0)),
                  pl.BlockSpec((1, d), lambda i: (0, 0)),
                  pl.BlockSpec((1, d), lambda i: (0, 0)),
                  pl.BlockSpec((d, LANES), lambda i: (0, 0)),
                  pl.BlockSpec((d, LANES), lambda i: (0, 0)),
                  pl.BlockSpec((1, LANES), lambda i: (0, 0))],
        out_specs=[pl.BlockSpec((tm, d), lambda i: (i, 0)),
                   pl.BlockSpec((tm, d), lambda i: (i, 0)),
                   pl.BlockSpec((tm, LANES), lambda i: (i, 0))],
        out_shape=[jax.ShapeDtypeStruct((n_rows, d), F32),
                   jax.ShapeDtypeStruct((n_rows, d), BF16),
                   jax.ShapeDtypeStruct((n_rows, LANES), F32)],
        compiler_params=_cparams(("parallel",)),
        name="out_proj_ln_route",
    )(mrg, xs, mod, w_out, ln_g, ln_b, wr_hi, wr_lo, br)


def _rank_kernel(r_ref, rank_ref, cnt_ref, carry):
    @pl.when(pl.program_id(0) == 0)
    def _():
        carry[...] = jnp.zeros_like(carry)

    slab = r_ref[...]
    tm = slab.shape[0]
    lane = lax.broadcasted_iota(jnp.int32, slab.shape, 1)
    lane_f = lane.astype(F32)
    e1 = slab[:, 0:1]
    e2 = slab[:, 1:2]
    hit1 = lane_f == e1
    hit2 = lane_f == e2
    onehot = jnp.where(hit1 | hit2, 1.0, 0.0)
    row = lax.broadcasted_iota(jnp.int32, (tm, tm), 0)
    col = lax.broadcasted_iota(jnp.int32, (tm, tm), 1)
    lower = jnp.where(col < row, 1.0, 0.0).astype(BF16)
    before = _dot(lower, onehot.astype(BF16)) + carry[0:1, :]
    r1 = jnp.sum(jnp.where(hit1, before, 0.0), axis=1, keepdims=True)
    r2 = jnp.sum(jnp.where(hit2, before, 0.0), axis=1, keepdims=True)
    rank_ref[...] = jnp.where(lane == 0, r1, jnp.where(lane == 1, r2, 0.0))
    carry[...] = carry[...] + jnp.sum(onehot, axis=0, keepdims=True)
    cnt_ref[...] = carry[...]


def _rank_call(route):
    n = route.shape[0]
    tm = ROW_TM
    return pl.pallas_call(
        _rank_kernel,
        grid=(n // tm,),
        in_specs=[pl.BlockSpec((tm, LANES), lambda i: (i, 0))],
        out_specs=[pl.BlockSpec((tm, LANES), lambda i: (i, 0)),
                   pl.BlockSpec((8, LANES), lambda i: (0, 0))],
        out_shape=[jax.ShapeDtypeStruct((n, LANES), F32), jax.ShapeDtypeStruct((8, LANES), F32)],
        scratch_shapes=[pltpu.VMEM((8, LANES), F32)],
        compiler_params=_cparams(("arbitrary",)),
        name="expert_ranks",
    )(route)


def _moe_kernel(be_ref, nu_ref, x_ref, wg_ref, wu_ref, wd_ref, y_ref):
    j = pl.program_id(0)

    @pl.when(j < nu_ref[0])
    def _():
        xb = x_ref[...]
        hg = _dot(xb, wg_ref[0, 0].astype(BF16))
        hu = _dot(xb, wu_ref[0, 0].astype(BF16))
        hb = (hg * jax.nn.sigmoid(hg) * hu).astype(BF16)
        y_ref[...] = _dot(hb, wd_ref[0, 0].astype(BF16)).astype(y_ref.dtype)

    @pl.when(j >= nu_ref[0])
    def _():
        y_ref[...] = jnp.zeros_like(y_ref)


def _moe_call(block_e, n_used, u_sorted, w_gate, w_up, w_down, layer):
    n_slots, d = u_sorted.shape
    bm = MOE_BM
    nb = n_slots // bm
    de = w_gate.shape[-1]

    def xmap(j, be, nu):
        return (jnp.minimum(j, nu[0] - 1), 0)

    grid_spec = pltpu.PrefetchScalarGridSpec(
        num_scalar_prefetch=2,
        grid=(nb,),
        in_specs=[pl.BlockSpec((bm, d), xmap),
                  pl.BlockSpec((1, 1, d, de), lambda j, be, nu: (layer, be[j], 0, 0)),
                  pl.BlockSpec((1, 1, d, de), lambda j, be, nu: (layer, be[j], 0, 0)),
                  pl.BlockSpec((1, 1, de, d), lambda j, be, nu: (layer, be[j], 0, 0))],
        out_specs=pl.BlockSpec((bm, d), lambda j, be, nu: (j, 0)),
    )
    return pl.pallas_call(
        _moe_kernel,
        grid_spec=grid_spec,
        out_shape=jax.ShapeDtypeStruct((n_slots, d), BF16),
        compiler_params=_cparams(("arbitrary",)),
        name="expert_mlp",
    )(block_e, n_used, u_sorted, w_gate, w_up, w_down)


def _combine_kernel(x_ref, y0_ref, y1_ref, r_ref, mod_ref, lg_ref, lb_ref, o_ref, *, alpha):
    slab = r_ref[...]
    g1 = slab[:, 2:3]
    g2 = slab[:, 3:4]
    mx = g1 * y0_ref[...].astype(F32) + g2 * y1_ref[...].astype(F32)
    gate_f = mod_ref[0, 5:6, :]
    o_ref[...] = _layer_norm(alpha * x_ref[...] + gate_f * mx, lg_ref[...], lb_ref[...])


def _combine_call(xs, y0, y1, route, mod, ln_g, ln_b, mod_row, alpha):
    n, d = xs.shape
    tm = ROW_TM
    return pl.pallas_call(
        functools.partial(_combine_kernel, alpha=alpha),
        grid=(n // tm,),
        in_specs=[pl.BlockSpec((tm, d), lambda i: (i, 0)),
                  pl.BlockSpec((tm, d), lambda i: (i, 0)),
                  pl.BlockSpec((tm, d), lambda i: (i, 0)),
                  pl.BlockSpec((tm, LANES), lambda i: (i, 0)),
                  pl.BlockSpec((1, 6, d), lambda i: (mod_row(i, tm), 0, 0)),
                  pl.BlockSpec((1, d), lambda i: (0, 0)),
                  pl.BlockSpec((1, d), lambda i: (0, 0))],
        out_specs=pl.BlockSpec((tm, d), lambda i: (i, 0)),
        out_shape=jax.ShapeDtypeStruct((n, d), F32),
        compiler_params=_cparams(("parallel",)),
        name="moe_combine_ln",
    )(xs, y0, y1, route, mod, ln_g, ln_b)


def _rope_angles(seq, dim):
    tpos = np.arange(seq)
    row = (tpos // GRID_W).astype(np.float32)
    col = (tpos % GRID_W).astype(np.float32)
    quarter = dim // 4
    inv_freq = jnp.asarray(ROPE_THETA, F32) ** (-jnp.arange(quarter, dtype=F32) / quarter)
    ang_r = jnp.asarray(row)[:, None] * inv_freq
    ang_c = jnp.asarray(col)[:, None] * inv_freq
    return jnp.concatenate([ang_r, ang_r, ang_c, ang_c], axis=-1)


def _rot_sign(dim):
    l = np.arange(dim)
    return np.where((l & (dim // 4)) == 0, -1.0, 1.0).astype(np.float32), l ^ (dim // 4)


def _rope_tables(seq, pad_rows):
    ang_a = _rope_angles(seq, HEAD_DIM)
    sign_a, _ = _rot_sign(HEAD_DIM)
    cos_a = jnp.concatenate([jnp.cos(ang_a), jnp.ones((pad_rows, HEAD_DIM), F32)], 0)
    sin_a = jnp.concatenate([jnp.sin(ang_a) * sign_a, jnp.zeros((pad_rows, HEAD_DIM), F32)], 0)
    ang_b = _rope_angles(seq, MLA_ROPE)
    cos_b = jnp.concatenate([jnp.cos(ang_b), jnp.ones((pad_rows, MLA_ROPE), F32)], 0)
    sin_b = jnp.concatenate([jnp.sin(ang_b), jnp.zeros((pad_rows, MLA_ROPE), F32)], 0)
    return cos_a, sin_a, jnp.tile(cos_b, (1, 2)), jnp.tile(sin_b, (1, 2))


def _permute_w_in(w):
    d = w.shape[0]
    o = np.cumsum((0, 768, 256, 256, 512, 256, 64, 640, 640, 640, 6144))
    aq, ak, av, bq, bkv, bkr, cq, ck, cv, gate = (w[:, o[i]:o[i + 1]] for i in range(10))
    sign, perm = _rot_sign(MLA_ROPE)
    bkr_rot = bkr[:, perm] * sign
    cq = cq * (1.0 / math.sqrt(HEAD_DIM))
    out = jnp.concatenate([aq, ak, bq, bkv, bkr, bkr_rot, av, cq, ck, cv, gate], axis=1)
    assert out.shape == (d, PROJ_W)
    return out.astype(BF16)


def _permute_w_uq(w):
    r = w.shape[0]
    w3 = w.reshape(r, MLA_HEADS, MLA_QK)
    nope = w3[:, :, :MLA_NOPE].reshape(r, MLA_HEADS * MLA_NOPE)
    rope = w3[:, :, MLA_NOPE:]
    sign, perm = _rot_sign(MLA_ROPE)
    rot = rope[:, :, perm] * sign
    zpad = jnp.zeros((r, MLA_HEADS, LANES - MLA_ROPE), w.dtype)
    rope_p = jnp.concatenate([rope, zpad], -1).reshape(r, MLA_HEADS * LANES)
    rot_p = jnp.concatenate([rot, zpad], -1).reshape(r, MLA_HEADS * LANES)
    return jnp.concatenate([nope, rope_p, rot_p], axis=1).astype(BF16)


def _permute_w_ukv(w):
    r = w.shape[0]
    w3 = w.reshape(r, MLA_HEADS, MLA_NOPE + MLA_V)
    kn = w3[:, :, :MLA_NOPE].reshape(r, MLA_HEADS * MLA_NOPE)
    vv = w3[:, :, MLA_NOPE:].reshape(r, MLA_HEADS * MLA_V)
    return jnp.concatenate([kn, vv], axis=1).astype(BF16)


def _na_bias_tables(rpb, seq):
    w, kh, kw = GRID_W, NA_KH, NA_KW
    rows = seq // w
    outs = []
    for r0 in (0, NA_ROWS, rows - NA_ROWS):
        rb = int(np.clip(r0 - kh // 2, 0, rows - NA_BAND))
        rq = r0 + np.arange(NA_ROWS)[:, None, None, None]
        cq = np.arange(w)[None, :, None, None]
        rk = rb + np.arange(NA_BAND)[None, None, :, None]
        ck = np.arange(w)[None, None, None, :]
        rs = np.clip(rq - kh // 2, 0, rows - kh)
        cs = np.clip(cq - kw // 2, 0, w - kw)
        valid = (rk >= rs) & (rk < rs + kh) & (ck >= cs) & (ck < cs + kw)
        shape = valid.shape
        dr = np.broadcast_to(np.clip(rk - rq + kh - 1, 0, 2 * kh - 2), shape).reshape(NA_ROWS * w, NA_BAND * w)
        dc = np.broadcast_to(np.clip(ck - cq + kw - 1, 0, 2 * kw - 2), shape).reshape(NA_ROWS * w, NA_BAND * w)
        bias = rpb[:, dr, dc].astype(F32)
        outs.append(jnp.where(valid.reshape(1, NA_ROWS * w, NA_BAND * w), bias, NEG_BIG))
    return jnp.stack(outs, axis=1)


def kernel(x, c, ctx, c_ctx, w_ada, b_ada, w_in, gqa_q_norm, gqa_k_norm, mla_q_norm, mla_kv_norm, mla_w_uq, mla_w_ukv, na_rpb, w_branch_a, w_branch_b, w_branch_c, w_out, ln1_g, ln1_b, w_router_group, b_router_group, w_router_expert, b_router_expert, w_expert_gate, w_expert_up, w_expert_down, ln2_g, ln2_b):
    bsz, seq, d = x.shape
    n_ctx = ctx.shape[1]
    depth = w_ada.shape[0]
    nx, nc = bsz * seq, bsz * n_ctx
    t = nx + nc
    assert seq % PROJ_TM == 0 and nc == PROJ_TM and seq % (NA_BAND * GRID_W) == 0 and nx % n_ctx == 0
    alpha = (2 * depth) ** 0.25

    def mod_row(i, tm):
        return jnp.minimum(i // (seq // tm), bsz)

    def rope_blk(i, tm):
        return jnp.where(i < nx // tm, i % (seq // tm), seq // tm)

    xs = jnp.concatenate([x.reshape(nx, d), ctx.reshape(nc, d)], axis=0)
    cc = jnp.concatenate([c, c_ctx[None], jnp.zeros((8 - bsz - 1, d), F32)], axis=0)
    mod_all = _ada_call(cc, w_ada, b_ada).reshape(depth, 8, 6, d)
    cos_a, sin_a, cos_b, sin_b = _rope_tables(seq, ROW_TM)

    attn_kw = dict(bsz=bsz, seq=seq, n_ctx=n_ctx)
    for i in range(depth):
        last = i == depth - 1
        n_rows = nx if last else t
        mod = mod_all[i]
        p = _proj_call(xs, mod, _permute_w_in(w_in[i]), mod_row)

        qa, ka = _prep_a_call(p, cos_a, sin_a, gqa_q_norm[i][None], gqa_k_norm[i][None], rope_blk)
        a_kw = dict(n_heads=GQA_HEADS, dq=HEAD_DIM, q_col=lambda h: h, k_col=lambda h: h // GQA_GROUP,
                    v_col=lambda h: COL_AV // HEAD_DIM + h // GQA_GROUP, **attn_kw)
        o_a = _dense_attn_call(qa, ka, p, **a_kw)
        qb, kb, vb = _prep_b_call(p, cos_b, sin_b, mla_q_norm[i][None], mla_kv_norm[i][None],
                                  _permute_w_uq(mla_w_uq[i]), _permute_w_ukv(mla_w_ukv[i]), rope_blk)
        b_kw = dict(n_heads=MLA_HEADS, dq=MLA_PAD, q_col=lambda h: h, k_col=lambda h: h, v_col=lambda h: h, **attn_kw)
        o_b = _dense_attn_call(qb, kb, vb, **b_kw)
        o_c = _na_call(p, _na_bias_tables(na_rpb[i], seq), **attn_kw)
        if not last:
            c_kw = dict(n_heads=NA_HEADS, dq=HEAD_DIM, q_col=lambda h: COL_CQ // HEAD_DIM + h,
                        k_col=lambda h: COL_CK // HEAD_DIM + h, v_col=lambda h: COL_CV // HEAD_DIM + h, **attn_kw)
            o_a = _ctx_attn_call(qa, ka, p, o_a, **a_kw)
            o_b = _ctx_attn_call(qb, kb, vb, o_b, **b_kw)
            o_c = _ctx_attn_call(p, p, p, o_c, **c_kw)

        mrg = _branch_call(o_a, o_b, o_c, p, w_branch_a[i].astype(BF16), w_branch_b[i].astype(BF16),
                           w_branch_c[i].astype(BF16), n_rows)
        wr = jnp.concatenate([w_router_group[i], w_router_expert[i],
                              jnp.zeros((d, LANES - N_GROUPS - N_EXPERTS), F32)], axis=1)
        br = jnp.concatenate([b_router_group[i], b_router_expert[i],
                              jnp.zeros((LANES - N_GROUPS - N_EXPERTS,), F32)])[None]
        wr_hi, wr_lo = _split_bf16(wr)
        xs, u_f, route = _out_call(mrg, xs, mod, w_out[i].astype(BF16), ln1_g[i][None], ln1_b[i][None],
                                   wr_hi, wr_lo, br, mod_row, n_rows, alpha)

        rank_slab, cnt = _rank_call(route)
        eid = route[:, :TOP_K].astype(jnp.int32)
        rank = rank_slab[:, :TOP_K].astype(jnp.int32)
        counts = cnt[0, :N_EXPERTS].astype(jnp.int32)
        padded = ((counts + MOE_BM - 1) // MOE_BM) * MOE_BM
        pends = jnp.cumsum(padded)
        pstarts = pends - padded
        dest = pstarts[eid] + rank
        nb = -(-(n_rows * TOP_K) // MOE_BM) + N_EXPERTS
        n_used = (pends[-1] // MOE_BM).astype(jnp.int32)
        blk = jnp.minimum(jnp.arange(nb, dtype=jnp.int32), n_used - 1)
        block_e = jnp.clip(jnp.searchsorted(pends, blk * MOE_BM, side="right"), 0, N_EXPERTS - 1).astype(jnp.int32)
        u_sorted = jnp.zeros((nb * MOE_BM, d), BF16).at[dest.reshape(-1)].set(jnp.repeat(u_f, TOP_K, axis=0))
        y_sorted = _moe_call(block_e, n_used[None], u_sorted, w_expert_gate, w_expert_up, w_expert_down, i)
        y0 = jnp.take(y_sorted, dest[:, 0], axis=0)
        y1 = jnp.take(y_sorted, dest[:, 1], axis=0)
        xs = _combine_call(xs, y0, y1, route, mod, ln2_g[i][None], ln2_b[i][None], mod_row, alpha)
    return xs.reshape(bsz, seq, d)
```

```python
import functools
import math

import jax
import jax.numpy as jnp
import numpy as np
from jax import lax
from jax.experimental import pallas as pl
from jax.experimental.pallas import tpu as pltpu

F32 = jnp.float32
BF16 = jnp.bfloat16

HEAD_DIM = 128
GRID_W = 64
ROPE_THETA = 10000.0
GQA_HEADS, GQA_KV_HEADS = 6, 2
GQA_GROUP = GQA_HEADS // GQA_KV_HEADS
MLA_HEADS, MLA_Q_RANK, MLA_KV_RANK = 5, 512, 256
MLA_NOPE, MLA_ROPE, MLA_V = 128, 64, 128
MLA_QK = MLA_NOPE + MLA_ROPE
MLA_PAD = 256
NA_HEADS, NA_KH, NA_KW = 5, 8, 16
N_GROUPS, EXPERTS_PER_GROUP, TOP_K, D_EXPERT = 8, 8, 2, 512
N_EXPERTS = N_GROUPS * EXPERTS_PER_GROUP
LN_EPS = 1e-6
RMS_EPS = 1e-6
NEG_BIG = -1e30

LANES = 128
VMEM_LIMIT = 56 * 1024 * 1024
PROJ_TM, PROJ_TN = 1024, 1024
ROW_TM = 512
MM_TM = 512
ATTN_TQ = 512
ATTN_KC = 1024
NA_ROWS = 8
NA_BAND = 16
MOE_BM = 256
ADA_TN = 512

COL_AQ, COL_AK = 0, 768
COL_BQ, COL_BKV, COL_BKR = 1024, 1536, 1792
COL_AV, COL_CQ, COL_CK, COL_CV = 1920, 2176, 2816, 3456
COL_GATE = 4096
PROJ_W = 10240


def _cparams(sem):
    return pltpu.CompilerParams(dimension_semantics=sem, vmem_limit_bytes=VMEM_LIMIT)


def _resident(shape):
    return pl.BlockSpec(shape, lambda *_: (0,) * len(shape), pipeline_mode=pl.Buffered(1))


def _dot(a, b):
    return jnp.dot(a, b, preferred_element_type=F32)


def _dot_nt(a, b):
    return lax.dot_general(a, b, (((1,), (1,)), ((), ())), preferred_element_type=F32)


def _split_bf16(a):
    hi = a.astype(BF16)
    lo = (a - hi.astype(F32)).astype(BF16)
    return hi, lo


def _ada_kernel(c_ref, w_ref, b_ref, o_ref):
    a = c_ref[...]
    a = a * jax.nn.sigmoid(a)
    a_hi, a_lo = _split_bf16(a)
    w_hi, w_lo = _split_bf16(w_ref[0])
    acc = _dot(a_hi, w_hi) + _dot(a_lo, w_hi) + _dot(a_hi, w_lo)
    o_ref[0] = acc + b_ref[0]


def _ada_call(cc, w_ada, b_ada):
    depth, d, n = w_ada.shape
    rows = cc.shape[0]
    return pl.pallas_call(
        _ada_kernel,
        grid=(depth, n // ADA_TN),
        in_specs=[pl.BlockSpec((rows, d), lambda l, j: (0, 0)),
                  pl.BlockSpec((1, d, ADA_TN), lambda l, j: (l, 0, j)),
                  pl.BlockSpec((1, 1, ADA_TN), lambda l, j: (l, 0, j))],
        out_specs=pl.BlockSpec((1, rows, ADA_TN), lambda l, j: (l, 0, j)),
        out_shape=jax.ShapeDtypeStruct((depth, rows, n), F32),
        compiler_params=_cparams(("parallel", "parallel")),
        name="ada_modulation",
    )(cc, w_ada, b_ada.reshape(depth, 1, n))


def _proj_kernel(x_ref, mod_ref, w_ref, o_ref, u_scr):
    @pl.when(pl.program_id(1) == 0)
    def _():
        shift = mod_ref[0, 0:1, :]
        scale = mod_ref[0, 1:2, :]
        u_scr[...] = (x_ref[...] * (1.0 + scale) + shift).astype(BF16)

    o_ref[...] = _dot(u_scr[...], w_ref[...]).astype(BF16)


def _proj_call(xs, mod, w_in_p, mod_row):
    t, d = xs.shape
    return pl.pallas_call(
        _proj_kernel,
        grid=(t // PROJ_TM, PROJ_W // PROJ_TN),
        in_specs=[pl.BlockSpec((PROJ_TM, d), lambda i, j: (i, 0)),
                  pl.BlockSpec((1, 6, d), lambda i, j: (mod_row(i, PROJ_TM), 0, 0)),
                  pl.BlockSpec((d, PROJ_TN), lambda i, j: (0, j))],
        out_specs=pl.BlockSpec((PROJ_TM, PROJ_TN), lambda i, j: (i, j)),
        out_shape=jax.ShapeDtypeStruct((t, PROJ_W), BF16),
        scratch_shapes=[pltpu.VMEM((PROJ_TM, d), BF16)],
        compiler_params=_cparams(("parallel", "arbitrary")),
        name="input_projection",
    )(xs, mod, w_in_p)


def _rope128(y, cos, sin_signed, first_quarter):
    rot = jnp.where(first_quarter, pltpu.roll(y, 96, 1), pltpu.roll(y, 32, 1))
    return y * cos + rot * sin_signed


def _prep_a_kernel(p_ref, cos_ref, sin_ref, gq_ref, gk_ref, q_ref, k_ref, *, scale):
    cos = cos_ref[...]
    sin_signed = sin_ref[...]
    lane = lax.broadcasted_iota(jnp.int32, cos.shape, 1)
    first_quarter = (lane & 32) == 0
    for h in range(GQA_HEADS + GQA_KV_HEADS):
        xh = p_ref[:, h * HEAD_DIM:(h + 1) * HEAD_DIM].astype(F32)
        r = lax.rsqrt(jnp.mean(xh * xh, axis=1, keepdims=True) + RMS_EPS)
        if h < GQA_HEADS:
            y = _rope128(xh * r * gq_ref[...], cos, sin_signed, first_quarter) * scale
            q_ref[:, h * HEAD_DIM:(h + 1) * HEAD_DIM] = y.astype(BF16)
        else:
            hk = h - GQA_HEADS
            y = _rope128(xh * r * gk_ref[...], cos, sin_signed, first_quarter)
            k_ref[:, hk * HEAD_DIM:(hk + 1) * HEAD_DIM] = y.astype(BF16)


def _prep_a_call(p, cos_t, sin_t, gq, gk, rope_blk):
    t = p.shape[0]
    tm = ROW_TM
    wq, wk = GQA_HEADS * HEAD_DIM, GQA_KV_HEADS * HEAD_DIM
    return pl.pallas_call(
        functools.partial(_prep_a_kernel, scale=1.0 / math.sqrt(HEAD_DIM)),
        grid=(t // tm,),
        in_specs=[pl.BlockSpec((tm, wq + wk), lambda i: (i, 0)),
                  pl.BlockSpec((tm, HEAD_DIM), lambda i: (rope_blk(i, tm), 0)),
                  pl.BlockSpec((tm, HEAD_DIM), lambda i: (rope_blk(i, tm), 0)),
                  pl.BlockSpec((1, HEAD_DIM), lambda i: (0, 0)),
                  pl.BlockSpec((1, HEAD_DIM), lambda i: (0, 0))],
        out_specs=[pl.BlockSpec((tm, wq), lambda i: (i, 0)),
                   pl.BlockSpec((tm, wk), lambda i: (i, 0))],
        out_shape=[jax.ShapeDtypeStruct((t, wq), BF16), jax.ShapeDtypeStruct((t, wk), BF16)],
        compiler_params=_cparams(("parallel",)),
        name="gqa_qk_prep",
    )(p, cos_t, sin_t, gq, gk)


def _prep_b_kernel(ql_ref, kvl_ref, kr_ref, cos_ref, sin_ref, gq_ref, gkv_ref, wq_ref, wkv_ref,
                   q_ref, k_ref, v_ref, *, scale):
    nh, hp = MLA_HEADS, MLA_PAD
    wn = nh * MLA_NOPE
    cos = cos_ref[...]
    sin = sin_ref[...]
    ql = ql_ref[...].astype(F32)
    ql = ql * lax.rsqrt(jnp.mean(ql * ql, axis=1, keepdims=True) + RMS_EPS) * gq_ref[...]
    qf = _dot(ql.astype(BF16), wq_ref[...])
    kvl = kvl_ref[...].astype(F32)
    kvl = kvl * lax.rsqrt(jnp.mean(kvl * kvl, axis=1, keepdims=True) + RMS_EPS) * gkv_ref[...]
    kvf = _dot(kvl.astype(BF16), wkv_ref[...])
    lane = lax.broadcasted_iota(jnp.int32, cos.shape, 1)
    low = lane < MLA_ROPE
    tk = kr_ref[...].astype(F32) * jnp.where(low, cos, sin)
    kr = jnp.where(low, tk + pltpu.roll(tk, MLA_ROPE, 1), 0.0).astype(BF16)
    for h in range(nh):
        q_ref[:, h * hp:h * hp + LANES] = (qf[:, h * LANES:(h + 1) * LANES] * scale).astype(BF16)
        qr = qf[:, wn + h * LANES:wn + (h + 1) * LANES] * cos + qf[:, 2 * wn + h * LANES:2 * wn + (h + 1) * LANES] * sin
        q_ref[:, h * hp + LANES:(h + 1) * hp] = (qr * scale).astype(BF16)
        k_ref[:, h * hp:h * hp + LANES] = kvf[:, h * LANES:(h + 1) * LANES].astype(BF16)
        k_ref[:, h * hp + LANES:(h + 1) * hp] = kr
    v_ref[...] = kvf[:, wn:].astype(BF16)


def _prep_b_call(p, cos_t, sin_t, gq, gkv, wq, wkv, rope_blk):
    t = p.shape[0]
    tm = ROW_TM
    nh = MLA_HEADS
    return pl.pallas_call(
        functools.partial(_prep_b_kernel, scale=1.0 / math.sqrt(MLA_QK)),
        grid=(t // tm,),
        in_specs=[pl.BlockSpec((tm, MLA_Q_RANK), lambda i: (i, COL_BQ // MLA_Q_RANK)),
                  pl.BlockSpec((tm, MLA_KV_RANK), lambda i: (i, COL_BKV // MLA_KV_RANK)),
                  pl.BlockSpec((tm, LANES), lambda i: (i, COL_BKR // LANES)),
                  pl.BlockSpec((tm, LANES), lambda i: (rope_blk(i, tm), 0)),
                  pl.BlockSpec((tm, LANES), lambda i: (rope_blk(i, tm), 0)),
                  pl.BlockSpec((1, MLA_Q_RANK), lambda i: (0, 0)),
                  pl.BlockSpec((1, MLA_KV_RANK), lambda i: (0, 0)),
                  _resident(wq.shape), _resident(wkv.shape)],
        out_specs=[pl.BlockSpec((tm, nh * MLA_PAD), lambda i: (i, 0)),
                   pl.BlockSpec((tm, nh * MLA_PAD), lambda i: (i, 0)),
                   pl.BlockSpec((tm, nh * MLA_V), lambda i: (i, 0))],
        out_shape=[jax.ShapeDtypeStruct((t, nh * MLA_PAD), BF16),
                   jax.ShapeDtypeStruct((t, nh * MLA_PAD), BF16),
                   jax.ShapeDtypeStruct((t, nh * MLA_V), BF16)],
        compiler_params=_cparams(("parallel",)),
        name="mla_prep",
    )(p, p, p, cos_t, sin_t, gq, gkv, wq, wkv)


def _online_attend(q, chunks):
    m = z = acc = None
    for k, v, bias in chunks:
        s = _dot_nt(q, k)
        if bias is not None:
            s = s + bias
        cm = jnp.max(s, axis=1, keepdims=True)
        if m is None:
            m = cm
            p = jnp.exp(s - m)
            z = jnp.sum(p, axis=1, keepdims=True)
            acc = _dot(p.astype(BF16), v)
        else:
            m_new = jnp.maximum(m, cm)
            corr = jnp.exp(m - m_new)
            p = jnp.exp(s - m_new)
            z = z * corr + jnp.sum(p, axis=1, keepdims=True)
            acc = acc * corr + _dot(p.astype(BF16), v)
            m = m_new
    return acc / z


def _dense_attn_kernel(q_ref, kl_ref, vl_ref, kc_ref, vc_ref, o_ref):
    n_lat = kl_ref.shape[0]
    chunks = [(kl_ref[c:c + ATTN_KC, :], vl_ref[c:c + ATTN_KC, :], None) for c in range(0, n_lat, ATTN_KC)]
    chunks.append((kc_ref[...], vc_ref[...], None))
    o_ref[...] = _online_attend(q_ref[...], chunks).astype(o_ref.dtype)


def _dense_attn_call(q, k, v, *, n_heads, dq, q_col, k_col, v_col, bsz, seq, n_ctx):
    t = q.shape[0]
    tq = ATTN_TQ
    tpb = seq // tq
    ctx0 = bsz * seq // n_ctx
    dv = HEAD_DIM
    return pl.pallas_call(
        _dense_attn_kernel,
        grid=(bsz, n_heads, tpb),
        in_specs=[pl.BlockSpec((tq, dq), lambda b, h, i: (b * tpb + i, q_col(h))),
                  pl.BlockSpec((seq, dq), lambda b, h, i: (b, k_col(h))),
                  pl.BlockSpec((seq, dv), lambda b, h, i: (b, v_col(h))),
                  pl.BlockSpec((n_ctx, dq), lambda b, h, i: (ctx0 + b, k_col(h))),
                  pl.BlockSpec((n_ctx, dv), lambda b, h, i: (ctx0 + b, v_col(h)))],
        out_specs=pl.BlockSpec((tq, dv), lambda b, h, i: (b * tpb + i, h)),
        out_shape=jax.ShapeDtypeStruct((t, n_heads * dv), BF16),
        compiler_params=_cparams(("parallel", "parallel", "arbitrary")),
        name="dense_attention",
    )(q, k, v, k, v)


def _ctx_attn_kernel(q_ref, kc_ref, vc_ref, o_in_ref, o_ref):
    del o_in_ref
    o_ref[...] = _online_attend(q_ref[...], [(kc_ref[...], vc_ref[...], None)]).astype(o_ref.dtype)


def _ctx_attn_call(q, k, v, o, *, n_heads, dq, q_col, k_col, v_col, bsz, seq, n_ctx):
    ctx0 = bsz * seq // n_ctx
    dv = HEAD_DIM
    return pl.pallas_call(
        _ctx_attn_kernel,
        grid=(bsz, n_heads),
        in_specs=[pl.BlockSpec((n_ctx, dq), lambda b, h: (ctx0 + b, q_col(h))),
                  pl.BlockSpec((n_ctx, dq), lambda b, h: (ctx0 + b, k_col(h))),
                  pl.BlockSpec((n_ctx, dv), lambda b, h: (ctx0 + b, v_col(h))),
                  pl.BlockSpec(memory_space=pl.ANY)],
        out_specs=pl.BlockSpec((n_ctx, dv), lambda b, h: (ctx0 + b, h)),
        out_shape=jax.ShapeDtypeStruct(o.shape, o.dtype),
        input_output_aliases={3: 0},
        compiler_params=_cparams(("parallel", "parallel")),
        name="context_attention",
    )(q, k, v, o)


def _na_kernel(q_ref, k_ref, v_ref, kc_ref, vc_ref, bias_ref, o_ref, *, rows_total):
    i = pl.program_id(2)
    band = NA_BAND * GRID_W
    start_blk = jnp.clip(2 * i - 1, 0, (rows_total - NA_BAND) // 4)
    start = pl.multiple_of(start_blk * (4 * GRID_W), 4 * GRID_W)
    kb = k_ref[pl.ds(start, band), :]
    vb = v_ref[pl.ds(start, band), :]
    chunks = [(kb, vb, bias_ref[0, 0]), (kc_ref[...], vc_ref[...], None)]
    o_ref[...] = _online_attend(q_ref[...], chunks).astype(o_ref.dtype)


def _na_call(p, bias_tab, *, bsz, seq, n_ctx):
    t = p.shape[0]
    tq = NA_ROWS * GRID_W
    tpb = seq // tq
    rows_total = seq // GRID_W
    ctx0 = bsz * seq // n_ctx
    d = HEAD_DIM
    qc, kc, vc = COL_CQ // d, COL_CK // d, COL_CV // d

    def variant(i):
        return jnp.where(i == 0, 0, jnp.where(i == tpb - 1, 2, 1))

    return pl.pallas_call(
        functools.partial(_na_kernel, rows_total=rows_total),
        grid=(bsz, NA_HEADS, tpb),
        in_specs=[pl.BlockSpec((tq, d), lambda b, h, i: (b * tpb + i, qc + h)),
                  pl.BlockSpec((seq, d), lambda b, h, i: (b, kc + h)),
                  pl.BlockSpec((seq, d), lambda b, h, i: (b, vc + h)),
                  pl.BlockSpec((n_ctx, d), lambda b, h, i: (ctx0 + b, kc + h)),
                  pl.BlockSpec((n_ctx, d), lambda b, h, i: (ctx0 + b, vc + h)),
                  pl.BlockSpec((1, 1, tq, NA_BAND * GRID_W), lambda b, h, i: (h, variant(i), 0, 0))],
        out_specs=pl.BlockSpec((tq, d), lambda b, h, i: (b * tpb + i, h)),
        out_shape=jax.ShapeDtypeStruct((t, NA_HEADS * d), BF16),
        compiler_params=_cparams(("parallel", "parallel", "arbitrary")),
        name="neighbourhood_attention",
    )(p, p, p, p, p, bias_tab)


def _branch_kernel(oa_ref, ob_ref, oc_ref, ga_ref, gb_ref, gc_ref, wa_ref, wb_ref, wc_ref, o_ref):
    acc = jax.nn.sigmoid(ga_ref[...].astype(F32)) * _dot(oa_ref[...], wa_ref[...])
    acc = acc + jax.nn.sigmoid(gb_ref[...].astype(F32)) * _dot(ob_ref[...], wb_ref[...])
    acc = acc + jax.nn.sigmoid(gc_ref[...].astype(F32)) * _dot(oc_ref[...], wc_ref[...])
    o_ref[...] = acc.astype(BF16)


def _branch_call(o_a, o_b, o_c, p, wa, wb, wc, n_rows):
    d = wa.shape[1]
    tm = MM_TM
    g0 = COL_GATE // d
    return pl.pallas_call(
        _branch_kernel,
        grid=(n_rows // tm,),
        in_specs=[pl.BlockSpec((tm, o_a.shape[1]), lambda i: (i, 0)),
                  pl.BlockSpec((tm, o_b.shape[1]), lambda i: (i, 0)),
                  pl.BlockSpec((tm, o_c.shape[1]), lambda i: (i, 0)),
                  pl.BlockSpec((tm, d), lambda i: (i, g0)),
                  pl.BlockSpec((tm, d), lambda i: (i, g0 + 1)),
                  pl.BlockSpec((tm, d), lambda i: (i, g0 + 2)),
                  _resident(wa.shape), _resident(wb.shape), _resident(wc.shape)],
        out_specs=pl.BlockSpec((tm, d), lambda i: (i, 0)),
        out_shape=jax.ShapeDtypeStruct((n_rows, d), BF16),
        compiler_params=_cparams(("parallel",)),
        name="branch_merge",
    )(o_a, o_b, o_c, p, p, p, wa, wb, wc)


def _layer_norm(z, g, b):
    mu = jnp.mean(z, axis=1, keepdims=True)
    zc = z - mu
    var = jnp.mean(zc * zc, axis=1, keepdims=True)
    return zc * lax.rsqrt(var + LN_EPS) * g + b


def _route(logits):
    lane = lax.broadcasted_iota(jnp.int32, logits.shape, 1)
    lane_f = lane.astype(F32)
    is_g = lane < N_GROUPS
    gl = jnp.where(is_g, logits, NEG_BIG)
    mg = jnp.max(gl, axis=1, keepdims=True)
    gsel = jnp.min(jnp.where(gl == mg, lane_f, float(LANES)), axis=1, keepdims=True)
    zg = jnp.sum(jnp.where(is_g, jnp.exp(gl - mg), 0.0), axis=1, keepdims=True)
    lo = N_GROUPS + gsel * EXPERTS_PER_GROUP
    is_e = (lane_f >= lo) & (lane_f < lo + EXPERTS_PER_GROUP)
    el = jnp.where(is_e, logits, NEG_BIG)
    t1 = jnp.max(el, axis=1, keepdims=True)
    i1 = jnp.min(jnp.where(el == t1, lane_f, float(LANES)), axis=1, keepdims=True)
    el2 = jnp.where(lane_f == i1, NEG_BIG, el)
    t2 = jnp.max(el2, axis=1, keepdims=True)
    i2 = jnp.min(jnp.where(el2 == t2, lane_f, float(LANES)), axis=1, keepdims=True)
    dd = jnp.exp(t2 - t1)
    g1 = 1.0 / (zg * (1.0 + dd))
    g2 = g1 * dd
    slab = jnp.where(lane == 0, i1 - N_GROUPS,
                     jnp.where(lane == 1, i2 - N_GROUPS,
                               jnp.where(lane == 2, g1, jnp.where(lane == 3, g2, 0.0))))
    return slab


def _out_kernel(m_ref, x_ref, mod_ref, w_ref, lg_ref, lb_ref, wrh_ref, wrl_ref, br_ref,
                xo_ref, u_ref, r_ref, *, alpha):
    y = _dot(m_ref[...], w_ref[...])
    gate_m = mod_ref[0, 2:3, :]
    shift_f = mod_ref[0, 3:4, :]
    scale_f = mod_ref[0, 4:5, :]
    xn = _layer_norm(alpha * x_ref[...] + gate_m * y, lg_ref[...], lb_ref[...])
    xo_ref[...] = xn
    u = xn * (1.0 + scale_f) + shift_f
    u_hi, u_lo = _split_bf16(u)
    u_ref[...] = u_hi
    logits = _dot(u_hi, wrh_ref[...]) + _dot(u_lo, wrh_ref[...]) + _dot(u_hi, wrl_ref[...]) + br_ref[...]
    r_ref[...] = _route(logits)


def _out_call(mrg, xs, mod, w_out, ln_g, ln_b, wr_hi, wr_lo, br, mod_row, n_rows, alpha):
    d = xs.shape[1]
    tm = MM_TM
    return pl.pallas_call(
        functools.partial(_out_kernel, alpha=alpha),
        grid=(n_rows // tm,),
        in_specs=[pl.BlockSpec((tm, d), lambda i: (i, 0)),
                  pl.BlockSpec((tm, d), lambda i: (i, 0)),
                  pl.BlockSpec((1, 6, d), lambda i: (mod_row(i, tm), 0, 0)),
                  _resident((d, d)), _resident((1, d)), _resident((1, d)),
                  _resident((d, LANES)), _resident((d, LANES)), _resident((1, LANES))],
        out_specs=[pl.BlockSpec((tm, d), lambda i: (i, 0)),
                   pl.BlockSpec((tm, d), lambda i: (i, 0)),
                   pl.BlockSpec((tm, LANES), lambda i: (i, 0))],
        out_shape=[jax.ShapeDtypeStruct((n_rows, d), F32),
                   jax.ShapeDtypeStruct((n_rows, d), BF16),
                   jax.ShapeDtypeStruct((n_rows, LANES), F32)],
        compiler_params=_cparams(("parallel",)),
        name="out_proj_ln_route",
    )(mrg, xs, mod, w_out, ln_g, ln_b, wr_hi, wr_lo, br)


def _rank_kernel(r_ref, rank_ref, cnt_ref, carry):
    @pl.when(pl.program_id(0) == 0)
    def _():
        carry[...] = jnp.zeros_like(carry)

    slab = r_ref[...]
    tm = slab.shape[0]
    lane = lax.broadcasted_iota(jnp.int32, slab.shape, 1)
    lane_f = lane.astype(F32)
    e1 = slab[:, 0:1]
    e2 = slab[:, 1:2]
    hit1 = lane_f == e1
    hit2 = lane_f == e2
    onehot = jnp.where(hit1 | hit2, 1.0, 0.0)
    row = lax.broadcasted_iota(jnp.int32, (tm, tm), 0)
    col = lax.broadcasted_iota(jnp.int32, (tm, tm), 1)
    lower = jnp.where(col < row, 1.0, 0.0).astype(BF16)
    before = _dot(lower, onehot.astype(BF16)) + carry[0:1, :]
    r1 = jnp.sum(jnp.where(hit1, before, 0.0), axis=1, keepdims=True)
    r2 = jnp.sum(jnp.where(hit2, before, 0.0), axis=1, keepdims=True)
    rank_ref[...] = jnp.where(lane == 0, r1, jnp.where(lane == 1, r2, 0.0))
    carry[...] = carry[...] + jnp.sum(onehot, axis=0, keepdims=True)
    cnt_ref[...] = carry[...]


def _rank_call(route):
    n = route.shape[0]
    tm = ROW_TM
    return pl.pallas_call(
        _rank_kernel,
        grid=(n // tm,),
        in_specs=[pl.BlockSpec((tm, LANES), lambda i: (i, 0))],
        out_specs=[pl.BlockSpec((tm, LANES), lambda i: (i, 0)),
                   pl.BlockSpec((8, LANES), lambda i: (0, 0))],
        out_shape=[jax.ShapeDtypeStruct((n, LANES), F32), jax.ShapeDtypeStruct((8, LANES), F32)],
        scratch_shapes=[pltpu.VMEM((8, LANES), F32)],
        compiler_params=_cparams(("arbitrary",)),
        name="expert_ranks",
    )(route)


def _moe_kernel(be_ref, nu_ref, x_ref, wg_ref, wu_ref, wd_ref, y_ref):
    j = pl.program_id(0)

    @pl.when(j < nu_ref[0])
    def _():
        xb = x_ref[...]
        hg = _dot(xb, wg_ref[0, 0].astype(BF16))
        hu = _dot(xb, wu_ref[0, 0].astype(BF16))
        hb = (hg * jax.nn.sigmoid(hg) * hu).astype(BF16)
        y_ref[...] = _dot(hb, wd_ref[0, 0].astype(BF16)).astype(y_ref.dtype)

    @pl.when(j >= nu_ref[0])
    def _():
        y_ref[...] = jnp.zeros_like(y_ref)


def _moe_call(block_e, n_used, u_sorted, w_gate, w_up, w_down, layer):
    n_slots, d = u_sorted.shape
    bm = MOE_BM
    nb = n_slots // bm
    de = w_gate.shape[-1]

    def xmap(j, be, nu):
        return (jnp.minimum(j, nu[0] - 1), 0)

    grid_spec = pltpu.PrefetchScalarGridSpec(
        num_scalar_prefetch=2,
        grid=(nb,),
        in_specs=[pl.BlockSpec((bm, d), xmap),
                  pl.BlockSpec((1, 1, d, de), lambda j, be, nu: (layer, be[j], 0, 0)),
                  pl.BlockSpec((1, 1, d, de), lambda j, be, nu: (layer, be[j], 0, 0)),
                  pl.BlockSpec((1, 1, de, d), lambda j, be, nu: (layer, be[j], 0, 0))],
        out_specs=pl.BlockSpec((bm, d), lambda j, be, nu: (j, 0)),
    )
    return pl.pallas_call(
        _moe_kernel,
        grid_spec=grid_spec,
        out_shape=jax.ShapeDtypeStruct((n_slots, d), BF16),
        compiler_params=_cparams(("arbitrary",)),
        name="expert_mlp",
    )(block_e, n_used, u_sorted, w_gate, w_up, w_down)


def _combine_kernel(x_ref, y0_ref, y1_ref, r_ref, mod_ref, lg_ref, lb_ref, o_ref, *, alpha):
    slab = r_ref[...]
    g1 = slab[:, 2:3]
    g2 = slab[:, 3:4]
    mx = g1 * y0_ref[...].astype(F32) + g2 * y1_ref[...].astype(F32)
    gate_f = mod_ref[0, 5:6, :]
    o_ref[...] = _layer_norm(alpha * x_ref[...] + gate_f * mx, lg_ref[...], lb_ref[...])


def _combine_call(xs, y0, y1, route, mod, ln_g, ln_b, mod_row, alpha):
    n, d = xs.shape
    tm = ROW_TM
    return pl.pallas_call(
        functools.partial(_combine_kernel, alpha=alpha),
        grid=(n // tm,),
        in_specs=[pl.BlockSpec((tm, d), lambda i: (i, 0)),
                  pl.BlockSpec((tm, d), lambda i: (i, 0)),
                  pl.BlockSpec((tm, d), lambda i: (i, 0)),
                  pl.BlockSpec((tm, LANES), lambda i: (i, 0)),
                  pl.BlockSpec((1, 6, d), lambda i: (mod_row(i, tm), 0, 0)),
                  pl.BlockSpec((1, d), lambda i: (0, 0)),
                  pl.BlockSpec((1, d), lambda i: (0, 0))],
        out_specs=pl.BlockSpec((tm, d), lambda i: (i, 0)),
        out_shape=jax.ShapeDtypeStruct((n, d), F32),
        compiler_params=_cparams(("parallel",)),
        name="moe_combine_ln",
    )(xs, y0, y1, route, mod, ln_g, ln_b)


def _rope_angles(seq, dim):
    tpos = np.arange(seq)
    row = (tpos // GRID_W).astype(np.float32)
    col = (tpos % GRID_W).astype(np.float32)
    quarter = dim // 4
    inv_freq = jnp.asarray(ROPE_THETA, F32) ** (-jnp.arange(quarter, dtype=F32) / quarter)
    ang_r = jnp.asarray(row)[:, None] * inv_freq
    ang_c = jnp.asarray(col)[:, None] * inv_freq
    return jnp.concatenate([ang_r, ang_r, ang_c, ang_c], axis=-1)


def _rot_sign(dim):
    l = np.arange(dim)
    return np.where((l & (dim // 4)) == 0, -1.0, 1.0).astype(np.float32), l ^ (dim // 4)


def _rope_tables(seq, pad_rows):
    ang_a = _rope_angles(seq, HEAD_DIM)
    sign_a, _ = _rot_sign(HEAD_DIM)
    cos_a = jnp.concatenate([jnp.cos(ang_a), jnp.ones((pad_rows, HEAD_DIM), F32)], 0)
    sin_a = jnp.concatenate([jnp.sin(ang_a) * sign_a, jnp.zeros((pad_rows, HEAD_DIM), F32)], 0)
    ang_b = _rope_angles(seq, MLA_ROPE)
    cos_b = jnp.concatenate([jnp.cos(ang_b), jnp.ones((pad_rows, MLA_ROPE), F32)], 0)
    sin_b = jnp.concatenate([jnp.sin(ang_b), jnp.zeros((pad_rows, MLA_ROPE), F32)], 0)
    return cos_a, sin_a, jnp.tile(cos_b, (1, 2)), jnp.tile(sin_b, (1, 2))


def _permute_w_in(w):
    d = w.shape[0]
    o = np.cumsum((0, 768, 256, 256, 512, 256, 64, 640, 640, 640, 6144))
    aq, ak, av, bq, bkv, bkr, cq, ck, cv, gate = (w[:, o[i]:o[i + 1]] for i in range(10))
    sign, perm = _rot_sign(MLA_ROPE)
    bkr_rot = bkr[:, perm] * sign
    cq = cq * (1.0 / math.sqrt(HEAD_DIM))
    out = jnp.concatenate([aq, ak, bq, bkv, bkr, bkr_rot, av, cq, ck, cv, gate], axis=1)
    assert out.shape == (d, PROJ_W)
    return out.astype(BF16)


def _permute_w_uq(w):
    r = w.shape[0]
    w3 = w.reshape(r, MLA_HEADS, MLA_QK)
    nope = w3[:, :, :MLA_NOPE].reshape(r, MLA_HEADS * MLA_NOPE)
    rope = w3[:, :, MLA_NOPE:]
    sign, perm = _rot_sign(MLA_ROPE)
    rot = rope[:, :, perm] * sign
    zpad = jnp.zeros((r, MLA_HEADS, LANES - MLA_ROPE), w.dtype)
    rope_p = jnp.concatenate([rope, zpad], -1).reshape(r, MLA_HEADS * LANES)
    rot_p = jnp.concatenate([rot, zpad], -1).reshape(r, MLA_HEADS * LANES)
    return jnp.concatenate([nope, rope_p, rot_p], axis=1).astype(BF16)


def _permute_w_ukv(w):
    r = w.shape[0]
    w3 = w.reshape(r, MLA_HEADS, MLA_NOPE + MLA_V)
    kn = w3[:, :, :MLA_NOPE].reshape(r, MLA_HEADS * MLA_NOPE)
    vv = w3[:, :, MLA_NOPE:].reshape(r, MLA_HEADS * MLA_V)
    return jnp.concatenate([kn, vv], axis=1).astype(BF16)


def _na_bias_tables(rpb, seq):
    w, kh, kw = GRID_W, NA_KH, NA_KW
    rows = seq // w
    nh, n_dr, n_dc = rpb.shape
    line = jnp.full((nh, n_dr, 2 * w), NEG_BIG, F32).at[:, :, w - kw:w - kw + n_dc].set(rpb.astype(F32))
    skew = jnp.broadcast_to(line[:, :, None, :], (nh, n_dr, w, 2 * w)).reshape(nh, n_dr, 2 * w * w)
    skew = skew[:, :, :w * (2 * w - 1)].reshape(nh, n_dr, w, 2 * w - 1)
    tc = skew[:, :, :, w - 1:2 * w - 1]
    cq = np.arange(w)[:, None]
    ck = np.arange(w)[None, :]
    cs = np.clip(cq - kw // 2, 0, w - kw)
    col_ok = (ck >= cs) & (ck < cs + kw)
    tc = jnp.where(col_ok[None, None], tc, NEG_BIG)
    tc = jnp.concatenate([tc, jnp.full((nh, 1, w, w), NEG_BIG, F32)], axis=1)
    outs = []
    for r0 in (0, NA_ROWS, rows - NA_ROWS):
        rb = int(np.clip(r0 - kh // 2, 0, rows - NA_BAND))
        rq = r0 + np.arange(NA_ROWS)[:, None]
        rk = rb + np.arange(NA_BAND)[None, :]
        rs = np.clip(rq - kh // 2, 0, rows - kh)
        row_ok = (rk >= rs) & (rk < rs + kh)
        blk = np.where(row_ok, rk - rq + kh - 1, n_dr)
        tab = tc[:, blk]
        outs.append(tab.transpose(0, 1, 3, 2, 4).reshape(nh, NA_ROWS * w, NA_BAND * w))
    return jnp.stack(outs, axis=1)


def kernel(x, c, ctx, c_ctx, w_ada, b_ada, w_in, gqa_q_norm, gqa_k_norm, mla_q_norm, mla_kv_norm, mla_w_uq, mla_w_ukv, na_rpb, w_branch_a, w_branch_b, w_branch_c, w_out, ln1_g, ln1_b, w_router_group, b_router_group, w_router_expert, b_router_expert, w_expert_gate, w_expert_up, w_expert_down, ln2_g, ln2_b):
    bsz, seq, d = x.shape
    n_ctx = ctx.shape[1]
    depth = w_ada.shape[0]
    nx, nc = bsz * seq, bsz * n_ctx
    t = nx + nc
    assert seq % PROJ_TM == 0 and nc == PROJ_TM and seq % (NA_BAND * GRID_W) == 0 and nx % n_ctx == 0
    alpha = (2 * depth) ** 0.25

    def mod_row(i, tm):
        return jnp.minimum(i // (seq // tm), bsz)

    def rope_blk(i, tm):
        return jnp.where(i < nx // tm, i % (seq // tm), seq // tm)

    xs = jnp.concatenate([x.reshape(nx, d), ctx.reshape(nc, d)], axis=0)
    cc = jnp.concatenate([c, c_ctx[None], jnp.zeros((8 - bsz - 1, d), F32)], axis=0)
    mod_all = _ada_call(cc, w_ada, b_ada).reshape(depth, 8, 6, d)
    cos_a, sin_a, cos_b, sin_b = _rope_tables(seq, ROW_TM)

    attn_kw = dict(bsz=bsz, seq=seq, n_ctx=n_ctx)
    for i in range(depth):
        last = i == depth - 1
        n_rows = nx if last else t
        mod = mod_all[i]
        p = _proj_call(xs, mod, _permute_w_in(w_in[i]), mod_row)

        qa, ka = _prep_a_call(p, cos_a, sin_a, gqa_q_norm[i][None], gqa_k_norm[i][None], rope_blk)
        a_kw = dict(n_heads=GQA_HEADS, dq=HEAD_DIM, q_col=lambda h: h, k_col=lambda h: h // GQA_GROUP,
                    v_col=lambda h: COL_AV // HEAD_DIM + h // GQA_GROUP, **attn_kw)
        o_a = _dense_attn_call(qa, ka, p, **a_kw)
        qb, kb, vb = _prep_b_call(p, cos_b, sin_b, mla_q_norm[i][None], mla_kv_norm[i][None],
                                  _permute_w_uq(mla_w_uq[i]), _permute_w_ukv(mla_w_ukv[i]), rope_blk)
        b_kw = dict(n_heads=MLA_HEADS, dq=MLA_PAD, q_col=lambda h: h, k_col=lambda h: h, v_col=lambda h: h, **attn_kw)
        o_b = _dense_attn_call(qb, kb, vb, **b_kw)
        o_c = _na_call(p, _na_bias_tables(na_rpb[i], seq), **attn_kw)
        if not last:
            c_kw = dict(n_heads=NA_HEADS, dq=HEAD_DIM, q_col=lambda h: COL_CQ // HEAD_DIM + h,
                        k_col=lambda h: COL_CK // HEAD_DIM + h, v_col=lambda h: COL_CV // HEAD_DIM + h, **attn_kw)
            o_a = _ctx_attn_call(qa, ka, p, o_a, **a_kw)
            o_b = _ctx_attn_call(qb, kb, vb, o_b, **b_kw)
            o_c = _ctx_attn_call(p, p, p, o_c, **c_kw)

        mrg = _branch_call(o_a, o_b, o_c, p, w_branch_a[i].astype(BF16), w_branch_b[i].astype(BF16),
                           w_branch_c[i].astype(BF16), n_rows)
        wr = jnp.concatenate([w_router_group[i], w_router_expert[i],
                              jnp.zeros((d, LANES - N_GROUPS - N_EXPERTS), F32)], axis=1)
        br = jnp.concatenate([b_router_group[i], b_router_expert[i],
                              jnp.zeros((LANES - N_GROUPS - N_EXPERTS,), F32)])[None]
        wr_hi, wr_lo = _split_bf16(wr)
        xs, u_f, route = _out_call(mrg, xs, mod, w_out[i].astype(BF16), ln1_g[i][None], ln1_b[i][None],
                                   wr_hi, wr_lo, br, mod_row, n_rows, alpha)

        rank_slab, cnt = _rank_call(route)
        eid = route[:, :TOP_K].astype(jnp.int32)
        rank = rank_slab[:, :TOP_K].astype(jnp.int32)
        counts = cnt[0, :N_EXPERTS].astype(jnp.int32)
        padded = ((counts + MOE_BM - 1) // MOE_BM) * MOE_BM
        pends = jnp.cumsum(padded)
        pstarts = pends - padded
        dest = pstarts[eid] + rank
        nb = -(-(n_rows * TOP_K) // MOE_BM) + N_EXPERTS
        n_used = (pends[-1] // MOE_BM).astype(jnp.int32)
        blk = jnp.minimum(jnp.arange(nb, dtype=jnp.int32), n_used - 1)
        block_e = jnp.clip(jnp.searchsorted(pends, blk * MOE_BM, side="right"), 0, N_EXPERTS - 1).astype(jnp.int32)
        order = jnp.argsort(eid.reshape(-1), stable=True).astype(jnp.int32)
        starts = jnp.cumsum(counts) - counts
        slot = jnp.arange(nb * MOE_BM, dtype=jnp.int32)
        slot_e = jnp.repeat(block_e, MOE_BM)
        off = slot - pstarts[slot_e]
        pos = jnp.clip(starts[slot_e] + off, 0, n_rows * TOP_K - 1)
        slot_tok = jnp.where(off < counts[slot_e], order[pos] // TOP_K, 0)
        u_sorted = jnp.take(u_f, slot_tok, axis=0)
        y_sorted = _moe_call(block_e, n_used[None], u_sorted, w_expert_gate, w_expert_up, w_expert_down, i)
        y0 = jnp.take(y_sorted, dest[:, 0], axis=0)
        y1 = jnp.take(y_sorted, dest[:, 1], axis=0)
        xs = _combine_call(xs, y0, y1, route, mod, ln2_g[i][None], ln2_b[i][None], mod_row, alpha)
    return xs.reshape(bsz, seq, d)
```

```python
import functools
import math

import jax
import jax.numpy as jnp
import numpy as np
from jax import lax
from jax.experimental import pallas as pl
from jax.experimental.pallas import tpu as pltpu

F32 = jnp.float32
BF16 = jnp.bfloat16

HEAD_DIM = 128
GRID_W = 64
ROPE_THETA = 10000.0
GQA_HEADS, GQA_KV_HEADS = 6, 2
GQA_GROUP = GQA_HEADS // GQA_KV_HEADS
MLA_HEADS, MLA_Q_RANK, MLA_KV_RANK = 5, 512, 256
MLA_NOPE, MLA_ROPE, MLA_V = 128, 64, 128
MLA_QK = MLA_NOPE + MLA_ROPE
MLA_PAD = 256
NA_HEADS, NA_KH, NA_KW = 5, 8, 16
N_GROUPS, EXPERTS_PER_GROUP, TOP_K, D_EXPERT = 8, 8, 2, 512
N_EXPERTS = N_GROUPS * EXPERTS_PER_GROUP
LN_EPS = 1e-6
RMS_EPS = 1e-6
NEG_BIG = -1e30

LANES = 128
VMEM_LIMIT = 56 * 1024 * 1024
PROJ_TM, PROJ_TN = 1024, 1024
ROW_TM = 512
MM_TM = 512
ATTN_TQ = 512
ATTN_KC = 1024
NA_ROWS = 8
NA_BAND = 16
MOE_BM = 256
ADA_TN = 512

COL_AQ, COL_AK = 0, 768
COL_BQ, COL_BKV, COL_BKR = 1024, 1536, 1792
COL_AV, COL_CQ, COL_CK, COL_CV = 1920, 2176, 2816, 3456
COL_GATE = 4096
PROJ_W = 10240


def _cparams(sem):
    return pltpu.CompilerParams(dimension_semantics=sem, vmem_limit_bytes=VMEM_LIMIT)


def _resident(shape):
    return pl.BlockSpec(shape, lambda *_: (0,) * len(shape), pipeline_mode=pl.Buffered(1))


def _dot(a, b):
    return jnp.dot(a, b, preferred_element_type=F32)


def _dot_nt(a, b):
    return lax.dot_general(a, b, (((1,), (1,)), ((), ())), preferred_element_type=F32)


def _split_bf16(a):
    hi = a.astype(BF16)
    lo = (a - hi.astype(F32)).astype(BF16)
    return hi, lo


def _ada_kernel(c_ref, w_ref, b_ref, o_ref):
    a = c_ref[...]
    a = a * jax.nn.sigmoid(a)
    a_hi, a_lo = _split_bf16(a)
    w_hi, w_lo = _split_bf16(w_ref[0])
    acc = _dot(a_hi, w_hi) + _dot(a_lo, w_hi) + _dot(a_hi, w_lo)
    o_ref[0] = acc + b_ref[0]


def _ada_call(cc, w_ada, b_ada):
    depth, d, n = w_ada.shape
    rows = cc.shape[0]
    return pl.pallas_call(
        _ada_kernel,
        grid=(depth, n // ADA_TN),
        in_specs=[pl.BlockSpec((rows, d), lambda l, j: (0, 0)),
                  pl.BlockSpec((1, d, ADA_TN), lambda l, j: (l, 0, j)),
                  pl.BlockSpec((1, 1, ADA_TN), lambda l, j: (l, 0, j))],
        out_specs=pl.BlockSpec((1, rows, ADA_TN), lambda l, j: (l, 0, j)),
        out_shape=jax.ShapeDtypeStruct((depth, rows, n), F32),
        compiler_params=_cparams(("parallel", "parallel")),
        name="ada_modulation",
    )(cc, w_ada, b_ada.reshape(depth, 1, n))


def _proj_kernel(x_ref, mod_ref, w_ref, o_ref, u_scr):
    @pl.when(pl.program_id(1) == 0)
    def _():
        shift = mod_ref[0, 0:1, :]
        scale = mod_ref[0, 1:2, :]
        u_scr[...] = (x_ref[...] * (1.0 + scale) + shift).astype(BF16)

    o_ref[...] = _dot(u_scr[...], w_ref[...]).astype(BF16)


def _proj_call(xs, mod, w_in_p, mod_row):
    t, d = xs.shape
    return pl.pallas_call(
        _proj_kernel,
        grid=(t // PROJ_TM, PROJ_W // PROJ_TN),
        in_specs=[pl.BlockSpec((PROJ_TM, d), lambda i, j: (i, 0)),
                  pl.BlockSpec((1, 6, d), lambda i, j: (mod_row(i, PROJ_TM), 0, 0)),
                  pl.BlockSpec((d, PROJ_TN), lambda i, j: (0, j))],
        out_specs=pl.BlockSpec((PROJ_TM, PROJ_TN), lambda i, j: (i, j)),
        out_shape=jax.ShapeDtypeStruct((t, PROJ_W), BF16),
        scratch_shapes=[pltpu.VMEM((PROJ_TM, d), BF16)],
        compiler_params=_cparams(("parallel", "arbitrary")),
        name="input_projection",
    )(xs, mod, w_in_p)


def _rope128(y, cos, sin_signed, first_quarter):
    rot = jnp.where(first_quarter, pltpu.roll(y, 96, 1), pltpu.roll(y, 32, 1))
    return y * cos + rot * sin_signed


def _prep_a_kernel(p_ref, cos_ref, sin_ref, gq_ref, gk_ref, q_ref, k_ref, *, scale):
    cos = cos_ref[...]
    sin_signed = sin_ref[...]
    lane = lax.broadcasted_iota(jnp.int32, cos.shape, 1)
    first_quarter = (lane & 32) == 0
    for h in range(GQA_HEADS + GQA_KV_HEADS):
        xh = p_ref[:, h * HEAD_DIM:(h + 1) * HEAD_DIM].astype(F32)
        r = lax.rsqrt(jnp.mean(xh * xh, axis=1, keepdims=True) + RMS_EPS)
        if h < GQA_HEADS:
            y = _rope128(xh * r * gq_ref[...], cos, sin_signed, first_quarter) * scale
            q_ref[:, h * HEAD_DIM:(h + 1) * HEAD_DIM] = y.astype(BF16)
        else:
            hk = h - GQA_HEADS
            y = _rope128(xh * r * gk_ref[...], cos, sin_signed, first_quarter)
            k_ref[:, hk * HEAD_DIM:(hk + 1) * HEAD_DIM] = y.astype(BF16)


def _prep_a_call(p, cos_t, sin_t, gq, gk, rope_blk):
    t = p.shape[0]
    tm = ROW_TM
    wq, wk = GQA_HEADS * HEAD_DIM, GQA_KV_HEADS * HEAD_DIM
    return pl.pallas_call(
        functools.partial(_prep_a_kernel, scale=1.0 / math.sqrt(HEAD_DIM)),
        grid=(t // tm,),
        in_specs=[pl.BlockSpec((tm, wq + wk), lambda i: (i, 0)),
                  pl.BlockSpec((tm, HEAD_DIM), lambda i: (rope_blk(i, tm), 0)),
                  pl.BlockSpec((tm, HEAD_DIM), lambda i: (rope_blk(i, tm), 0)),
                  pl.BlockSpec((1, HEAD_DIM), lambda i: (0, 0)),
                  pl.BlockSpec((1, HEAD_DIM), lambda i: (0, 0))],
        out_specs=[pl.BlockSpec((tm, wq), lambda i: (i, 0)),
                   pl.BlockSpec((tm, wk), lambda i: (i, 0))],
        out_shape=[jax.ShapeDtypeStruct((t, wq), BF16), jax.ShapeDtypeStruct((t, wk), BF16)],
        compiler_params=_cparams(("parallel",)),
        name="gqa_qk_prep",
    )(p, cos_t, sin_t, gq, gk)


def _prep_b_kernel(ql_ref, kvl_ref, kr_ref, cos_ref, sin_ref, gq_ref, gkv_ref, wq_ref, wkv_ref,
                   q_ref, k_ref, v_ref, *, scale):
    nh, hp = MLA_HEADS, MLA_PAD
    wn = nh * MLA_NOPE
    cos = cos_ref[...]
    sin = sin_ref[...]
    ql = ql_ref[...].astype(F32)
    ql = ql * lax.rsqrt(jnp.mean(ql * ql, axis=1, keepdims=True) + RMS_EPS) * gq_ref[...]
    qf = _dot(ql.astype(BF16), wq_ref[...])
    kvl = kvl_ref[...].astype(F32)
    kvl = kvl * lax.rsqrt(jnp.mean(kvl * kvl, axis=1, keepdims=True) + RMS_EPS) * gkv_ref[...]
    kvf = _dot(kvl.astype(BF16), wkv_ref[...])
    lane = lax.broadcasted_iota(jnp.int32, cos.shape, 1)
    low = lane < MLA_ROPE
    tk = kr_ref[...].astype(F32) * jnp.where(low, cos, sin)
    kr = jnp.where(low, tk + pltpu.roll(tk, MLA_ROPE, 1), 0.0).astype(BF16)
    for h in range(nh):
        q_ref[:, h * hp:h * hp + LANES] = (qf[:, h * LANES:(h + 1) * LANES] * scale).astype(BF16)
        qr = qf[:, wn + h * LANES:wn + (h + 1) * LANES] * cos + qf[:, 2 * wn + h * LANES:2 * wn + (h + 1) * LANES] * sin
        q_ref[:, h * hp + LANES:(h + 1) * hp] = (qr * scale).astype(BF16)
        k_ref[:, h * hp:h * hp + LANES] = kvf[:, h * LANES:(h + 1) * LANES].astype(BF16)
        k_ref[:, h * hp + LANES:(h + 1) * hp] = kr
    v_ref[...] = kvf[:, wn:].astype(BF16)


def _prep_b_call(p, cos_t, sin_t, gq, gkv, wq, wkv, rope_blk):
    t = p.shape[0]
    tm = ROW_TM
    nh = MLA_HEADS
    return pl.pallas_call(
        functools.partial(_prep_b_kernel, scale=1.0 / math.sqrt(MLA_QK)),
        grid=(t // tm,),
        in_specs=[pl.BlockSpec((tm, MLA_Q_RANK), lambda i: (i, COL_BQ // MLA_Q_RANK)),
                  pl.BlockSpec((tm, MLA_KV_RANK), lambda i: (i, COL_BKV // MLA_KV_RANK)),
                  pl.BlockSpec((tm, LANES), lambda i: (i, COL_BKR // LANES)),
                  pl.BlockSpec((tm, LANES), lambda i: (rope_blk(i, tm), 0)),
                  pl.BlockSpec((tm, LANES), lambda i: (rope_blk(i, tm), 0)),
                  pl.BlockSpec((1, MLA_Q_RANK), lambda i: (0, 0)),
                  pl.BlockSpec((1, MLA_KV_RANK), lambda i: (0, 0)),
                  _resident(wq.shape), _resident(wkv.shape)],
        out_specs=[pl.BlockSpec((tm, nh * MLA_PAD), lambda i: (i, 0)),
                   pl.BlockSpec((tm, nh * MLA_PAD), lambda i: (i, 0)),
                   pl.BlockSpec((tm, nh * MLA_V), lambda i: (i, 0))],
        out_shape=[jax.ShapeDtypeStruct((t, nh * MLA_PAD), BF16),
                   jax.ShapeDtypeStruct((t, nh * MLA_PAD), BF16),
                   jax.ShapeDtypeStruct((t, nh * MLA_V), BF16)],
        compiler_params=_cparams(("parallel",)),
        name="mla_prep",
    )(p, p, p, cos_t, sin_t, gq, gkv, wq, wkv)


def _online_attend(q, chunks):
    m = z = acc = None
    for k, v, bias in chunks:
        s = _dot_nt(q, k)
        if bias is not None:
            s = s + bias
        cm = jnp.max(s, axis=1, keepdims=True)
        if m is None:
            m = cm
            p = jnp.exp(s - m)
            z = jnp.sum(p, axis=1, keepdims=True)
            acc = _dot(p.astype(BF16), v)
        else:
            m_new = jnp.maximum(m, cm)
            corr = jnp.exp(m - m_new)
            p = jnp.exp(s - m_new)
            z = z * corr + jnp.sum(p, axis=1, keepdims=True)
            acc = acc * corr + _dot(p.astype(BF16), v)
            m = m_new
    return acc / z


def _dense_attn_kernel(q_ref, kl_ref, vl_ref, kc_ref, vc_ref, o_ref):
    n_lat = kl_ref.shape[0]
    chunks = [(kl_ref[c:c + ATTN_KC, :], vl_ref[c:c + ATTN_KC, :], None) for c in range(0, n_lat, ATTN_KC)]
    chunks.append((kc_ref[...], vc_ref[...], None))
    o_ref[...] = _online_attend(q_ref[...], chunks).astype(o_ref.dtype)


def _dense_attn_call(q, k, v, *, n_heads, dq, q_col, k_col, v_col, bsz, seq, n_ctx):
    tq = ATTN_TQ
    tpb = seq // tq
    ctx0 = bsz * seq // n_ctx
    dv = HEAD_DIM
    return pl.pallas_call(
        _dense_attn_kernel,
        grid=(bsz, n_heads, tpb),
        in_specs=[pl.BlockSpec((tq, dq), lambda b, h, i: (b * tpb + i, q_col(h))),
                  pl.BlockSpec((seq, dq), lambda b, h, i: (b, k_col(h))),
                  pl.BlockSpec((seq, dv), lambda b, h, i: (b, v_col(h))),
                  pl.BlockSpec((n_ctx, dq), lambda b, h, i: (ctx0 + b, k_col(h))),
                  pl.BlockSpec((n_ctx, dv), lambda b, h, i: (ctx0 + b, v_col(h)))],
        out_specs=pl.BlockSpec((tq, dv), lambda b, h, i: (b * tpb + i, h)),
        out_shape=jax.ShapeDtypeStruct((bsz * seq, n_heads * dv), BF16),
        compiler_params=_cparams(("parallel", "parallel", "arbitrary")),
        name="dense_attention",
    )(q, k, v, k, v)


def _ctx_attn_kernel(q_ref, kc_ref, vc_ref, o_ref):
    o_ref[...] = _online_attend(q_ref[...], [(kc_ref[...], vc_ref[...], None)]).astype(o_ref.dtype)


def _ctx_attn_call(q, k, v, *, n_heads, dq, q_col, k_col, v_col, bsz, seq, n_ctx):
    ctx0 = bsz * seq // n_ctx
    dv = HEAD_DIM
    return pl.pallas_call(
        _ctx_attn_kernel,
        grid=(bsz, n_heads),
        in_specs=[pl.BlockSpec((n_ctx, dq), lambda b, h: (ctx0 + b, q_col(h))),
                  pl.BlockSpec((n_ctx, dq), lambda b, h: (ctx0 + b, k_col(h))),
                  pl.BlockSpec((n_ctx, dv), lambda b, h: (ctx0 + b, v_col(h)))],
        out_specs=pl.BlockSpec((n_ctx, dv), lambda b, h: (b, h)),
        out_shape=jax.ShapeDtypeStruct((bsz * n_ctx, n_heads * dv), BF16),
        compiler_params=_cparams(("parallel", "parallel")),
        name="context_attention",
    )(q, k, v)


def _na_kernel(q_ref, k_ref, v_ref, kc_ref, vc_ref, bias_ref, o_ref, *, rows_total):
    i = pl.program_id(2)
    band = NA_BAND * GRID_W
    start_blk = jnp.clip(2 * i - 1, 0, (rows_total - NA_BAND) // 4)
    start = pl.multiple_of(start_blk * (4 * GRID_W), 4 * GRID_W)
    kb = k_ref[pl.ds(start, band), :]
    vb = v_ref[pl.ds(start, band), :]
    chunks = [(kb, vb, bias_ref[0, 0]), (kc_ref[...], vc_ref[...], None)]
    o_ref[...] = _online_attend(q_ref[...], chunks).astype(o_ref.dtype)


def _na_call(p, bias_tab, *, bsz, seq, n_ctx):
    tq = NA_ROWS * GRID_W
    tpb = seq // tq
    rows_total = seq // GRID_W
    ctx0 = bsz * seq // n_ctx
    d = HEAD_DIM
    qc, kc, vc = COL_CQ // d, COL_CK // d, COL_CV // d

    def variant(i):
        return jnp.where(i == 0, 0, jnp.where(i == tpb - 1, 2, 1))

    return pl.pallas_call(
        functools.partial(_na_kernel, rows_total=rows_total),
        grid=(bsz, NA_HEADS, tpb),
        in_specs=[pl.BlockSpec((tq, d), lambda b, h, i: (b * tpb + i, qc + h)),
                  pl.BlockSpec((seq, d), lambda b, h, i: (b, kc + h)),
                  pl.BlockSpec((seq, d), lambda b, h, i: (b, vc + h)),
                  pl.BlockSpec((n_ctx, d), lambda b, h, i: (ctx0 + b, kc + h)),
                  pl.BlockSpec((n_ctx, d), lambda b, h, i: (ctx0 + b, vc + h)),
                  pl.BlockSpec((1, 1, tq, NA_BAND * GRID_W), lambda b, h, i: (h, variant(i), 0, 0))],
        out_specs=pl.BlockSpec((tq, d), lambda b, h, i: (b * tpb + i, h)),
        out_shape=jax.ShapeDtypeStruct((bsz * seq, NA_HEADS * d), BF16),
        compiler_params=_cparams(("parallel", "parallel", "arbitrary")),
        name="neighbourhood_attention",
    )(p, p, p, p, p, bias_tab)


def _branch_kernel(oa_ref, ob_ref, oc_ref, oac_ref, obc_ref, occ_ref, ga_ref, gb_ref, gc_ref,
                   wa_ref, wb_ref, wc_ref, o_ref, *, n_lat_tiles):
    def gated(g_ref, w_ref, o_tile):
        return jax.nn.sigmoid(g_ref[...].astype(F32)) * _dot(o_tile, w_ref[...])

    @pl.when(pl.program_id(0) < n_lat_tiles)
    def _():
        acc = gated(ga_ref, wa_ref, oa_ref[...]) + gated(gb_ref, wb_ref, ob_ref[...]) + gated(gc_ref, wc_ref, oc_ref[...])
        o_ref[...] = acc.astype(BF16)

    @pl.when(pl.program_id(0) >= n_lat_tiles)
    def _():
        acc = gated(ga_ref, wa_ref, oac_ref[...]) + gated(gb_ref, wb_ref, obc_ref[...]) + gated(gc_ref, wc_ref, occ_ref[...])
        o_ref[...] = acc.astype(BF16)


def _branch_call(o_lat, o_ctx, p, wa, wb, wc, n_rows):
    d = wa.shape[1]
    tm = MM_TM
    g0 = COL_GATE // d
    n_lat_tiles = o_lat[0].shape[0] // tm
    lat_specs = [pl.BlockSpec((tm, o.shape[1]), lambda i: (jnp.minimum(i, n_lat_tiles - 1), 0)) for o in o_lat]
    ctx_specs = [pl.BlockSpec((tm, o.shape[1]), lambda i: (jnp.maximum(i - n_lat_tiles, 0), 0)) for o in o_ctx]
    return pl.pallas_call(
        functools.partial(_branch_kernel, n_lat_tiles=n_lat_tiles),
        grid=(n_rows // tm,),
        in_specs=lat_specs + ctx_specs + [
            pl.BlockSpec((tm, d), lambda i: (i, g0)),
            pl.BlockSpec((tm, d), lambda i: (i, g0 + 1)),
            pl.BlockSpec((tm, d), lambda i: (i, g0 + 2)),
            _resident(wa.shape), _resident(wb.shape), _resident(wc.shape)],
        out_specs=pl.BlockSpec((tm, d), lambda i: (i, 0)),
        out_shape=jax.ShapeDtypeStruct((n_rows, d), BF16),
        compiler_params=_cparams(("parallel",)),
        name="branch_merge",
    )(*o_lat, *o_ctx, p, p, p, wa, wb, wc)


def _layer_norm(z, g, b):
    mu = jnp.mean(z, axis=1, keepdims=True)
    zc = z - mu
    var = jnp.mean(zc * zc, axis=1, keepdims=True)
    return zc * lax.rsqrt(var + LN_EPS) * g + b


def _route(logits):
    lane = lax.broadcasted_iota(jnp.int32, logits.shape, 1)
    lane_f = lane.astype(F32)
    is_g = lane < N_GROUPS
    gl = jnp.where(is_g, logits, NEG_BIG)
    mg = jnp.max(gl, axis=1, keepdims=True)
    gsel = jnp.min(jnp.where(gl == mg, lane_f, float(LANES)), axis=1, keepdims=True)
    zg = jnp.sum(jnp.where(is_g, jnp.exp(gl - mg), 0.0), axis=1, keepdims=True)
    lo = N_GROUPS + gsel * EXPERTS_PER_GROUP
    is_e = (lane_f >= lo) & (lane_f < lo + EXPERTS_PER_GROUP)
    el = jnp.where(is_e, logits, NEG_BIG)
    t1 = jnp.max(el, axis=1, keepdims=True)
    i1 = jnp.min(jnp.where(el == t1, lane_f, float(LANES)), axis=1, keepdims=True)
    el2 = jnp.where(lane_f == i1, NEG_BIG, el)
    t2 = jnp.max(el2, axis=1, keepdims=True)
    i2 = jnp.min(jnp.where(el2 == t2, lane_f, float(LANES)), axis=1, keepdims=True)
    dd = jnp.exp(t2 - t1)
    g1 = 1.0 / (zg * (1.0 + dd))
    g2 = g1 * dd
    slab = jnp.where(lane == 0, i1 - N_GROUPS,
                     jnp.where(lane == 1, i2 - N_GROUPS,
                               jnp.where(lane == 2, g1, jnp.where(lane == 3, g2, 0.0))))
    return slab


def _out_kernel(m_ref, x_ref, mod_ref, w_ref, lg_ref, lb_ref, wrh_ref, wrl_ref, br_ref,
                xo_ref, u_ref, r_ref, *, alpha):
    y = _dot(m_ref[...], w_ref[...])
    gate_m = mod_ref[0, 2:3, :]
    shift_f = mod_ref[0, 3:4, :]
    scale_f = mod_ref[0, 4:5, :]
    xn = _layer_norm(alpha * x_ref[...] + gate_m * y, lg_ref[...], lb_ref[...])
    xo_ref[...] = xn
    u = xn * (1.0 + scale_f) + shift_f
    u_hi, u_lo = _split_bf16(u)
    u_ref[...] = u_hi
    logits = _dot(u_hi, wrh_ref[...]) + _dot(u_lo, wrh_ref[...]) + _dot(u_hi, wrl_ref[...]) + br_ref[...]
    r_ref[...] = _route(logits)


def _out_call(mrg, xs, mod, w_out, ln_g, ln_b, wr_hi, wr_lo, br, mod_row, n_rows, alpha):
    d = xs.shape[1]
    tm = MM_TM
    return pl.pallas_call(
        functools.partial(_out_kernel, alpha=alpha),
        grid=(n_rows // tm,),
        in_specs=[pl.BlockSpec((tm, d), lambda i: (i, 0)),
                  pl.BlockSpec((tm, d), lambda i: (i, 0)),
                  pl.BlockSpec((1, 6, d), lambda i: (mod_row(i, tm), 0, 0)),
                  _resident((d, d)), _resident((1, d)), _resident((1, d)),
                  _resident((d, LANES)), _resident((d, LANES)), _resident((1, LANES))],
        out_specs=[pl.BlockSpec((tm, d), lambda i: (i, 0)),
                   pl.BlockSpec((tm, d), lambda i: (i, 0)),
                   pl.BlockSpec((tm, LANES), lambda i: (i, 0))],
        out_shape=[jax.ShapeDtypeStruct((n_rows, d), F32),
                   jax.ShapeDtypeStruct((n_rows, d), BF16),
                   jax.ShapeDtypeStruct((n_rows, LANES), F32)],
        compiler_params=_cparams(("parallel",)),
        name="out_proj_ln_route",
    )(mrg, xs, mod, w_out, ln_g, ln_b, wr_hi, wr_lo, br)


def _rank_kernel(r_ref, rank_ref, cnt_ref, carry):
    @pl.when(pl.program_id(0) == 0)
    def _():
        carry[...] = jnp.zeros_like(carry)

    slab = r_ref[...]
    tm = slab.shape[0]
    lane = lax.broadcasted_iota(jnp.int32, slab.shape, 1)
    lane_f = lane.astype(F32)
    e1 = slab[:, 0:1]
    e2 = slab[:, 1:2]
    hit1 = lane_f == e1
    hit2 = lane_f == e2
    onehot = jnp.where(hit1 | hit2, 1.0, 0.0)
    row = lax.broadcasted_iota(jnp.int32, (tm, tm), 0)
    col = lax.broadcasted_iota(jnp.int32, (tm, tm), 1)
    lower = jnp.where(col < row, 1.0, 0.0).astype(BF16)
    before = _dot(lower, onehot.astype(BF16)) + carry[0:1, :]
    r1 = jnp.sum(jnp.where(hit1, before, 0.0), axis=1, keepdims=True)
    r2 = jnp.sum(jnp.where(hit2, before, 0.0), axis=1, keepdims=True)
    rank_ref[...] = jnp.where(lane == 0, r1, jnp.where(lane == 1, r2, 0.0))
    carry[...] = carry[...] + jnp.sum(onehot, axis=0, keepdims=True)
    cnt_ref[...] = carry[...]


def _rank_call(route):
    n = route.shape[0]
    tm = ROW_TM
    return pl.pallas_call(
        _rank_kernel,
        grid=(n // tm,),
        in_specs=[pl.BlockSpec((tm, LANES), lambda i: (i, 0))],
        out_specs=[pl.BlockSpec((tm, LANES), lambda i: (i, 0)),
                   pl.BlockSpec((8, LANES), lambda i: (0, 0))],
        out_shape=[jax.ShapeDtypeStruct((n, LANES), F32), jax.ShapeDtypeStruct((8, LANES), F32)],
        scratch_shapes=[pltpu.VMEM((8, LANES), F32)],
        compiler_params=_cparams(("arbitrary",)),
        name="expert_ranks",
    )(route)


def _moe_kernel(be_ref, nu_ref, x_ref, wg_ref, wu_ref, wd_ref, y_ref):
    j = pl.program_id(0)

    @pl.when(j < nu_ref[0])
    def _():
        xb = x_ref[...]
        hg = _dot(xb, wg_ref[0, 0].astype(BF16))
        hu = _dot(xb, wu_ref[0, 0].astype(BF16))
        hb = (hg * jax.nn.sigmoid(hg) * hu).astype(BF16)
        y_ref[...] = _dot(hb, wd_ref[0, 0].astype(BF16)).astype(y_ref.dtype)

    @pl.when(j >= nu_ref[0])
    def _():
        y_ref[...] = jnp.zeros_like(y_ref)


def _moe_call(block_e, n_used, u_sorted, w_gate, w_up, w_down, layer):
    n_slots, d = u_sorted.shape
    bm = MOE_BM
    nb = n_slots // bm
    de = w_gate.shape[-1]

    def xmap(j, be, nu):
        return (jnp.minimum(j, nu[0] - 1), 0)

    grid_spec = pltpu.PrefetchScalarGridSpec(
        num_scalar_prefetch=2,
        grid=(nb,),
        in_specs=[pl.BlockSpec((bm, d), xmap),
                  pl.BlockSpec((1, 1, d, de), lambda j, be, nu: (layer, be[j], 0, 0)),
                  pl.BlockSpec((1, 1, d, de), lambda j, be, nu: (layer, be[j], 0, 0)),
                  pl.BlockSpec((1, 1, de, d), lambda j, be, nu: (layer, be[j], 0, 0))],
        out_specs=pl.BlockSpec((bm, d), lambda j, be, nu: (j, 0)),
    )
    return pl.pallas_call(
        _moe_kernel,
        grid_spec=grid_spec,
        out_shape=jax.ShapeDtypeStruct((n_slots, d), BF16),
        compiler_params=_cparams(("arbitrary",)),
        name="expert_mlp",
    )(block_e, n_used, u_sorted, w_gate, w_up, w_down)


def _combine_kernel(x_ref, y0_ref, y1_ref, r_ref, mod_ref, lg_ref, lb_ref, o_ref, *, alpha):
    slab = r_ref[...]
    g1 = slab[:, 2:3]
    g2 = slab[:, 3:4]
    mx = g1 * y0_ref[...].astype(F32) + g2 * y1_ref[...].astype(F32)
    gate_f = mod_ref[0, 5:6, :]
    o_ref[...] = _layer_norm(alpha * x_ref[...] + gate_f * mx, lg_ref[...], lb_ref[...])


def _combine_call(xs, y0, y1, route, mod, ln_g, ln_b, mod_row, alpha):
    n, d = xs.shape
    tm = ROW_TM
    return pl.pallas_call(
        functools.partial(_combine_kernel, alpha=alpha),
        grid=(n // tm,),
        in_specs=[pl.BlockSpec((tm, d), lambda i: (i, 0)),
                  pl.BlockSpec((tm, d), lambda i: (i, 0)),
                  pl.BlockSpec((tm, d), lambda i: (i, 0)),
                  pl.BlockSpec((tm, LANES), lambda i: (i, 0)),
                  pl.BlockSpec((1, 6, d), lambda i: (mod_row(i, tm), 0, 0)),
                  pl.BlockSpec((1, d), lambda i: (0, 0)),
                  pl.BlockSpec((1, d), lambda i: (0, 0))],
        out_specs=pl.BlockSpec((tm, d), lambda i: (i, 0)),
        out_shape=jax.ShapeDtypeStruct((n, d), F32),
        compiler_params=_cparams(("parallel",)),
        name="moe_combine_ln",
    )(xs, y0, y1, route, mod, ln_g, ln_b)


def _rope_angles(seq, dim):
    tpos = np.arange(seq)
    row = (tpos // GRID_W).astype(np.float32)
    col = (tpos % GRID_W).astype(np.float32)
    quarter = dim // 4
    inv_freq = jnp.asarray(ROPE_THETA, F32) ** (-jnp.arange(quarter, dtype=F32) / quarter)
    ang_r = jnp.asarray(row)[:, None] * inv_freq
    ang_c = jnp.asarray(col)[:, None] * inv_freq
    return jnp.concatenate([ang_r, ang_r, ang_c, ang_c], axis=-1)


def _rot_sign(dim):
    l = np.arange(dim)
    return np.where((l & (dim // 4)) == 0, -1.0, 1.0).astype(np.float32), l ^ (dim // 4)


def _rope_tables(seq, pad_rows):
    ang_a = _rope_angles(seq, HEAD_DIM)
    sign_a, _ = _rot_sign(HEAD_DIM)
    cos_a = jnp.concatenate([jnp.cos(ang_a), jnp.ones((pad_rows, HEAD_DIM), F32)], 0)
    sin_a = jnp.concatenate([jnp.sin(ang_a) * sign_a, jnp.zeros((pad_rows, HEAD_DIM), F32)], 0)
    ang_b = _rope_angles(seq, MLA_ROPE)
    cos_b = jnp.concatenate([jnp.cos(ang_b), jnp.ones((pad_rows, MLA_ROPE), F32)], 0)
    sin_b = jnp.concatenate([jnp.sin(ang_b), jnp.zeros((pad_rows, MLA_ROPE), F32)], 0)
    return cos_a, sin_a, jnp.tile(cos_b, (1, 2)), jnp.tile(sin_b, (1, 2))


def _permute_w_in(w):
    d = w.shape[0]
    o = np.cumsum((0, 768, 256, 256, 512, 256, 64, 640, 640, 640, 6144))
    aq, ak, av, bq, bkv, bkr, cq, ck, cv, gate = (w[:, o[i]:o[i + 1]] for i in range(10))
    sign, perm = _rot_sign(MLA_ROPE)
    bkr_rot = bkr[:, perm] * sign
    cq = cq * (1.0 / math.sqrt(HEAD_DIM))
    out = jnp.concatenate([aq, ak, bq, bkv, bkr, bkr_rot, av, cq, ck, cv, gate], axis=1)
    assert out.shape == (d, PROJ_W)
    return out.astype(BF16)


def _permute_w_uq(w):
    r = w.shape[0]
    w3 = w.reshape(r, MLA_HEADS, MLA_QK)
    nope = w3[:, :, :MLA_NOPE].reshape(r, MLA_HEADS * MLA_NOPE)
    rope = w3[:, :, MLA_NOPE:]
    sign, perm = _rot_sign(MLA_ROPE)
    rot = rope[:, :, perm] * sign
    zpad = jnp.zeros((r, MLA_HEADS, LANES - MLA_ROPE), w.dtype)
    rope_p = jnp.concatenate([rope, zpad], -1).reshape(r, MLA_HEADS * LANES)
    rot_p = jnp.concatenate([rot, zpad], -1).reshape(r, MLA_HEADS * LANES)
    return jnp.concatenate([nope, rope_p, rot_p], axis=1).astype(BF16)


def _permute_w_ukv(w):
    r = w.shape[0]
    w3 = w.reshape(r, MLA_HEADS, MLA_NOPE + MLA_V)
    kn = w3[:, :, :MLA_NOPE].reshape(r, MLA_HEADS * MLA_NOPE)
    vv = w3[:, :, MLA_NOPE:].reshape(r, MLA_HEADS * MLA_V)
    return jnp.concatenate([kn, vv], axis=1).astype(BF16)


def _na_bias_tables(rpb, seq):
    w, kh, kw = GRID_W, NA_KH, NA_KW
    rows = seq // w
    nh, n_dr, n_dc = rpb.shape
    line = jnp.full((nh, n_dr, 2 * w), NEG_BIG, F32).at[:, :, w - kw:w - kw + n_dc].set(rpb.astype(F32))
    skew = jnp.broadcast_to(line[:, :, None, :], (nh, n_dr, w, 2 * w)).reshape(nh, n_dr, 2 * w * w)
    skew = skew[:, :, :w * (2 * w - 1)].reshape(nh, n_dr, w, 2 * w - 1)
    tc = skew[:, :, :, w - 1:2 * w - 1]
    cq = np.arange(w)[:, None]
    ck = np.arange(w)[None, :]
    cs = np.clip(cq - kw // 2, 0, w - kw)
    col_ok = (ck >= cs) & (ck < cs + kw)
    tc = jnp.where(col_ok[None, None], tc, NEG_BIG)
    tc = jnp.concatenate([tc, jnp.full((nh, 1, w, w), NEG_BIG, F32)], axis=1)
    tc2 = jnp.concatenate([tc, tc], axis=-1)
    blks = []
    for r0 in (0, NA_ROWS, rows - NA_ROWS):
        rb = int(np.clip(r0 - kh // 2, 0, rows - NA_BAND))
        rq = r0 + np.arange(NA_ROWS)[:, None]
        rk = rb + np.arange(NA_BAND)[None, :]
        rs = np.clip(rq - kh // 2, 0, rows - kh)
        row_ok = (rk >= rs) & (rk < rs + kh)
        blks.append(np.where(row_ok, rk - rq + kh - 1, n_dr).tolist())
    return pl.pallas_call(
        functools.partial(_na_bias_kernel, blks=blks),
        grid=(nh,),
        in_specs=[pl.BlockSpec((1, n_dr + 1, w, 2 * w), lambda h: (h, 0, 0, 0))],
        out_specs=pl.BlockSpec((1, len(blks), NA_ROWS * w, NA_BAND * w), lambda h: (h, 0, 0, 0)),
        out_shape=jax.ShapeDtypeStruct((nh, len(blks), NA_ROWS * w, NA_BAND * w), F32),
        compiler_params=_cparams(("parallel",)),
        name="na_bias_table",
    )(tc2)


def _na_bias_kernel(tc_ref, o_ref, *, blks):
    w = GRID_W
    low = lax.broadcasted_iota(jnp.int32, (w, 2 * w), 1) < w
    for v, blk in enumerate(blks):
        for rq, row in enumerate(blk):
            for pr in range(len(row) // 2):
                pair = jnp.where(low, tc_ref[0, row[2 * pr]], tc_ref[0, row[2 * pr + 1]])
                o_ref[0, v, rq * w:(rq + 1) * w, pr * 2 * w:(pr + 1) * 2 * w] = pair


def kernel(x, c, ctx, c_ctx, w_ada, b_ada, w_in, gqa_q_norm, gqa_k_norm, mla_q_norm, mla_kv_norm, mla_w_uq, mla_w_ukv, na_rpb, w_branch_a, w_branch_b, w_branch_c, w_out, ln1_g, ln1_b, w_router_group, b_router_group, w_router_expert, b_router_expert, w_expert_gate, w_expert_up, w_expert_down, ln2_g, ln2_b):
    bsz, seq, d = x.shape
    n_ctx = ctx.shape[1]
    depth = w_ada.shape[0]
    nx, nc = bsz * seq, bsz * n_ctx
    t = nx + nc
    assert seq % PROJ_TM == 0 and nc == PROJ_TM and seq % (NA_BAND * GRID_W) == 0 and nx % n_ctx == 0
    alpha = (2 * depth) ** 0.25

    def mod_row(i, tm):
        return jnp.minimum(i // (seq // tm), bsz)

    def rope_blk(i, tm):
        return jnp.where(i < nx // tm, i % (seq // tm), seq // tm)

    xs = jnp.concatenate([x.reshape(nx, d), ctx.reshape(nc, d)], axis=0)
    cc = jnp.concatenate([c, c_ctx[None], jnp.zeros((8 - bsz - 1, d), F32)], axis=0)
    mod_all = _ada_call(cc, w_ada, b_ada).reshape(depth, 8, 6, d)
    cos_a, sin_a, cos_b, sin_b = _rope_tables(seq, ROW_TM)

    attn_kw = dict(bsz=bsz, seq=seq, n_ctx=n_ctx)
    for i in range(depth):
        last = i == depth - 1
        n_rows = nx if last else t
        mod = mod_all[i]
        p = _proj_call(xs, mod, _permute_w_in(w_in[i]), mod_row)

        qa, ka = _prep_a_call(p, cos_a, sin_a, gqa_q_norm[i][None], gqa_k_norm[i][None], rope_blk)
        a_kw = dict(n_heads=GQA_HEADS, dq=HEAD_DIM, q_col=lambda h: h, k_col=lambda h: h // GQA_GROUP,
                    v_col=lambda h: COL_AV // HEAD_DIM + h // GQA_GROUP, **attn_kw)
        o_a = _dense_attn_call(qa, ka, p, **a_kw)
        qb, kb, vb = _prep_b_call(p, cos_b, sin_b, mla_q_norm[i][None], mla_kv_norm[i][None],
                                  _permute_w_uq(mla_w_uq[i]), _permute_w_ukv(mla_w_ukv[i]), rope_blk)
        b_kw = dict(n_heads=MLA_HEADS, dq=MLA_PAD, q_col=lambda h: h, k_col=lambda h: h, v_col=lambda h: h, **attn_kw)
        o_b = _dense_attn_call(qb, kb, vb, **b_kw)
        o_c = _na_call(p, _na_bias_tables(na_rpb[i], seq), **attn_kw)
        o_lat = (o_a, o_b, o_c)
        if last:
            o_ctx = o_lat
        else:
            c_kw = dict(n_heads=NA_HEADS, dq=HEAD_DIM, q_col=lambda h: COL_CQ // HEAD_DIM + h,
                        k_col=lambda h: COL_CK // HEAD_DIM + h, v_col=lambda h: COL_CV // HEAD_DIM + h, **attn_kw)
            o_ctx = (_ctx_attn_call(qa, ka, p, **a_kw), _ctx_attn_call(qb, kb, vb, **b_kw),
                     _ctx_attn_call(p, p, p, **c_kw))

        mrg = _branch_call(o_lat, o_ctx, p, w_branch_a[i].astype(BF16), w_branch_b[i].astype(BF16),
                           w_branch_c[i].astype(BF16), n_rows)
        wr = jnp.concatenate([w_router_group[i], w_router_expert[i],
                              jnp.zeros((d, LANES - N_GROUPS - N_EXPERTS), F32)], axis=1)
        br = jnp.concatenate([b_router_group[i], b_router_expert[i],
                              jnp.zeros((LANES - N_GROUPS - N_EXPERTS,), F32)])[None]
        wr_hi, wr_lo = _split_bf16(wr)
        xs, u_f, route = _out_call(mrg, xs, mod, w_out[i].astype(BF16), ln1_g[i][None], ln1_b[i][None],
                                   wr_hi, wr_lo, br, mod_row, n_rows, alpha)

        rank_slab, cnt = _rank_call(route)
        eid = route[:, :TOP_K].astype(jnp.int32)
        rank = rank_slab[:, :TOP_K].astype(jnp.int32)
        counts = cnt[0, :N_EXPERTS].astype(jnp.int32)
        padded = ((counts + MOE_BM - 1) // MOE_BM) * MOE_BM
        pends = jnp.cumsum(padded)
        pstarts = pends - padded
        dest = pstarts[eid] + rank
        nb = -(-(n_rows * TOP_K) // MOE_BM) + N_EXPERTS
        n_used = (pends[-1] // MOE_BM).astype(jnp.int32)
        blk = jnp.minimum(jnp.arange(nb, dtype=jnp.int32), n_used - 1)
        block_e = jnp.clip(jnp.searchsorted(pends, blk * MOE_BM, side="right"), 0, N_EXPERTS - 1).astype(jnp.int32)
        order = jnp.argsort(eid.reshape(-1), stable=True).astype(jnp.int32)
        starts = jnp.cumsum(counts) - counts
        slot = jnp.arange(nb * MOE_BM, dtype=jnp.int32)
        slot_e = jnp.repeat(block_e, MOE_BM)
        off = slot - pstarts[slot_e]
        pos = jnp.clip(starts[slot_e] + off, 0, n_rows * TOP_K - 1)
        slot_tok = jnp.where(off < counts[slot_e], order[pos] // TOP_K, slot % n_rows)
        u_sorted = jnp.take(u_f, slot_tok, axis=0)
        y_sorted = _moe_call(block_e, n_used[None], u_sorted, w_expert_gate, w_expert_up, w_expert_down, i)
        y0 = jnp.take(y_sorted, dest[:, 0], axis=0)
        y1 = jnp.take(y_sorted, dest[:, 1], axis=0)
        xs = _combine_call(xs, y0, y1, route, mod, ln2_g[i][None], ln2_b[i][None], mod_row, alpha)
    return xs.reshape(bsz, seq, d)
```

```python
import functools
import math

import jax
import jax.numpy as jnp
import numpy as np
from jax import lax
from jax.experimental import pallas as pl
from jax.experimental.pallas import tpu as pltpu

F32 = jnp.float32
BF16 = jnp.bfloat16

HEAD_DIM = 128
GRID_W = 64
ROPE_THETA = 10000.0
GQA_HEADS, GQA_KV_HEADS = 6, 2
GQA_GROUP = GQA_HEADS // GQA_KV_HEADS
MLA_HEADS, MLA_Q_RANK, MLA_KV_RANK = 5, 512, 256
MLA_NOPE, MLA_ROPE, MLA_V = 128, 64, 128
MLA_QK = MLA_NOPE + MLA_ROPE
MLA_PAD = 256
NA_HEADS, NA_KH, NA_KW = 5, 8, 16
N_GROUPS, EXPERTS_PER_GROUP, TOP_K, D_EXPERT = 8, 8, 2, 512
N_EXPERTS = N_GROUPS * EXPERTS_PER_GROUP
LN_EPS = 1e-6
RMS_EPS = 1e-6
NEG_BIG = -1e30

LANES = 128
VMEM_LIMIT = 56 * 1024 * 1024
PROJ_TM, PROJ_TN = 1024, 1024
ROW_TM = 512
MM_TM = 512
ATTN_TQ = 512
ATTN_KC = 1024
NA_ROWS = 8
NA_BAND = 16
MOE_BM = 256
ADA_TN = 512

COL_AQ, COL_AK = 0, 768
COL_BQ, COL_BKV, COL_BKR = 1024, 1536, 1792
COL_AV, COL_CQ, COL_CK, COL_CV = 1920, 2176, 2816, 3456
COL_GATE = 4096
PROJ_W = 10240


def _cparams(sem):
    return pltpu.CompilerParams(dimension_semantics=sem, vmem_limit_bytes=VMEM_LIMIT)


def _resident(shape):
    return pl.BlockSpec(shape, lambda *_: (0,) * len(shape), pipeline_mode=pl.Buffered(1))


def _dot(a, b):
    return jnp.dot(a, b, preferred_element_type=F32)


def _dot_nt(a, b):
    return lax.dot_general(a, b, (((1,), (1,)), ((), ())), preferred_element_type=F32)


def _split_bf16(a):
    hi = a.astype(BF16)
    lo = (a - hi.astype(F32)).astype(BF16)
    return hi, lo


def _ada_kernel(c_ref, w_ref, b_ref, o_ref):
    a = c_ref[...]
    a = a * jax.nn.sigmoid(a)
    a_hi, a_lo = _split_bf16(a)
    w_hi, w_lo = _split_bf16(w_ref[0])
    acc = _dot(a_hi, w_hi) + _dot(a_lo, w_hi) + _dot(a_hi, w_lo)
    o_ref[0] = acc + b_ref[0]


def _ada_call(cc, w_ada, b_ada):
    depth, d, n = w_ada.shape
    rows = cc.shape[0]
    return pl.pallas_call(
        _ada_kernel,
        grid=(depth, n // ADA_TN),
        in_specs=[pl.BlockSpec((rows, d), lambda l, j: (0, 0)),
                  pl.BlockSpec((1, d, ADA_TN), lambda l, j: (l, 0, j)),
                  pl.BlockSpec((1, 1, ADA_TN), lambda l, j: (l, 0, j))],
        out_specs=pl.BlockSpec((1, rows, ADA_TN), lambda l, j: (l, 0, j)),
        out_shape=jax.ShapeDtypeStruct((depth, rows, n), F32),
        compiler_params=_cparams(("parallel", "parallel")),
        name="ada_modulation",
    )(cc, w_ada, b_ada.reshape(depth, 1, n))


def _proj_kernel(x_ref, mod_ref, w_ref, o_ref, u_scr):
    @pl.when(pl.program_id(1) == 0)
    def _():
        shift = mod_ref[0, 0:1, :]
        scale = mod_ref[0, 1:2, :]
        u_scr[...] = (x_ref[...] * (1.0 + scale) + shift).astype(BF16)

    o_ref[...] = _dot(u_scr[...], w_ref[0]).astype(BF16)


def _proj_call(xs, mod, w_in_p, layer, mod_row):
    t, d = xs.shape
    return pl.pallas_call(
        _proj_kernel,
        grid=(t // PROJ_TM, PROJ_W // PROJ_TN),
        in_specs=[pl.BlockSpec((PROJ_TM, d), lambda i, j: (i, 0)),
                  pl.BlockSpec((1, 6, d), lambda i, j: (mod_row(i, PROJ_TM), 0, 0)),
                  pl.BlockSpec((1, d, PROJ_TN), lambda i, j: (layer, 0, j))],
        out_specs=pl.BlockSpec((PROJ_TM, PROJ_TN), lambda i, j: (i, j)),
        out_shape=jax.ShapeDtypeStruct((t, PROJ_W), BF16),
        scratch_shapes=[pltpu.VMEM((PROJ_TM, d), BF16)],
        compiler_params=_cparams(("parallel", "arbitrary")),
        name="input_projection",
    )(xs, mod, w_in_p)


def _rope128(y, cos, sin_signed, first_quarter):
    rot = jnp.where(first_quarter, pltpu.roll(y, 96, 1), pltpu.roll(y, 32, 1))
    return y * cos + rot * sin_signed


def _prep_a_kernel(p_ref, cos_ref, sin_ref, gq_ref, gk_ref, q_ref, k_ref, *, scale):
    cos = cos_ref[...]
    sin_signed = sin_ref[...]
    lane = lax.broadcasted_iota(jnp.int32, cos.shape, 1)
    first_quarter = (lane & 32) == 0
    for h in range(GQA_HEADS + GQA_KV_HEADS):
        xh = p_ref[:, h * HEAD_DIM:(h + 1) * HEAD_DIM].astype(F32)
        r = lax.rsqrt(jnp.mean(xh * xh, axis=1, keepdims=True) + RMS_EPS)
        if h < GQA_HEADS:
            y = _rope128(xh * r * gq_ref[...], cos, sin_signed, first_quarter) * scale
            q_ref[:, h * HEAD_DIM:(h + 1) * HEAD_DIM] = y.astype(BF16)
        else:
            hk = h - GQA_HEADS
            y = _rope128(xh * r * gk_ref[...], cos, sin_signed, first_quarter)
            k_ref[:, hk * HEAD_DIM:(hk + 1) * HEAD_DIM] = y.astype(BF16)


def _prep_a_call(p, cos_t, sin_t, gq, gk, rope_blk):
    t = p.shape[0]
    tm = ROW_TM
    wq, wk = GQA_HEADS * HEAD_DIM, GQA_KV_HEADS * HEAD_DIM
    return pl.pallas_call(
        functools.partial(_prep_a_kernel, scale=1.0 / math.sqrt(HEAD_DIM)),
        grid=(t // tm,),
        in_specs=[pl.BlockSpec((tm, wq + wk), lambda i: (i, 0)),
                  pl.BlockSpec((tm, HEAD_DIM), lambda i: (rope_blk(i, tm), 0)),
                  pl.BlockSpec((tm, HEAD_DIM), lambda i: (rope_blk(i, tm), 0)),
                  pl.BlockSpec((1, HEAD_DIM), lambda i: (0, 0)),
                  pl.BlockSpec((1, HEAD_DIM), lambda i: (0, 0))],
        out_specs=[pl.BlockSpec((tm, wq), lambda i: (i, 0)),
                   pl.BlockSpec((tm, wk), lambda i: (i, 0))],
        out_shape=[jax.ShapeDtypeStruct((t, wq), BF16), jax.ShapeDtypeStruct((t, wk), BF16)],
        compiler_params=_cparams(("parallel",)),
        name="gqa_qk_prep",
    )(p, cos_t, sin_t, gq, gk)


def _prep_b_kernel(ql_ref, kvl_ref, kr_ref, cos_ref, sin_ref, gq_ref, gkv_ref, wq_ref, wkv_ref,
                   q_ref, k_ref, v_ref, *, scale):
    nh, hp = MLA_HEADS, MLA_PAD
    wn = nh * MLA_NOPE
    cos = cos_ref[...]
    sin = sin_ref[...]
    ql = ql_ref[...].astype(F32)
    ql = ql * lax.rsqrt(jnp.mean(ql * ql, axis=1, keepdims=True) + RMS_EPS) * gq_ref[...]
    qf = _dot(ql.astype(BF16), wq_ref[...])
    kvl = kvl_ref[...].astype(F32)
    kvl = kvl * lax.rsqrt(jnp.mean(kvl * kvl, axis=1, keepdims=True) + RMS_EPS) * gkv_ref[...]
    kvf = _dot(kvl.astype(BF16), wkv_ref[...])
    lane = lax.broadcasted_iota(jnp.int32, cos.shape, 1)
    low = lane < MLA_ROPE
    tk = kr_ref[...].astype(F32) * jnp.where(low, cos, sin)
    kr = jnp.where(low, tk + pltpu.roll(tk, MLA_ROPE, 1), 0.0).astype(BF16)
    for h in range(nh):
        q_ref[:, h * hp:h * hp + LANES] = (qf[:, h * LANES:(h + 1) * LANES] * scale).astype(BF16)
        qr = qf[:, wn + h * LANES:wn + (h + 1) * LANES] * cos + qf[:, 2 * wn + h * LANES:2 * wn + (h + 1) * LANES] * sin
        q_ref[:, h * hp + LANES:(h + 1) * hp] = (qr * scale).astype(BF16)
        k_ref[:, h * hp:h * hp + LANES] = kvf[:, h * LANES:(h + 1) * LANES].astype(BF16)
        k_ref[:, h * hp + LANES:(h + 1) * hp] = kr
    v_ref[...] = kvf[:, wn:].astype(BF16)


def _prep_b_call(p, cos_t, sin_t, gq, gkv, wq, wkv, rope_blk):
    t = p.shape[0]
    tm = ROW_TM
    nh = MLA_HEADS
    return pl.pallas_call(
        functools.partial(_prep_b_kernel, scale=1.0 / math.sqrt(MLA_QK)),
        grid=(t // tm,),
        in_specs=[pl.BlockSpec((tm, MLA_Q_RANK), lambda i: (i, COL_BQ // MLA_Q_RANK)),
                  pl.BlockSpec((tm, MLA_KV_RANK), lambda i: (i, COL_BKV // MLA_KV_RANK)),
                  pl.BlockSpec((tm, LANES), lambda i: (i, COL_BKR // LANES)),
                  pl.BlockSpec((tm, LANES), lambda i: (rope_blk(i, tm), 0)),
                  pl.BlockSpec((tm, LANES), lambda i: (rope_blk(i, tm), 0)),
                  pl.BlockSpec((1, MLA_Q_RANK), lambda i: (0, 0)),
                  pl.BlockSpec((1, MLA_KV_RANK), lambda i: (0, 0)),
                  _resident(wq.shape), _resident(wkv.shape)],
        out_specs=[pl.BlockSpec((tm, nh * MLA_PAD), lambda i: (i, 0)),
                   pl.BlockSpec((tm, nh * MLA_PAD), lambda i: (i, 0)),
                   pl.BlockSpec((tm, nh * MLA_V), lambda i: (i, 0))],
        out_shape=[jax.ShapeDtypeStruct((t, nh * MLA_PAD), BF16),
                   jax.ShapeDtypeStruct((t, nh * MLA_PAD), BF16),
                   jax.ShapeDtypeStruct((t, nh * MLA_V), BF16)],
        compiler_params=_cparams(("parallel",)),
        name="mla_prep",
    )(p, p, p, cos_t, sin_t, gq, gkv, wq, wkv)


def _online_attend(q, chunks):
    m = z = acc = None
    for k, v, bias in chunks:
        s = _dot_nt(q, k)
        if bias is not None:
            s = s + bias
        cm = jnp.max(s, axis=1, keepdims=True)
        if m is None:
            m = cm
            p = jnp.exp(s - m)
            z = jnp.sum(p, axis=1, keepdims=True)
            acc = _dot(p.astype(BF16), v)
        else:
            m_new = jnp.maximum(m, cm)
            corr = jnp.exp(m - m_new)
            p = jnp.exp(s - m_new)
            z = z * corr + jnp.sum(p, axis=1, keepdims=True)
            acc = acc * corr + _dot(p.astype(BF16), v)
            m = m_new
    return acc / z


def _dense_attn_kernel(q_ref, kl_ref, vl_ref, kc_ref, vc_ref, o_ref):
    n_lat = kl_ref.shape[0]
    chunks = [(kl_ref[c:c + ATTN_KC, :], vl_ref[c:c + ATTN_KC, :], None) for c in range(0, n_lat, ATTN_KC)]
    chunks.append((kc_ref[...], vc_ref[...], None))
    o_ref[...] = _online_attend(q_ref[...], chunks).astype(o_ref.dtype)


def _dense_attn_call(q, k, v, *, n_heads, dq, q_col, k_col, v_col, bsz, seq, n_ctx):
    tq = ATTN_TQ
    tpb = seq // tq
    ctx0 = bsz * seq // n_ctx
    dv = HEAD_DIM
    return pl.pallas_call(
        _dense_attn_kernel,
        grid=(bsz, n_heads, tpb),
        in_specs=[pl.BlockSpec((tq, dq), lambda b, h, i: (b * tpb + i, q_col(h))),
                  pl.BlockSpec((seq, dq), lambda b, h, i: (b, k_col(h))),
                  pl.BlockSpec((seq, dv), lambda b, h, i: (b, v_col(h))),
                  pl.BlockSpec((n_ctx, dq), lambda b, h, i: (ctx0 + b, k_col(h))),
                  pl.BlockSpec((n_ctx, dv), lambda b, h, i: (ctx0 + b, v_col(h)))],
        out_specs=pl.BlockSpec((tq, dv), lambda b, h, i: (b * tpb + i, h)),
        out_shape=jax.ShapeDtypeStruct((bsz * seq, n_heads * dv), BF16),
        compiler_params=_cparams(("parallel", "parallel", "arbitrary")),
        name="dense_attention",
    )(q, k, v, k, v)


def _ctx_attn_kernel(q_ref, kc_ref, vc_ref, o_ref):
    o_ref[...] = _online_attend(q_ref[...], [(kc_ref[...], vc_ref[...], None)]).astype(o_ref.dtype)


def _ctx_attn_call(q, k, v, *, n_heads, dq, q_col, k_col, v_col, bsz, seq, n_ctx):
    ctx0 = bsz * seq // n_ctx
    dv = HEAD_DIM
    return pl.pallas_call(
        _ctx_attn_kernel,
        grid=(bsz, n_heads),
        in_specs=[pl.BlockSpec((n_ctx, dq), lambda b, h: (ctx0 + b, q_col(h))),
                  pl.BlockSpec((n_ctx, dq), lambda b, h: (ctx0 + b, k_col(h))),
                  pl.BlockSpec((n_ctx, dv), lambda b, h: (ctx0 + b, v_col(h)))],
        out_specs=pl.BlockSpec((n_ctx, dv), lambda b, h: (b, h)),
        out_shape=jax.ShapeDtypeStruct((bsz * n_ctx, n_heads * dv), BF16),
        compiler_params=_cparams(("parallel", "parallel")),
        name="context_attention",
    )(q, k, v)


def _na_kernel(q_ref, k_ref, v_ref, kc_ref, vc_ref, bias_ref, o_ref, *, rows_total):
    i = pl.program_id(2)
    band = NA_BAND * GRID_W
    start_blk = jnp.clip(2 * i - 1, 0, (rows_total - NA_BAND) // 4)
    start = pl.multiple_of(start_blk * (4 * GRID_W), 4 * GRID_W)
    kb = k_ref[pl.ds(start, band), :]
    vb = v_ref[pl.ds(start, band), :]
    chunks = [(kb, vb, bias_ref[0, 0]), (kc_ref[...], vc_ref[...], None)]
    o_ref[...] = _online_attend(q_ref[...], chunks).astype(o_ref.dtype)


def _na_call(p, bias_tab, *, bsz, seq, n_ctx):
    tq = NA_ROWS * GRID_W
    tpb = seq // tq
    rows_total = seq // GRID_W
    ctx0 = bsz * seq // n_ctx
    d = HEAD_DIM
    qc, kc, vc = COL_CQ // d, COL_CK // d, COL_CV // d

    def variant(i):
        return jnp.where(i == 0, 0, jnp.where(i == tpb - 1, 2, 1))

    return pl.pallas_call(
        functools.partial(_na_kernel, rows_total=rows_total),
        grid=(bsz, NA_HEADS, tpb),
        in_specs=[pl.BlockSpec((tq, d), lambda b, h, i: (b * tpb + i, qc + h)),
                  pl.BlockSpec((seq, d), lambda b, h, i: (b, kc + h)),
                  pl.BlockSpec((seq, d), lambda b, h, i: (b, vc + h)),
                  pl.BlockSpec((n_ctx, d), lambda b, h, i: (ctx0 + b, kc + h)),
                  pl.BlockSpec((n_ctx, d), lambda b, h, i: (ctx0 + b, vc + h)),
                  pl.BlockSpec((1, 1, tq, NA_BAND * GRID_W), lambda b, h, i: (h, variant(i), 0, 0))],
        out_specs=pl.BlockSpec((tq, d), lambda b, h, i: (b * tpb + i, h)),
        out_shape=jax.ShapeDtypeStruct((bsz * seq, NA_HEADS * d), BF16),
        compiler_params=_cparams(("parallel", "parallel", "arbitrary")),
        name="neighbourhood_attention",
    )(p, p, p, p, p, bias_tab)


def _branch_kernel(oa_ref, ob_ref, oc_ref, oac_ref, obc_ref, occ_ref, ga_ref, gb_ref, gc_ref,
                   wa_ref, wb_ref, wc_ref, o_ref, *, n_lat_tiles):
    def gated(g_ref, w_ref, o_tile):
        return jax.nn.sigmoid(g_ref[...].astype(F32)) * _dot(o_tile, w_ref[...])

    @pl.when(pl.program_id(0) < n_lat_tiles)
    def _():
        acc = gated(ga_ref, wa_ref, oa_ref[...]) + gated(gb_ref, wb_ref, ob_ref[...]) + gated(gc_ref, wc_ref, oc_ref[...])
        o_ref[...] = acc.astype(BF16)

    @pl.when(pl.program_id(0) >= n_lat_tiles)
    def _():
        acc = gated(ga_ref, wa_ref, oac_ref[...]) + gated(gb_ref, wb_ref, obc_ref[...]) + gated(gc_ref, wc_ref, occ_ref[...])
        o_ref[...] = acc.astype(BF16)


def _branch_call(o_lat, o_ctx, p, wa, wb, wc, n_rows):
    d = wa.shape[1]
    tm = MM_TM
    g0 = COL_GATE // d
    n_lat_tiles = o_lat[0].shape[0] // tm
    lat_specs = [pl.BlockSpec((tm, o.shape[1]), lambda i: (jnp.minimum(i, n_lat_tiles - 1), 0)) for o in o_lat]
    ctx_specs = [pl.BlockSpec((tm, o.shape[1]), lambda i: (jnp.maximum(i - n_lat_tiles, 0), 0)) for o in o_ctx]
    return pl.pallas_call(
        functools.partial(_branch_kernel, n_lat_tiles=n_lat_tiles),
        grid=(n_rows // tm,),
        in_specs=lat_specs + ctx_specs + [
            pl.BlockSpec((tm, d), lambda i: (i, g0)),
            pl.BlockSpec((tm, d), lambda i: (i, g0 + 1)),
            pl.BlockSpec((tm, d), lambda i: (i, g0 + 2)),
            _resident(wa.shape), _resident(wb.shape), _resident(wc.shape)],
        out_specs=pl.BlockSpec((tm, d), lambda i: (i, 0)),
        out_shape=jax.ShapeDtypeStruct((n_rows, d), BF16),
        compiler_params=_cparams(("parallel",)),
        name="branch_merge",
    )(*o_lat, *o_ctx, p, p, p, wa, wb, wc)


def _layer_norm(z, g, b):
    mu = jnp.mean(z, axis=1, keepdims=True)
    zc = z - mu
    var = jnp.mean(zc * zc, axis=1, keepdims=True)
    return zc * lax.rsqrt(var + LN_EPS) * g + b


def _route(logits):
    lane = lax.broadcasted_iota(jnp.int32, logits.shape, 1)
    lane_f = lane.astype(F32)
    is_g = lane < N_GROUPS
    gl = jnp.where(is_g, logits, NEG_BIG)
    mg = jnp.max(gl, axis=1, keepdims=True)
    gsel = jnp.min(jnp.where(gl == mg, lane_f, float(LANES)), axis=1, keepdims=True)
    zg = jnp.sum(jnp.where(is_g, jnp.exp(gl - mg), 0.0), axis=1, keepdims=True)
    lo = N_GROUPS + gsel * EXPERTS_PER_GROUP
    is_e = (lane_f >= lo) & (lane_f < lo + EXPERTS_PER_GROUP)
    el = jnp.where(is_e, logits, NEG_BIG)
    t1 = jnp.max(el, axis=1, keepdims=True)
    i1 = jnp.min(jnp.where(el == t1, lane_f, float(LANES)), axis=1, keepdims=True)
    el2 = jnp.where(lane_f == i1, NEG_BIG, el)
    t2 = jnp.max(el2, axis=1, keepdims=True)
    i2 = jnp.min(jnp.where(el2 == t2, lane_f, float(LANES)), axis=1, keepdims=True)
    dd = jnp.exp(t2 - t1)
    g1 = 1.0 / (zg * (1.0 + dd))
    g2 = g1 * dd
    slab = jnp.where(lane == 0, i1 - N_GROUPS,
                     jnp.where(lane == 1, i2 - N_GROUPS,
                               jnp.where(lane == 2, g1, jnp.where(lane == 3, g2, 0.0))))
    return slab


def _out_kernel(m_ref, x_ref, mod_ref, w_ref, lg_ref, lb_ref, wrh_ref, wrl_ref, br_ref,
                xo_ref, u_ref, r_ref, *, alpha):
    y = _dot(m_ref[...], w_ref[...])
    gate_m = mod_ref[0, 2:3, :]
    shift_f = mod_ref[0, 3:4, :]
    scale_f = mod_ref[0, 4:5, :]
    xn = _layer_norm(alpha * x_ref[...] + gate_m * y, lg_ref[...], lb_ref[...])
    xo_ref[...] = xn
    u = xn * (1.0 + scale_f) + shift_f
    u_hi, u_lo = _split_bf16(u)
    u_ref[...] = u_hi
    logits = _dot(u_hi, wrh_ref[...]) + _dot(u_lo, wrh_ref[...]) + _dot(u_hi, wrl_ref[...]) + br_ref[...]
    r_ref[...] = _route(logits)


def _out_call(mrg, xs, mod, w_out, ln_g, ln_b, wr_hi, wr_lo, br, mod_row, n_rows, alpha):
    d = xs.shape[1]
    tm = MM_TM
    return pl.pallas_call(
        functools.partial(_out_kernel, alpha=alpha),
        grid=(n_rows // tm,),
        in_specs=[pl.BlockSpec((tm, d), lambda i: (i, 0)),
                  pl.BlockSpec((tm, d), lambda i: (i, 0)),
                  pl.BlockSpec((1, 6, d), lambda i: (mod_row(i, tm), 0, 0)),
                  _resident((d, d)), _resident((1, d)), _resident((1, d)),
                  _resident((d, LANES)), _resident((d, LANES)), _resident((1, LANES))],
        out_specs=[pl.BlockSpec((tm, d), lambda i: (i, 0)),
                   pl.BlockSpec((tm, d), lambda i: (i, 0)),
                   pl.BlockSpec((tm, LANES), lambda i: (i, 0))],
        out_shape=[jax.ShapeDtypeStruct((n_rows, d), F32),
                   jax.ShapeDtypeStruct((n_rows, d), BF16),
                   jax.ShapeDtypeStruct((n_rows, LANES), F32)],
        compiler_params=_cparams(("parallel",)),
        name="out_proj_ln_route",
    )(mrg, xs, mod, w_out, ln_g, ln_b, wr_hi, wr_lo, br)


def _rank_kernel(r_ref, rank_ref, cnt_ref, carry):
    @pl.when(pl.program_id(0) == 0)
    def _():
        carry[...] = jnp.zeros_like(carry)

    slab = r_ref[...]
    tm = slab.shape[0]
    lane = lax.broadcasted_iota(jnp.int32, slab.shape, 1)
    lane_f = lane.astype(F32)
    e1 = slab[:, 0:1]
    e2 = slab[:, 1:2]
    hit1 = lane_f == e1
    hit2 = lane_f == e2
    onehot = jnp.where(hit1 | hit2, 1.0, 0.0)
    row = lax.broadcasted_iota(jnp.int32, (tm, tm), 0)
    col = lax.broadcasted_iota(jnp.int32, (tm, tm), 1)
    lower = jnp.where(col < row, 1.0, 0.0).astype(BF16)
    before = _dot(lower, onehot.astype(BF16)) + carry[0:1, :]
    r1 = jnp.sum(jnp.where(hit1, before, 0.0), axis=1, keepdims=True)
    r2 = jnp.sum(jnp.where(hit2, before, 0.0), axis=1, keepdims=True)
    rank_ref[...] = jnp.where(lane == 0, r1, jnp.where(lane == 1, r2, 0.0))
    carry[...] = carry[...] + jnp.sum(onehot, axis=0, keepdims=True)
    cnt_ref[...] = carry[...]


def _rank_call(route):
    n = route.shape[0]
    tm = ROW_TM
    return pl.pallas_call(
        _rank_kernel,
        grid=(n // tm,),
        in_specs=[pl.BlockSpec((tm, LANES), lambda i: (i, 0))],
        out_specs=[pl.BlockSpec((tm, LANES), lambda i: (i, 0)),
                   pl.BlockSpec((8, LANES), lambda i: (0, 0))],
        out_shape=[jax.ShapeDtypeStruct((n, LANES), F32), jax.ShapeDtypeStruct((8, LANES), F32)],
        scratch_shapes=[pltpu.VMEM((8, LANES), F32)],
        compiler_params=_cparams(("arbitrary",)),
        name="expert_ranks",
    )(route)


def _moe_kernel(be_ref, first_ref, nxt_ref, par_ref, nu_ref, x_ref, wg_hbm, wu_hbm, wd_hbm, y_ref,
                gbuf, ubuf, dbuf, g16, u16, d16, sems, *, layer):
    j = pl.program_id(0)

    def weight_copies(e, slot):
        return (pltpu.make_async_copy(wg_hbm.at[layer, e], gbuf.at[slot], sems.at[slot, 0]),
                pltpu.make_async_copy(wu_hbm.at[layer, e], ubuf.at[slot], sems.at[slot, 1]),
                pltpu.make_async_copy(wd_hbm.at[layer, e], dbuf.at[slot], sems.at[slot, 2]))

    @pl.when(j < nu_ref[0])
    def _():
        slot = par_ref[j]

        @pl.when(j == 0)
        def _():
            for cp in weight_copies(be_ref[0], 0):
                cp.start()

        @pl.when(first_ref[j] == 1)
        def _():
            @pl.when(nxt_ref[j] >= 0)
            def _():
                for cp in weight_copies(nxt_ref[j], 1 - slot):
                    cp.start()

            for cp in weight_copies(be_ref[j], slot):
                cp.wait()
            g16[...] = gbuf[slot].astype(BF16)
            u16[...] = ubuf[slot].astype(BF16)
            d16[...] = dbuf[slot].astype(BF16)

        xb = x_ref[...]
        hg = _dot(xb, g16[...])
        hu = _dot(xb, u16[...])
        hb = (hg * jax.nn.sigmoid(hg) * hu).astype(BF16)
        y_ref[...] = _dot(hb, d16[...]).astype(y_ref.dtype)

    @pl.when(j >= nu_ref[0])
    def _():
        y_ref[...] = jnp.zeros_like(y_ref)


def _moe_call(block_e, first, nxt, parity, n_used, u_sorted, w_gate, w_up, w_down, layer):
    n_slots, d = u_sorted.shape
    bm = MOE_BM
    nb = n_slots // bm
    de = w_gate.shape[-1]

    def xmap(j, be, fi, nx_, pa, nu):
        return (jnp.minimum(j, nu[0] - 1), 0)

    grid_spec = pltpu.PrefetchScalarGridSpec(
        num_scalar_prefetch=5,
        grid=(nb,),
        in_specs=[pl.BlockSpec((bm, d), xmap),
                  pl.BlockSpec(memory_space=pl.ANY),
                  pl.BlockSpec(memory_space=pl.ANY),
                  pl.BlockSpec(memory_space=pl.ANY)],
        out_specs=pl.BlockSpec((bm, d), lambda j, *_: (j, 0)),
        scratch_shapes=[pltpu.VMEM((2, d, de), F32), pltpu.VMEM((2, d, de), F32), pltpu.VMEM((2, de, d), F32),
                        pltpu.VMEM((d, de), BF16), pltpu.VMEM((d, de), BF16), pltpu.VMEM((de, d), BF16),
                        pltpu.SemaphoreType.DMA((2, 3))],
    )
    return pl.pallas_call(
        functools.partial(_moe_kernel, layer=layer),
        grid_spec=grid_spec,
        out_shape=jax.ShapeDtypeStruct((n_slots, d), BF16),
        compiler_params=_cparams(("arbitrary",)),
        name="expert_mlp",
    )(block_e, first, nxt, parity, n_used, u_sorted, w_gate, w_up, w_down)


def _combine_kernel(x_ref, y0_ref, y1_ref, r_ref, mod_ref, lg_ref, lb_ref, o_ref, *, alpha):
    slab = r_ref[...]
    g1 = slab[:, 2:3]
    g2 = slab[:, 3:4]
    mx = g1 * y0_ref[...].astype(F32) + g2 * y1_ref[...].astype(F32)
    gate_f = mod_ref[0, 5:6, :]
    o_ref[...] = _layer_norm(alpha * x_ref[...] + gate_f * mx, lg_ref[...], lb_ref[...])


def _combine_call(xs, y0, y1, route, mod, ln_g, ln_b, mod_row, alpha):
    n, d = xs.shape
    tm = ROW_TM
    return pl.pallas_call(
        functools.partial(_combine_kernel, alpha=alpha),
        grid=(n // tm,),
        in_specs=[pl.BlockSpec((tm, d), lambda i: (i, 0)),
                  pl.BlockSpec((tm, d), lambda i: (i, 0)),
                  pl.BlockSpec((tm, d), lambda i: (i, 0)),
                  pl.BlockSpec((tm, LANES), lambda i: (i, 0)),
                  pl.BlockSpec((1, 6, d), lambda i: (mod_row(i, tm), 0, 0)),
                  pl.BlockSpec((1, d), lambda i: (0, 0)),
                  pl.BlockSpec((1, d), lambda i: (0, 0))],
        out_specs=pl.BlockSpec((tm, d), lambda i: (i, 0)),
        out_shape=jax.ShapeDtypeStruct((n, d), F32),
        compiler_params=_cparams(("parallel",)),
        name="moe_combine_ln",
    )(xs, y0, y1, route, mod, ln_g, ln_b)


def _rope_angles(seq, dim):
    tpos = np.arange(seq)
    row = (tpos // GRID_W).astype(np.float32)
    col = (tpos % GRID_W).astype(np.float32)
    quarter = dim // 4
    inv_freq = jnp.asarray(ROPE_THETA, F32) ** (-jnp.arange(quarter, dtype=F32) / quarter)
    ang_r = jnp.asarray(row)[:, None] * inv_freq
    ang_c = jnp.asarray(col)[:, None] * inv_freq
    return jnp.concatenate([ang_r, ang_r, ang_c, ang_c], axis=-1)


def _rot_sign(dim):
    l = np.arange(dim)
    return np.where((l & (dim // 4)) == 0, -1.0, 1.0).astype(np.float32), l ^ (dim // 4)


def _rope_tables(seq, pad_rows):
    ang_a = _rope_angles(seq, HEAD_DIM)
    sign_a, _ = _rot_sign(HEAD_DIM)
    cos_a = jnp.concatenate([jnp.cos(ang_a), jnp.ones((pad_rows, HEAD_DIM), F32)], 0)
    sin_a = jnp.concatenate([jnp.sin(ang_a) * sign_a, jnp.zeros((pad_rows, HEAD_DIM), F32)], 0)
    ang_b = _rope_angles(seq, MLA_ROPE)
    cos_b = jnp.concatenate([jnp.cos(ang_b), jnp.ones((pad_rows, MLA_ROPE), F32)], 0)
    sin_b = jnp.concatenate([jnp.sin(ang_b), jnp.zeros((pad_rows, MLA_ROPE), F32)], 0)
    return cos_a, sin_a, jnp.tile(cos_b, (1, 2)), jnp.tile(sin_b, (1, 2))


def _permute_w_in(w):
    o = np.cumsum((0, 768, 256, 256, 512, 256, 64, 640, 640, 640, 6144))
    aq, ak, av, bq, bkv, bkr, cq, ck, cv, gate = (w[..., o[i]:o[i + 1]] for i in range(10))
    bkr_rot = _rot_cols(bkr)
    cq = cq * (1.0 / math.sqrt(HEAD_DIM))
    out = jnp.concatenate([aq, ak, bq, bkv, bkr, bkr_rot, av, cq, ck, cv, gate], axis=-1)
    assert out.shape[-1] == PROJ_W
    return out.astype(BF16)


def _rot_cols(w):
    dim = w.shape[-1]
    wr = w.reshape(w.shape[:-1] + (2, 2, dim // 4))
    return jnp.stack([-wr[..., 1, :], wr[..., 0, :]], axis=-2).reshape(w.shape)


def _permute_w_uq(w):
    r = w.shape[0]
    w3 = w.reshape(r, MLA_HEADS, MLA_QK)
    nope = w3[:, :, :MLA_NOPE].reshape(r, MLA_HEADS * MLA_NOPE)
    rope = w3[:, :, MLA_NOPE:]
    rot = _rot_cols(rope)
    zpad = jnp.zeros((r, MLA_HEADS, LANES - MLA_ROPE), w.dtype)
    rope_p = jnp.concatenate([rope, zpad], -1).reshape(r, MLA_HEADS * LANES)
    rot_p = jnp.concatenate([rot, zpad], -1).reshape(r, MLA_HEADS * LANES)
    return jnp.concatenate([nope, rope_p, rot_p], axis=1).astype(BF16)


def _permute_w_ukv(w):
    r = w.shape[0]
    w3 = w.reshape(r, MLA_HEADS, MLA_NOPE + MLA_V)
    kn = w3[:, :, :MLA_NOPE].reshape(r, MLA_HEADS * MLA_NOPE)
    vv = w3[:, :, MLA_NOPE:].reshape(r, MLA_HEADS * MLA_V)
    return jnp.concatenate([kn, vv], axis=1).astype(BF16)


def _na_bias_tables(rpb, seq):
    w, kh, kw = GRID_W, NA_KH, NA_KW
    rows = seq // w
    nh, n_dr, n_dc = rpb.shape
    line = jnp.full((nh, n_dr, 2 * w), NEG_BIG, F32).at[:, :, w - kw:w - kw + n_dc].set(rpb.astype(F32))
    skew = jnp.broadcast_to(line[:, :, None, :], (nh, n_dr, w, 2 * w)).reshape(nh, n_dr, 2 * w * w)
    skew = skew[:, :, :w * (2 * w - 1)].reshape(nh, n_dr, w, 2 * w - 1)
    tc = skew[:, :, :, w - 1:2 * w - 1]
    cq = np.arange(w)[:, None]
    ck = np.arange(w)[None, :]
    cs = np.clip(cq - kw // 2, 0, w - kw)
    col_ok = (ck >= cs) & (ck < cs + kw)
    tc = jnp.where(col_ok[None, None], tc, NEG_BIG)
    tc = jnp.concatenate([tc, jnp.full((nh, 1, w, w), NEG_BIG, F32)], axis=1)
    tc2 = jnp.concatenate([tc, tc], axis=-1)
    blks = []
    for r0 in (0, NA_ROWS, rows - NA_ROWS):
        rb = int(np.clip(r0 - kh // 2, 0, rows - NA_BAND))
        rq = r0 + np.arange(NA_ROWS)[:, None]
        rk = rb + np.arange(NA_BAND)[None, :]
        rs = np.clip(rq - kh // 2, 0, rows - kh)
        row_ok = (rk >= rs) & (rk < rs + kh)
        blks.append(np.where(row_ok, rk - rq + kh - 1, n_dr).tolist())
    return pl.pallas_call(
        functools.partial(_na_bias_kernel, blks=blks),
        grid=(nh,),
        in_specs=[pl.BlockSpec((1, n_dr + 1, w, 2 * w), lambda h: (h, 0, 0, 0))],
        out_specs=pl.BlockSpec((1, len(blks), NA_ROWS * w, NA_BAND * w), lambda h: (h, 0, 0, 0)),
        out_shape=jax.ShapeDtypeStruct((nh, len(blks), NA_ROWS * w, NA_BAND * w), F32),
        compiler_params=_cparams(("parallel",)),
        name="na_bias_table",
    )(tc2)


def _na_bias_kernel(tc_ref, o_ref, *, blks):
    w = GRID_W
    low = lax.broadcasted_iota(jnp.int32, (w, 2 * w), 1) < w
    for v, blk in enumerate(blks):
        for rq, row in enumerate(blk):
            for pr in range(len(row) // 2):
                pair = jnp.where(low, tc_ref[0, row[2 * pr]], tc_ref[0, row[2 * pr + 1]])
                o_ref[0, v, rq * w:(rq + 1) * w, pr * 2 * w:(pr + 1) * 2 * w] = pair


def kernel(x, c, ctx, c_ctx, w_ada, b_ada, w_in, gqa_q_norm, gqa_k_norm, mla_q_norm, mla_kv_norm, mla_w_uq, mla_w_ukv, na_rpb, w_branch_a, w_branch_b, w_branch_c, w_out, ln1_g, ln1_b, w_router_group, b_router_group, w_router_expert, b_router_expert, w_expert_gate, w_expert_up, w_expert_down, ln2_g, ln2_b):
    bsz, seq, d = x.shape
    n_ctx = ctx.shape[1]
    depth = w_ada.shape[0]
    nx, nc = bsz * seq, bsz * n_ctx
    t = nx + nc
    assert seq % PROJ_TM == 0 and nc == PROJ_TM and seq % (NA_BAND * GRID_W) == 0 and nx % n_ctx == 0
    alpha = (2 * depth) ** 0.25

    def mod_row(i, tm):
        return jnp.minimum(i // (seq // tm), bsz)

    def rope_blk(i, tm):
        return jnp.where(i < nx // tm, i % (seq // tm), seq // tm)

    xs = jnp.concatenate([x.reshape(nx, d), ctx.reshape(nc, d)], axis=0)
    cc = jnp.concatenate([c, c_ctx[None], jnp.zeros((8 - bsz - 1, d), F32)], axis=0)
    mod_all = _ada_call(cc, w_ada, b_ada).reshape(depth, 8, 6, d)
    cos_a, sin_a, cos_b, sin_b = _rope_tables(seq, ROW_TM)
    w_in_p = _permute_w_in(w_in)

    attn_kw = dict(bsz=bsz, seq=seq, n_ctx=n_ctx)
    for i in range(depth):
        last = i == depth - 1
        n_rows = nx if last else t
        mod = mod_all[i]
        p = _proj_call(xs, mod, w_in_p, i, mod_row)

        qa, ka = _prep_a_call(p, cos_a, sin_a, gqa_q_norm[i][None], gqa_k_norm[i][None], rope_blk)
        a_kw = dict(n_heads=GQA_HEADS, dq=HEAD_DIM, q_col=lambda h: h, k_col=lambda h: h // GQA_GROUP,
                    v_col=lambda h: COL_AV // HEAD_DIM + h // GQA_GROUP, **attn_kw)
        o_a = _dense_attn_call(qa, ka, p, **a_kw)
        qb, kb, vb = _prep_b_call(p, cos_b, sin_b, mla_q_norm[i][None], mla_kv_norm[i][None],
                                  _permute_w_uq(mla_w_uq[i]), _permute_w_ukv(mla_w_ukv[i]), rope_blk)
        b_kw = dict(n_heads=MLA_HEADS, dq=MLA_PAD, q_col=lambda h: h, k_col=lambda h: h, v_col=lambda h: h, **attn_kw)
        o_b = _dense_attn_call(qb, kb, vb, **b_kw)
        o_c = _na_call(p, _na_bias_tables(na_rpb[i], seq), **attn_kw)
        o_lat = (o_a, o_b, o_c)
        if last:
            o_ctx = o_lat
        else:
            c_kw = dict(n_heads=NA_HEADS, dq=HEAD_DIM, q_col=lambda h: COL_CQ // HEAD_DIM + h,
                        k_col=lambda h: COL_CK // HEAD_DIM + h, v_col=lambda h: COL_CV // HEAD_DIM + h, **attn_kw)
            o_ctx = (_ctx_attn_call(qa, ka, p, **a_kw), _ctx_attn_call(qb, kb, vb, **b_kw),
                     _ctx_attn_call(p, p, p, **c_kw))

        mrg = _branch_call(o_lat, o_ctx, p, w_branch_a[i].astype(BF16), w_branch_b[i].astype(BF16),
                           w_branch_c[i].astype(BF16), n_rows)
        wr = jnp.concatenate([w_router_group[i], w_router_expert[i],
                              jnp.zeros((d, LANES - N_GROUPS - N_EXPERTS), F32)], axis=1)
        br = jnp.concatenate([b_router_group[i], b_router_expert[i],
                              jnp.zeros((LANES - N_GROUPS - N_EXPERTS,), F32)])[None]
        wr_hi, wr_lo = _split_bf16(wr)
        xs, u_f, route = _out_call(mrg, xs, mod, w_out[i].astype(BF16), ln1_g[i][None], ln1_b[i][None],
                                   wr_hi, wr_lo, br, mod_row, n_rows, alpha)

        rank_slab, cnt = _rank_call(route)
        eid = route[:, :TOP_K].astype(jnp.int32)
        rank = rank_slab[:, :TOP_K].astype(jnp.int32)
        counts = cnt[0, :N_EXPERTS].astype(jnp.int32)
        padded = ((counts + MOE_BM - 1) // MOE_BM) * MOE_BM
        pends = jnp.cumsum(padded)
        pstarts = pends - padded
        dest = pstarts[eid] + rank
        nb = -(-(n_rows * TOP_K) // MOE_BM) + N_EXPERTS
        n_used = (pends[-1] // MOE_BM).astype(jnp.int32)
        blk = jnp.minimum(jnp.arange(nb, dtype=jnp.int32), n_used - 1)
        block_e = jnp.clip(jnp.searchsorted(pends, blk * MOE_BM, side="right"), 0, N_EXPERTS - 1).astype(jnp.int32)
        order = jnp.argsort(eid.reshape(-1), stable=True).astype(jnp.int32)
        starts = jnp.cumsum(counts) - counts
        slot = jnp.arange(nb * MOE_BM, dtype=jnp.int32).reshape(nb, MOE_BM)
        blk_shift = (starts - pstarts)[block_e][:, None]
        blk_limit = (pstarts + counts)[block_e][:, None]
        pos = jnp.clip(slot + blk_shift, 0, n_rows * TOP_K - 1)
        slot_tok = jnp.where(slot < blk_limit, order.at[pos].get(mode="promise_in_bounds") // TOP_K,
                             slot % n_rows).reshape(-1)
        u_sorted = u_f.at[slot_tok].get(mode="promise_in_bounds")
        jj = jnp.arange(nb, dtype=jnp.int32)
        first = ((jj == 0) | (block_e != jnp.roll(block_e, 1))).astype(jnp.int32)
        parity = (jnp.cumsum(first) - 1) % 2
        live = jnp.where(counts > 0, jnp.arange(N_EXPERTS, dtype=jnp.int32), N_EXPERTS)
        next_live = jnp.concatenate([lax.cummin(live, reverse=True)[1:], jnp.full((1,), N_EXPERTS, jnp.int32)])
        nxt = jnp.where(next_live < N_EXPERTS, next_live, -1)[block_e]
        y_sorted = _moe_call(block_e, first, nxt.astype(jnp.int32), parity.astype(jnp.int32), n_used[None],
                             u_sorted, w_expert_gate, w_expert_up, w_expert_down, i)
        y0 = y_sorted.at[dest[:, 0]].get(mode="promise_in_bounds")
        y1 = y_sorted.at[dest[:, 1]].get(mode="promise_in_bounds")
        xs = _combine_call(xs, y0, y1, route, mod, ln2_g[i][None], ln2_b[i][None], mod_row, alpha)
    return xs.reshape(bsz, seq, d)
```

```python
import functools
import math

import jax
import jax.numpy as jnp
import numpy as np
from jax import lax
from jax.experimental import pallas as pl
from jax.experimental.pallas import tpu as pltpu

F32 = jnp.float32
BF16 = jnp.bfloat16

HEAD_DIM = 128
GRID_W = 64
ROPE_THETA = 10000.0
GQA_HEADS, GQA_KV_HEADS = 6, 2
GQA_GROUP = GQA_HEADS // GQA_KV_HEADS
MLA_HEADS, MLA_Q_RANK, MLA_KV_RANK = 5, 512, 256
MLA_NOPE, MLA_ROPE, MLA_V = 128, 64, 128
MLA_QK = MLA_NOPE + MLA_ROPE
MLA_PAD = 256
NA_HEADS, NA_KH, NA_KW = 5, 8, 16
N_GROUPS, EXPERTS_PER_GROUP, TOP_K, D_EXPERT = 8, 8, 2, 512
N_EXPERTS = N_GROUPS * EXPERTS_PER_GROUP
LN_EPS = 1e-6
RMS_EPS = 1e-6
NEG_BIG = -1e30

LANES = 128
VMEM_LIMIT = 56 * 1024 * 1024
PROJ_TM, PROJ_TN = 1024, 1024
ROW_TM = 512
MM_TM = 512
ATTN_TQ = 512
ATTN_SUB = 2
ATTN_KC = 1024
NA_ROWS = 8
NA_BAND = 16
MOE_BM = 256
ADA_TN = 512

COL_AQ, COL_AK = 0, 768
COL_BQ, COL_BKV, COL_BKR = 1024, 1536, 1792
COL_AV, COL_CQ, COL_CK, COL_CV = 1920, 2176, 2816, 3456
COL_GATE = 4096
PROJ_W = 10240


def _cparams(sem):
    return pltpu.CompilerParams(dimension_semantics=sem, vmem_limit_bytes=VMEM_LIMIT)


def _resident(shape):
    return pl.BlockSpec(shape, lambda *_: (0,) * len(shape), pipeline_mode=pl.Buffered(1))


def _dot(a, b):
    return jnp.dot(a, b, preferred_element_type=F32)


def _dot_nt(a, b):
    return lax.dot_general(a, b, (((1,), (1,)), ((), ())), preferred_element_type=F32)


def _split_bf16(a):
    hi = a.astype(BF16)
    lo = (a - hi.astype(F32)).astype(BF16)
    return hi, lo


def _ada_kernel(c_ref, w_ref, b_ref, o_ref):
    a = c_ref[...]
    a = a * jax.nn.sigmoid(a)
    a_hi, a_lo = _split_bf16(a)
    w_hi, w_lo = _split_bf16(w_ref[0])
    acc = _dot(a_hi, w_hi) + _dot(a_lo, w_hi) + _dot(a_hi, w_lo)
    o_ref[0] = acc + b_ref[0]


def _ada_call(cc, w_ada, b_ada):
    depth, d, n = w_ada.shape
    rows = cc.shape[0]
    return pl.pallas_call(
        _ada_kernel,
        grid=(depth, n // ADA_TN),
        in_specs=[pl.BlockSpec((rows, d), lambda l, j: (0, 0)),
                  pl.BlockSpec((1, d, ADA_TN), lambda l, j: (l, 0, j)),
                  pl.BlockSpec((1, 1, ADA_TN), lambda l, j: (l, 0, j))],
        out_specs=pl.BlockSpec((1, rows, ADA_TN), lambda l, j: (l, 0, j)),
        out_shape=jax.ShapeDtypeStruct((depth, rows, n), F32),
        compiler_params=_cparams(("parallel", "parallel")),
        name="ada_modulation",
    )(cc, w_ada, b_ada.reshape(depth, 1, n))


def _proj_kernel(x_ref, mod_ref, w_ref, o_ref, u_scr):
    @pl.when(pl.program_id(1) == 0)
    def _():
        shift = mod_ref[0, 0:1, :]
        scale = mod_ref[0, 1:2, :]
        u_scr[...] = (x_ref[...] * (1.0 + scale) + shift).astype(BF16)

    o_ref[...] = _dot(u_scr[...], w_ref[0]).astype(BF16)


def _proj_call(xs, mod, w_in_p, layer, mod_row):
    t, d = xs.shape
    return pl.pallas_call(
        _proj_kernel,
        grid=(t // PROJ_TM, PROJ_W // PROJ_TN),
        in_specs=[pl.BlockSpec((PROJ_TM, d), lambda i, j: (i, 0)),
                  pl.BlockSpec((1, 6, d), lambda i, j: (mod_row(i, PROJ_TM), 0, 0)),
                  pl.BlockSpec((1, d, PROJ_TN), lambda i, j: (layer, 0, j))],
        out_specs=pl.BlockSpec((PROJ_TM, PROJ_TN), lambda i, j: (i, j)),
        out_shape=jax.ShapeDtypeStruct((t, PROJ_W), BF16),
        scratch_shapes=[pltpu.VMEM((PROJ_TM, d), BF16)],
        compiler_params=_cparams(("parallel", "arbitrary")),
        name="input_projection",
    )(xs, mod, w_in_p)


def _rope128(y, cos, sin_signed, first_quarter):
    rot = jnp.where(first_quarter, pltpu.roll(y, 96, 1), pltpu.roll(y, 32, 1))
    return y * cos + rot * sin_signed


def _prep_a_kernel(p_ref, cos_ref, sin_ref, gq_ref, gk_ref, q_ref, k_ref, *, scale):
    cos = cos_ref[...]
    sin_signed = sin_ref[...]
    lane = lax.broadcasted_iota(jnp.int32, cos.shape, 1)
    first_quarter = (lane & 32) == 0
    for h in range(GQA_HEADS + GQA_KV_HEADS):
        xh = p_ref[:, h * HEAD_DIM:(h + 1) * HEAD_DIM].astype(F32)
        r = lax.rsqrt(jnp.mean(xh * xh, axis=1, keepdims=True) + RMS_EPS)
        if h < GQA_HEADS:
            y = _rope128(xh * r * gq_ref[...], cos, sin_signed, first_quarter) * scale
            q_ref[:, h * HEAD_DIM:(h + 1) * HEAD_DIM] = y.astype(BF16)
        else:
            hk = h - GQA_HEADS
            y = _rope128(xh * r * gk_ref[...], cos, sin_signed, first_quarter)
            k_ref[:, hk * HEAD_DIM:(hk + 1) * HEAD_DIM] = y.astype(BF16)


def _prep_a_call(p, cos_t, sin_t, gq, gk, rope_blk):
    t = p.shape[0]
    tm = ROW_TM
    wq, wk = GQA_HEADS * HEAD_DIM, GQA_KV_HEADS * HEAD_DIM
    return pl.pallas_call(
        functools.partial(_prep_a_kernel, scale=1.0 / math.sqrt(HEAD_DIM)),
        grid=(t // tm,),
        in_specs=[pl.BlockSpec((tm, wq + wk), lambda i: (i, 0)),
                  pl.BlockSpec((tm, HEAD_DIM), lambda i: (rope_blk(i, tm), 0)),
                  pl.BlockSpec((tm, HEAD_DIM), lambda i: (rope_blk(i, tm), 0)),
                  pl.BlockSpec((1, HEAD_DIM), lambda i: (0, 0)),
                  pl.BlockSpec((1, HEAD_DIM), lambda i: (0, 0))],
        out_specs=[pl.BlockSpec((tm, wq), lambda i: (i, 0)),
                   pl.BlockSpec((tm, wk), lambda i: (i, 0))],
        out_shape=[jax.ShapeDtypeStruct((t, wq), BF16), jax.ShapeDtypeStruct((t, wk), BF16)],
        compiler_params=_cparams(("parallel",)),
        name="gqa_qk_prep",
    )(p, cos_t, sin_t, gq, gk)


def _prep_b_kernel(ql_ref, kvl_ref, kr_ref, cos_ref, sin_ref, gq_ref, gkv_ref, wq_ref, wkv_ref,
                   q_ref, k_ref, v_ref, *, scale):
    nh, hp = MLA_HEADS, MLA_PAD
    wn = nh * MLA_NOPE
    cos = cos_ref[...]
    sin = sin_ref[...]
    ql = ql_ref[...].astype(F32)
    ql = ql * lax.rsqrt(jnp.mean(ql * ql, axis=1, keepdims=True) + RMS_EPS) * gq_ref[...]
    qf = _dot(ql.astype(BF16), wq_ref[...])
    kvl = kvl_ref[...].astype(F32)
    kvl = kvl * lax.rsqrt(jnp.mean(kvl * kvl, axis=1, keepdims=True) + RMS_EPS) * gkv_ref[...]
    kvf = _dot(kvl.astype(BF16), wkv_ref[...])
    lane = lax.broadcasted_iota(jnp.int32, cos.shape, 1)
    low = lane < MLA_ROPE
    tk = kr_ref[...].astype(F32) * jnp.where(low, cos, sin)
    kr = jnp.where(low, tk + pltpu.roll(tk, MLA_ROPE, 1), 0.0).astype(BF16)
    for h in range(nh):
        q_ref[:, h * hp:h * hp + LANES] = (qf[:, h * LANES:(h + 1) * LANES] * scale).astype(BF16)
        qr = qf[:, wn + h * LANES:wn + (h + 1) * LANES] * cos + qf[:, 2 * wn + h * LANES:2 * wn + (h + 1) * LANES] * sin
        q_ref[:, h * hp + LANES:(h + 1) * hp] = (qr * scale).astype(BF16)
        k_ref[:, h * hp:h * hp + LANES] = kvf[:, h * LANES:(h + 1) * LANES].astype(BF16)
        k_ref[:, h * hp + LANES:(h + 1) * hp] = kr
    v_ref[...] = kvf[:, wn:].astype(BF16)


def _prep_b_call(p, cos_t, sin_t, gq, gkv, wq, wkv, rope_blk):
    t = p.shape[0]
    tm = ROW_TM
    nh = MLA_HEADS
    return pl.pallas_call(
        functools.partial(_prep_b_kernel, scale=1.0 / math.sqrt(MLA_QK)),
        grid=(t // tm,),
        in_specs=[pl.BlockSpec((tm, MLA_Q_RANK), lambda i: (i, COL_BQ // MLA_Q_RANK)),
                  pl.BlockSpec((tm, MLA_KV_RANK), lambda i: (i, COL_BKV // MLA_KV_RANK)),
                  pl.BlockSpec((tm, LANES), lambda i: (i, COL_BKR // LANES)),
                  pl.BlockSpec((tm, LANES), lambda i: (rope_blk(i, tm), 0)),
                  pl.BlockSpec((tm, LANES), lambda i: (rope_blk(i, tm), 0)),
                  pl.BlockSpec((1, MLA_Q_RANK), lambda i: (0, 0)),
                  pl.BlockSpec((1, MLA_KV_RANK), lambda i: (0, 0)),
                  _resident(wq.shape), _resident(wkv.shape)],
        out_specs=[pl.BlockSpec((tm, nh * MLA_PAD), lambda i: (i, 0)),
                   pl.BlockSpec((tm, nh * MLA_PAD), lambda i: (i, 0)),
                   pl.BlockSpec((tm, nh * MLA_V), lambda i: (i, 0))],
        out_shape=[jax.ShapeDtypeStruct((t, nh * MLA_PAD), BF16),
                   jax.ShapeDtypeStruct((t, nh * MLA_PAD), BF16),
                   jax.ShapeDtypeStruct((t, nh * MLA_V), BF16)],
        compiler_params=_cparams(("parallel",)),
        name="mla_prep",
    )(p, p, p, cos_t, sin_t, gq, gkv, wq, wkv)


def _online_attend(q, chunks):
    m = z = acc = None
    for k, v, bias in chunks:
        s = _dot_nt(q, k)
        if bias is not None:
            s = s + bias
        cm = jnp.max(s, axis=1, keepdims=True)
        if m is None:
            m = cm
            p = jnp.exp(s - m)
            z = jnp.sum(p, axis=1, keepdims=True)
            acc = _dot(p.astype(BF16), v)
        else:
            m_new = jnp.maximum(m, cm)
            corr = jnp.exp(m - m_new)
            p = jnp.exp(s - m_new)
            z = z * corr + jnp.sum(p, axis=1, keepdims=True)
            acc = acc * corr + _dot(p.astype(BF16), v)
            m = m_new
    return acc / z


def _dense_attn_kernel(q_ref, kl_ref, vl_ref, kc_ref, vc_ref, o_ref):
    n_lat = kl_ref.shape[0]
    chunks = [(kl_ref[c:c + ATTN_KC, :], vl_ref[c:c + ATTN_KC, :], None) for c in range(0, n_lat, ATTN_KC)]
    chunks.append((kc_ref[...], vc_ref[...], None))
    for r in range(0, q_ref.shape[0], ATTN_TQ):
        o_ref[r:r + ATTN_TQ, :] = _online_attend(q_ref[r:r + ATTN_TQ, :], chunks).astype(o_ref.dtype)


def _dense_attn_call(q, k, v, *, n_heads, dq, q_col, k_col, v_col, bsz, seq, n_ctx):
    tq = ATTN_TQ * ATTN_SUB
    tpb = seq // tq
    ctx0 = bsz * seq // n_ctx
    dv = HEAD_DIM
    return pl.pallas_call(
        _dense_attn_kernel,
        grid=(bsz, n_heads, tpb),
        in_specs=[pl.BlockSpec((tq, dq), lambda b, h, i: (b * tpb + i, q_col(h))),
                  pl.BlockSpec((seq, dq), lambda b, h, i: (b, k_col(h))),
                  pl.BlockSpec((seq, dv), lambda b, h, i: (b, v_col(h))),
                  pl.BlockSpec((n_ctx, dq), lambda b, h, i: (ctx0 + b, k_col(h))),
                  pl.BlockSpec((n_ctx, dv), lambda b, h, i: (ctx0 + b, v_col(h)))],
        out_specs=pl.BlockSpec((tq, dv), lambda b, h, i: (b * tpb + i, h)),
        out_shape=jax.ShapeDtypeStruct((bsz * seq, n_heads * dv), BF16),
        compiler_params=_cparams(("parallel", "parallel", "arbitrary")),
        name="dense_attention",
    )(q, k, v, k, v)


def _ctx_attn_kernel(q_ref, kc_ref, vc_ref, o_ref):
    o_ref[...] = _online_attend(q_ref[...], [(kc_ref[...], vc_ref[...], None)]).astype(o_ref.dtype)


def _ctx_attn_call(q, k, v, *, n_heads, dq, q_col, k_col, v_col, bsz, seq, n_ctx):
    ctx0 = bsz * seq // n_ctx
    dv = HEAD_DIM
    return pl.pallas_call(
        _ctx_attn_kernel,
        grid=(bsz, n_heads),
        in_specs=[pl.BlockSpec((n_ctx, dq), lambda b, h: (ctx0 + b, q_col(h))),
                  pl.BlockSpec((n_ctx, dq), lambda b, h: (ctx0 + b, k_col(h))),
                  pl.BlockSpec((n_ctx, dv), lambda b, h: (ctx0 + b, v_col(h)))],
        out_specs=pl.BlockSpec((n_ctx, dv), lambda b, h: (b, h)),
        out_shape=jax.ShapeDtypeStruct((bsz * n_ctx, n_heads * dv), BF16),
        compiler_params=_cparams(("parallel", "parallel")),
        name="context_attention",
    )(q, k, v)


def _na_kernel(q_ref, k_ref, v_ref, kc_ref, vc_ref, bias_ref, o_ref, *, rows_total):
    i = pl.program_id(2)
    band = NA_BAND * GRID_W
    start_blk = jnp.clip(2 * i - 1, 0, (rows_total - NA_BAND) // 4)
    start = pl.multiple_of(start_blk * (4 * GRID_W), 4 * GRID_W)
    kb = k_ref[pl.ds(start, band), :]
    vb = v_ref[pl.ds(start, band), :]
    chunks = [(kb, vb, bias_ref[0, 0]), (kc_ref[...], vc_ref[...], None)]
    o_ref[...] = _online_attend(q_ref[...], chunks).astype(o_ref.dtype)


def _na_call(p, bias_tab, *, bsz, seq, n_ctx):
    tq = NA_ROWS * GRID_W
    tpb = seq // tq
    rows_total = seq // GRID_W
    ctx0 = bsz * seq // n_ctx
    d = HEAD_DIM
    qc, kc, vc = COL_CQ // d, COL_CK // d, COL_CV // d

    def variant(i):
        return jnp.where(i == 0, 0, jnp.where(i == tpb - 1, 2, 1))

    return pl.pallas_call(
        functools.partial(_na_kernel, rows_total=rows_total),
        grid=(bsz, NA_HEADS, tpb),
        in_specs=[pl.BlockSpec((tq, d), lambda b, h, i: (b * tpb + i, qc + h)),
                  pl.BlockSpec((seq, d), lambda b, h, i: (b, kc + h)),
                  pl.BlockSpec((seq, d), lambda b, h, i: (b, vc + h)),
                  pl.BlockSpec((n_ctx, d), lambda b, h, i: (ctx0 + b, kc + h)),
                  pl.BlockSpec((n_ctx, d), lambda b, h, i: (ctx0 + b, vc + h)),
                  pl.BlockSpec((1, 1, tq, NA_BAND * GRID_W), lambda b, h, i: (h, variant(i), 0, 0))],
        out_specs=pl.BlockSpec((tq, d), lambda b, h, i: (b * tpb + i, h)),
        out_shape=jax.ShapeDtypeStruct((bsz * seq, NA_HEADS * d), BF16),
        compiler_params=_cparams(("parallel", "parallel", "arbitrary")),
        name="neighbourhood_attention",
    )(p, p, p, p, p, bias_tab)


def _branch_kernel(oa_ref, ob_ref, oc_ref, oac_ref, obc_ref, occ_ref, ga_ref, gb_ref, gc_ref,
                   wa_ref, wb_ref, wc_ref, o_ref, *, n_lat_tiles):
    def gated(g_ref, w_ref, o_tile):
        return jax.nn.sigmoid(g_ref[...].astype(F32)) * _dot(o_tile, w_ref[...])

    @pl.when(pl.program_id(0) < n_lat_tiles)
    def _():
        acc = gated(ga_ref, wa_ref, oa_ref[...]) + gated(gb_ref, wb_ref, ob_ref[...]) + gated(gc_ref, wc_ref, oc_ref[...])
        o_ref[...] = acc.astype(BF16)

    @pl.when(pl.program_id(0) >= n_lat_tiles)
    def _():
        acc = gated(ga_ref, wa_ref, oac_ref[...]) + gated(gb_ref, wb_ref, obc_ref[...]) + gated(gc_ref, wc_ref, occ_ref[...])
        o_ref[...] = acc.astype(BF16)


def _branch_call(o_lat, o_ctx, p, wa, wb, wc, n_rows):
    d = wa.shape[1]
    tm = MM_TM
    g0 = COL_GATE // d
    n_lat_tiles = o_lat[0].shape[0] // tm
    lat_specs = [pl.BlockSpec((tm, o.shape[1]), lambda i: (jnp.minimum(i, n_lat_tiles - 1), 0)) for o in o_lat]
    ctx_specs = [pl.BlockSpec((tm, o.shape[1]), lambda i: (jnp.maximum(i - n_lat_tiles, 0), 0)) for o in o_ctx]
    return pl.pallas_call(
        functools.partial(_branch_kernel, n_lat_tiles=n_lat_tiles),
        grid=(n_rows // tm,),
        in_specs=lat_specs + ctx_specs + [
            pl.BlockSpec((tm, d), lambda i: (i, g0)),
            pl.BlockSpec((tm, d), lambda i: (i, g0 + 1)),
            pl.BlockSpec((tm, d), lambda i: (i, g0 + 2)),
            _resident(wa.shape), _resident(wb.shape), _resident(wc.shape)],
        out_specs=pl.BlockSpec((tm, d), lambda i: (i, 0)),
        out_shape=jax.ShapeDtypeStruct((n_rows, d), BF16),
        compiler_params=_cparams(("parallel",)),
        name="branch_merge",
    )(*o_lat, *o_ctx, p, p, p, wa, wb, wc)


def _layer_norm(z, g, b):
    mu = jnp.mean(z, axis=1, keepdims=True)
    zc = z - mu
    var = jnp.mean(zc * zc, axis=1, keepdims=True)
    return zc * lax.rsqrt(var + LN_EPS) * g + b


def _route(logits):
    lane = lax.broadcasted_iota(jnp.int32, logits.shape, 1)
    lane_f = lane.astype(F32)
    is_g = lane < N_GROUPS
    gl = jnp.where(is_g, logits, NEG_BIG)
    mg = jnp.max(gl, axis=1, keepdims=True)
    gsel = jnp.min(jnp.where(gl == mg, lane_f, float(LANES)), axis=1, keepdims=True)
    zg = jnp.sum(jnp.where(is_g, jnp.exp(gl - mg), 0.0), axis=1, keepdims=True)
    lo = N_GROUPS + gsel * EXPERTS_PER_GROUP
    is_e = (lane_f >= lo) & (lane_f < lo + EXPERTS_PER_GROUP)
    el = jnp.where(is_e, logits, NEG_BIG)
    t1 = jnp.max(el, axis=1, keepdims=True)
    i1 = jnp.min(jnp.where(el == t1, lane_f, float(LANES)), axis=1, keepdims=True)
    el2 = jnp.where(lane_f == i1, NEG_BIG, el)
    t2 = jnp.max(el2, axis=1, keepdims=True)
    i2 = jnp.min(jnp.where(el2 == t2, lane_f, float(LANES)), axis=1, keepdims=True)
    dd = jnp.exp(t2 - t1)
    g1 = 1.0 / (zg * (1.0 + dd))
    g2 = g1 * dd
    slab = jnp.where(lane == 0, i1 - N_GROUPS,
                     jnp.where(lane == 1, i2 - N_GROUPS,
                               jnp.where(lane == 2, g1, jnp.where(lane == 3, g2, 0.0))))
    return slab


def _out_kernel(m_ref, x_ref, mod_ref, w_ref, lg_ref, lb_ref, wr_ref, br_ref,
                xo_ref, u_ref, r_ref, *, alpha):
    y = _dot(m_ref[...], w_ref[...])
    gate_m = mod_ref[0, 2:3, :]
    shift_f = mod_ref[0, 3:4, :]
    scale_f = mod_ref[0, 4:5, :]
    xn = _layer_norm(alpha * x_ref[...] + gate_m * y, lg_ref[...], lb_ref[...])
    xo_ref[...] = xn
    u = xn * (1.0 + scale_f) + shift_f
    u_hi, u_lo = _split_bf16(u)
    u_ref[...] = u_hi
    both = _dot(u_hi, wr_ref[...])
    logits = both[:, :LANES] + both[:, LANES:] + _dot(u_lo, wr_ref[:, :LANES]) + br_ref[...]
    r_ref[...] = _route(logits)


def _out_call(mrg, xs, mod, w_out, ln_g, ln_b, wr_cat, br, mod_row, n_rows, alpha):
    d = xs.shape[1]
    tm = MM_TM
    return pl.pallas_call(
        functools.partial(_out_kernel, alpha=alpha),
        grid=(n_rows // tm,),
        in_specs=[pl.BlockSpec((tm, d), lambda i: (i, 0)),
                  pl.BlockSpec((tm, d), lambda i: (i, 0)),
                  pl.BlockSpec((1, 6, d), lambda i: (mod_row(i, tm), 0, 0)),
                  _resident((d, d)), _resident((1, d)), _resident((1, d)),
                  _resident((d, 2 * LANES)), _resident((1, LANES))],
        out_specs=[pl.BlockSpec((tm, d), lambda i: (i, 0)),
                   pl.BlockSpec((tm, d), lambda i: (i, 0)),
                   pl.BlockSpec((tm, LANES), lambda i: (i, 0))],
        out_shape=[jax.ShapeDtypeStruct((n_rows, d), F32),
                   jax.ShapeDtypeStruct((n_rows, d), BF16),
                   jax.ShapeDtypeStruct((n_rows, LANES), F32)],
        compiler_params=_cparams(("parallel",)),
        name="out_proj_ln_route",
    )(mrg, xs, mod, w_out, ln_g, ln_b, wr_cat, br)


def _rank_kernel(r_ref, rank_ref, cnt_ref, carry):
    @pl.when(pl.program_id(0) == 0)
    def _():
        carry[...] = jnp.zeros_like(carry)

    slab = r_ref[...]
    tm = slab.shape[0]
    lane = lax.broadcasted_iota(jnp.int32, slab.shape, 1)
    lane_f = lane.astype(F32)
    e1 = slab[:, 0:1]
    e2 = slab[:, 1:2]
    hit1 = lane_f == e1
    hit2 = lane_f == e2
    onehot = jnp.where(hit1 | hit2, 1.0, 0.0)
    row = lax.broadcasted_iota(jnp.int32, (tm, tm), 0)
    col = lax.broadcasted_iota(jnp.int32, (tm, tm), 1)
    lower = jnp.where(col < row, 1.0, 0.0).astype(BF16)
    before = _dot(lower, onehot.astype(BF16)) + carry[0:1, :]
    r1 = jnp.sum(jnp.where(hit1, before, 0.0), axis=1, keepdims=True)
    r2 = jnp.sum(jnp.where(hit2, before, 0.0), axis=1, keepdims=True)
    rank_ref[...] = jnp.where(lane == 0, r1, jnp.where(lane == 1, r2, 0.0))
    carry[...] = carry[...] + jnp.sum(onehot, axis=0, keepdims=True)
    cnt_ref[...] = carry[...]


def _rank_call(route):
    n = route.shape[0]
    tm = ROW_TM
    return pl.pallas_call(
        _rank_kernel,
        grid=(n // tm,),
        in_specs=[pl.BlockSpec((tm, LANES), lambda i: (i, 0))],
        out_specs=[pl.BlockSpec((tm, LANES), lambda i: (i, 0)),
                   pl.BlockSpec((8, LANES), lambda i: (0, 0))],
        out_shape=[jax.ShapeDtypeStruct((n, LANES), F32), jax.ShapeDtypeStruct((8, LANES), F32)],
        scratch_shapes=[pltpu.VMEM((8, LANES), F32)],
        compiler_params=_cparams(("arbitrary",)),
        name="expert_ranks",
    )(route)


def _dest_kernel(r_ref, rank_ref, ps_ref, o_ref):
    slab = r_ref[...]
    rank = rank_ref[...]
    lane = lax.broadcasted_iota(jnp.int32, slab.shape, 1)
    lane_f = lane.astype(F32)
    ps = ps_ref[...]
    s1 = jnp.sum(jnp.where(lane_f == slab[:, 0:1], ps, 0.0), axis=1, keepdims=True) + rank[:, 0:1]
    s2 = jnp.sum(jnp.where(lane_f == slab[:, 1:2], ps, 0.0), axis=1, keepdims=True) + rank[:, 1:2]
    o_ref[...] = jnp.where(lane == 0, s1, jnp.where(lane == 1, s2, 0.0))


def _dest_call(route, rank_slab, pstarts_row):
    n = route.shape[0]
    tm = ROW_TM
    return pl.pallas_call(
        _dest_kernel,
        grid=(n // tm,),
        in_specs=[pl.BlockSpec((tm, LANES), lambda i: (i, 0)),
                  pl.BlockSpec((tm, LANES), lambda i: (i, 0)),
                  _resident((1, LANES))],
        out_specs=pl.BlockSpec((tm, LANES), lambda i: (i, 0)),
        out_shape=jax.ShapeDtypeStruct((n, LANES), F32),
        compiler_params=_cparams(("parallel",)),
        name="expert_slots",
    )(route, rank_slab, pstarts_row)


def _moe_kernel(be_ref, first_ref, nxt_ref, par_ref, nu_ref, x_ref, wg_hbm, wu_hbm, wd_hbm, y_ref,
                gbuf, ubuf, dbuf, g16, u16, d16, sems, *, layer):
    j = pl.program_id(0)

    def weight_copies(e, slot):
        return (pltpu.make_async_copy(wg_hbm.at[layer, e], gbuf.at[slot], sems.at[slot, 0]),
                pltpu.make_async_copy(wu_hbm.at[layer, e], ubuf.at[slot], sems.at[slot, 1]),
                pltpu.make_async_copy(wd_hbm.at[layer, e], dbuf.at[slot], sems.at[slot, 2]))

    @pl.when(j < nu_ref[0])
    def _():
        slot = par_ref[j]

        @pl.when(j == 0)
        def _():
            for cp in weight_copies(be_ref[0], 0):
                cp.start()

        @pl.when(first_ref[j] == 1)
        def _():
            @pl.when(nxt_ref[j] >= 0)
            def _():
                for cp in weight_copies(nxt_ref[j], 1 - slot):
                    cp.start()

            for cp in weight_copies(be_ref[j], slot):
                cp.wait()
            g16[...] = gbuf[slot].astype(BF16)
            u16[...] = ubuf[slot].astype(BF16)
            d16[...] = dbuf[slot].astype(BF16)

        xb = x_ref[...]
        hg = _dot(xb, g16[...])
        hu = _dot(xb, u16[...])
        hb = (hg * jax.nn.sigmoid(hg) * hu).astype(BF16)
        y_ref[...] = _dot(hb, d16[...]).astype(y_ref.dtype)

    @pl.when(j >= nu_ref[0])
    def _():
        y_ref[...] = jnp.zeros_like(y_ref)


def _moe_call(block_e, first, nxt, parity, n_used, u_sorted, w_gate, w_up, w_down, layer):
    n_slots, d = u_sorted.shape
    bm = MOE_BM
    nb = n_slots // bm
    de = w_gate.shape[-1]

    def xmap(j, be, fi, nx_, pa, nu):
        return (jnp.minimum(j, nu[0] - 1), 0)

    grid_spec = pltpu.PrefetchScalarGridSpec(
        num_scalar_prefetch=5,
        grid=(nb,),
        in_specs=[pl.BlockSpec((bm, d), xmap),
                  pl.BlockSpec(memory_space=pl.ANY),
                  pl.BlockSpec(memory_space=pl.ANY),
                  pl.BlockSpec(memory_space=pl.ANY)],
        out_specs=pl.BlockSpec((bm, d), lambda j, *_: (j, 0)),
        scratch_shapes=[pltpu.VMEM((2, d, de), F32), pltpu.VMEM((2, d, de), F32), pltpu.VMEM((2, de, d), F32),
                        pltpu.VMEM((d, de), BF16), pltpu.VMEM((d, de), BF16), pltpu.VMEM((de, d), BF16),
                        pltpu.SemaphoreType.DMA((2, 3))],
    )
    return pl.pallas_call(
        functools.partial(_moe_kernel, layer=layer),
        grid_spec=grid_spec,
        out_shape=jax.ShapeDtypeStruct((n_slots, d), BF16),
        compiler_params=_cparams(("arbitrary",)),
        name="expert_mlp",
    )(block_e, first, nxt, parity, n_used, u_sorted, w_gate, w_up, w_down)


def _combine_kernel(x_ref, y0_ref, y1_ref, r_ref, mod_ref, lg_ref, lb_ref, o_ref, *, alpha):
    slab = r_ref[...]
    g1 = slab[:, 2:3]
    g2 = slab[:, 3:4]
    mx = g1 * y0_ref[...].astype(F32) + g2 * y1_ref[...].astype(F32)
    gate_f = mod_ref[0, 5:6, :]
    o_ref[...] = _layer_norm(alpha * x_ref[...] + gate_f * mx, lg_ref[...], lb_ref[...])


def _combine_call(xs, y0, y1, route, mod, ln_g, ln_b, mod_row, alpha):
    n, d = xs.shape
    tm = ROW_TM
    return pl.pallas_call(
        functools.partial(_combine_kernel, alpha=alpha),
        grid=(n // tm,),
        in_specs=[pl.BlockSpec((tm, d), lambda i: (i, 0)),
                  pl.BlockSpec((tm, d), lambda i: (i, 0)),
                  pl.BlockSpec((tm, d), lambda i: (i, 0)),
                  pl.BlockSpec((tm, LANES), lambda i: (i, 0)),
                  pl.BlockSpec((1, 6, d), lambda i: (mod_row(i, tm), 0, 0)),
                  pl.BlockSpec((1, d), lambda i: (0, 0)),
                  pl.BlockSpec((1, d), lambda i: (0, 0))],
        out_specs=pl.BlockSpec((tm, d), lambda i: (i, 0)),
        out_shape=jax.ShapeDtypeStruct((n, d), F32),
        compiler_params=_cparams(("parallel",)),
        name="moe_combine_ln",
    )(xs, y0, y1, route, mod, ln_g, ln_b)


def _rope_angles(seq, dim):
    tpos = np.arange(seq)
    row = (tpos // GRID_W).astype(np.float32)
    col = (tpos % GRID_W).astype(np.float32)
    quarter = dim // 4
    inv_freq = jnp.asarray(ROPE_THETA, F32) ** (-jnp.arange(quarter, dtype=F32) / quarter)
    ang_r = jnp.asarray(row)[:, None] * inv_freq
    ang_c = jnp.asarray(col)[:, None] * inv_freq
    return jnp.concatenate([ang_r, ang_r, ang_c, ang_c], axis=-1)


def _rot_sign(dim):
    l = np.arange(dim)
    return np.where((l & (dim // 4)) == 0, -1.0, 1.0).astype(np.float32), l ^ (dim // 4)


def _rope_tables(seq, pad_rows):
    ang_a = _rope_angles(seq, HEAD_DIM)
    sign_a, _ = _rot_sign(HEAD_DIM)
    cos_a = jnp.concatenate([jnp.cos(ang_a), jnp.ones((pad_rows, HEAD_DIM), F32)], 0)
    sin_a = jnp.concatenate([jnp.sin(ang_a) * sign_a, jnp.zeros((pad_rows, HEAD_DIM), F32)], 0)
    ang_b = _rope_angles(seq, MLA_ROPE)
    cos_b = jnp.concatenate([jnp.cos(ang_b), jnp.ones((pad_rows, MLA_ROPE), F32)], 0)
    sin_b = jnp.concatenate([jnp.sin(ang_b), jnp.zeros((pad_rows, MLA_ROPE), F32)], 0)
    return cos_a, sin_a, jnp.tile(cos_b, (1, 2)), jnp.tile(sin_b, (1, 2))


def _permute_w_in(w):
    o = np.cumsum((0, 768, 256, 256, 512, 256, 64, 640, 640, 640, 6144))
    aq, ak, av, bq, bkv, bkr, cq, ck, cv, gate = (w[..., o[i]:o[i + 1]] for i in range(10))
    bkr_rot = _rot_cols(bkr)
    cq = cq * (1.0 / math.sqrt(HEAD_DIM))
    pieces = (aq, ak, bq, bkv, bkr, bkr_rot, av, cq, ck, cv, gate)
    out = jnp.concatenate([piece.astype(BF16) for piece in pieces], axis=-1)
    assert out.shape[-1] == PROJ_W
    return out


def _rot_cols(w):
    dim = w.shape[-1]
    wr = w.reshape(w.shape[:-1] + (2, 2, dim // 4))
    return jnp.stack([-wr[..., 1, :], wr[..., 0, :]], axis=-2).reshape(w.shape)


def _permute_w_uq(w):
    r = w.shape[0]
    w3 = w.reshape(r, MLA_HEADS, MLA_QK)
    nope = w3[:, :, :MLA_NOPE].reshape(r, MLA_HEADS * MLA_NOPE)
    rope = w3[:, :, MLA_NOPE:]
    rot = _rot_cols(rope)
    zpad = jnp.zeros((r, MLA_HEADS, LANES - MLA_ROPE), w.dtype)
    rope_p = jnp.concatenate([rope, zpad], -1).reshape(r, MLA_HEADS * LANES)
    rot_p = jnp.concatenate([rot, zpad], -1).reshape(r, MLA_HEADS * LANES)
    return jnp.concatenate([nope, rope_p, rot_p], axis=1).astype(BF16)


def _permute_w_ukv(w):
    r = w.shape[0]
    w3 = w.reshape(r, MLA_HEADS, MLA_NOPE + MLA_V)
    kn = w3[:, :, :MLA_NOPE].reshape(r, MLA_HEADS * MLA_NOPE)
    vv = w3[:, :, MLA_NOPE:].reshape(r, MLA_HEADS * MLA_V)
    return jnp.concatenate([kn, vv], axis=1).astype(BF16)


def _na_bias_tables(rpb, seq):
    w, kh, kw = GRID_W, NA_KH, NA_KW
    rows = seq // w
    nh, n_dr, n_dc = rpb.shape
    line = jnp.full((nh, n_dr, 2 * w), NEG_BIG, F32).at[:, :, w - kw:w - kw + n_dc].set(rpb.astype(F32))
    skew = jnp.broadcast_to(line[:, :, None, :], (nh, n_dr, w, 2 * w)).reshape(nh, n_dr, 2 * w * w)
    skew = skew[:, :, :w * (2 * w - 1)].reshape(nh, n_dr, w, 2 * w - 1)
    tc = skew[:, :, :, w - 1:2 * w - 1]
    cq = np.arange(w)[:, None]
    ck = np.arange(w)[None, :]
    cs = np.clip(cq - kw // 2, 0, w - kw)
    col_ok = (ck >= cs) & (ck < cs + kw)
    tc = jnp.where(col_ok[None, None], tc, NEG_BIG)
    tc = jnp.concatenate([tc, jnp.full((nh, 1, w, w), NEG_BIG, F32)], axis=1)
    tc2 = jnp.concatenate([tc, tc], axis=-1)
    blks = []
    for r0 in (0, NA_ROWS, rows - NA_ROWS):
        rb = int(np.clip(r0 - kh // 2, 0, rows - NA_BAND))
        rq = r0 + np.arange(NA_ROWS)[:, None]
        rk = rb + np.arange(NA_BAND)[None, :]
        rs = np.clip(rq - kh // 2, 0, rows - kh)
        row_ok = (rk >= rs) & (rk < rs + kh)
        blks.append(np.where(row_ok, rk - rq + kh - 1, n_dr).tolist())
    return pl.pallas_call(
        functools.partial(_na_bias_kernel, blks=blks),
        grid=(nh,),
        in_specs=[pl.BlockSpec((1, n_dr + 1, w, 2 * w), lambda h: (h, 0, 0, 0))],
        out_specs=pl.BlockSpec((1, len(blks), NA_ROWS * w, NA_BAND * w), lambda h: (h, 0, 0, 0)),
        out_shape=jax.ShapeDtypeStruct((nh, len(blks), NA_ROWS * w, NA_BAND * w), F32),
        compiler_params=_cparams(("parallel",)),
        name="na_bias_table",
    )(tc2)


def _na_bias_kernel(tc_ref, o_ref, *, blks):
    w = GRID_W
    low = lax.broadcasted_iota(jnp.int32, (w, 2 * w), 1) < w
    for v, blk in enumerate(blks):
        for rq, row in enumerate(blk):
            for pr in range(len(row) // 2):
                pair = jnp.where(low, tc_ref[0, row[2 * pr]], tc_ref[0, row[2 * pr + 1]])
                o_ref[0, v, rq * w:(rq + 1) * w, pr * 2 * w:(pr + 1) * 2 * w] = pair


def kernel(x, c, ctx, c_ctx, w_ada, b_ada, w_in, gqa_q_norm, gqa_k_norm, mla_q_norm, mla_kv_norm, mla_w_uq, mla_w_ukv, na_rpb, w_branch_a, w_branch_b, w_branch_c, w_out, ln1_g, ln1_b, w_router_group, b_router_group, w_router_expert, b_router_expert, w_expert_gate, w_expert_up, w_expert_down, ln2_g, ln2_b):
    bsz, seq, d = x.shape
    n_ctx = ctx.shape[1]
    depth = w_ada.shape[0]
    nx, nc = bsz * seq, bsz * n_ctx
    t = nx + nc
    assert seq % PROJ_TM == 0 and nc == PROJ_TM and seq % (NA_BAND * GRID_W) == 0 and nx % n_ctx == 0
    alpha = (2 * depth) ** 0.25

    def mod_row(i, tm):
        return jnp.minimum(i // (seq // tm), bsz)

    def rope_blk(i, tm):
        return jnp.where(i < nx // tm, i % (seq // tm), seq // tm)

    xs = jnp.concatenate([x.reshape(nx, d), ctx.reshape(nc, d)], axis=0)
    cc = jnp.concatenate([c, c_ctx[None], jnp.zeros((8 - bsz - 1, d), F32)], axis=0)
    mod_all = _ada_call(cc, w_ada, b_ada).reshape(depth, 8, 6, d)
    cos_a, sin_a, cos_b, sin_b = _rope_tables(seq, ROW_TM)
    w_in_p = _permute_w_in(w_in)

    attn_kw = dict(bsz=bsz, seq=seq, n_ctx=n_ctx)
    for i in range(depth):
        last = i == depth - 1
        n_rows = nx if last else t
        mod = mod_all[i]
        p = _proj_call(xs, mod, w_in_p, i, mod_row)

        qa, ka = _prep_a_call(p, cos_a, sin_a, gqa_q_norm[i][None], gqa_k_norm[i][None], rope_blk)
        a_kw = dict(n_heads=GQA_HEADS, dq=HEAD_DIM, q_col=lambda h: h, k_col=lambda h: h // GQA_GROUP,
                    v_col=lambda h: COL_AV // HEAD_DIM + h // GQA_GROUP, **attn_kw)
        o_a = _dense_attn_call(qa, ka, p, **a_kw)
        qb, kb, vb = _prep_b_call(p, cos_b, sin_b, mla_q_norm[i][None], mla_kv_norm[i][None],
                                  _permute_w_uq(mla_w_uq[i]), _permute_w_ukv(mla_w_ukv[i]), rope_blk)
        b_kw = dict(n_heads=MLA_HEADS, dq=MLA_PAD, q_col=lambda h: h, k_col=lambda h: h, v_col=lambda h: h, **attn_kw)
        o_b = _dense_attn_call(qb, kb, vb, **b_kw)
        o_c = _na_call(p, _na_bias_tables(na_rpb[i], seq), **attn_kw)
        o_lat = (o_a, o_b, o_c)
        if last:
            o_ctx = o_lat
        else:
            c_kw = dict(n_heads=NA_HEADS, dq=HEAD_DIM, q_col=lambda h: COL_CQ // HEAD_DIM + h,
                        k_col=lambda h: COL_CK // HEAD_DIM + h, v_col=lambda h: COL_CV // HEAD_DIM + h, **attn_kw)
            o_ctx = (_ctx_attn_call(qa, ka, p, **a_kw), _ctx_attn_call(qb, kb, vb, **b_kw),
                     _ctx_attn_call(p, p, p, **c_kw))

        mrg = _branch_call(o_lat, o_ctx, p, w_branch_a[i].astype(BF16), w_branch_b[i].astype(BF16),
                           w_branch_c[i].astype(BF16), n_rows)
        wr = jnp.concatenate([w_router_group[i], w_router_expert[i],
                              jnp.zeros((d, LANES - N_GROUPS - N_EXPERTS), F32)], axis=1)
        br = jnp.concatenate([b_router_group[i], b_router_expert[i],
                              jnp.zeros((LANES - N_GROUPS - N_EXPERTS,), F32)])[None]
        wr_cat = jnp.concatenate(_split_bf16(wr), axis=1)
        xs, u_f, route = _out_call(mrg, xs, mod, w_out[i].astype(BF16), ln1_g[i][None], ln1_b[i][None],
                                   wr_cat, br, mod_row, n_rows, alpha)

        rank_slab, cnt = _rank_call(route)
        eid = route[:, :TOP_K].astype(jnp.int32)
        counts = cnt[0, :N_EXPERTS].astype(jnp.int32)
        padded = ((counts + MOE_BM - 1) // MOE_BM) * MOE_BM
        pends = jnp.cumsum(padded)
        pstarts = pends - padded
        pstarts_row = jnp.concatenate([pstarts.astype(F32), jnp.zeros((LANES - N_EXPERTS,), F32)])[None]
        dest = _dest_call(route, rank_slab, pstarts_row)[:, :TOP_K].astype(jnp.int32)
        nb = -(-(n_rows * TOP_K) // MOE_BM) + N_EXPERTS
        n_used = (pends[-1] // MOE_BM).astype(jnp.int32)
        blk = jnp.minimum(jnp.arange(nb, dtype=jnp.int32), n_used - 1)
        block_e = jnp.clip(jnp.searchsorted(pends, blk * MOE_BM, side="right"), 0, N_EXPERTS - 1).astype(jnp.int32)
        order = jnp.argsort(eid.reshape(-1), stable=True).astype(jnp.int32)
        starts = jnp.cumsum(counts) - counts
        slot = jnp.arange(nb * MOE_BM, dtype=jnp.int32).reshape(nb, MOE_BM)
        blk_shift = (starts - pstarts)[block_e][:, None]
        blk_limit = (pstarts + counts)[block_e][:, None]
        pos = jnp.clip(slot + blk_shift, 0, n_rows * TOP_K - 1)
        slot_tok = jnp.where(slot < blk_limit, order.at[pos].get(mode="promise_in_bounds") // TOP_K,
                             slot % n_rows).reshape(-1)
        u_sorted = u_f.at[slot_tok].get(mode="promise_in_bounds")
        jj = jnp.arange(nb, dtype=jnp.int32)
        first = ((jj == 0) | (block_e != jnp.roll(block_e, 1))).astype(jnp.int32)
        parity = (jnp.cumsum(first) - 1) % 2
        live = jnp.where(counts > 0, jnp.arange(N_EXPERTS, dtype=jnp.int32), N_EXPERTS)
        next_live = jnp.concatenate([lax.cummin(live, reverse=True)[1:], jnp.full((1,), N_EXPERTS, jnp.int32)])
        nxt = jnp.where(next_live < N_EXPERTS, next_live, -1)[block_e]
        y_sorted = _moe_call(block_e, first, nxt.astype(jnp.int32), parity.astype(jnp.int32), n_used[None],
                             u_sorted, w_expert_gate, w_expert_up, w_expert_down, i)
        y0 = y_sorted.at[dest[:, 0]].get(mode="promise_in_bounds")
        y1 = y_sorted.at[dest[:, 1]].get(mode="promise_in_bounds")
        xs = _combine_call(xs, y0, y1, route, mod, ln2_g[i][None], ln2_b[i][None], mod_row, alpha)
    return xs.reshape(bsz, seq, d)
```

```python
import functools
import math

import jax
import jax.numpy as jnp
import numpy as np
from jax import lax
from jax.experimental import pallas as pl
from jax.experimental.pallas import tpu as pltpu

F32 = jnp.float32
BF16 = jnp.bfloat16

HEAD_DIM = 128
GRID_W = 64
ROPE_THETA = 10000.0
GQA_HEADS, GQA_KV_HEADS = 6, 2
GQA_GROUP = GQA_HEADS // GQA_KV_HEADS
MLA_HEADS, MLA_Q_RANK, MLA_KV_RANK = 5, 512, 256
MLA_NOPE, MLA_ROPE, MLA_V = 128, 64, 128
MLA_QK = MLA_NOPE + MLA_ROPE
MLA_PAD = 256
NA_HEADS, NA_KH, NA_KW = 5, 8, 16
N_GROUPS, EXPERTS_PER_GROUP, TOP_K, D_EXPERT = 8, 8, 2, 512
N_EXPERTS = N_GROUPS * EXPERTS_PER_GROUP
LN_EPS = 1e-6
RMS_EPS = 1e-6
NEG_BIG = -1e30
SCORE_LOG2E = math.log2(math.e)

LANES = 128
VMEM_LIMIT = 56 * 1024 * 1024
PROJ_TM, PROJ_TN = 1024, 1024
ROW_TM = 512
MM_TM = 512
ATTN_TQ = 512
ATTN_SUB = 2
ATTN_KC = 1024
NA_ROWS = 8
NA_BAND = 16
MOE_BM = 256
ADA_TN = 512

COL_AQ, COL_AK = 0, 768
COL_BQ, COL_BKV, COL_BKR = 1024, 1536, 1792
COL_AV, COL_CQ, COL_CK, COL_CV = 1920, 2176, 2816, 3456
COL_GATE = 4096
PROJ_W = 10240


def _cparams(sem):
    return pltpu.CompilerParams(dimension_semantics=sem, vmem_limit_bytes=VMEM_LIMIT)


def _resident(shape):
    return pl.BlockSpec(shape, lambda *_: (0,) * len(shape), pipeline_mode=pl.Buffered(1))


def _dot(a, b):
    return jnp.dot(a, b, preferred_element_type=F32)


def _dot_nt(a, b):
    return lax.dot_general(a, b, (((1,), (1,)), ((), ())), preferred_element_type=F32)


def _split_bf16(a):
    hi = a.astype(BF16)
    lo = (a - hi.astype(F32)).astype(BF16)
    return hi, lo


def _ada_kernel(c_ref, w_ref, b_ref, o_ref):
    a = c_ref[...]
    a = a * jax.nn.sigmoid(a)
    a_hi, a_lo = _split_bf16(a)
    w_hi, w_lo = _split_bf16(w_ref[0])
    acc = _dot(a_hi, w_hi) + _dot(a_lo, w_hi) + _dot(a_hi, w_lo)
    o_ref[0] = acc + b_ref[0]


def _ada_call(cc, w_ada, b_ada):
    depth, d, n = w_ada.shape
    rows = cc.shape[0]
    return pl.pallas_call(
        _ada_kernel,
        grid=(depth, n // ADA_TN),
        in_specs=[pl.BlockSpec((rows, d), lambda l, j: (0, 0)),
                  pl.BlockSpec((1, d, ADA_TN), lambda l, j: (l, 0, j)),
                  pl.BlockSpec((1, 1, ADA_TN), lambda l, j: (l, 0, j))],
        out_specs=pl.BlockSpec((1, rows, ADA_TN), lambda l, j: (l, 0, j)),
        out_shape=jax.ShapeDtypeStruct((depth, rows, n), F32),
        compiler_params=_cparams(("parallel", "parallel")),
        name="ada_modulation",
    )(cc, w_ada, b_ada.reshape(depth, 1, n))


def _proj_kernel(xl_ref, xc_ref, mod_ref, w_ref, o_ref, u_scr, *, n_lat_tiles):
    @pl.when(pl.program_id(1) == 0)
    def _():
        def modulate(x):
            u_scr[...] = (x * (1.0 + mod_ref[0, 1:2, :]) + mod_ref[0, 0:1, :]).astype(BF16)

        @pl.when(pl.program_id(0) < n_lat_tiles)
        def _():
            modulate(xl_ref[...])

        @pl.when(pl.program_id(0) >= n_lat_tiles)
        def _():
            modulate(xc_ref[...])

    o_ref[...] = _dot(u_scr[...], w_ref[0]).astype(BF16)


def _stream_specs(stream, tm, n_lat):
    lat, ctx, ctx_row0 = stream
    d = lat.shape[1]
    n_lat_tiles = n_lat // tm
    ctx_tile0 = ctx_row0 // tm

    def lat_map(i, *_):
        return (jnp.minimum(i, n_lat_tiles - 1), 0)

    def ctx_map(i, *_):
        return (ctx_tile0 + jnp.maximum(i - n_lat_tiles, 0), 0)

    return [pl.BlockSpec((tm, d), lat_map), pl.BlockSpec((tm, d), ctx_map, pipeline_mode=pl.Buffered(1))]


def _proj_call(stream, n_lat, n_tok, mod, w_in_p, layer, mod_row):
    d = stream[0].shape[1]
    return pl.pallas_call(
        functools.partial(_proj_kernel, n_lat_tiles=n_lat // PROJ_TM),
        grid=(n_tok // PROJ_TM, PROJ_W // PROJ_TN),
        in_specs=_stream_specs(stream, PROJ_TM, n_lat) + [
            pl.BlockSpec((1, 6, d), lambda i, j: (mod_row(i, PROJ_TM), 0, 0)),
            pl.BlockSpec((1, d, PROJ_TN), lambda i, j: (layer, 0, j))],
        out_specs=pl.BlockSpec((PROJ_TM, PROJ_TN), lambda i, j: (i, j)),
        out_shape=jax.ShapeDtypeStruct((n_tok, PROJ_W), BF16),
        scratch_shapes=[pltpu.VMEM((PROJ_TM, d), BF16)],
        compiler_params=_cparams(("parallel", "arbitrary")),
        name="input_projection",
    )(stream[0], stream[1], mod, w_in_p)


def _rope128(y, cos, sin_signed, first_quarter):
    rot = jnp.where(first_quarter, pltpu.roll(y, 96, 1), pltpu.roll(y, 32, 1))
    return y * cos + rot * sin_signed


def _prep_a_kernel(p_ref, cos_ref, sin_ref, gq_ref, gk_ref, q_ref, k_ref, *, scale):
    cos = cos_ref[...]
    sin_signed = sin_ref[...]
    lane = lax.broadcasted_iota(jnp.int32, cos.shape, 1)
    first_quarter = (lane & 32) == 0
    for h in range(GQA_HEADS + GQA_KV_HEADS):
        xh = p_ref[:, h * HEAD_DIM:(h + 1) * HEAD_DIM].astype(F32)
        r = lax.rsqrt(jnp.mean(xh * xh, axis=1, keepdims=True) + RMS_EPS)
        if h < GQA_HEADS:
            y = _rope128(xh * r * gq_ref[...], cos, sin_signed, first_quarter) * scale
            q_ref[:, h * HEAD_DIM:(h + 1) * HEAD_DIM] = y.astype(BF16)
        else:
            hk = h - GQA_HEADS
            y = _rope128(xh * r * gk_ref[...], cos, sin_signed, first_quarter)
            k_ref[:, hk * HEAD_DIM:(hk + 1) * HEAD_DIM] = y.astype(BF16)


def _prep_a_call(p, cos_t, sin_t, gq, gk, rope_blk):
    t = p.shape[0]
    tm = ROW_TM
    wq, wk = GQA_HEADS * HEAD_DIM, GQA_KV_HEADS * HEAD_DIM
    return pl.pallas_call(
        functools.partial(_prep_a_kernel, scale=SCORE_LOG2E / math.sqrt(HEAD_DIM)),
        grid=(t // tm,),
        in_specs=[pl.BlockSpec((tm, wq + wk), lambda i: (i, 0)),
                  pl.BlockSpec((tm, HEAD_DIM), lambda i: (rope_blk(i, tm), 0)),
                  pl.BlockSpec((tm, HEAD_DIM), lambda i: (rope_blk(i, tm), 0)),
                  pl.BlockSpec((1, HEAD_DIM), lambda i: (0, 0)),
                  pl.BlockSpec((1, HEAD_DIM), lambda i: (0, 0))],
        out_specs=[pl.BlockSpec((tm, wq), lambda i: (i, 0)),
                   pl.BlockSpec((tm, wk), lambda i: (i, 0))],
        out_shape=[jax.ShapeDtypeStruct((t, wq), BF16), jax.ShapeDtypeStruct((t, wk), BF16)],
        compiler_params=_cparams(("parallel",)),
        name="gqa_qk_prep",
    )(p, cos_t, sin_t, gq, gk)


def _prep_b_kernel(ql_ref, kvl_ref, kr_ref, cos_ref, sin_ref, gq_ref, gkv_ref, wq_ref, wkv_ref,
                   q_ref, k_ref, v_ref, *, scale):
    nh, hp = MLA_HEADS, MLA_PAD
    wn = nh * MLA_NOPE
    cos = cos_ref[...]
    sin = sin_ref[...]
    ql = ql_ref[...].astype(F32)
    ql = ql * lax.rsqrt(jnp.mean(ql * ql, axis=1, keepdims=True) + RMS_EPS) * gq_ref[...]
    qf = _dot(ql.astype(BF16), wq_ref[...])
    kvl = kvl_ref[...].astype(F32)
    kvl = kvl * lax.rsqrt(jnp.mean(kvl * kvl, axis=1, keepdims=True) + RMS_EPS) * gkv_ref[...]
    kvf = _dot(kvl.astype(BF16), wkv_ref[...])
    lane = lax.broadcasted_iota(jnp.int32, cos.shape, 1)
    low = lane < MLA_ROPE
    tk = kr_ref[...].astype(F32) * jnp.where(low, cos, sin)
    kr = jnp.where(low, tk + pltpu.roll(tk, MLA_ROPE, 1), 0.0).astype(BF16)
    for h in range(nh):
        q_ref[:, h * hp:h * hp + LANES] = (qf[:, h * LANES:(h + 1) * LANES] * scale).astype(BF16)
        qr = qf[:, wn + h * LANES:wn + (h + 1) * LANES] * cos + qf[:, 2 * wn + h * LANES:2 * wn + (h + 1) * LANES] * sin
        q_ref[:, h * hp + LANES:(h + 1) * hp] = (qr * scale).astype(BF16)
        k_ref[:, h * hp:h * hp + LANES] = kvf[:, h * LANES:(h + 1) * LANES].astype(BF16)
        k_ref[:, h * hp + LANES:(h + 1) * hp] = kr
    v_ref[...] = kvf[:, wn:].astype(BF16)


def _prep_b_call(p, cos_t, sin_t, gq, gkv, wq, wkv, rope_blk):
    t = p.shape[0]
    tm = ROW_TM
    nh = MLA_HEADS
    return pl.pallas_call(
        functools.partial(_prep_b_kernel, scale=SCORE_LOG2E / math.sqrt(MLA_QK)),
        grid=(t // tm,),
        in_specs=[pl.BlockSpec((tm, MLA_Q_RANK), lambda i: (i, COL_BQ // MLA_Q_RANK)),
                  pl.BlockSpec((tm, MLA_KV_RANK), lambda i: (i, COL_BKV // MLA_KV_RANK)),
                  pl.BlockSpec((tm, LANES), lambda i: (i, COL_BKR // LANES)),
                  pl.BlockSpec((tm, LANES), lambda i: (rope_blk(i, tm), 0)),
                  pl.BlockSpec((tm, LANES), lambda i: (rope_blk(i, tm), 0)),
                  pl.BlockSpec((1, MLA_Q_RANK), lambda i: (0, 0)),
                  pl.BlockSpec((1, MLA_KV_RANK), lambda i: (0, 0)),
                  _resident(wq.shape), _resident(wkv.shape)],
        out_specs=[pl.BlockSpec((tm, nh * MLA_PAD), lambda i: (i, 0)),
                   pl.BlockSpec((tm, nh * MLA_PAD), lambda i: (i, 0)),
                   pl.BlockSpec((tm, nh * MLA_V), lambda i: (i, 0))],
        out_shape=[jax.ShapeDtypeStruct((t, nh * MLA_PAD), BF16),
                   jax.ShapeDtypeStruct((t, nh * MLA_PAD), BF16),
                   jax.ShapeDtypeStruct((t, nh * MLA_V), BF16)],
        compiler_params=_cparams(("parallel",)),
        name="mla_prep",
    )(p, p, p, cos_t, sin_t, gq, gkv, wq, wkv)


def _online_attend(q, chunks):
    m = z = acc = None
    for k, v, bias in chunks:
        s = _dot_nt(q, k)
        if bias is not None:
            s = s + bias
        cm = jnp.max(s, axis=1, keepdims=True)
        if m is None:
            m = cm
            p = jnp.exp2(s - m)
            z = jnp.sum(p, axis=1, keepdims=True)
            acc = _dot(p.astype(BF16), v)
        else:
            m_new = jnp.maximum(m, cm)
            corr = jnp.exp2(m - m_new)
            p = jnp.exp2(s - m_new)
            z = z * corr + jnp.sum(p, axis=1, keepdims=True)
            acc = acc * corr + _dot(p.astype(BF16), v)
            m = m_new
    return acc / z


def _dense_attn_kernel(q_ref, kl_ref, vl_ref, kc_ref, vc_ref, o_ref):
    n_lat = kl_ref.shape[0]
    chunks = [(kl_ref[c:c + ATTN_KC, :], vl_ref[c:c + ATTN_KC, :], None) for c in range(0, n_lat, ATTN_KC)]
    chunks.append((kc_ref[...], vc_ref[...], None))
    for r in range(0, q_ref.shape[0], ATTN_TQ):
        o_ref[r:r + ATTN_TQ, :] = _online_attend(q_ref[r:r + ATTN_TQ, :], chunks).astype(o_ref.dtype)


def _dense_attn_call(q, k, v, *, n_heads, dq, q_col, k_col, v_col, bsz, seq, n_ctx):
    tq = ATTN_TQ * ATTN_SUB
    tpb = seq // tq
    ctx0 = bsz * seq // n_ctx
    dv = HEAD_DIM
    return pl.pallas_call(
        _dense_attn_kernel,
        grid=(bsz, n_heads, tpb),
        in_specs=[pl.BlockSpec((tq, dq), lambda b, h, i: (b * tpb + i, q_col(h))),
                  pl.BlockSpec((seq, dq), lambda b, h, i: (b, k_col(h))),
                  pl.BlockSpec((seq, dv), lambda b, h, i: (b, v_col(h))),
                  pl.BlockSpec((n_ctx, dq), lambda b, h, i: (ctx0 + b, k_col(h))),
                  pl.BlockSpec((n_ctx, dv), lambda b, h, i: (ctx0 + b, v_col(h)))],
        out_specs=pl.BlockSpec((tq, dv), lambda b, h, i: (b * tpb + i, h)),
        out_shape=jax.ShapeDtypeStruct((bsz * seq, n_heads * dv), BF16),
        compiler_params=_cparams(("parallel", "parallel", "arbitrary")),
        name="dense_attention",
    )(q, k, v, k, v)


def _ctx_attn_kernel(q_ref, kc_ref, vc_ref, o_ref):
    o_ref[...] = _online_attend(q_ref[...], [(kc_ref[...], vc_ref[...], None)]).astype(o_ref.dtype)


def _ctx_attn_call(q, k, v, *, n_heads, dq, q_col, k_col, v_col, bsz, seq, n_ctx):
    ctx0 = bsz * seq // n_ctx
    dv = HEAD_DIM
    return pl.pallas_call(
        _ctx_attn_kernel,
        grid=(bsz, n_heads),
        in_specs=[pl.BlockSpec((n_ctx, dq), lambda b, h: (ctx0 + b, q_col(h))),
                  pl.BlockSpec((n_ctx, dq), lambda b, h: (ctx0 + b, k_col(h))),
                  pl.BlockSpec((n_ctx, dv), lambda b, h: (ctx0 + b, v_col(h)))],
        out_specs=pl.BlockSpec((n_ctx, dv), lambda b, h: (b, h)),
        out_shape=jax.ShapeDtypeStruct((bsz * n_ctx, n_heads * dv), BF16),
        compiler_params=_cparams(("parallel", "parallel")),
        name="context_attention",
    )(q, k, v)


def _na_kernel(q_ref, k_ref, v_ref, kc_ref, vc_ref, bias_ref, o_ref, *, rows_total):
    i = pl.program_id(2)
    band = NA_BAND * GRID_W
    start_blk = jnp.clip(2 * i - 1, 0, (rows_total - NA_BAND) // 4)
    start = pl.multiple_of(start_blk * (4 * GRID_W), 4 * GRID_W)
    kb = k_ref[pl.ds(start, band), :]
    vb = v_ref[pl.ds(start, band), :]
    chunks = [(kb, vb, bias_ref[0, 0]), (kc_ref[...], vc_ref[...], None)]
    o_ref[...] = _online_attend(q_ref[...], chunks).astype(o_ref.dtype)


def _na_call(p, bias_tab, *, bsz, seq, n_ctx):
    tq = NA_ROWS * GRID_W
    tpb = seq // tq
    rows_total = seq // GRID_W
    ctx0 = bsz * seq // n_ctx
    d = HEAD_DIM
    qc, kc, vc = COL_CQ // d, COL_CK // d, COL_CV // d

    def variant(i):
        return jnp.where(i == 0, 0, jnp.where(i == tpb - 1, 2, 1))

    return pl.pallas_call(
        functools.partial(_na_kernel, rows_total=rows_total),
        grid=(bsz, NA_HEADS, tpb),
        in_specs=[pl.BlockSpec((tq, d), lambda b, h, i: (b * tpb + i, qc + h)),
                  pl.BlockSpec((seq, d), lambda b, h, i: (b, kc + h)),
                  pl.BlockSpec((seq, d), lambda b, h, i: (b, vc + h)),
                  pl.BlockSpec((n_ctx, d), lambda b, h, i: (ctx0 + b, kc + h)),
                  pl.BlockSpec((n_ctx, d), lambda b, h, i: (ctx0 + b, vc + h)),
                  pl.BlockSpec((1, 1, tq, NA_BAND * GRID_W), lambda b, h, i: (h, variant(i), 0, 0))],
        out_specs=pl.BlockSpec((tq, d), lambda b, h, i: (b * tpb + i, h)),
        out_shape=jax.ShapeDtypeStruct((bsz * seq, NA_HEADS * d), BF16),
        compiler_params=_cparams(("parallel", "parallel", "arbitrary")),
        name="neighbourhood_attention",
    )(p, p, p, p, p, bias_tab)


def _branch_kernel(oa_ref, ob_ref, oc_ref, oac_ref, obc_ref, occ_ref, ga_ref, gb_ref, gc_ref,
                   wa_ref, wb_ref, wc_ref, o_ref, *, n_lat_tiles):
    def gated(g_ref, w_ref, o_tile):
        return jax.nn.sigmoid(g_ref[...].astype(F32)) * _dot(o_tile, w_ref[...])

    @pl.when(pl.program_id(0) < n_lat_tiles)
    def _():
        acc = gated(ga_ref, wa_ref, oa_ref[...]) + gated(gb_ref, wb_ref, ob_ref[...]) + gated(gc_ref, wc_ref, oc_ref[...])
        o_ref[...] = acc.astype(BF16)

    @pl.when(pl.program_id(0) >= n_lat_tiles)
    def _():
        acc = gated(ga_ref, wa_ref, oac_ref[...]) + gated(gb_ref, wb_ref, obc_ref[...]) + gated(gc_ref, wc_ref, occ_ref[...])
        o_ref[...] = acc.astype(BF16)


def _branch_call(o_lat, o_ctx, p, wa, wb, wc, n_rows):
    d = wa.shape[1]
    tm = MM_TM
    g0 = COL_GATE // d
    n_lat_tiles = o_lat[0].shape[0] // tm
    lat_specs = [pl.BlockSpec((tm, o.shape[1]), lambda i: (jnp.minimum(i, n_lat_tiles - 1), 0)) for o in o_lat]
    ctx_specs = [pl.BlockSpec((tm, o.shape[1]), lambda i: (jnp.maximum(i - n_lat_tiles, 0), 0)) for o in o_ctx]
    return pl.pallas_call(
        functools.partial(_branch_kernel, n_lat_tiles=n_lat_tiles),
        grid=(n_rows // tm,),
        in_specs=lat_specs + ctx_specs + [
            pl.BlockSpec((tm, d), lambda i: (i, g0)),
            pl.BlockSpec((tm, d), lambda i: (i, g0 + 1)),
            pl.BlockSpec((tm, d), lambda i: (i, g0 + 2)),
            _resident(wa.shape), _resident(wb.shape), _resident(wc.shape)],
        out_specs=pl.BlockSpec((tm, d), lambda i: (i, 0)),
        out_shape=jax.ShapeDtypeStruct((n_rows, d), BF16),
        compiler_params=_cparams(("parallel",)),
        name="branch_merge",
    )(*o_lat, *o_ctx, p, p, p, wa, wb, wc)


def _layer_norm(z, g, b):
    mu = jnp.mean(z, axis=1, keepdims=True)
    zc = z - mu
    var = jnp.mean(zc * zc, axis=1, keepdims=True)
    return zc * lax.rsqrt(var + LN_EPS) * g + b


def _route(logits):
    lane = lax.broadcasted_iota(jnp.int32, logits.shape, 1)
    lane_f = lane.astype(F32)
    is_g = lane < N_GROUPS
    gl = jnp.where(is_g, logits, NEG_BIG)
    mg = jnp.max(gl, axis=1, keepdims=True)
    gsel = jnp.min(jnp.where(gl == mg, lane_f, float(LANES)), axis=1, keepdims=True)
    zg = jnp.sum(jnp.where(is_g, jnp.exp(gl - mg), 0.0), axis=1, keepdims=True)
    lo = N_GROUPS + gsel * EXPERTS_PER_GROUP
    is_e = (lane_f >= lo) & (lane_f < lo + EXPERTS_PER_GROUP)
    el = jnp.where(is_e, logits, NEG_BIG)
    t1 = jnp.max(el, axis=1, keepdims=True)
    i1 = jnp.min(jnp.where(el == t1, lane_f, float(LANES)), axis=1, keepdims=True)
    el2 = jnp.where(lane_f == i1, NEG_BIG, el)
    t2 = jnp.max(el2, axis=1, keepdims=True)
    i2 = jnp.min(jnp.where(el2 == t2, lane_f, float(LANES)), axis=1, keepdims=True)
    dd = jnp.exp(t2 - t1)
    g1 = 1.0 / (zg * (1.0 + dd))
    g2 = g1 * dd
    slab = jnp.where(lane == 0, i1 - N_GROUPS,
                     jnp.where(lane == 1, i2 - N_GROUPS,
                               jnp.where(lane == 2, g1, jnp.where(lane == 3, g2, 0.0))))
    return slab


def _out_kernel(m_ref, xl_ref, xc_ref, mod_ref, w_ref, lg_ref, lb_ref, wr_ref, br_ref,
                xo_ref, u_ref, r_ref, *, alpha, n_lat_tiles):
    @pl.when(pl.program_id(0) < n_lat_tiles)
    def _():
        xo_ref[...] = xl_ref[...]

    @pl.when(pl.program_id(0) >= n_lat_tiles)
    def _():
        xo_ref[...] = xc_ref[...]

    y = _dot(m_ref[...], w_ref[...])
    gate_m = mod_ref[0, 2:3, :]
    shift_f = mod_ref[0, 3:4, :]
    scale_f = mod_ref[0, 4:5, :]
    xn = _layer_norm(alpha * xo_ref[...] + gate_m * y, lg_ref[...], lb_ref[...])
    xo_ref[...] = xn
    u = xn * (1.0 + scale_f) + shift_f
    u_hi, u_lo = _split_bf16(u)
    u_ref[...] = u_hi
    both = _dot(u_hi, wr_ref[...])
    logits = both[:, :LANES] + both[:, LANES:] + _dot(u_lo, wr_ref[:, :LANES]) + br_ref[...]
    r_ref[...] = _route(logits)


def _out_call(mrg, stream, n_lat, mod, w_out, ln_g, ln_b, wr_cat, br, mod_row, n_rows, alpha):
    d = stream[0].shape[1]
    tm = MM_TM
    return pl.pallas_call(
        functools.partial(_out_kernel, alpha=alpha, n_lat_tiles=n_lat // tm),
        grid=(n_rows // tm,),
        in_specs=[pl.BlockSpec((tm, d), lambda i: (i, 0))] + _stream_specs(stream, tm, n_lat) + [
                  pl.BlockSpec((1, 6, d), lambda i: (mod_row(i, tm), 0, 0)),
                  _resident((d, d)), _resident((1, d)), _resident((1, d)),
                  _resident((d, 2 * LANES)), _resident((1, LANES))],
        out_specs=[pl.BlockSpec((tm, d), lambda i: (i, 0)),
                   pl.BlockSpec((tm, d), lambda i: (i, 0)),
                   pl.BlockSpec((tm, LANES), lambda i: (i, 0))],
        out_shape=[jax.ShapeDtypeStruct((n_rows, d), F32),
                   jax.ShapeDtypeStruct((n_rows, d), BF16),
                   jax.ShapeDtypeStruct((n_rows, LANES), F32)],
        compiler_params=_cparams(("parallel",)),
        name="out_proj_ln_route",
    )(mrg, stream[0], stream[1], mod, w_out, ln_g, ln_b, wr_cat, br)


def _rank_kernel(r_ref, rank_ref, cnt_ref, carry):
    @pl.when(pl.program_id(0) == 0)
    def _():
        carry[...] = jnp.zeros_like(carry)

    slab = r_ref[...]
    tm = slab.shape[0]
    lane = lax.broadcasted_iota(jnp.int32, slab.shape, 1)
    lane_f = lane.astype(F32)
    e1 = slab[:, 0:1]
    e2 = slab[:, 1:2]
    hit1 = lane_f == e1
    hit2 = lane_f == e2
    onehot = jnp.where(hit1 | hit2, 1.0, 0.0)
    row = lax.broadcasted_iota(jnp.int32, (tm, tm), 0)
    col = lax.broadcasted_iota(jnp.int32, (tm, tm), 1)
    lower = jnp.where(col < row, 1.0, 0.0).astype(BF16)
    before = _dot(lower, onehot.astype(BF16)) + carry[0:1, :]
    r1 = jnp.sum(jnp.where(hit1, before, 0.0), axis=1, keepdims=True)
    r2 = jnp.sum(jnp.where(hit2, before, 0.0), axis=1, keepdims=True)
    rank_ref[...] = jnp.where(lane == 0, r1, jnp.where(lane == 1, r2, 0.0))
    carry[...] = carry[...] + jnp.sum(onehot, axis=0, keepdims=True)
    cnt_ref[...] = carry[...]


def _rank_call(route):
    n = route.shape[0]
    tm = ROW_TM
    return pl.pallas_call(
        _rank_kernel,
        grid=(n // tm,),
        in_specs=[pl.BlockSpec((tm, LANES), lambda i: (i, 0))],
        out_specs=[pl.BlockSpec((tm, LANES), lambda i: (i, 0)),
                   pl.BlockSpec((8, LANES), lambda i: (0, 0))],
        out_shape=[jax.ShapeDtypeStruct((n, LANES), F32), jax.ShapeDtypeStruct((8, LANES), F32)],
        scratch_shapes=[pltpu.VMEM((8, LANES), F32)],
        compiler_params=_cparams(("arbitrary",)),
        name="expert_ranks",
    )(route)


def _dest_kernel(r_ref, rank_ref, ps_ref, o_ref):
    slab = r_ref[...]
    rank = rank_ref[...]
    lane = lax.broadcasted_iota(jnp.int32, slab.shape, 1)
    lane_f = lane.astype(F32)
    ps = ps_ref[...]
    s1 = jnp.sum(jnp.where(lane_f == slab[:, 0:1], ps, 0.0), axis=1, keepdims=True) + rank[:, 0:1]
    s2 = jnp.sum(jnp.where(lane_f == slab[:, 1:2], ps, 0.0), axis=1, keepdims=True) + rank[:, 1:2]
    o_ref[...] = jnp.where(lane == 0, s1, jnp.where(lane == 1, s2, 0.0))


def _dest_call(route, rank_slab, pstarts_row):
    n = route.shape[0]
    tm = n // 8
    return pl.pallas_call(
        _dest_kernel,
        grid=(n // tm,),
        in_specs=[pl.BlockSpec((tm, LANES), lambda i: (i, 0)),
                  pl.BlockSpec((tm, LANES), lambda i: (i, 0)),
                  _resident((1, LANES))],
        out_specs=pl.BlockSpec((tm, LANES), lambda i: (i, 0)),
        out_shape=jax.ShapeDtypeStruct((n, LANES), F32),
        compiler_params=_cparams(("parallel",)),
        name="expert_slots",
    )(route, rank_slab, pstarts_row)


def _moe_kernel(be_ref, first_ref, nxt_ref, par_ref, nu_ref, x_ref, wg_hbm, wu_hbm, wd_hbm, y_ref,
                gbuf, ubuf, dbuf, g16, u16, d16, sems, *, layer):
    j = pl.program_id(0)

    def weight_copies(e, slot):
        return (pltpu.make_async_copy(wg_hbm.at[layer, e], gbuf.at[slot], sems.at[slot, 0]),
                pltpu.make_async_copy(wu_hbm.at[layer, e], ubuf.at[slot], sems.at[slot, 1]),
                pltpu.make_async_copy(wd_hbm.at[layer, e], dbuf.at[slot], sems.at[slot, 2]))

    @pl.when(j < nu_ref[0])
    def _():
        slot = par_ref[j]

        @pl.when(j == 0)
        def _():
            for cp in weight_copies(be_ref[0], 0):
                cp.start()

        @pl.when(first_ref[j] == 1)
        def _():
            @pl.when(nxt_ref[j] >= 0)
            def _():
                for cp in weight_copies(nxt_ref[j], 1 - slot):
                    cp.start()

            for cp in weight_copies(be_ref[j], slot):
                cp.wait()
            g16[...] = gbuf[slot].astype(BF16)
            u16[...] = ubuf[slot].astype(BF16)
            d16[...] = dbuf[slot].astype(BF16)

        xb = x_ref[...]
        hg = _dot(xb, g16[...])
        hu = _dot(xb, u16[...])
        hb = (hg * jax.nn.sigmoid(hg) * hu).astype(BF16)
        y_ref[...] = _dot(hb, d16[...]).astype(y_ref.dtype)

    @pl.when(j >= nu_ref[0])
    def _():
        y_ref[...] = jnp.zeros_like(y_ref)


def _moe_call(block_e, first, nxt, parity, n_used, u_sorted, w_gate, w_up, w_down, layer):
    n_slots, d = u_sorted.shape
    bm = MOE_BM
    nb = n_slots // bm
    de = w_gate.shape[-1]

    def xmap(j, be, fi, nx_, pa, nu):
        return (jnp.minimum(j, nu[0] - 1), 0)

    grid_spec = pltpu.PrefetchScalarGridSpec(
        num_scalar_prefetch=5,
        grid=(nb,),
        in_specs=[pl.BlockSpec((bm, d), xmap),
                  pl.BlockSpec(memory_space=pl.ANY),
                  pl.BlockSpec(memory_space=pl.ANY),
                  pl.BlockSpec(memory_space=pl.ANY)],
        out_specs=pl.BlockSpec((bm, d), lambda j, *_: (j, 0)),
        scratch_shapes=[pltpu.VMEM((2, d, de), F32), pltpu.VMEM((2, d, de), F32), pltpu.VMEM((2, de, d), F32),
                        pltpu.VMEM((d, de), BF16), pltpu.VMEM((d, de), BF16), pltpu.VMEM((de, d), BF16),
                        pltpu.SemaphoreType.DMA((2, 3))],
    )
    return pl.pallas_call(
        functools.partial(_moe_kernel, layer=layer),
        grid_spec=grid_spec,
        out_shape=jax.ShapeDtypeStruct((n_slots, d), BF16),
        compiler_params=_cparams(("arbitrary",)),
        name="expert_mlp",
    )(block_e, first, nxt, parity, n_used, u_sorted, w_gate, w_up, w_down)


def _combine_kernel(x_ref, y0_ref, y1_ref, r_ref, mod_ref, lg_ref, lb_ref, o_ref, *, alpha):
    slab = r_ref[...]
    g1 = slab[:, 2:3]
    g2 = slab[:, 3:4]
    mx = g1 * y0_ref[...].astype(F32) + g2 * y1_ref[...].astype(F32)
    gate_f = mod_ref[0, 5:6, :]
    o_ref[...] = _layer_norm(alpha * x_ref[...] + gate_f * mx, lg_ref[...], lb_ref[...])


def _combine_call(xs, y0, y1, route, mod, ln_g, ln_b, mod_row, alpha):
    n, d = xs.shape
    tm = ROW_TM
    return pl.pallas_call(
        functools.partial(_combine_kernel, alpha=alpha),
        grid=(n // tm,),
        in_specs=[pl.BlockSpec((tm, d), lambda i: (i, 0)),
                  pl.BlockSpec((tm, d), lambda i: (i, 0)),
                  pl.BlockSpec((tm, d), lambda i: (i, 0)),
                  pl.BlockSpec((tm, LANES), lambda i: (i, 0)),
                  pl.BlockSpec((1, 6, d), lambda i: (mod_row(i, tm), 0, 0)),
                  pl.BlockSpec((1, d), lambda i: (0, 0)),
                  pl.BlockSpec((1, d), lambda i: (0, 0))],
        out_specs=pl.BlockSpec((tm, d), lambda i: (i, 0)),
        out_shape=jax.ShapeDtypeStruct((n, d), F32),
        compiler_params=_cparams(("parallel",)),
        name="moe_combine_ln",
    )(xs, y0, y1, route, mod, ln_g, ln_b)


def _rope_angles(seq, dim):
    tpos = np.arange(seq)
    row = (tpos // GRID_W).astype(np.float32)
    col = (tpos % GRID_W).astype(np.float32)
    quarter = dim // 4
    inv_freq = jnp.asarray(ROPE_THETA, F32) ** (-jnp.arange(quarter, dtype=F32) / quarter)
    ang_r = jnp.asarray(row)[:, None] * inv_freq
    ang_c = jnp.asarray(col)[:, None] * inv_freq
    return jnp.concatenate([ang_r, ang_r, ang_c, ang_c], axis=-1)


def _rot_sign(dim):
    l = np.arange(dim)
    return np.where((l & (dim // 4)) == 0, -1.0, 1.0).astype(np.float32), l ^ (dim // 4)


def _rope_tables(seq, pad_rows):
    ang_a = _rope_angles(seq, HEAD_DIM)
    sign_a, _ = _rot_sign(HEAD_DIM)
    cos_a = jnp.concatenate([jnp.cos(ang_a), jnp.ones((pad_rows, HEAD_DIM), F32)], 0)
    sin_a = jnp.concatenate([jnp.sin(ang_a) * sign_a, jnp.zeros((pad_rows, HEAD_DIM), F32)], 0)
    ang_b = _rope_angles(seq, MLA_ROPE)
    cos_b = jnp.concatenate([jnp.cos(ang_b), jnp.ones((pad_rows, MLA_ROPE), F32)], 0)
    sin_b = jnp.concatenate([jnp.sin(ang_b), jnp.zeros((pad_rows, MLA_ROPE), F32)], 0)
    return cos_a, sin_a, jnp.tile(cos_b, (1, 2)), jnp.tile(sin_b, (1, 2))


ORIG_AV, ORIG_BQ, ORIG_BKR, ORIG_CQ, ORIG_W = 1024, 1280, 2048, 2112, 10176
W_PREP_ROWS = 128


def _w_in_prep_kernel(w_ref, tail_ref, o_ref, *, cq_scale):
    rows = w_ref.shape[1]
    lane = lax.broadcasted_iota(jnp.int32, (rows, LANES), 1)
    low = lane < MLA_ROPE
    quarter = MLA_ROPE // 4

    o_ref[0, :, COL_AQ:COL_BQ] = w_ref[0, :, 0:ORIG_AV].astype(BF16)
    o_ref[0, :, COL_BQ:COL_BKR] = w_ref[0, :, ORIG_BQ:ORIG_BKR].astype(BF16)
    o_ref[0, :, COL_AV:COL_CQ] = w_ref[0, :, ORIG_AV:ORIG_BQ].astype(BF16)
    kb = w_ref[0, :, ORIG_BKR:ORIG_BKR + LANES]
    rot = jnp.where((lane & quarter) == 0, -pltpu.roll(kb, MLA_ROPE - quarter, 1), pltpu.roll(kb, MLA_ROPE + quarter, 1))
    o_ref[0, :, COL_BKR:COL_AV] = jnp.where(low, kb, rot).astype(BF16)
    n_shift = (PROJ_W - COL_CQ) // LANES
    for b in range(n_shift):
        src = ORIG_BKR + LANES * b
        first = w_ref[0, :, src:src + LANES]
        second = w_ref[0, :, src + LANES:src + 2 * LANES] if b < n_shift - 1 else tail_ref[0]
        v = jnp.where(low, pltpu.roll(first, MLA_ROPE, 1), pltpu.roll(second, MLA_ROPE, 1))
        if COL_CQ + LANES * b < COL_CK:
            v = v * cq_scale
        o_ref[0, :, COL_CQ + LANES * b:COL_CQ + LANES * (b + 1)] = v.astype(BF16)


def _permute_w_in(w):
    depth, d, width = w.shape
    assert width == ORIG_W and ORIG_W - ORIG_CQ == PROJ_W - COL_CQ
    tail = jnp.pad(w[:, :, ORIG_W - MLA_ROPE:], ((0, 0), (0, 0), (0, LANES - MLA_ROPE)))
    return pl.pallas_call(
        functools.partial(_w_in_prep_kernel, cq_scale=SCORE_LOG2E / math.sqrt(HEAD_DIM)),
        grid=(depth, d // W_PREP_ROWS),
        in_specs=[pl.BlockSpec((1, W_PREP_ROWS, width), lambda l, i: (l, i, 0)),
                  pl.BlockSpec((1, W_PREP_ROWS, LANES), lambda l, i: (l, i, 0))],
        out_specs=pl.BlockSpec((1, W_PREP_ROWS, PROJ_W), lambda l, i: (l, i, 0)),
        out_shape=jax.ShapeDtypeStruct((depth, d, PROJ_W), BF16),
        compiler_params=_cparams(("parallel", "parallel")),
        name="w_in_prep",
    )(w, tail)


def _rot_cols(w):
    dim = w.shape[-1]
    wr = w.reshape(w.shape[:-1] + (2, 2, dim // 4))
    return jnp.stack([-wr[..., 1, :], wr[..., 0, :]], axis=-2).reshape(w.shape)


def _permute_w_uq(w):
    r = w.shape[0]
    w3 = w.reshape(r, MLA_HEADS, MLA_QK)
    nope = w3[:, :, :MLA_NOPE].reshape(r, MLA_HEADS * MLA_NOPE)
    rope = w3[:, :, MLA_NOPE:]
    rot = _rot_cols(rope)
    zpad = jnp.zeros((r, MLA_HEADS, LANES - MLA_ROPE), w.dtype)
    rope_p = jnp.concatenate([rope, zpad], -1).reshape(r, MLA_HEADS * LANES)
    rot_p = jnp.concatenate([rot, zpad], -1).reshape(r, MLA_HEADS * LANES)
    return jnp.concatenate([nope, rope_p, rot_p], axis=1).astype(BF16)


def _permute_w_ukv(w):
    r = w.shape[0]
    w3 = w.reshape(r, MLA_HEADS, MLA_NOPE + MLA_V)
    kn = w3[:, :, :MLA_NOPE].reshape(r, MLA_HEADS * MLA_NOPE)
    vv = w3[:, :, MLA_NOPE:].reshape(r, MLA_HEADS * MLA_V)
    return jnp.concatenate([kn, vv], axis=1).astype(BF16)


def _na_bias_tables(rpb, seq):
    w, kh, kw = GRID_W, NA_KH, NA_KW
    rows = seq // w
    nh, n_dr, n_dc = rpb.shape
    line = jnp.full((nh, n_dr, 2 * w), NEG_BIG, F32).at[:, :, w - kw:w - kw + n_dc].set(rpb.astype(F32) * SCORE_LOG2E)
    skew = jnp.broadcast_to(line[:, :, None, :], (nh, n_dr, w, 2 * w)).reshape(nh, n_dr, 2 * w * w)
    skew = skew[:, :, :w * (2 * w - 1)].reshape(nh, n_dr, w, 2 * w - 1)
    tc = skew[:, :, :, w - 1:2 * w - 1]
    cq = np.arange(w)[:, None]
    ck = np.arange(w)[None, :]
    cs = np.clip(cq - kw // 2, 0, w - kw)
    col_ok = (ck >= cs) & (ck < cs + kw)
    tc = jnp.where(col_ok[None, None], tc, NEG_BIG)
    tc = jnp.concatenate([tc, jnp.full((nh, 1, w, w), NEG_BIG, F32)], axis=1)
    tc2 = jnp.concatenate([tc, tc], axis=-1)
    blks = []
    for r0 in (0, NA_ROWS, rows - NA_ROWS):
        rb = int(np.clip(r0 - kh // 2, 0, rows - NA_BAND))
        rq = r0 + np.arange(NA_ROWS)[:, None]
        rk = rb + np.arange(NA_BAND)[None, :]
        rs = np.clip(rq - kh // 2, 0, rows - kh)
        row_ok = (rk >= rs) & (rk < rs + kh)
        blks.append(np.where(row_ok, rk - rq + kh - 1, n_dr).tolist())
    return pl.pallas_call(
        functools.partial(_na_bias_kernel, blks=blks),
        grid=(nh,),
        in_specs=[pl.BlockSpec((1, n_dr + 1, w, 2 * w), lambda h: (h, 0, 0, 0))],
        out_specs=pl.BlockSpec((1, len(blks), NA_ROWS * w, NA_BAND * w), lambda h: (h, 0, 0, 0)),
        out_shape=jax.ShapeDtypeStruct((nh, len(blks), NA_ROWS * w, NA_BAND * w), F32),
        compiler_params=_cparams(("parallel",)),
        name="na_bias_table",
    )(tc2)


def _na_bias_kernel(tc_ref, o_ref, *, blks):
    w = GRID_W
    low = lax.broadcasted_iota(jnp.int32, (w, 2 * w), 1) < w
    for v, blk in enumerate(blks):
        for rq, row in enumerate(blk):
            for pr in range(len(row) // 2):
                pair = jnp.where(low, tc_ref[0, row[2 * pr]], tc_ref[0, row[2 * pr + 1]])
                o_ref[0, v, rq * w:(rq + 1) * w, pr * 2 * w:(pr + 1) * 2 * w] = pair


def kernel(x, c, ctx, c_ctx, w_ada, b_ada, w_in, gqa_q_norm, gqa_k_norm, mla_q_norm, mla_kv_norm, mla_w_uq, mla_w_ukv, na_rpb, w_branch_a, w_branch_b, w_branch_c, w_out, ln1_g, ln1_b, w_router_group, b_router_group, w_router_expert, b_router_expert, w_expert_gate, w_expert_up, w_expert_down, ln2_g, ln2_b):
    bsz, seq, d = x.shape
    n_ctx = ctx.shape[1]
    depth = w_ada.shape[0]
    nx, nc = bsz * seq, bsz * n_ctx
    t = nx + nc
    assert seq % PROJ_TM == 0 and nc == PROJ_TM and seq % (NA_BAND * GRID_W) == 0 and nx % n_ctx == 0
    alpha = (2 * depth) ** 0.25

    def mod_row(i, tm):
        return jnp.minimum(i // (seq // tm), bsz)

    def rope_blk(i, tm):
        return jnp.where(i < nx // tm, i % (seq // tm), seq // tm)

    stream = (x.reshape(nx, d), ctx.reshape(nc, d), 0)
    cc = jnp.concatenate([c, c_ctx[None], jnp.zeros((8 - bsz - 1, d), F32)], axis=0)
    mod_all = _ada_call(cc, w_ada, b_ada).reshape(depth, 8, 6, d)
    cos_a, sin_a, cos_b, sin_b = _rope_tables(seq, ROW_TM)
    w_in_p = _permute_w_in(w_in)

    attn_kw = dict(bsz=bsz, seq=seq, n_ctx=n_ctx)
    for i in range(depth):
        last = i == depth - 1
        n_rows = nx if last else t
        mod = mod_all[i]
        p = _proj_call(stream, nx, t, mod, w_in_p, i, mod_row)

        qa, ka = _prep_a_call(p, cos_a, sin_a, gqa_q_norm[i][None], gqa_k_norm[i][None], rope_blk)
        a_kw = dict(n_heads=GQA_HEADS, dq=HEAD_DIM, q_col=lambda h: h, k_col=lambda h: h // GQA_GROUP,
                    v_col=lambda h: COL_AV // HEAD_DIM + h // GQA_GROUP, **attn_kw)
        o_a = _dense_attn_call(qa, ka, p, **a_kw)
        qb, kb, vb = _prep_b_call(p, cos_b, sin_b, mla_q_norm[i][None], mla_kv_norm[i][None],
                                  _permute_w_uq(mla_w_uq[i]), _permute_w_ukv(mla_w_ukv[i]), rope_blk)
        b_kw = dict(n_heads=MLA_HEADS, dq=MLA_PAD, q_col=lambda h: h, k_col=lambda h: h, v_col=lambda h: h, **attn_kw)
        o_b = _dense_attn_call(qb, kb, vb, **b_kw)
        o_c = _na_call(p, _na_bias_tables(na_rpb[i], seq), **attn_kw)
        o_lat = (o_a, o_b, o_c)
        if last:
            o_ctx = o_lat
        else:
            c_kw = dict(n_heads=NA_HEADS, dq=HEAD_DIM, q_col=lambda h: COL_CQ // HEAD_DIM + h,
                        k_col=lambda h: COL_CK // HEAD_DIM + h, v_col=lambda h: COL_CV // HEAD_DIM + h, **attn_kw)
            o_ctx = (_ctx_attn_call(qa, ka, p, **a_kw), _ctx_attn_call(qb, kb, vb, **b_kw),
                     _ctx_attn_call(p, p, p, **c_kw))

        mrg = _branch_call(o_lat, o_ctx, p, w_branch_a[i].astype(BF16), w_branch_b[i].astype(BF16),
                           w_branch_c[i].astype(BF16), n_rows)
        wr = jnp.concatenate([w_router_group[i], w_router_expert[i],
                              jnp.zeros((d, LANES - N_GROUPS - N_EXPERTS), F32)], axis=1)
        br = jnp.concatenate([b_router_group[i], b_router_expert[i],
                              jnp.zeros((LANES - N_GROUPS - N_EXPERTS,), F32)])[None]
        wr_cat = jnp.concatenate(_split_bf16(wr), axis=1)
        xs, u_f, route = _out_call(mrg, stream, nx, mod, w_out[i].astype(BF16), ln1_g[i][None], ln1_b[i][None],
                                   wr_cat, br, mod_row, n_rows, alpha)

        rank_slab, cnt = _rank_call(route)
        eid = route[:, :TOP_K].astype(jnp.int32)
        counts = cnt[0, :N_EXPERTS].astype(jnp.int32)
        padded = ((counts + MOE_BM - 1) // MOE_BM) * MOE_BM
        pends = jnp.cumsum(padded)
        pstarts = pends - padded
        pstarts_row = jnp.concatenate([pstarts.astype(F32), jnp.zeros((LANES - N_EXPERTS,), F32)])[None]
        dest = _dest_call(route, rank_slab, pstarts_row)[:, :TOP_K].astype(jnp.int32)
        nb = -(-(n_rows * TOP_K) // MOE_BM) + N_EXPERTS
        n_used = (pends[-1] // MOE_BM).astype(jnp.int32)
        blk = jnp.minimum(jnp.arange(nb, dtype=jnp.int32), n_used - 1)
        block_e = jnp.clip(jnp.searchsorted(pends, blk * MOE_BM, side="right"), 0, N_EXPERTS - 1).astype(jnp.int32)
        order = jnp.argsort(eid.reshape(-1), stable=True).astype(jnp.int32)
        starts = jnp.cumsum(counts) - counts
        slot = jnp.arange(nb * MOE_BM, dtype=jnp.int32).reshape(nb, MOE_BM)
        blk_shift = (starts - pstarts)[block_e][:, None]
        blk_limit = (pstarts + counts)[block_e][:, None]
        pos = jnp.clip(slot + blk_shift, 0, n_rows * TOP_K - 1)
        slot_tok = jnp.where(slot < blk_limit, order.at[pos].get(mode="promise_in_bounds") // TOP_K,
                             slot % n_rows).reshape(-1)
        u_sorted = u_f.at[slot_tok].get(mode="promise_in_bounds")
        jj = jnp.arange(nb, dtype=jnp.int32)
        first = ((jj == 0) | (block_e != jnp.roll(block_e, 1))).astype(jnp.int32)
        parity = (jnp.cumsum(first) - 1) % 2
        live = jnp.where(counts > 0, jnp.arange(N_EXPERTS, dtype=jnp.int32), N_EXPERTS)
        next_live = jnp.concatenate([lax.cummin(live, reverse=True)[1:], jnp.full((1,), N_EXPERTS, jnp.int32)])
        nxt = jnp.where(next_live < N_EXPERTS, next_live, -1)[block_e]
        y_sorted = _moe_call(block_e, first, nxt.astype(jnp.int32), parity.astype(jnp.int32), n_used[None],
                             u_sorted, w_expert_gate, w_expert_up, w_expert_down, i)
        y0 = y_sorted.at[dest[:, 0]].get(mode="promise_in_bounds")
        y1 = y_sorted.at[dest[:, 1]].get(mode="promise_in_bounds")
        xs = _combine_call(xs, y0, y1, route, mod, ln2_g[i][None], ln2_b[i][None], mod_row, alpha)
        stream = (xs, xs, nx)
    return xs.reshape(bsz, seq, d)
```

```python
import functools
import math

import jax
import jax.numpy as jnp
import numpy as np
from jax import lax
from jax.experimental import pallas as pl
from jax.experimental.pallas import tpu as pltpu

F32 = jnp.float32
BF16 = jnp.bfloat16

HEAD_DIM = 128
GRID_W = 64
ROPE_THETA = 10000.0
GQA_HEADS, GQA_KV_HEADS = 6, 2
GQA_GROUP = GQA_HEADS // GQA_KV_HEADS
MLA_HEADS, MLA_Q_RANK, MLA_KV_RANK = 5, 512, 256
MLA_NOPE, MLA_ROPE, MLA_V = 128, 64, 128
MLA_QK = MLA_NOPE + MLA_ROPE
MLA_PAD = 256
NA_HEADS, NA_KH, NA_KW = 5, 8, 16
N_GROUPS, EXPERTS_PER_GROUP, TOP_K, D_EXPERT = 8, 8, 2, 512
N_EXPERTS = N_GROUPS * EXPERTS_PER_GROUP
LN_EPS = 1e-6
RMS_EPS = 1e-6
NEG_BIG = -1e30
SCORE_LOG2E = math.log2(math.e)

LANES = 128
VMEM_LIMIT = 56 * 1024 * 1024
PROJ_TM, PROJ_TN = 1024, 1024
ROW_TM = 512
MM_TM = 512
ATTN_TQ = 512
ATTN_SUB = 2
ATTN_KC = 1024
NA_ROWS = 8
NA_BAND = 16
NA_WIN = 12
MOE_BM = 256
ADA_TN = 512

COL_AQ, COL_AK = 0, 768
COL_BQ, COL_BKV, COL_BKR = 1024, 1536, 1792
COL_AV, COL_CQ, COL_CK, COL_CV = 1920, 2176, 2816, 3456
COL_GATE = 4096
PROJ_W = 10240


def _cparams(sem):
    return pltpu.CompilerParams(dimension_semantics=sem, vmem_limit_bytes=VMEM_LIMIT)


def _resident(shape):
    return pl.BlockSpec(shape, lambda *_: (0,) * len(shape), pipeline_mode=pl.Buffered(1))


def _dot(a, b):
    return jnp.dot(a, b, preferred_element_type=F32)


def _dot_nt(a, b):
    return lax.dot_general(a, b, (((1,), (1,)), ((), ())), preferred_element_type=F32)


def _split_bf16(a):
    hi = a.astype(BF16)
    lo = (a - hi.astype(F32)).astype(BF16)
    return hi, lo


def _ada_kernel(c_ref, w_ref, b_ref, o_ref):
    a = c_ref[...]
    a = a * jax.nn.sigmoid(a)
    a_hi, a_lo = _split_bf16(a)
    w_hi, w_lo = _split_bf16(w_ref[0])
    acc = _dot(a_hi, w_hi) + _dot(a_lo, w_hi) + _dot(a_hi, w_lo)
    o_ref[0] = acc + b_ref[0]


def _ada_call(cc, w_ada, b_ada):
    depth, d, n = w_ada.shape
    rows = cc.shape[0]
    return pl.pallas_call(
        _ada_kernel,
        grid=(depth, n // ADA_TN),
        in_specs=[pl.BlockSpec((rows, d), lambda l, j: (0, 0)),
                  pl.BlockSpec((1, d, ADA_TN), lambda l, j: (l, 0, j)),
                  pl.BlockSpec((1, 1, ADA_TN), lambda l, j: (l, 0, j))],
        out_specs=pl.BlockSpec((1, rows, ADA_TN), lambda l, j: (l, 0, j)),
        out_shape=jax.ShapeDtypeStruct((depth, rows, n), F32),
        compiler_params=_cparams(("parallel", "parallel")),
        name="ada_modulation",
    )(cc, w_ada, b_ada.reshape(depth, 1, n))


def _proj_kernel(xl_ref, xc_ref, mod_ref, w_ref, o_ref, u_scr, *, n_lat_tiles):
    @pl.when(pl.program_id(1) == 0)
    def _():
        def modulate(x):
            u_scr[...] = (x * (1.0 + mod_ref[0, 1:2, :]) + mod_ref[0, 0:1, :]).astype(BF16)

        @pl.when(pl.program_id(0) < n_lat_tiles)
        def _():
            modulate(xl_ref[...])

        @pl.when(pl.program_id(0) >= n_lat_tiles)
        def _():
            modulate(xc_ref[...])

    o_ref[...] = _dot(u_scr[...], w_ref[0]).astype(BF16)


def _stream_specs(stream, tm, n_lat):
    lat, ctx, ctx_row0 = stream
    d = lat.shape[1]
    n_lat_tiles = n_lat // tm
    ctx_tile0 = ctx_row0 // tm

    def lat_map(i, *_):
        return (jnp.minimum(i, n_lat_tiles - 1), 0)

    def ctx_map(i, *_):
        return (ctx_tile0 + jnp.maximum(i - n_lat_tiles, 0), 0)

    return [pl.BlockSpec((tm, d), lat_map), pl.BlockSpec((tm, d), ctx_map, pipeline_mode=pl.Buffered(1))]


def _proj_call(stream, n_lat, n_tok, mod, w_in_p, layer, mod_row):
    d = stream[0].shape[1]
    return pl.pallas_call(
        functools.partial(_proj_kernel, n_lat_tiles=n_lat // PROJ_TM),
        grid=(n_tok // PROJ_TM, PROJ_W // PROJ_TN),
        in_specs=_stream_specs(stream, PROJ_TM, n_lat) + [
            pl.BlockSpec((1, 6, d), lambda i, j: (mod_row(i, PROJ_TM), 0, 0)),
            pl.BlockSpec((1, d, PROJ_TN), lambda i, j: (layer, 0, j))],
        out_specs=pl.BlockSpec((PROJ_TM, PROJ_TN), lambda i, j: (i, j)),
        out_shape=jax.ShapeDtypeStruct((n_tok, PROJ_W), BF16),
        scratch_shapes=[pltpu.VMEM((PROJ_TM, d), BF16)],
        compiler_params=_cparams(("parallel", "arbitrary")),
        name="input_projection",
    )(stream[0], stream[1], mod, w_in_p)


def _rope128(y, cos, sin_signed, first_quarter):
    rot = jnp.where(first_quarter, pltpu.roll(y, 96, 1), pltpu.roll(y, 32, 1))
    return y * cos + rot * sin_signed


def _prep_a_kernel(p_ref, cos_ref, sin_ref, gq_ref, gk_ref, q_ref, k_ref, *, scale):
    cos = cos_ref[...]
    sin_signed = sin_ref[...]
    lane = lax.broadcasted_iota(jnp.int32, cos.shape, 1)
    first_quarter = (lane & 32) == 0
    for h in range(GQA_HEADS + GQA_KV_HEADS):
        xh = p_ref[:, h * HEAD_DIM:(h + 1) * HEAD_DIM].astype(F32)
        r = lax.rsqrt(jnp.mean(xh * xh, axis=1, keepdims=True) + RMS_EPS)
        if h < GQA_HEADS:
            y = _rope128(xh * r * gq_ref[...], cos, sin_signed, first_quarter) * scale
            q_ref[:, h * HEAD_DIM:(h + 1) * HEAD_DIM] = y.astype(BF16)
        else:
            hk = h - GQA_HEADS
            y = _rope128(xh * r * gk_ref[...], cos, sin_signed, first_quarter)
            k_ref[:, hk * HEAD_DIM:(hk + 1) * HEAD_DIM] = y.astype(BF16)


def _prep_a_call(p, cos_t, sin_t, gq, gk, rope_blk):
    t = p.shape[0]
    tm = ROW_TM
    wq, wk = GQA_HEADS * HEAD_DIM, GQA_KV_HEADS * HEAD_DIM
    return pl.pallas_call(
        functools.partial(_prep_a_kernel, scale=SCORE_LOG2E / math.sqrt(HEAD_DIM)),
        grid=(t // tm,),
        in_specs=[pl.BlockSpec((tm, wq + wk), lambda i: (i, 0)),
                  pl.BlockSpec((tm, HEAD_DIM), lambda i: (rope_blk(i, tm), 0)),
                  pl.BlockSpec((tm, HEAD_DIM), lambda i: (rope_blk(i, tm), 0)),
                  pl.BlockSpec((1, HEAD_DIM), lambda i: (0, 0)),
                  pl.BlockSpec((1, HEAD_DIM), lambda i: (0, 0))],
        out_specs=[pl.BlockSpec((tm, wq), lambda i: (i, 0)),
                   pl.BlockSpec((tm, wk), lambda i: (i, 0))],
        out_shape=[jax.ShapeDtypeStruct((t, wq), BF16), jax.ShapeDtypeStruct((t, wk), BF16)],
        compiler_params=_cparams(("parallel",)),
        name="gqa_qk_prep",
    )(p, cos_t, sin_t, gq, gk)


def _prep_b_kernel(ql_ref, kvl_ref, kr_ref, cos_ref, sin_ref, gq_ref, gkv_ref, wq_ref, wkv_ref,
                   q_ref, k_ref, v_ref, *, scale):
    nh, hp = MLA_HEADS, MLA_PAD
    wn = nh * MLA_NOPE
    cos = cos_ref[...]
    sin = sin_ref[...]
    ql = ql_ref[...].astype(F32)
    ql = ql * lax.rsqrt(jnp.mean(ql * ql, axis=1, keepdims=True) + RMS_EPS) * gq_ref[...]
    qf = _dot(ql.astype(BF16), wq_ref[...])
    kvl = kvl_ref[...].astype(F32)
    kvl = kvl * lax.rsqrt(jnp.mean(kvl * kvl, axis=1, keepdims=True) + RMS_EPS) * gkv_ref[...]
    kvf = _dot(kvl.astype(BF16), wkv_ref[...])
    lane = lax.broadcasted_iota(jnp.int32, cos.shape, 1)
    low = lane < MLA_ROPE
    tk = kr_ref[...].astype(F32) * jnp.where(low, cos, sin)
    kr = jnp.where(low, tk + pltpu.roll(tk, MLA_ROPE, 1), 0.0).astype(BF16)
    for h in range(nh):
        q_ref[:, h * hp:h * hp + LANES] = (qf[:, h * LANES:(h + 1) * LANES] * scale).astype(BF16)
        qr = qf[:, wn + h * LANES:wn + (h + 1) * LANES] * cos + qf[:, 2 * wn + h * LANES:2 * wn + (h + 1) * LANES] * sin
        q_ref[:, h * hp + LANES:(h + 1) * hp] = (qr * scale).astype(BF16)
        k_ref[:, h * hp:h * hp + LANES] = kvf[:, h * LANES:(h + 1) * LANES].astype(BF16)
        k_ref[:, h * hp + LANES:(h + 1) * hp] = kr
    v_ref[...] = kvf[:, wn:].astype(BF16)


def _prep_b_call(p, cos_t, sin_t, gq, gkv, wq, wkv, rope_blk):
    t = p.shape[0]
    tm = ROW_TM
    nh = MLA_HEADS
    return pl.pallas_call(
        functools.partial(_prep_b_kernel, scale=SCORE_LOG2E / math.sqrt(MLA_QK)),
        grid=(t // tm,),
        in_specs=[pl.BlockSpec((tm, MLA_Q_RANK), lambda i: (i, COL_BQ // MLA_Q_RANK)),
                  pl.BlockSpec((tm, MLA_KV_RANK), lambda i: (i, COL_BKV // MLA_KV_RANK)),
                  pl.BlockSpec((tm, LANES), lambda i: (i, COL_BKR // LANES)),
                  pl.BlockSpec((tm, LANES), lambda i: (rope_blk(i, tm), 0)),
                  pl.BlockSpec((tm, LANES), lambda i: (rope_blk(i, tm), 0)),
                  pl.BlockSpec((1, MLA_Q_RANK), lambda i: (0, 0)),
                  pl.BlockSpec((1, MLA_KV_RANK), lambda i: (0, 0)),
                  _resident(wq.shape), _resident(wkv.shape)],
        out_specs=[pl.BlockSpec((tm, nh * MLA_PAD), lambda i: (i, 0)),
                   pl.BlockSpec((tm, nh * MLA_PAD), lambda i: (i, 0)),
                   pl.BlockSpec((tm, nh * MLA_V), lambda i: (i, 0))],
        out_shape=[jax.ShapeDtypeStruct((t, nh * MLA_PAD), BF16),
                   jax.ShapeDtypeStruct((t, nh * MLA_PAD), BF16),
                   jax.ShapeDtypeStruct((t, nh * MLA_V), BF16)],
        compiler_params=_cparams(("parallel",)),
        name="mla_prep",
    )(p, p, p, cos_t, sin_t, gq, gkv, wq, wkv)


def _online_attend(q, chunks):
    m = z = acc = None
    for k, v, bias in chunks:
        s = _dot_nt(q, k)
        if bias is not None:
            s = s + bias
        cm = jnp.max(s, axis=1, keepdims=True)
        if m is None:
            m = cm
            p = jnp.exp2(s - m)
            z = jnp.sum(p, axis=1, keepdims=True)
            acc = _dot(p.astype(BF16), v)
        else:
            m_new = jnp.maximum(m, cm)
            corr = jnp.exp2(m - m_new)
            p = jnp.exp2(s - m_new)
            z = z * corr + jnp.sum(p, axis=1, keepdims=True)
            acc = acc * corr + _dot(p.astype(BF16), v)
            m = m_new
    return acc / z


def _online_attend_ones(q, chunks, dv):
    m = acc = None
    for k, v_ones, bias in chunks:
        s = _dot_nt(q, k)
        if bias is not None:
            s = s + bias
        cm = jnp.max(s, axis=1, keepdims=True)
        if m is None:
            m = cm
            acc = _dot(jnp.exp2((s - m).astype(BF16)), v_ones)
        else:
            m_new = jnp.maximum(m, cm)
            corr = jnp.exp2(m - m_new)
            acc = acc * corr + _dot(jnp.exp2((s - m_new).astype(BF16)), v_ones)
            m = m_new
    return acc[:, :dv] / acc[:, dv:dv + 1]


def _dense_attn_kernel(q_ref, kl_ref, vl_ref, kc_ref, vc_ref, o_ref, v_ones):
    n_lat = kl_ref.shape[0]
    dv = vl_ref.shape[1]

    @pl.when(pl.program_id(2) == 0)
    def _():
        v_ones[0:n_lat, 0:dv] = vl_ref[...]
        v_ones[n_lat:, 0:dv] = vc_ref[...]
        v_ones[:, dv:] = jnp.ones((v_ones.shape[0], dv), BF16)

    chunks = [(kl_ref[c:c + ATTN_KC, :], v_ones[c:c + ATTN_KC, :], None) for c in range(0, n_lat, ATTN_KC)]
    chunks.append((kc_ref[...], v_ones[n_lat:, :], None))
    for r in range(0, q_ref.shape[0], ATTN_TQ):
        o_ref[r:r + ATTN_TQ, :] = _online_attend_ones(q_ref[r:r + ATTN_TQ, :], chunks, dv).astype(o_ref.dtype)


def _dense_attn_call(q, k, v, *, n_heads, dq, q_col, k_col, v_col, bsz, seq, n_ctx):
    tq = ATTN_TQ * ATTN_SUB
    tpb = seq // tq
    ctx0 = bsz * seq // n_ctx
    dv = HEAD_DIM
    return pl.pallas_call(
        _dense_attn_kernel,
        grid=(bsz, n_heads, tpb),
        in_specs=[pl.BlockSpec((tq, dq), lambda b, h, i: (b * tpb + i, q_col(h))),
                  pl.BlockSpec((seq, dq), lambda b, h, i: (b, k_col(h))),
                  pl.BlockSpec((seq, dv), lambda b, h, i: (b, v_col(h))),
                  pl.BlockSpec((n_ctx, dq), lambda b, h, i: (ctx0 + b, k_col(h))),
                  pl.BlockSpec((n_ctx, dv), lambda b, h, i: (ctx0 + b, v_col(h)))],
        out_specs=pl.BlockSpec((tq, dv), lambda b, h, i: (b * tpb + i, h)),
        out_shape=jax.ShapeDtypeStruct((bsz * seq, n_heads * dv), BF16),
        scratch_shapes=[pltpu.VMEM((seq + n_ctx, 2 * dv), BF16)],
        compiler_params=_cparams(("parallel", "parallel", "arbitrary")),
        name="dense_attention",
    )(q, k, v, k, v)


def _ctx_attn_kernel(q_ref, kc_ref, vc_ref, o_ref):
    o_ref[...] = _online_attend(q_ref[...], [(kc_ref[...], vc_ref[...], None)]).astype(o_ref.dtype)


def _ctx_attn_call(q, k, v, *, n_heads, dq, q_col, k_col, v_col, bsz, seq, n_ctx):
    ctx0 = bsz * seq // n_ctx
    dv = HEAD_DIM
    return pl.pallas_call(
        _ctx_attn_kernel,
        grid=(bsz, n_heads),
        in_specs=[pl.BlockSpec((n_ctx, dq), lambda b, h: (ctx0 + b, q_col(h))),
                  pl.BlockSpec((n_ctx, dq), lambda b, h: (ctx0 + b, k_col(h))),
                  pl.BlockSpec((n_ctx, dv), lambda b, h: (ctx0 + b, v_col(h)))],
        out_specs=pl.BlockSpec((n_ctx, dv), lambda b, h: (b, h)),
        out_shape=jax.ShapeDtypeStruct((bsz * n_ctx, n_heads * dv), BF16),
        compiler_params=_cparams(("parallel", "parallel")),
        name="context_attention",
    )(q, k, v)


def _na_window_offsets(variant, n_variants):
    shift = NA_BAND - NA_WIN
    return ((0, 0), (0, shift), (shift, shift))[0 if variant == 0 else (2 if variant == n_variants - 1 else 1)]


def _na_kernel(q_ref, k_ref, v_ref, kc_ref, vc_ref, bias_ref, o_ref, v_ones, *, rows_total):
    i = pl.program_id(2)
    n_tiles = pl.num_programs(2)
    n_lat = k_ref.shape[0]
    dv = v_ref.shape[1]
    half = q_ref.shape[0] // 2
    win = NA_WIN * GRID_W
    shift = (NA_BAND - NA_WIN) * GRID_W

    @pl.when(i == 0)
    def _():
        v_ones[0:n_lat, 0:dv] = v_ref[...]
        v_ones[n_lat:, 0:dv] = vc_ref[...]
        v_ones[:, dv:] = jnp.ones((v_ones.shape[0], dv), BF16)

    start_blk = jnp.clip(2 * i - 1, 0, (rows_total - NA_BAND) // 4)
    band0 = start_blk * (4 * GRID_W)
    starts = (band0 + jnp.where(i == n_tiles - 1, shift, 0), band0 + jnp.where(i == 0, 0, shift))
    ctx = (kc_ref[...], v_ones[n_lat:, :], None)
    for hq, start in enumerate(starts):
        start = pl.multiple_of(start, 4 * GRID_W)
        rows = slice(hq * half, (hq + 1) * half)
        chunks = [(k_ref[pl.ds(start, win), :], v_ones[pl.ds(start, win), :], bias_ref[0, 0, rows, :]), ctx]
        o_ref[rows, :] = _online_attend_ones(q_ref[rows, :], chunks, dv).astype(o_ref.dtype)


def _na_call(p, bias_tab, *, bsz, seq, n_ctx):
    tq = NA_ROWS * GRID_W
    tpb = seq // tq
    rows_total = seq // GRID_W
    ctx0 = bsz * seq // n_ctx
    d = HEAD_DIM
    qc, kc, vc = COL_CQ // d, COL_CK // d, COL_CV // d

    def variant(i):
        return jnp.where(i == 0, 0, jnp.where(i == tpb - 1, 2, 1))

    return pl.pallas_call(
        functools.partial(_na_kernel, rows_total=rows_total),
        grid=(bsz, NA_HEADS, tpb),
        in_specs=[pl.BlockSpec((tq, d), lambda b, h, i: (b * tpb + i, qc + h)),
                  pl.BlockSpec((seq, d), lambda b, h, i: (b, kc + h)),
                  pl.BlockSpec((seq, d), lambda b, h, i: (b, vc + h)),
                  pl.BlockSpec((n_ctx, d), lambda b, h, i: (ctx0 + b, kc + h)),
                  pl.BlockSpec((n_ctx, d), lambda b, h, i: (ctx0 + b, vc + h)),
                  pl.BlockSpec((1, 1, tq, NA_WIN * GRID_W), lambda b, h, i: (h, variant(i), 0, 0))],
        out_specs=pl.BlockSpec((tq, d), lambda b, h, i: (b * tpb + i, h)),
        out_shape=jax.ShapeDtypeStruct((bsz * seq, NA_HEADS * d), BF16),
        scratch_shapes=[pltpu.VMEM((seq + n_ctx, 2 * d), BF16)],
        compiler_params=_cparams(("parallel", "parallel", "arbitrary")),
        name="neighbourhood_attention",
    )(p, p, p, p, p, bias_tab)


def _branch_kernel(oa_ref, ob_ref, oc_ref, oac_ref, obc_ref, occ_ref, ga_ref, gb_ref, gc_ref,
                   wa_ref, wb_ref, wc_ref, o_ref, *, n_lat_tiles):
    def gated(g_ref, w_ref, o_tile):
        return jax.nn.sigmoid(g_ref[...].astype(F32)) * _dot(o_tile, w_ref[...])

    @pl.when(pl.program_id(0) < n_lat_tiles)
    def _():
        acc = gated(ga_ref, wa_ref, oa_ref[...]) + gated(gb_ref, wb_ref, ob_ref[...]) + gated(gc_ref, wc_ref, oc_ref[...])
        o_ref[...] = acc.astype(BF16)

    @pl.when(pl.program_id(0) >= n_lat_tiles)
    def _():
        acc = gated(ga_ref, wa_ref, oac_ref[...]) + gated(gb_ref, wb_ref, obc_ref[...]) + gated(gc_ref, wc_ref, occ_ref[...])
        o_ref[...] = acc.astype(BF16)


def _branch_call(o_lat, o_ctx, p, wa, wb, wc, n_rows):
    d = wa.shape[1]
    tm = MM_TM
    g0 = COL_GATE // d
    n_lat_tiles = o_lat[0].shape[0] // tm
    lat_specs = [pl.BlockSpec((tm, o.shape[1]), lambda i: (jnp.minimum(i, n_lat_tiles - 1), 0)) for o in o_lat]
    ctx_specs = [pl.BlockSpec((tm, o.shape[1]), lambda i: (jnp.maximum(i - n_lat_tiles, 0), 0)) for o in o_ctx]
    return pl.pallas_call(
        functools.partial(_branch_kernel, n_lat_tiles=n_lat_tiles),
        grid=(n_rows // tm,),
        in_specs=lat_specs + ctx_specs + [
            pl.BlockSpec((tm, d), lambda i: (i, g0)),
            pl.BlockSpec((tm, d), lambda i: (i, g0 + 1)),
            pl.BlockSpec((tm, d), lambda i: (i, g0 + 2)),
            _resident(wa.shape), _resident(wb.shape), _resident(wc.shape)],
        out_specs=pl.BlockSpec((tm, d), lambda i: (i, 0)),
        out_shape=jax.ShapeDtypeStruct((n_rows, d), BF16),
        compiler_params=_cparams(("parallel",)),
        name="branch_merge",
    )(*o_lat, *o_ctx, p, p, p, wa, wb, wc)


def _layer_norm(z, g, b):
    mu = jnp.mean(z, axis=1, keepdims=True)
    zc = z - mu
    var = jnp.mean(zc * zc, axis=1, keepdims=True)
    return zc * lax.rsqrt(var + LN_EPS) * g + b


def _route(logits):
    lane = lax.broadcasted_iota(jnp.int32, logits.shape, 1)
    lane_f = lane.astype(F32)
    is_g = lane < N_GROUPS
    gl = jnp.where(is_g, logits, NEG_BIG)
    mg = jnp.max(gl, axis=1, keepdims=True)
    gsel = jnp.min(jnp.where(gl == mg, lane_f, float(LANES)), axis=1, keepdims=True)
    zg = jnp.sum(jnp.where(is_g, jnp.exp(gl - mg), 0.0), axis=1, keepdims=True)
    lo = N_GROUPS + gsel * EXPERTS_PER_GROUP
    is_e = (lane_f >= lo) & (lane_f < lo + EXPERTS_PER_GROUP)
    el = jnp.where(is_e, logits, NEG_BIG)
    t1 = jnp.max(el, axis=1, keepdims=True)
    i1 = jnp.min(jnp.where(el == t1, lane_f, float(LANES)), axis=1, keepdims=True)
    el2 = jnp.where(lane_f == i1, NEG_BIG, el)
    t2 = jnp.max(el2, axis=1, keepdims=True)
    i2 = jnp.min(jnp.where(el2 == t2, lane_f, float(LANES)), axis=1, keepdims=True)
    dd = jnp.exp(t2 - t1)
    g1 = 1.0 / (zg * (1.0 + dd))
    g2 = g1 * dd
    slab = jnp.where(lane == 0, i1 - N_GROUPS,
                     jnp.where(lane == 1, i2 - N_GROUPS,
                               jnp.where(lane == 2, g1, jnp.where(lane == 3, g2, 0.0))))
    return slab


def _out_kernel(m_ref, xl_ref, xc_ref, mod_ref, w_ref, lg_ref, lb_ref, wr_ref, br_ref,
                xo_ref, u_ref, r_ref, *, alpha, n_lat_tiles):
    @pl.when(pl.program_id(0) < n_lat_tiles)
    def _():
        xo_ref[...] = xl_ref[...]

    @pl.when(pl.program_id(0) >= n_lat_tiles)
    def _():
        xo_ref[...] = xc_ref[...]

    y = _dot(m_ref[...], w_ref[...])
    gate_m = mod_ref[0, 2:3, :]
    shift_f = mod_ref[0, 3:4, :]
    scale_f = mod_ref[0, 4:5, :]
    xn = _layer_norm(alpha * xo_ref[...] + gate_m * y, lg_ref[...], lb_ref[...])
    xo_ref[...] = xn
    u = xn * (1.0 + scale_f) + shift_f
    u_hi, u_lo = _split_bf16(u)
    u_ref[...] = u_hi
    both = _dot(u_hi, wr_ref[...])
    logits = both[:, :LANES] + both[:, LANES:] + _dot(u_lo, wr_ref[:, :LANES]) + br_ref[...]
    r_ref[...] = _route(logits)


def _out_call(mrg, stream, n_lat, mod, w_out, ln_g, ln_b, wr_cat, br, mod_row, n_rows, alpha):
    d = stream[0].shape[1]
    tm = MM_TM
    return pl.pallas_call(
        functools.partial(_out_kernel, alpha=alpha, n_lat_tiles=n_lat // tm),
        grid=(n_rows // tm,),
        in_specs=[pl.BlockSpec((tm, d), lambda i: (i, 0))] + _stream_specs(stream, tm, n_lat) + [
                  pl.BlockSpec((1, 6, d), lambda i: (mod_row(i, tm), 0, 0)),
                  _resident((d, d)), _resident((1, d)), _resident((1, d)),
                  _resident((d, 2 * LANES)), _resident((1, LANES))],
        out_specs=[pl.BlockSpec((tm, d), lambda i: (i, 0)),
                   pl.BlockSpec((tm, d), lambda i: (i, 0)),
                   pl.BlockSpec((tm, LANES), lambda i: (i, 0))],
        out_shape=[jax.ShapeDtypeStruct((n_rows, d), F32),
                   jax.ShapeDtypeStruct((n_rows, d), BF16),
                   jax.ShapeDtypeStruct((n_rows, LANES), F32)],
        compiler_params=_cparams(("parallel",)),
        name="out_proj_ln_route",
    )(mrg, stream[0], stream[1], mod, w_out, ln_g, ln_b, wr_cat, br)


def _rank_kernel(r_ref, rank_ref, cnt_ref, carry):
    @pl.when(pl.program_id(0) == 0)
    def _():
        carry[...] = jnp.zeros_like(carry)

    slab = r_ref[...]
    tm = slab.shape[0]
    lane = lax.broadcasted_iota(jnp.int32, slab.shape, 1)
    lane_f = lane.astype(F32)
    e1 = slab[:, 0:1]
    e2 = slab[:, 1:2]
    hit1 = lane_f == e1
    hit2 = lane_f == e2
    onehot = jnp.where(hit1 | hit2, 1.0, 0.0)
    row = lax.broadcasted_iota(jnp.int32, (tm, tm), 0)
    col = lax.broadcasted_iota(jnp.int32, (tm, tm), 1)
    lower = jnp.where(col < row, 1.0, 0.0).astype(BF16)
    before = _dot(lower, onehot.astype(BF16)) + carry[0:1, :]
    r1 = jnp.sum(jnp.where(hit1, before, 0.0), axis=1, keepdims=True)
    r2 = jnp.sum(jnp.where(hit2, before, 0.0), axis=1, keepdims=True)
    rank_ref[...] = jnp.where(lane == 0, r1, jnp.where(lane == 1, r2, 0.0))
    carry[...] = carry[...] + jnp.sum(onehot, axis=0, keepdims=True)
    cnt_ref[...] = carry[...]


def _rank_call(route):
    n = route.shape[0]
    tm = ROW_TM
    return pl.pallas_call(
        _rank_kernel,
        grid=(n // tm,),
        in_specs=[pl.BlockSpec((tm, LANES), lambda i: (i, 0))],
        out_specs=[pl.BlockSpec((tm, LANES), lambda i: (i, 0)),
                   pl.BlockSpec((8, LANES), lambda i: (0, 0))],
        out_shape=[jax.ShapeDtypeStruct((n, LANES), F32), jax.ShapeDtypeStruct((8, LANES), F32)],
        scratch_shapes=[pltpu.VMEM((8, LANES), F32)],
        compiler_params=_cparams(("arbitrary",)),
        name="expert_ranks",
    )(route)


def _dest_kernel(r_ref, rank_ref, ps_ref, o_ref):
    slab = r_ref[...]
    rank = rank_ref[...]
    lane = lax.broadcasted_iota(jnp.int32, slab.shape, 1)
    lane_f = lane.astype(F32)
    ps = ps_ref[...]
    s1 = jnp.sum(jnp.where(lane_f == slab[:, 0:1], ps, 0.0), axis=1, keepdims=True) + rank[:, 0:1]
    s2 = jnp.sum(jnp.where(lane_f == slab[:, 1:2], ps, 0.0), axis=1, keepdims=True) + rank[:, 1:2]
    o_ref[...] = jnp.where(lane == 0, s1, jnp.where(lane == 1, s2, 0.0))


def _dest_call(route, rank_slab, pstarts_row):
    n = route.shape[0]
    tm = n // 8
    return pl.pallas_call(
        _dest_kernel,
        grid=(n // tm,),
        in_specs=[pl.BlockSpec((tm, LANES), lambda i: (i, 0)),
                  pl.BlockSpec((tm, LANES), lambda i: (i, 0)),
                  _resident((1, LANES))],
        out_specs=pl.BlockSpec((tm, LANES), lambda i: (i, 0)),
        out_shape=jax.ShapeDtypeStruct((n, LANES), F32),
        compiler_params=_cparams(("parallel",)),
        name="expert_slots",
    )(route, rank_slab, pstarts_row)


def _moe_kernel(be_ref, first_ref, nxt_ref, par_ref, nu_ref, x_ref, wg_hbm, wu_hbm, wd_hbm, y_ref,
                gbuf, ubuf, dbuf, g16, u16, d16, sems, *, layer):
    j = pl.program_id(0)

    def weight_copies(e, slot):
        return (pltpu.make_async_copy(wg_hbm.at[layer, e], gbuf.at[slot], sems.at[slot, 0]),
                pltpu.make_async_copy(wu_hbm.at[layer, e], ubuf.at[slot], sems.at[slot, 1]),
                pltpu.make_async_copy(wd_hbm.at[layer, e], dbuf.at[slot], sems.at[slot, 2]))

    @pl.when(j < nu_ref[0])
    def _():
        slot = par_ref[j]

        @pl.when(j == 0)
        def _():
            for cp in weight_copies(be_ref[0], 0):
                cp.start()

        @pl.when(first_ref[j] == 1)
        def _():
            @pl.when(nxt_ref[j] >= 0)
            def _():
                for cp in weight_copies(nxt_ref[j], 1 - slot):
                    cp.start()

            for cp in weight_copies(be_ref[j], slot):
                cp.wait()
            g16[...] = gbuf[slot].astype(BF16)
            u16[...] = ubuf[slot].astype(BF16)
            d16[...] = dbuf[slot].astype(BF16)

        xb = x_ref[...]
        hg = _dot(xb, g16[...])
        hu = _dot(xb, u16[...])
        hb = (hg * jax.nn.sigmoid(hg) * hu).astype(BF16)
        y_ref[...] = _dot(hb, d16[...]).astype(y_ref.dtype)

    @pl.when(j >= nu_ref[0])
    def _():
        y_ref[...] = jnp.zeros_like(y_ref)


def _moe_call(block_e, first, nxt, parity, n_used, u_sorted, w_gate, w_up, w_down, layer):
    n_slots, d = u_sorted.shape
    bm = MOE_BM
    nb = n_slots // bm
    de = w_gate.shape[-1]

    def xmap(j, be, fi, nx_, pa, nu):
        return (jnp.minimum(j, nu[0] - 1), 0)

    grid_spec = pltpu.PrefetchScalarGridSpec(
        num_scalar_prefetch=5,
        grid=(nb,),
        in_specs=[pl.BlockSpec((bm, d), xmap),
                  pl.BlockSpec(memory_space=pl.ANY),
                  pl.BlockSpec(memory_space=pl.ANY),
                  pl.BlockSpec(memory_space=pl.ANY)],
        out_specs=pl.BlockSpec((bm, d), lambda j, *_: (j, 0)),
        scratch_shapes=[pltpu.VMEM((2, d, de), F32), pltpu.VMEM((2, d, de), F32), pltpu.VMEM((2, de, d), F32),
                        pltpu.VMEM((d, de), BF16), pltpu.VMEM((d, de), BF16), pltpu.VMEM((de, d), BF16),
                        pltpu.SemaphoreType.DMA((2, 3))],
    )
    return pl.pallas_call(
        functools.partial(_moe_kernel, layer=layer),
        grid_spec=grid_spec,
        out_shape=jax.ShapeDtypeStruct((n_slots, d), BF16),
        compiler_params=_cparams(("arbitrary",)),
        name="expert_mlp",
    )(block_e, first, nxt, parity, n_used, u_sorted, w_gate, w_up, w_down)


def _combine_kernel(x_ref, y0_ref, y1_ref, r_ref, mod_ref, lg_ref, lb_ref, o_ref, *, alpha):
    slab = r_ref[...]
    g1 = slab[:, 2:3]
    g2 = slab[:, 3:4]
    mx = g1 * y0_ref[...].astype(F32) + g2 * y1_ref[...].astype(F32)
    gate_f = mod_ref[0, 5:6, :]
    o_ref[...] = _layer_norm(alpha * x_ref[...] + gate_f * mx, lg_ref[...], lb_ref[...])


def _combine_call(xs, y0, y1, route, mod, ln_g, ln_b, mod_row, alpha):
    n, d = xs.shape
    tm = ROW_TM
    return pl.pallas_call(
        functools.partial(_combine_kernel, alpha=alpha),
        grid=(n // tm,),
        in_specs=[pl.BlockSpec((tm, d), lambda i: (i, 0)),
                  pl.BlockSpec((tm, d), lambda i: (i, 0)),
                  pl.BlockSpec((tm, d), lambda i: (i, 0)),
                  pl.BlockSpec((tm, LANES), lambda i: (i, 0)),
                  pl.BlockSpec((1, 6, d), lambda i: (mod_row(i, tm), 0, 0)),
                  pl.BlockSpec((1, d), lambda i: (0, 0)),
                  pl.BlockSpec((1, d), lambda i: (0, 0))],
        out_specs=pl.BlockSpec((tm, d), lambda i: (i, 0)),
        out_shape=jax.ShapeDtypeStruct((n, d), F32),
        compiler_params=_cparams(("parallel",)),
        name="moe_combine_ln",
    )(xs, y0, y1, route, mod, ln_g, ln_b)


def _rope_angles(seq, dim):
    tpos = np.arange(seq)
    row = (tpos // GRID_W).astype(np.float32)
    col = (tpos % GRID_W).astype(np.float32)
    quarter = dim // 4
    inv_freq = jnp.asarray(ROPE_THETA, F32) ** (-jnp.arange(quarter, dtype=F32) / quarter)
    ang_r = jnp.asarray(row)[:, None] * inv_freq
    ang_c = jnp.asarray(col)[:, None] * inv_freq
    return jnp.concatenate([ang_r, ang_r, ang_c, ang_c], axis=-1)


def _rot_sign(dim):
    l = np.arange(dim)
    return np.where((l & (dim // 4)) == 0, -1.0, 1.0).astype(np.float32), l ^ (dim // 4)


def _rope_tables(seq, pad_rows):
    ang_a = _rope_angles(seq, HEAD_DIM)
    sign_a, _ = _rot_sign(HEAD_DIM)
    cos_a = jnp.concatenate([jnp.cos(ang_a), jnp.ones((pad_rows, HEAD_DIM), F32)], 0)
    sin_a = jnp.concatenate([jnp.sin(ang_a) * sign_a, jnp.zeros((pad_rows, HEAD_DIM), F32)], 0)
    ang_b = _rope_angles(seq, MLA_ROPE)
    cos_b = jnp.concatenate([jnp.cos(ang_b), jnp.ones((pad_rows, MLA_ROPE), F32)], 0)
    sin_b = jnp.concatenate([jnp.sin(ang_b), jnp.zeros((pad_rows, MLA_ROPE), F32)], 0)
    return cos_a, sin_a, jnp.tile(cos_b, (1, 2)), jnp.tile(sin_b, (1, 2))


ORIG_AV, ORIG_BQ, ORIG_BKR, ORIG_CQ, ORIG_W = 1024, 1280, 2048, 2112, 10176
W_PREP_ROWS = 128


def _w_in_prep_kernel(w_ref, tail_ref, o_ref, *, cq_scale):
    rows = w_ref.shape[1]
    lane = lax.broadcasted_iota(jnp.int32, (rows, LANES), 1)
    low = lane < MLA_ROPE
    quarter = MLA_ROPE // 4

    o_ref[0, :, COL_AQ:COL_BQ] = w_ref[0, :, 0:ORIG_AV].astype(BF16)
    o_ref[0, :, COL_BQ:COL_BKR] = w_ref[0, :, ORIG_BQ:ORIG_BKR].astype(BF16)
    o_ref[0, :, COL_AV:COL_CQ] = w_ref[0, :, ORIG_AV:ORIG_BQ].astype(BF16)
    kb = w_ref[0, :, ORIG_BKR:ORIG_BKR + LANES]
    rot = jnp.where((lane & quarter) == 0, -pltpu.roll(kb, MLA_ROPE - quarter, 1), pltpu.roll(kb, MLA_ROPE + quarter, 1))
    o_ref[0, :, COL_BKR:COL_AV] = jnp.where(low, kb, rot).astype(BF16)
    n_shift = (PROJ_W - COL_CQ) // LANES
    for b in range(n_shift):
        src = ORIG_BKR + LANES * b
        first = w_ref[0, :, src:src + LANES]
        second = w_ref[0, :, src + LANES:src + 2 * LANES] if b < n_shift - 1 else tail_ref[0]
        v = jnp.where(low, pltpu.roll(first, MLA_ROPE, 1), pltpu.roll(second, MLA_ROPE, 1))
        if COL_CQ + LANES * b < COL_CK:
            v = v * cq_scale
        o_ref[0, :, COL_CQ + LANES * b:COL_CQ + LANES * (b + 1)] = v.astype(BF16)


def _permute_w_in(w):
    depth, d, width = w.shape
    assert width == ORIG_W and ORIG_W - ORIG_CQ == PROJ_W - COL_CQ
    tail = jnp.pad(w[:, :, ORIG_W - MLA_ROPE:], ((0, 0), (0, 0), (0, LANES - MLA_ROPE)))
    return pl.pallas_call(
        functools.partial(_w_in_prep_kernel, cq_scale=SCORE_LOG2E / math.sqrt(HEAD_DIM)),
        grid=(depth, d // W_PREP_ROWS),
        in_specs=[pl.BlockSpec((1, W_PREP_ROWS, width), lambda l, i: (l, i, 0)),
                  pl.BlockSpec((1, W_PREP_ROWS, LANES), lambda l, i: (l, i, 0))],
        out_specs=pl.BlockSpec((1, W_PREP_ROWS, PROJ_W), lambda l, i: (l, i, 0)),
        out_shape=jax.ShapeDtypeStruct((depth, d, PROJ_W), BF16),
        compiler_params=_cparams(("parallel", "parallel")),
        name="w_in_prep",
    )(w, tail)


def _rot_cols(w):
    dim = w.shape[-1]
    wr = w.reshape(w.shape[:-1] + (2, 2, dim // 4))
    return jnp.stack([-wr[..., 1, :], wr[..., 0, :]], axis=-2).reshape(w.shape)


def _permute_w_uq(w):
    r = w.shape[0]
    w3 = w.reshape(r, MLA_HEADS, MLA_QK)
    nope = w3[:, :, :MLA_NOPE].reshape(r, MLA_HEADS * MLA_NOPE)
    rope = w3[:, :, MLA_NOPE:]
    rot = _rot_cols(rope)
    zpad = jnp.zeros((r, MLA_HEADS, LANES - MLA_ROPE), w.dtype)
    rope_p = jnp.concatenate([rope, zpad], -1).reshape(r, MLA_HEADS * LANES)
    rot_p = jnp.concatenate([rot, zpad], -1).reshape(r, MLA_HEADS * LANES)
    return jnp.concatenate([nope, rope_p, rot_p], axis=1).astype(BF16)


def _permute_w_ukv(w):
    r = w.shape[0]
    w3 = w.reshape(r, MLA_HEADS, MLA_NOPE + MLA_V)
    kn = w3[:, :, :MLA_NOPE].reshape(r, MLA_HEADS * MLA_NOPE)
    vv = w3[:, :, MLA_NOPE:].reshape(r, MLA_HEADS * MLA_V)
    return jnp.concatenate([kn, vv], axis=1).astype(BF16)


def _na_bias_tables(rpb, seq):
    w, kh, kw = GRID_W, NA_KH, NA_KW
    rows = seq // w
    nh, n_dr, n_dc = rpb.shape
    line = jnp.full((nh, n_dr, 2 * w), NEG_BIG, F32).at[:, :, w - kw:w - kw + n_dc].set(rpb.astype(F32) * SCORE_LOG2E)
    skew = jnp.broadcast_to(line[:, :, None, :], (nh, n_dr, w, 2 * w)).reshape(nh, n_dr, 2 * w * w)
    skew = skew[:, :, :w * (2 * w - 1)].reshape(nh, n_dr, w, 2 * w - 1)
    tc = skew[:, :, :, w - 1:2 * w - 1]
    cq = np.arange(w)[:, None]
    ck = np.arange(w)[None, :]
    cs = np.clip(cq - kw // 2, 0, w - kw)
    col_ok = (ck >= cs) & (ck < cs + kw)
    tc = jnp.where(col_ok[None, None], tc, NEG_BIG)
    tc = jnp.concatenate([tc, jnp.full((nh, 1, w, w), NEG_BIG, F32)], axis=1)
    tc2 = jnp.concatenate([tc, tc], axis=-1)
    blks = []
    tile_starts = (0, NA_ROWS, rows - NA_ROWS)
    for variant, r0 in enumerate(tile_starts):
        rb = int(np.clip(r0 - kh // 2, 0, rows - NA_BAND))
        rq = r0 + np.arange(NA_ROWS)[:, None]
        rk = rb + np.arange(NA_BAND)[None, :]
        rs = np.clip(rq - kh // 2, 0, rows - kh)
        row_ok = (rk >= rs) & (rk < rs + kh)
        band_blk = np.where(row_ok, rk - rq + kh - 1, n_dr)
        win_blk = []
        for hq, off in enumerate(_na_window_offsets(variant, len(tile_starts))):
            part = band_blk[hq * NA_ROWS // 2:(hq + 1) * NA_ROWS // 2]
            assert (np.delete(part, np.s_[off:off + NA_WIN], axis=1) == n_dr).all()
            win_blk.extend(part[:, off:off + NA_WIN].tolist())
        blks.append(win_blk)
    return pl.pallas_call(
        functools.partial(_na_bias_kernel, blks=blks),
        grid=(nh,),
        in_specs=[pl.BlockSpec((1, n_dr + 1, w, 2 * w), lambda h: (h, 0, 0, 0))],
        out_specs=pl.BlockSpec((1, len(blks), NA_ROWS * w, NA_WIN * w), lambda h: (h, 0, 0, 0)),
        out_shape=jax.ShapeDtypeStruct((nh, len(blks), NA_ROWS * w, NA_WIN * w), F32),
        compiler_params=_cparams(("parallel",)),
        name="na_bias_table",
    )(tc2)


def _na_bias_kernel(tc_ref, o_ref, *, blks):
    w = GRID_W
    low = lax.broadcasted_iota(jnp.int32, (w, 2 * w), 1) < w
    for v, blk in enumerate(blks):
        for rq, row in enumerate(blk):
            for pr in range(len(row) // 2):
                pair = jnp.where(low, tc_ref[0, row[2 * pr]], tc_ref[0, row[2 * pr + 1]])
                o_ref[0, v, rq * w:(rq + 1) * w, pr * 2 * w:(pr + 1) * 2 * w] = pair


def kernel(x, c, ctx, c_ctx, w_ada, b_ada, w_in, gqa_q_norm, gqa_k_norm, mla_q_norm, mla_kv_norm, mla_w_uq, mla_w_ukv, na_rpb, w_branch_a, w_branch_b, w_branch_c, w_out, ln1_g, ln1_b, w_router_group, b_router_group, w_router_expert, b_router_expert, w_expert_gate, w_expert_up, w_expert_down, ln2_g, ln2_b):
    bsz, seq, d = x.shape
    n_ctx = ctx.shape[1]
    depth = w_ada.shape[0]
    nx, nc = bsz * seq, bsz * n_ctx
    t = nx + nc
    assert seq % PROJ_TM == 0 and nc == PROJ_TM and seq % (NA_BAND * GRID_W) == 0 and nx % n_ctx == 0
    alpha = (2 * depth) ** 0.25

    def mod_row(i, tm):
        return jnp.minimum(i // (seq // tm), bsz)

    def rope_blk(i, tm):
        return jnp.where(i < nx // tm, i % (seq // tm), seq // tm)

    stream = (x.reshape(nx, d), ctx.reshape(nc, d), 0)
    cc = jnp.concatenate([c, c_ctx[None], jnp.zeros((8 - bsz - 1, d), F32)], axis=0)
    mod_all = _ada_call(cc, w_ada, b_ada).reshape(depth, 8, 6, d)
    cos_a, sin_a, cos_b, sin_b = _rope_tables(seq, ROW_TM)
    w_in_p = _permute_w_in(w_in)

    attn_kw = dict(bsz=bsz, seq=seq, n_ctx=n_ctx)
    for i in range(depth):
        last = i == depth - 1
        n_rows = nx if last else t
        mod = mod_all[i]
        p = _proj_call(stream, nx, t, mod, w_in_p, i, mod_row)

        qa, ka = _prep_a_call(p, cos_a, sin_a, gqa_q_norm[i][None], gqa_k_norm[i][None], rope_blk)
        a_kw = dict(n_heads=GQA_HEADS, dq=HEAD_DIM, q_col=lambda h: h, k_col=lambda h: h // GQA_GROUP,
                    v_col=lambda h: COL_AV // HEAD_DIM + h // GQA_GROUP, **attn_kw)
        o_a = _dense_attn_call(qa, ka, p, **a_kw)
        qb, kb, vb = _prep_b_call(p, cos_b, sin_b, mla_q_norm[i][None], mla_kv_norm[i][None],
                                  _permute_w_uq(mla_w_uq[i]), _permute_w_ukv(mla_w_ukv[i]), rope_blk)
        b_kw = dict(n_heads=MLA_HEADS, dq=MLA_PAD, q_col=lambda h: h, k_col=lambda h: h, v_col=lambda h: h, **attn_kw)
        o_b = _dense_attn_call(qb, kb, vb, **b_kw)
        o_c = _na_call(p, _na_bias_tables(na_rpb[i], seq), **attn_kw)
        o_lat = (o_a, o_b, o_c)
        if last:
            o_ctx = o_lat
        else:
            c_kw = dict(n_heads=NA_HEADS, dq=HEAD_DIM, q_col=lambda h: COL_CQ // HEAD_DIM + h,
                        k_col=lambda h: COL_CK // HEAD_DIM + h, v_col=lambda h: COL_CV // HEAD_DIM + h, **attn_kw)
            o_ctx = (_ctx_attn_call(qa, ka, p, **a_kw), _ctx_attn_call(qb, kb, vb, **b_kw),
                     _ctx_attn_call(p, p, p, **c_kw))

        mrg = _branch_call(o_lat, o_ctx, p, w_branch_a[i].astype(BF16), w_branch_b[i].astype(BF16),
                           w_branch_c[i].astype(BF16), n_rows)
        wr = jnp.concatenate([w_router_group[i], w_router_expert[i],
                              jnp.zeros((d, LANES - N_GROUPS - N_EXPERTS), F32)], axis=1)
        br = jnp.concatenate([b_router_group[i], b_router_expert[i],
                              jnp.zeros((LANES - N_GROUPS - N_EXPERTS,), F32)])[None]
        wr_cat = jnp.concatenate(_split_bf16(wr), axis=1)
        xs, u_f, route = _out_call(mrg, stream, nx, mod, w_out[i].astype(BF16), ln1_g[i][None], ln1_b[i][None],
                                   wr_cat, br, mod_row, n_rows, alpha)

        rank_slab, cnt = _rank_call(route)
        eid = route[:, :TOP_K].astype(jnp.int32)
        counts = cnt[0, :N_EXPERTS].astype(jnp.int32)
        padded = ((counts + MOE_BM - 1) // MOE_BM) * MOE_BM
        pends = jnp.cumsum(padded)
        pstarts = pends - padded
        pstarts_row = jnp.concatenate([pstarts.astype(F32), jnp.zeros((LANES - N_EXPERTS,), F32)])[None]
        dest = _dest_call(route, rank_slab, pstarts_row)[:, :TOP_K].astype(jnp.int32)
        nb = -(-(n_rows * TOP_K) // MOE_BM) + N_EXPERTS
        n_used = (pends[-1] // MOE_BM).astype(jnp.int32)
        blk = jnp.minimum(jnp.arange(nb, dtype=jnp.int32), n_used - 1)
        block_e = jnp.clip(jnp.searchsorted(pends, blk * MOE_BM, side="right"), 0, N_EXPERTS - 1).astype(jnp.int32)
        order = jnp.argsort(eid.reshape(-1), stable=True).astype(jnp.int32)
        starts = jnp.cumsum(counts) - counts
        slot = jnp.arange(nb * MOE_BM, dtype=jnp.int32).reshape(nb, MOE_BM)
        blk_shift = (starts - pstarts)[block_e][:, None]
        blk_limit = (pstarts + counts)[block_e][:, None]
        pos = jnp.clip(slot + blk_shift, 0, n_rows * TOP_K - 1)
        slot_tok = jnp.where(slot < blk_limit, order.at[pos].get(mode="promise_in_bounds") // TOP_K,
                             slot % n_rows).reshape(-1)
        u_sorted = u_f.at[slot_tok].get(mode="promise_in_bounds")
        jj = jnp.arange(nb, dtype=jnp.int32)
        first = ((jj == 0) | (block_e != jnp.roll(block_e, 1))).astype(jnp.int32)
        parity = (jnp.cumsum(first) - 1) % 2
        live = jnp.where(counts > 0, jnp.arange(N_EXPERTS, dtype=jnp.int32), N_EXPERTS)
        next_live = jnp.concatenate([lax.cummin(live, reverse=True)[1:], jnp.full((1,), N_EXPERTS, jnp.int32)])
        nxt = jnp.where(next_live < N_EXPERTS, next_live, -1)[block_e]
        y_sorted = _moe_call(block_e, first, nxt.astype(jnp.int32), parity.astype(jnp.int32), n_used[None],
                             u_sorted, w_expert_gate, w_expert_up, w_expert_down, i)
        y0 = y_sorted.at[dest[:, 0]].get(mode="promise_in_bounds")
        y1 = y_sorted.at[dest[:, 1]].get(mode="promise_in_bounds")
        xs = _combine_call(xs, y0, y1, route, mod, ln2_g[i][None], ln2_b[i][None], mod_row, alpha)
        stream = (xs, xs, nx)
    return xs.reshape(bsz, seq, d)
```

```python
import functools
import math

import jax
import jax.numpy as jnp
import numpy as np
from jax import lax
from jax.experimental import pallas as pl
from jax.experimental.pallas import tpu as pltpu

F32 = jnp.float32
BF16 = jnp.bfloat16

HEAD_DIM = 128
GRID_W = 64
ROPE_THETA = 10000.0
GQA_HEADS, GQA_KV_HEADS = 6, 2
GQA_GROUP = GQA_HEADS // GQA_KV_HEADS
MLA_HEADS, MLA_Q_RANK, MLA_KV_RANK = 5, 512, 256
MLA_NOPE, MLA_ROPE, MLA_V = 128, 64, 128
MLA_QK = MLA_NOPE + MLA_ROPE
MLA_PAD = 256
NA_HEADS, NA_KH, NA_KW = 5, 8, 16
N_GROUPS, EXPERTS_PER_GROUP, TOP_K, D_EXPERT = 8, 8, 2, 512
N_EXPERTS = N_GROUPS * EXPERTS_PER_GROUP
LN_EPS = 1e-6
RMS_EPS = 1e-6
NEG_BIG = -1e30
SCORE_LOG2E = math.log2(math.e)

LANES = 128
VMEM_LIMIT = 56 * 1024 * 1024
PROJ_TM, PROJ_TN = 1024, 1024
ROW_TM = 512
MM_TM = 512
ATTN_TQ = 512
ATTN_SUB = 4
ATTN_KC = 1024
NA_ROWS = 8
NA_SUB = 2
NA_BAND = 16
NA_WIN = 12
MOE_BM = 256
ADA_TN = 512

COL_AQ, COL_AK = 0, 768
COL_BQ, COL_BKV, COL_BKR = 1024, 1536, 1792
COL_AV, COL_CQ, COL_CK, COL_CV = 1920, 2176, 2816, 3456
COL_GATE = 4096
PROJ_W = 10240


def _cparams(sem):
    return pltpu.CompilerParams(dimension_semantics=sem, vmem_limit_bytes=VMEM_LIMIT)


def _resident(shape):
    return pl.BlockSpec(shape, lambda *_: (0,) * len(shape), pipeline_mode=pl.Buffered(1))


def _dot(a, b):
    return jnp.dot(a, b, preferred_element_type=F32)


def _dot_nt(a, b):
    return lax.dot_general(a, b, (((1,), (1,)), ((), ())), preferred_element_type=F32)


def _split_bf16(a):
    hi = a.astype(BF16)
    lo = (a - hi.astype(F32)).astype(BF16)
    return hi, lo


def _ada_kernel(c_ref, w_ref, b_ref, o_ref):
    a = c_ref[...]
    a = a * jax.nn.sigmoid(a)
    a_hi, a_lo = _split_bf16(a)
    w_hi, w_lo = _split_bf16(w_ref[0])
    acc = _dot(a_hi, w_hi) + _dot(a_lo, w_hi) + _dot(a_hi, w_lo)
    o_ref[0] = acc + b_ref[0]


def _ada_call(cc, w_ada, b_ada):
    depth, d, n = w_ada.shape
    rows = cc.shape[0]
    return pl.pallas_call(
        _ada_kernel,
        grid=(depth, n // ADA_TN),
        in_specs=[pl.BlockSpec((rows, d), lambda l, j: (0, 0)),
                  pl.BlockSpec((1, d, ADA_TN), lambda l, j: (l, 0, j)),
                  pl.BlockSpec((1, 1, ADA_TN), lambda l, j: (l, 0, j))],
        out_specs=pl.BlockSpec((1, rows, ADA_TN), lambda l, j: (l, 0, j)),
        out_shape=jax.ShapeDtypeStruct((depth, rows, n), F32),
        compiler_params=_cparams(("parallel", "parallel")),
        name="ada_modulation",
    )(cc, w_ada, b_ada.reshape(depth, 1, n))


def _proj_kernel(xl_ref, xc_ref, mod_ref, w_ref, o_ref, u_scr, *, n_lat_tiles):
    @pl.when(pl.program_id(1) == 0)
    def _():
        def modulate(x):
            u_scr[...] = (x * (1.0 + mod_ref[0, 1:2, :]) + mod_ref[0, 0:1, :]).astype(BF16)

        @pl.when(pl.program_id(0) < n_lat_tiles)
        def _():
            modulate(xl_ref[...])

        @pl.when(pl.program_id(0) >= n_lat_tiles)
        def _():
            modulate(xc_ref[...])

    o_ref[...] = _dot(u_scr[...], w_ref[0]).astype(BF16)


def _stream_specs(stream, tm, n_lat):
    lat, ctx, ctx_row0 = stream
    d = lat.shape[1]
    n_lat_tiles = n_lat // tm
    ctx_tile0 = ctx_row0 // tm

    def lat_map(i, *_):
        return (jnp.minimum(i, n_lat_tiles - 1), 0)

    def ctx_map(i, *_):
        return (ctx_tile0 + jnp.maximum(i - n_lat_tiles, 0), 0)

    return [pl.BlockSpec((tm, d), lat_map), pl.BlockSpec((tm, d), ctx_map, pipeline_mode=pl.Buffered(1))]


def _proj_call(stream, n_lat, n_tok, mod, w_in_p, layer, mod_row):
    d = stream[0].shape[1]
    return pl.pallas_call(
        functools.partial(_proj_kernel, n_lat_tiles=n_lat // PROJ_TM),
        grid=(n_tok // PROJ_TM, PROJ_W // PROJ_TN),
        in_specs=_stream_specs(stream, PROJ_TM, n_lat) + [
            pl.BlockSpec((1, 6, d), lambda i, j: (mod_row(i, PROJ_TM), 0, 0)),
            pl.BlockSpec((1, d, PROJ_TN), lambda i, j: (layer, 0, j))],
        out_specs=pl.BlockSpec((PROJ_TM, PROJ_TN), lambda i, j: (i, j)),
        out_shape=jax.ShapeDtypeStruct((n_tok, PROJ_W), BF16),
        scratch_shapes=[pltpu.VMEM((PROJ_TM, d), BF16)],
        compiler_params=_cparams(("parallel", "arbitrary")),
        name="input_projection",
    )(stream[0], stream[1], mod, w_in_p)


def _rope128(y, cos, sin_signed, first_quarter):
    rot = jnp.where(first_quarter, pltpu.roll(y, 96, 1), pltpu.roll(y, 32, 1))
    return y * cos + rot * sin_signed


def _prep_a_kernel(p_ref, cos_ref, sin_ref, gq_ref, gk_ref, q_ref, k_ref, *, scale):
    cos = cos_ref[...]
    sin_signed = sin_ref[...]
    lane = lax.broadcasted_iota(jnp.int32, cos.shape, 1)
    first_quarter = (lane & 32) == 0
    for h in range(GQA_HEADS + GQA_KV_HEADS):
        xh = p_ref[:, h * HEAD_DIM:(h + 1) * HEAD_DIM].astype(F32)
        r = lax.rsqrt(jnp.mean(xh * xh, axis=1, keepdims=True) + RMS_EPS)
        if h < GQA_HEADS:
            y = _rope128(xh * r * gq_ref[...], cos, sin_signed, first_quarter) * scale
            q_ref[:, h * HEAD_DIM:(h + 1) * HEAD_DIM] = y.astype(BF16)
        else:
            hk = h - GQA_HEADS
            y = _rope128(xh * r * gk_ref[...], cos, sin_signed, first_quarter)
            k_ref[:, hk * HEAD_DIM:(hk + 1) * HEAD_DIM] = y.astype(BF16)


def _prep_a_call(p, cos_t, sin_t, gq, gk, rope_blk):
    t = p.shape[0]
    tm = ROW_TM
    wq, wk = GQA_HEADS * HEAD_DIM, GQA_KV_HEADS * HEAD_DIM
    return pl.pallas_call(
        functools.partial(_prep_a_kernel, scale=SCORE_LOG2E / math.sqrt(HEAD_DIM)),
        grid=(t // tm,),
        in_specs=[pl.BlockSpec((tm, wq + wk), lambda i: (i, 0)),
                  pl.BlockSpec((tm, HEAD_DIM), lambda i: (rope_blk(i, tm), 0)),
                  pl.BlockSpec((tm, HEAD_DIM), lambda i: (rope_blk(i, tm), 0)),
                  pl.BlockSpec((1, HEAD_DIM), lambda i: (0, 0)),
                  pl.BlockSpec((1, HEAD_DIM), lambda i: (0, 0))],
        out_specs=[pl.BlockSpec((tm, wq), lambda i: (i, 0)),
                   pl.BlockSpec((tm, wk), lambda i: (i, 0))],
        out_shape=[jax.ShapeDtypeStruct((t, wq), BF16), jax.ShapeDtypeStruct((t, wk), BF16)],
        compiler_params=_cparams(("parallel",)),
        name="gqa_qk_prep",
    )(p, cos_t, sin_t, gq, gk)


def _prep_b_kernel(ql_ref, kvl_ref, kr_ref, cos_ref, sin_ref, gq_ref, gkv_ref, wq_ref, wkv_ref,
                   q_ref, k_ref, v_ref, *, scale):
    nh, hp = MLA_HEADS, MLA_PAD
    wn = nh * MLA_NOPE
    cos = cos_ref[...]
    sin = sin_ref[...]
    ql = ql_ref[...].astype(F32)
    ql = ql * lax.rsqrt(jnp.mean(ql * ql, axis=1, keepdims=True) + RMS_EPS) * gq_ref[...]
    qf = _dot(ql.astype(BF16), wq_ref[...])
    kvl = kvl_ref[...].astype(F32)
    kvl = kvl * lax.rsqrt(jnp.mean(kvl * kvl, axis=1, keepdims=True) + RMS_EPS) * gkv_ref[...]
    kvf = _dot(kvl.astype(BF16), wkv_ref[...])
    lane = lax.broadcasted_iota(jnp.int32, cos.shape, 1)
    low = lane < MLA_ROPE
    tk = kr_ref[...].astype(F32) * jnp.where(low, cos, sin)
    kr = jnp.where(low, tk + pltpu.roll(tk, MLA_ROPE, 1), 0.0).astype(BF16)
    for h in range(nh):
        q_ref[:, h * hp:h * hp + LANES] = (qf[:, h * LANES:(h + 1) * LANES] * scale).astype(BF16)
        qr = qf[:, wn + h * LANES:wn + (h + 1) * LANES] * cos + qf[:, 2 * wn + h * LANES:2 * wn + (h + 1) * LANES] * sin
        q_ref[:, h * hp + LANES:(h + 1) * hp] = (qr * scale).astype(BF16)
        k_ref[:, h * hp:h * hp + LANES] = kvf[:, h * LANES:(h + 1) * LANES].astype(BF16)
        k_ref[:, h * hp + LANES:(h + 1) * hp] = kr
    v_ref[...] = kvf[:, wn:].astype(BF16)


def _prep_b_call(p, cos_t, sin_t, gq, gkv, wq, wkv, rope_blk):
    t = p.shape[0]
    tm = ROW_TM
    nh = MLA_HEADS
    return pl.pallas_call(
        functools.partial(_prep_b_kernel, scale=SCORE_LOG2E / math.sqrt(MLA_QK)),
        grid=(t // tm,),
        in_specs=[pl.BlockSpec((tm, MLA_Q_RANK), lambda i: (i, COL_BQ // MLA_Q_RANK)),
                  pl.BlockSpec((tm, MLA_KV_RANK), lambda i: (i, COL_BKV // MLA_KV_RANK)),
                  pl.BlockSpec((tm, LANES), lambda i: (i, COL_BKR // LANES)),
                  pl.BlockSpec((tm, LANES), lambda i: (rope_blk(i, tm), 0)),
                  pl.BlockSpec((tm, LANES), lambda i: (rope_blk(i, tm), 0)),
                  pl.BlockSpec((1, MLA_Q_RANK), lambda i: (0, 0)),
                  pl.BlockSpec((1, MLA_KV_RANK), lambda i: (0, 0)),
                  _resident(wq.shape), _resident(wkv.shape)],
        out_specs=[pl.BlockSpec((tm, nh * MLA_PAD), lambda i: (i, 0)),
                   pl.BlockSpec((tm, nh * MLA_PAD), lambda i: (i, 0)),
                   pl.BlockSpec((tm, nh * MLA_V), lambda i: (i, 0))],
        out_shape=[jax.ShapeDtypeStruct((t, nh * MLA_PAD), BF16),
                   jax.ShapeDtypeStruct((t, nh * MLA_PAD), BF16),
                   jax.ShapeDtypeStruct((t, nh * MLA_V), BF16)],
        compiler_params=_cparams(("parallel",)),
        name="mla_prep",
    )(p, p, p, cos_t, sin_t, gq, gkv, wq, wkv)


def _online_attend(q, chunks):
    m = z = acc = None
    for k, v, bias in chunks:
        s = _dot_nt(q, k)
        if bias is not None:
            s = s + bias
        cm = jnp.max(s, axis=1, keepdims=True)
        if m is None:
            m = cm
            p = jnp.exp2(s - m)
            z = jnp.sum(p, axis=1, keepdims=True)
            acc = _dot(p.astype(BF16), v)
        else:
            m_new = jnp.maximum(m, cm)
            corr = jnp.exp2(m - m_new)
            p = jnp.exp2(s - m_new)
            z = z * corr + jnp.sum(p, axis=1, keepdims=True)
            acc = acc * corr + _dot(p.astype(BF16), v)
            m = m_new
    return acc / z


def _online_attend_ones(q, chunks, dv):
    m = acc = None
    for k, v_ones, bias in chunks:
        s = _dot_nt(q, k)
        if bias is not None:
            s = s + bias
        cm = jnp.max(s, axis=1, keepdims=True)
        if m is None:
            m = cm
            acc = _dot(jnp.exp2((s - m).astype(BF16)), v_ones)
        else:
            m_new = jnp.maximum(m, cm)
            corr = jnp.exp2(m - m_new)
            acc = acc * corr + _dot(jnp.exp2((s - m_new).astype(BF16)), v_ones)
            m = m_new
    return acc[:, :dv] / acc[:, dv:dv + 1]


def _dense_attn_kernel(q_ref, kl_ref, vl_ref, kc_ref, vc_ref, o_ref, v_ones):
    n_lat = kl_ref.shape[0]
    dv = vl_ref.shape[1]

    @pl.when(pl.program_id(2) == 0)
    def _():
        v_ones[0:n_lat, 0:dv] = vl_ref[...]
        v_ones[n_lat:, 0:dv] = vc_ref[...]
        v_ones[:, dv:] = jnp.ones((v_ones.shape[0], dv), BF16)

    chunks = [(kl_ref[c:c + ATTN_KC, :], v_ones[c:c + ATTN_KC, :], None) for c in range(0, n_lat, ATTN_KC)]
    chunks.append((kc_ref[...], v_ones[n_lat:, :], None))
    for r in range(0, q_ref.shape[0], ATTN_TQ):
        o_ref[r:r + ATTN_TQ, :] = _online_attend_ones(q_ref[r:r + ATTN_TQ, :], chunks, dv).astype(o_ref.dtype)


def _dense_attn_call(q, k, v, *, n_heads, dq, q_col, k_col, v_col, bsz, seq, n_ctx):
    tq = ATTN_TQ * ATTN_SUB
    tpb = seq // tq
    ctx0 = bsz * seq // n_ctx
    dv = HEAD_DIM
    return pl.pallas_call(
        _dense_attn_kernel,
        grid=(bsz, n_heads, tpb),
        in_specs=[pl.BlockSpec((tq, dq), lambda b, h, i: (b * tpb + i, q_col(h))),
                  pl.BlockSpec((seq, dq), lambda b, h, i: (b, k_col(h))),
                  pl.BlockSpec((seq, dv), lambda b, h, i: (b, v_col(h))),
                  pl.BlockSpec((n_ctx, dq), lambda b, h, i: (ctx0 + b, k_col(h))),
                  pl.BlockSpec((n_ctx, dv), lambda b, h, i: (ctx0 + b, v_col(h)))],
        out_specs=pl.BlockSpec((tq, dv), lambda b, h, i: (b * tpb + i, h)),
        out_shape=jax.ShapeDtypeStruct((bsz * seq, n_heads * dv), BF16),
        scratch_shapes=[pltpu.VMEM((seq + n_ctx, 2 * dv), BF16)],
        compiler_params=_cparams(("parallel", "parallel", "arbitrary")),
        name="dense_attention",
    )(q, k, v, k, v)


def _ctx_attn_kernel(q_ref, kc_ref, vc_ref, o_ref):
    o_ref[...] = _online_attend(q_ref[...], [(kc_ref[...], vc_ref[...], None)]).astype(o_ref.dtype)


def _ctx_attn_call(q, k, v, *, n_heads, dq, q_col, k_col, v_col, bsz, seq, n_ctx):
    ctx0 = bsz * seq // n_ctx
    dv = HEAD_DIM
    return pl.pallas_call(
        _ctx_attn_kernel,
        grid=(bsz, n_heads),
        in_specs=[pl.BlockSpec((n_ctx, dq), lambda b, h: (ctx0 + b, q_col(h))),
                  pl.BlockSpec((n_ctx, dq), lambda b, h: (ctx0 + b, k_col(h))),
                  pl.BlockSpec((n_ctx, dv), lambda b, h: (ctx0 + b, v_col(h)))],
        out_specs=pl.BlockSpec((n_ctx, dv), lambda b, h: (b, h)),
        out_shape=jax.ShapeDtypeStruct((bsz * n_ctx, n_heads * dv), BF16),
        compiler_params=_cparams(("parallel", "parallel")),
        name="context_attention",
    )(q, k, v)


def _na_window_offsets(variant, n_variants):
    shift = NA_BAND - NA_WIN
    return ((0, 0), (0, shift), (shift, shift))[0 if variant == 0 else (2 if variant == n_variants - 1 else 1)]


def _na_kernel(q_ref, k_ref, v_ref, kc_ref, vc_ref, *rest, rows_total, n_tiles):
    bias_refs, o_ref, v_ones = rest[:NA_SUB], rest[NA_SUB], rest[NA_SUB + 1]
    n_lat = k_ref.shape[0]
    dv = v_ref.shape[1]
    tile = NA_ROWS * GRID_W
    half = tile // 2
    win = NA_WIN * GRID_W
    shift = (NA_BAND - NA_WIN) * GRID_W

    @pl.when(pl.program_id(2) == 0)
    def _():
        v_ones[0:n_lat, 0:dv] = v_ref[...]
        v_ones[n_lat:, 0:dv] = vc_ref[...]
        v_ones[:, dv:] = jnp.ones((v_ones.shape[0], dv), BF16)

    ctx = (kc_ref[...], v_ones[n_lat:, :], None)
    for t in range(NA_SUB):
        i = pl.program_id(2) * NA_SUB + t
        start_blk = jnp.clip(2 * i - 1, 0, (rows_total - NA_BAND) // 4)
        band0 = start_blk * (4 * GRID_W)
        starts = (band0 + jnp.where(i == n_tiles - 1, shift, 0), band0 + jnp.where(i == 0, 0, shift))
        for hq, start in enumerate(starts):
            start = pl.multiple_of(start, 4 * GRID_W)
            rows = slice(t * tile + hq * half, t * tile + (hq + 1) * half)
            bias = bias_refs[t][0, 0, hq * half:(hq + 1) * half, :]
            chunks = [(k_ref[pl.ds(start, win), :], v_ones[pl.ds(start, win), :], bias), ctx]
            o_ref[rows, :] = _online_attend_ones(q_ref[rows, :], chunks, dv).astype(o_ref.dtype)


def _na_call(p, bias_tab, *, bsz, seq, n_ctx):
    tile = NA_ROWS * GRID_W
    n_tiles = seq // tile
    tq = tile * NA_SUB
    tpb = seq // tq
    rows_total = seq // GRID_W
    ctx0 = bsz * seq // n_ctx
    d = HEAD_DIM
    qc, kc, vc = COL_CQ // d, COL_CK // d, COL_CV // d

    def bias_spec(t):
        def index(b, h, i):
            tile_idx = i * NA_SUB + t
            return (h, jnp.where(tile_idx == 0, 0, jnp.where(tile_idx == n_tiles - 1, 2, 1)), 0, 0)
        return pl.BlockSpec((1, 1, tile, NA_WIN * GRID_W), index)

    return pl.pallas_call(
        functools.partial(_na_kernel, rows_total=rows_total, n_tiles=n_tiles),
        grid=(bsz, NA_HEADS, tpb),
        in_specs=[pl.BlockSpec((tq, d), lambda b, h, i: (b * tpb + i, qc + h)),
                  pl.BlockSpec((seq, d), lambda b, h, i: (b, kc + h)),
                  pl.BlockSpec((seq, d), lambda b, h, i: (b, vc + h)),
                  pl.BlockSpec((n_ctx, d), lambda b, h, i: (ctx0 + b, kc + h)),
                  pl.BlockSpec((n_ctx, d), lambda b, h, i: (ctx0 + b, vc + h))]
                 + [bias_spec(t) for t in range(NA_SUB)],
        out_specs=pl.BlockSpec((tq, d), lambda b, h, i: (b * tpb + i, h)),
        out_shape=jax.ShapeDtypeStruct((bsz * seq, NA_HEADS * d), BF16),
        scratch_shapes=[pltpu.VMEM((seq + n_ctx, 2 * d), BF16)],
        compiler_params=_cparams(("parallel", "parallel", "arbitrary")),
        name="neighbourhood_attention",
    )(p, p, p, p, p, *([bias_tab] * NA_SUB))


def _branch_kernel(oa_ref, ob_ref, oc_ref, oac_ref, obc_ref, occ_ref, ga_ref, gb_ref, gc_ref,
                   wa_ref, wb_ref, wc_ref, o_ref, *, n_lat_tiles):
    def gated(g_ref, w_ref, o_tile):
        return jax.nn.sigmoid(g_ref[...].astype(F32)) * _dot(o_tile, w_ref[...])

    @pl.when(pl.program_id(0) < n_lat_tiles)
    def _():
        acc = gated(ga_ref, wa_ref, oa_ref[...]) + gated(gb_ref, wb_ref, ob_ref[...]) + gated(gc_ref, wc_ref, oc_ref[...])
        o_ref[...] = acc.astype(BF16)

    @pl.when(pl.program_id(0) >= n_lat_tiles)
    def _():
        acc = gated(ga_ref, wa_ref, oac_ref[...]) + gated(gb_ref, wb_ref, obc_ref[...]) + gated(gc_ref, wc_ref, occ_ref[...])
        o_ref[...] = acc.astype(BF16)


def _branch_call(o_lat, o_ctx, p, wa, wb, wc, n_rows):
    d = wa.shape[1]
    tm = MM_TM
    g0 = COL_GATE // d
    n_lat_tiles = o_lat[0].shape[0] // tm
    lat_specs = [pl.BlockSpec((tm, o.shape[1]), lambda i: (jnp.minimum(i, n_lat_tiles - 1), 0)) for o in o_lat]
    ctx_specs = [pl.BlockSpec((tm, o.shape[1]), lambda i: (jnp.maximum(i - n_lat_tiles, 0), 0)) for o in o_ctx]
    return pl.pallas_call(
        functools.partial(_branch_kernel, n_lat_tiles=n_lat_tiles),
        grid=(n_rows // tm,),
        in_specs=lat_specs + ctx_specs + [
            pl.BlockSpec((tm, d), lambda i: (i, g0)),
            pl.BlockSpec((tm, d), lambda i: (i, g0 + 1)),
            pl.BlockSpec((tm, d), lambda i: (i, g0 + 2)),
            _resident(wa.shape), _resident(wb.shape), _resident(wc.shape)],
        out_specs=pl.BlockSpec((tm, d), lambda i: (i, 0)),
        out_shape=jax.ShapeDtypeStruct((n_rows, d), BF16),
        compiler_params=_cparams(("parallel",)),
        name="branch_merge",
    )(*o_lat, *o_ctx, p, p, p, wa, wb, wc)


def _layer_norm(z, g, b):
    mu = jnp.mean(z, axis=1, keepdims=True)
    zc = z - mu
    var = jnp.mean(zc * zc, axis=1, keepdims=True)
    return zc * lax.rsqrt(var + LN_EPS) * g + b


def _route(logits):
    lane = lax.broadcasted_iota(jnp.int32, logits.shape, 1)
    lane_f = lane.astype(F32)
    is_g = lane < N_GROUPS
    gl = jnp.where(is_g, logits, NEG_BIG)
    mg = jnp.max(gl, axis=1, keepdims=True)
    gsel = jnp.min(jnp.where(gl == mg, lane_f, float(LANES)), axis=1, keepdims=True)
    zg = jnp.sum(jnp.where(is_g, jnp.exp(gl - mg), 0.0), axis=1, keepdims=True)
    lo = N_GROUPS + gsel * EXPERTS_PER_GROUP
    is_e = (lane_f >= lo) & (lane_f < lo + EXPERTS_PER_GROUP)
    el = jnp.where(is_e, logits, NEG_BIG)
    t1 = jnp.max(el, axis=1, keepdims=True)
    i1 = jnp.min(jnp.where(el == t1, lane_f, float(LANES)), axis=1, keepdims=True)
    el2 = jnp.where(lane_f == i1, NEG_BIG, el)
    t2 = jnp.max(el2, axis=1, keepdims=True)
    i2 = jnp.min(jnp.where(el2 == t2, lane_f, float(LANES)), axis=1, keepdims=True)
    dd = jnp.exp(t2 - t1)
    g1 = 1.0 / (zg * (1.0 + dd))
    g2 = g1 * dd
    slab = jnp.where(lane == 0, i1 - N_GROUPS,
                     jnp.where(lane == 1, i2 - N_GROUPS,
                               jnp.where(lane == 2, g1, jnp.where(lane == 3, g2, 0.0))))
    return slab


def _out_kernel(m_ref, xl_ref, xc_ref, mod_ref, w_ref, lg_ref, lb_ref, wr_ref, br_ref,
                xo_ref, u_ref, r_ref, *, alpha, n_lat_tiles):
    @pl.when(pl.program_id(0) < n_lat_tiles)
    def _():
        xo_ref[...] = xl_ref[...]

    @pl.when(pl.program_id(0) >= n_lat_tiles)
    def _():
        xo_ref[...] = xc_ref[...]

    y = _dot(m_ref[...], w_ref[...])
    gate_m = mod_ref[0, 2:3, :]
    shift_f = mod_ref[0, 3:4, :]
    scale_f = mod_ref[0, 4:5, :]
    xn = _layer_norm(alpha * xo_ref[...] + gate_m * y, lg_ref[...], lb_ref[...])
    xo_ref[...] = xn
    u = xn * (1.0 + scale_f) + shift_f
    u_hi, u_lo = _split_bf16(u)
    u_ref[...] = u_hi
    both = _dot(u_hi, wr_ref[...])
    logits = both[:, :LANES] + both[:, LANES:] + _dot(u_lo, wr_ref[:, :LANES]) + br_ref[...]
    r_ref[...] = _route(logits)


def _out_call(mrg, stream, n_lat, mod, w_out, ln_g, ln_b, wr_cat, br, mod_row, n_rows, alpha):
    d = stream[0].shape[1]
    tm = MM_TM
    return pl.pallas_call(
        functools.partial(_out_kernel, alpha=alpha, n_lat_tiles=n_lat // tm),
        grid=(n_rows // tm,),
        in_specs=[pl.BlockSpec((tm, d), lambda i: (i, 0))] + _stream_specs(stream, tm, n_lat) + [
                  pl.BlockSpec((1, 6, d), lambda i: (mod_row(i, tm), 0, 0)),
                  _resident((d, d)), _resident((1, d)), _resident((1, d)),
                  _resident((d, 2 * LANES)), _resident((1, LANES))],
        out_specs=[pl.BlockSpec((tm, d), lambda i: (i, 0)),
                   pl.BlockSpec((tm, d), lambda i: (i, 0)),
                   pl.BlockSpec((tm, LANES), lambda i: (i, 0))],
        out_shape=[jax.ShapeDtypeStruct((n_rows, d), F32),
                   jax.ShapeDtypeStruct((n_rows, d), BF16),
                   jax.ShapeDtypeStruct((n_rows, LANES), F32)],
        compiler_params=_cparams(("parallel",)),
        name="out_proj_ln_route",
    )(mrg, stream[0], stream[1], mod, w_out, ln_g, ln_b, wr_cat, br)


def _rank_kernel(r_ref, rank_ref, cnt_ref, carry):
    @pl.when(pl.program_id(0) == 0)
    def _():
        carry[...] = jnp.zeros_like(carry)

    slab = r_ref[...]
    tm = slab.shape[0]
    lane = lax.broadcasted_iota(jnp.int32, slab.shape, 1)
    lane_f = lane.astype(F32)
    e1 = slab[:, 0:1]
    e2 = slab[:, 1:2]
    hit1 = lane_f == e1
    hit2 = lane_f == e2
    onehot = jnp.where(hit1 | hit2, 1.0, 0.0)
    row = lax.broadcasted_iota(jnp.int32, (tm, tm), 0)
    col = lax.broadcasted_iota(jnp.int32, (tm, tm), 1)
    lower = jnp.where(col < row, 1.0, 0.0).astype(BF16)
    before = _dot(lower, onehot.astype(BF16)) + carry[0:1, :]
    r1 = jnp.sum(jnp.where(hit1, before, 0.0), axis=1, keepdims=True)
    r2 = jnp.sum(jnp.where(hit2, before, 0.0), axis=1, keepdims=True)
    rank_ref[...] = jnp.where(lane == 0, r1, jnp.where(lane == 1, r2, 0.0))
    carry[...] = carry[...] + jnp.sum(onehot, axis=0, keepdims=True)
    cnt_ref[...] = carry[...]


def _rank_call(route):
    n = route.shape[0]
    tm = ROW_TM
    return pl.pallas_call(
        _rank_kernel,
        grid=(n // tm,),
        in_specs=[pl.BlockSpec((tm, LANES), lambda i: (i, 0))],
        out_specs=[pl.BlockSpec((tm, LANES), lambda i: (i, 0)),
                   pl.BlockSpec((8, LANES), lambda i: (0, 0))],
        out_shape=[jax.ShapeDtypeStruct((n, LANES), F32), jax.ShapeDtypeStruct((8, LANES), F32)],
        scratch_shapes=[pltpu.VMEM((8, LANES), F32)],
        compiler_params=_cparams(("arbitrary",)),
        name="expert_ranks",
    )(route)


def _dest_kernel(r_ref, rank_ref, ps_ref, o_ref):
    slab = r_ref[...]
    rank = rank_ref[...]
    lane = lax.broadcasted_iota(jnp.int32, slab.shape, 1)
    lane_f = lane.astype(F32)
    ps = ps_ref[...]
    s1 = jnp.sum(jnp.where(lane_f == slab[:, 0:1], ps, 0.0), axis=1, keepdims=True) + rank[:, 0:1]
    s2 = jnp.sum(jnp.where(lane_f == slab[:, 1:2], ps, 0.0), axis=1, keepdims=True) + rank[:, 1:2]
    o_ref[...] = jnp.where(lane == 0, s1, jnp.where(lane == 1, s2, 0.0))


def _dest_call(route, rank_slab, pstarts_row):
    n = route.shape[0]
    tm = n // 8
    return pl.pallas_call(
        _dest_kernel,
        grid=(n // tm,),
        in_specs=[pl.BlockSpec((tm, LANES), lambda i: (i, 0)),
                  pl.BlockSpec((tm, LANES), lambda i: (i, 0)),
                  _resident((1, LANES))],
        out_specs=pl.BlockSpec((tm, LANES), lambda i: (i, 0)),
        out_shape=jax.ShapeDtypeStruct((n, LANES), F32),
        compiler_params=_cparams(("parallel",)),
        name="expert_slots",
    )(route, rank_slab, pstarts_row)


def _moe_kernel(be_ref, first_ref, nxt_ref, par_ref, nu_ref, x_ref, wg_hbm, wu_hbm, wd_hbm, y_ref,
                gbuf, ubuf, dbuf, sems, *, layer):
    j = pl.program_id(0)

    def weight_copies(e, slot):
        return (pltpu.make_async_copy(wg_hbm.at[layer, e], gbuf.at[slot], sems.at[slot, 0]),
                pltpu.make_async_copy(wu_hbm.at[layer, e], ubuf.at[slot], sems.at[slot, 1]),
                pltpu.make_async_copy(wd_hbm.at[layer, e], dbuf.at[slot], sems.at[slot, 2]))

    @pl.when(j < nu_ref[0])
    def _():
        slot = par_ref[j]

        @pl.when(j == 0)
        def _():
            for cp in weight_copies(be_ref[0], 0):
                cp.start()

        @pl.when(first_ref[j] == 1)
        def _():
            @pl.when(nxt_ref[j] >= 0)
            def _():
                for cp in weight_copies(nxt_ref[j], 1 - slot):
                    cp.start()

            for cp in weight_copies(be_ref[j], slot):
                cp.wait()

        xb = x_ref[...]
        hg = _dot(xb, gbuf[slot].astype(BF16))
        hu = _dot(xb, ubuf[slot].astype(BF16))
        hb = (hg * jax.nn.sigmoid(hg) * hu).astype(BF16)
        y_ref[...] = _dot(hb, dbuf[slot].astype(BF16)).astype(y_ref.dtype)

    @pl.when(j >= nu_ref[0])
    def _():
        y_ref[...] = jnp.zeros_like(y_ref)


def _moe_call(block_e, first, nxt, parity, n_used, u_sorted, w_gate, w_up, w_down, layer):
    n_slots, d = u_sorted.shape
    bm = MOE_BM
    nb = n_slots // bm
    de = w_gate.shape[-1]

    def xmap(j, be, fi, nx_, pa, nu):
        return (jnp.minimum(j, nu[0] - 1), 0)

    grid_spec = pltpu.PrefetchScalarGridSpec(
        num_scalar_prefetch=5,
        grid=(nb,),
        in_specs=[pl.BlockSpec((bm, d), xmap),
                  pl.BlockSpec(memory_space=pl.ANY),
                  pl.BlockSpec(memory_space=pl.ANY),
                  pl.BlockSpec(memory_space=pl.ANY)],
        out_specs=pl.BlockSpec((bm, d), lambda j, *_: (j, 0)),
        scratch_shapes=[pltpu.VMEM((2, d, de), F32), pltpu.VMEM((2, d, de), F32), pltpu.VMEM((2, de, d), F32),
                        pltpu.SemaphoreType.DMA((2, 3))],
    )
    return pl.pallas_call(
        functools.partial(_moe_kernel, layer=layer),
        grid_spec=grid_spec,
        out_shape=jax.ShapeDtypeStruct((n_slots, d), BF16),
        compiler_params=_cparams(("arbitrary",)),
        name="expert_mlp",
    )(block_e, first, nxt, parity, n_used, u_sorted, w_gate, w_up, w_down)


def _combine_kernel(x_ref, y0_ref, y1_ref, r_ref, mod_ref, lg_ref, lb_ref, o_ref, *, alpha):
    slab = r_ref[...]
    g1 = slab[:, 2:3]
    g2 = slab[:, 3:4]
    mx = g1 * y0_ref[...].astype(F32) + g2 * y1_ref[...].astype(F32)
    gate_f = mod_ref[0, 5:6, :]
    o_ref[...] = _layer_norm(alpha * x_ref[...] + gate_f * mx, lg_ref[...], lb_ref[...])


def _combine_call(xs, y0, y1, route, mod, ln_g, ln_b, mod_row, alpha):
    n, d = xs.shape
    tm = ROW_TM
    return pl.pallas_call(
        functools.partial(_combine_kernel, alpha=alpha),
        grid=(n // tm,),
        in_specs=[pl.BlockSpec((tm, d), lambda i: (i, 0)),
                  pl.BlockSpec((tm, d), lambda i: (i, 0)),
                  pl.BlockSpec((tm, d), lambda i: (i, 0)),
                  pl.BlockSpec((tm, LANES), lambda i: (i, 0)),
                  pl.BlockSpec((1, 6, d), lambda i: (mod_row(i, tm), 0, 0)),
                  pl.BlockSpec((1, d), lambda i: (0, 0)),
                  pl.BlockSpec((1, d), lambda i: (0, 0))],
        out_specs=pl.BlockSpec((tm, d), lambda i: (i, 0)),
        out_shape=jax.ShapeDtypeStruct((n, d), F32),
        compiler_params=_cparams(("parallel",)),
        name="moe_combine_ln",
    )(xs, y0, y1, route, mod, ln_g, ln_b)


def _rope_angles(seq, dim):
    tpos = np.arange(seq)
    row = (tpos // GRID_W).astype(np.float32)
    col = (tpos % GRID_W).astype(np.float32)
    quarter = dim // 4
    inv_freq = jnp.asarray(ROPE_THETA, F32) ** (-jnp.arange(quarter, dtype=F32) / quarter)
    ang_r = jnp.asarray(row)[:, None] * inv_freq
    ang_c = jnp.asarray(col)[:, None] * inv_freq
    return jnp.concatenate([ang_r, ang_r, ang_c, ang_c], axis=-1)


def _rot_sign(dim):
    l = np.arange(dim)
    return np.where((l & (dim // 4)) == 0, -1.0, 1.0).astype(np.float32), l ^ (dim // 4)


def _rope_tables(seq, pad_rows):
    ang_a = _rope_angles(seq, HEAD_DIM)
    sign_a, _ = _rot_sign(HEAD_DIM)
    cos_a = jnp.concatenate([jnp.cos(ang_a), jnp.ones((pad_rows, HEAD_DIM), F32)], 0)
    sin_a = jnp.concatenate([jnp.sin(ang_a) * sign_a, jnp.zeros((pad_rows, HEAD_DIM), F32)], 0)
    ang_b = _rope_angles(seq, MLA_ROPE)
    cos_b = jnp.concatenate([jnp.cos(ang_b), jnp.ones((pad_rows, MLA_ROPE), F32)], 0)
    sin_b = jnp.concatenate([jnp.sin(ang_b), jnp.zeros((pad_rows, MLA_ROPE), F32)], 0)
    return cos_a, sin_a, jnp.tile(cos_b, (1, 2)), jnp.tile(sin_b, (1, 2))


ORIG_AV, ORIG_BQ, ORIG_BKR, ORIG_CQ, ORIG_W = 1024, 1280, 2048, 2112, 10176
W_PREP_ROWS = 128


def _w_in_prep_kernel(w_ref, tail_ref, o_ref, *, cq_scale):
    rows = w_ref.shape[1]
    lane = lax.broadcasted_iota(jnp.int32, (rows, LANES), 1)
    low = lane < MLA_ROPE
    quarter = MLA_ROPE // 4

    o_ref[0, :, COL_AQ:COL_BQ] = w_ref[0, :, 0:ORIG_AV].astype(BF16)
    o_ref[0, :, COL_BQ:COL_BKR] = w_ref[0, :, ORIG_BQ:ORIG_BKR].astype(BF16)
    o_ref[0, :, COL_AV:COL_CQ] = w_ref[0, :, ORIG_AV:ORIG_BQ].astype(BF16)
    kb = w_ref[0, :, ORIG_BKR:ORIG_BKR + LANES]
    rot = jnp.where((lane & quarter) == 0, -pltpu.roll(kb, MLA_ROPE - quarter, 1), pltpu.roll(kb, MLA_ROPE + quarter, 1))
    o_ref[0, :, COL_BKR:COL_AV] = jnp.where(low, kb, rot).astype(BF16)
    n_shift = (PROJ_W - COL_CQ) // LANES
    for b in range(n_shift):
        src = ORIG_BKR + LANES * b
        first = w_ref[0, :, src:src + LANES]
        second = w_ref[0, :, src + LANES:src + 2 * LANES] if b < n_shift - 1 else tail_ref[0]
        v = jnp.where(low, pltpu.roll(first, MLA_ROPE, 1), pltpu.roll(second, MLA_ROPE, 1))
        if COL_CQ + LANES * b < COL_CK:
            v = v * cq_scale
        o_ref[0, :, COL_CQ + LANES * b:COL_CQ + LANES * (b + 1)] = v.astype(BF16)


def _permute_w_in(w):
    depth, d, width = w.shape
    assert width == ORIG_W and ORIG_W - ORIG_CQ == PROJ_W - COL_CQ
    tail = jnp.pad(w[:, :, ORIG_W - MLA_ROPE:], ((0, 0), (0, 0), (0, LANES - MLA_ROPE)))
    return pl.pallas_call(
        functools.partial(_w_in_prep_kernel, cq_scale=SCORE_LOG2E / math.sqrt(HEAD_DIM)),
        grid=(depth, d // W_PREP_ROWS),
        in_specs=[pl.BlockSpec((1, W_PREP_ROWS, width), lambda l, i: (l, i, 0)),
                  pl.BlockSpec((1, W_PREP_ROWS, LANES), lambda l, i: (l, i, 0))],
        out_specs=pl.BlockSpec((1, W_PREP_ROWS, PROJ_W), lambda l, i: (l, i, 0)),
        out_shape=jax.ShapeDtypeStruct((depth, d, PROJ_W), BF16),
        compiler_params=_cparams(("parallel", "parallel")),
        name="w_in_prep",
    )(w, tail)


def _rot_cols(w):
    dim = w.shape[-1]
    wr = w.reshape(w.shape[:-1] + (2, 2, dim // 4))
    return jnp.stack([-wr[..., 1, :], wr[..., 0, :]], axis=-2).reshape(w.shape)


def _permute_w_uq(w):
    r = w.shape[0]
    w3 = w.reshape(r, MLA_HEADS, MLA_QK)
    nope = w3[:, :, :MLA_NOPE].reshape(r, MLA_HEADS * MLA_NOPE)
    rope = w3[:, :, MLA_NOPE:]
    rot = _rot_cols(rope)
    zpad = jnp.zeros((r, MLA_HEADS, LANES - MLA_ROPE), w.dtype)
    rope_p = jnp.concatenate([rope, zpad], -1).reshape(r, MLA_HEADS * LANES)
    rot_p = jnp.concatenate([rot, zpad], -1).reshape(r, MLA_HEADS * LANES)
    return jnp.concatenate([nope, rope_p, rot_p], axis=1).astype(BF16)


def _permute_w_ukv(w):
    r = w.shape[0]
    w3 = w.reshape(r, MLA_HEADS, MLA_NOPE + MLA_V)
    kn = w3[:, :, :MLA_NOPE].reshape(r, MLA_HEADS * MLA_NOPE)
    vv = w3[:, :, MLA_NOPE:].reshape(r, MLA_HEADS * MLA_V)
    return jnp.concatenate([kn, vv], axis=1).astype(BF16)


def _na_bias_tables(rpb, seq):
    w, kh, kw = GRID_W, NA_KH, NA_KW
    rows = seq // w
    nh, n_dr, n_dc = rpb.shape
    line = jnp.full((nh, n_dr, 2 * w), NEG_BIG, F32).at[:, :, w - kw:w - kw + n_dc].set(rpb.astype(F32) * SCORE_LOG2E)
    skew = jnp.broadcast_to(line[:, :, None, :], (nh, n_dr, w, 2 * w)).reshape(nh, n_dr, 2 * w * w)
    skew = skew[:, :, :w * (2 * w - 1)].reshape(nh, n_dr, w, 2 * w - 1)
    tc = skew[:, :, :, w - 1:2 * w - 1]
    cq = np.arange(w)[:, None]
    ck = np.arange(w)[None, :]
    cs = np.clip(cq - kw // 2, 0, w - kw)
    col_ok = (ck >= cs) & (ck < cs + kw)
    tc = jnp.where(col_ok[None, None], tc, NEG_BIG)
    tc = jnp.concatenate([tc, jnp.full((nh, 1, w, w), NEG_BIG, F32)], axis=1)
    tc2 = jnp.concatenate([tc, tc], axis=-1)
    blks = []
    tile_starts = (0, NA_ROWS, rows - NA_ROWS)
    for variant, r0 in enumerate(tile_starts):
        rb = int(np.clip(r0 - kh // 2, 0, rows - NA_BAND))
        rq = r0 + np.arange(NA_ROWS)[:, None]
        rk = rb + np.arange(NA_BAND)[None, :]
        rs = np.clip(rq - kh // 2, 0, rows - kh)
        row_ok = (rk >= rs) & (rk < rs + kh)
        band_blk = np.where(row_ok, rk - rq + kh - 1, n_dr)
        win_blk = []
        for hq, off in enumerate(_na_window_offsets(variant, len(tile_starts))):
            part = band_blk[hq * NA_ROWS // 2:(hq + 1) * NA_ROWS // 2]
            assert (np.delete(part, np.s_[off:off + NA_WIN], axis=1) == n_dr).all()
            win_blk.extend(part[:, off:off + NA_WIN].tolist())
        blks.append(win_blk)
    return pl.pallas_call(
        functools.partial(_na_bias_kernel, blks=blks),
        grid=(nh,),
        in_specs=[pl.BlockSpec((1, n_dr + 1, w, 2 * w), lambda h: (h, 0, 0, 0))],
        out_specs=pl.BlockSpec((1, len(blks), NA_ROWS * w, NA_WIN * w), lambda h: (h, 0, 0, 0)),
        out_shape=jax.ShapeDtypeStruct((nh, len(blks), NA_ROWS * w, NA_WIN * w), F32),
        compiler_params=_cparams(("parallel",)),
        name="na_bias_table",
    )(tc2)


def _na_bias_kernel(tc_ref, o_ref, *, blks):
    w = GRID_W
    low = lax.broadcasted_iota(jnp.int32, (w, 2 * w), 1) < w
    for v, blk in enumerate(blks):
        for rq, row in enumerate(blk):
            for pr in range(len(row) // 2):
                pair = jnp.where(low, tc_ref[0, row[2 * pr]], tc_ref[0, row[2 * pr + 1]])
                o_ref[0, v, rq * w:(rq + 1) * w, pr * 2 * w:(pr + 1) * 2 * w] = pair


def kernel(x, c, ctx, c_ctx, w_ada, b_ada, w_in, gqa_q_norm, gqa_k_norm, mla_q_norm, mla_kv_norm, mla_w_uq, mla_w_ukv, na_rpb, w_branch_a, w_branch_b, w_branch_c, w_out, ln1_g, ln1_b, w_router_group, b_router_group, w_router_expert, b_router_expert, w_expert_gate, w_expert_up, w_expert_down, ln2_g, ln2_b):
    bsz, seq, d = x.shape
    n_ctx = ctx.shape[1]
    depth = w_ada.shape[0]
    nx, nc = bsz * seq, bsz * n_ctx
    t = nx + nc
    assert seq % PROJ_TM == 0 and nc == PROJ_TM and seq % (NA_BAND * GRID_W) == 0 and nx % n_ctx == 0
    alpha = (2 * depth) ** 0.25

    def mod_row(i, tm):
        return jnp.minimum(i // (seq // tm), bsz)

    def rope_blk(i, tm):
        return jnp.where(i < nx // tm, i % (seq // tm), seq // tm)

    stream = (x.reshape(nx, d), ctx.reshape(nc, d), 0)
    cc = jnp.concatenate([c, c_ctx[None], jnp.zeros((8 - bsz - 1, d), F32)], axis=0)
    mod_all = _ada_call(cc, w_ada, b_ada).reshape(depth, 8, 6, d)
    cos_a, sin_a, cos_b, sin_b = _rope_tables(seq, ROW_TM)
    w_in_p = _permute_w_in(w_in)

    attn_kw = dict(bsz=bsz, seq=seq, n_ctx=n_ctx)
    for i in range(depth):
        last = i == depth - 1
        n_rows = nx if last else t
        mod = mod_all[i]
        p = _proj_call(stream, nx, t, mod, w_in_p, i, mod_row)

        qa, ka = _prep_a_call(p, cos_a, sin_a, gqa_q_norm[i][None], gqa_k_norm[i][None], rope_blk)
        a_kw = dict(n_heads=GQA_HEADS, dq=HEAD_DIM, q_col=lambda h: h, k_col=lambda h: h // GQA_GROUP,
                    v_col=lambda h: COL_AV // HEAD_DIM + h // GQA_GROUP, **attn_kw)
        o_a = _dense_attn_call(qa, ka, p, **a_kw)
        qb, kb, vb = _prep_b_call(p, cos_b, sin_b, mla_q_norm[i][None], mla_kv_norm[i][None],
                                  _permute_w_uq(mla_w_uq[i]), _permute_w_ukv(mla_w_ukv[i]), rope_blk)
        b_kw = dict(n_heads=MLA_HEADS, dq=MLA_PAD, q_col=lambda h: h, k_col=lambda h: h, v_col=lambda h: h, **attn_kw)
        o_b = _dense_attn_call(qb, kb, vb, **b_kw)
        o_c = _na_call(p, _na_bias_tables(na_rpb[i], seq), **attn_kw)
        o_lat = (o_a, o_b, o_c)
        if last:
            o_ctx = o_lat
        else:
            c_kw = dict(n_heads=NA_HEADS, dq=HEAD_DIM, q_col=lambda h: COL_CQ // HEAD_DIM + h,
                        k_col=lambda h: COL_CK // HEAD_DIM + h, v_col=lambda h: COL_CV // HEAD_DIM + h, **attn_kw)
            o_ctx = (_ctx_attn_call(qa, ka, p, **a_kw), _ctx_attn_call(qb, kb, vb, **b_kw),
                     _ctx_attn_call(p, p, p, **c_kw))

        mrg = _branch_call(o_lat, o_ctx, p, w_branch_a[i].astype(BF16), w_branch_b[i].astype(BF16),
                           w_branch_c[i].astype(BF16), n_rows)
        wr = jnp.concatenate([w_router_group[i], w_router_expert[i],
                              jnp.zeros((d, LANES - N_GROUPS - N_EXPERTS), F32)], axis=1)
        br = jnp.concatenate([b_router_group[i], b_router_expert[i],
                              jnp.zeros((LANES - N_GROUPS - N_EXPERTS,), F32)])[None]
        wr_cat = jnp.concatenate(_split_bf16(wr), axis=1)
        xs, u_f, route = _out_call(mrg, stream, nx, mod, w_out[i].astype(BF16), ln1_g[i][None], ln1_b[i][None],
                                   wr_cat, br, mod_row, n_rows, alpha)

        rank_slab, cnt = _rank_call(route)
        eid = route[:, :TOP_K].astype(jnp.int32)
        counts = cnt[0, :N_EXPERTS].astype(jnp.int32)
        padded = ((counts + MOE_BM - 1) // MOE_BM) * MOE_BM
        pends = jnp.cumsum(padded)
        pstarts = pends - padded
        pstarts_row = jnp.concatenate([pstarts.astype(F32), jnp.zeros((LANES - N_EXPERTS,), F32)])[None]
        dest = _dest_call(route, rank_slab, pstarts_row)[:, :TOP_K].astype(jnp.int32)
        nb = -(-(n_rows * TOP_K) // MOE_BM) + N_EXPERTS
        n_used = (pends[-1] // MOE_BM).astype(jnp.int32)
        blk = jnp.minimum(jnp.arange(nb, dtype=jnp.int32), n_used - 1)
        block_e = jnp.clip(jnp.searchsorted(pends, blk * MOE_BM, side="right"), 0, N_EXPERTS - 1).astype(jnp.int32)
        order = jnp.argsort(eid.reshape(-1), stable=True).astype(jnp.int32)
        starts = jnp.cumsum(counts) - counts
        slot = jnp.arange(nb * MOE_BM, dtype=jnp.int32).reshape(nb, MOE_BM)
        blk_shift = (starts - pstarts)[block_e][:, None]
        blk_limit = (pstarts + counts)[block_e][:, None]
        pos = jnp.clip(slot + blk_shift, 0, n_rows * TOP_K - 1)
        slot_tok = jnp.where(slot < blk_limit, order.at[pos].get(mode="promise_in_bounds") // TOP_K,
                             slot % n_rows).reshape(-1)
        u_sorted = u_f.at[slot_tok].get(mode="promise_in_bounds")
        jj = jnp.arange(nb, dtype=jnp.int32)
        first = ((jj == 0) | (block_e != jnp.roll(block_e, 1))).astype(jnp.int32)
        parity = (jnp.cumsum(first) - 1) % 2
        live = jnp.where(counts > 0, jnp.arange(N_EXPERTS, dtype=jnp.int32), N_EXPERTS)
        next_live = jnp.concatenate([lax.cummin(live, reverse=True)[1:], jnp.full((1,), N_EXPERTS, jnp.int32)])
        nxt = jnp.where(next_live < N_EXPERTS, next_live, -1)[block_e]
        y_sorted = _moe_call(block_e, first, nxt.astype(jnp.int32), parity.astype(jnp.int32), n_used[None],
                             u_sorted, w_expert_gate, w_expert_up, w_expert_down, i)
        y0 = y_sorted.at[dest[:, 0]].get(mode="promise_in_bounds")
        y1 = y_sorted.at[dest[:, 1]].get(mode="promise_in_bounds")
        xs = _combine_call(xs, y0, y1, route, mod, ln2_g[i][None], ln2_b[i][None], mod_row, alpha)
        stream = (xs, xs, nx)
    return xs.reshape(bsz, seq, d)
```

```python
import functools
import math

import jax
import jax.numpy as jnp
import numpy as np
from jax import lax
from jax.experimental import pallas as pl
from jax.experimental.pallas import tpu as pltpu

F32 = jnp.float32
BF16 = jnp.bfloat16

HEAD_DIM = 128
GRID_W = 64
ROPE_THETA = 10000.0
GQA_HEADS, GQA_KV_HEADS = 6, 2
GQA_GROUP = GQA_HEADS // GQA_KV_HEADS
MLA_HEADS, MLA_Q_RANK, MLA_KV_RANK = 5, 512, 256
MLA_NOPE, MLA_ROPE, MLA_V = 128, 64, 128
MLA_QK = MLA_NOPE + MLA_ROPE
MLA_PAD = 256
NA_HEADS, NA_KH, NA_KW = 5, 8, 16
N_GROUPS, EXPERTS_PER_GROUP, TOP_K, D_EXPERT = 8, 8, 2, 512
N_EXPERTS = N_GROUPS * EXPERTS_PER_GROUP
LN_EPS = 1e-6
RMS_EPS = 1e-6
NEG_BIG = -1e30
SCORE_LOG2E = math.log2(math.e)

LANES = 128
VMEM_LIMIT = 56 * 1024 * 1024
PROJ_TM, PROJ_TN = 1024, 1024
ROW_TM = 512
MM_TM = 512
ATTN_TQ = 512
ATTN_SUB = 4
ATTN_KC = 1024
NA_ROWS = 8
NA_SUB = 4
NA_BAND = 16
NA_WIN = 12
MOE_BM = 256
MOE_DMA_CHUNKS = 4
ADA_TN = 512

COL_AQ, COL_AK = 0, 768
COL_BQ, COL_BKV, COL_BKR = 1024, 1536, 1792
COL_AV, COL_CQ, COL_CK, COL_CV = 1920, 2176, 2816, 3456
COL_GATE = 4096
PROJ_W = 10240


def _cparams(sem):
    return pltpu.CompilerParams(dimension_semantics=sem, vmem_limit_bytes=VMEM_LIMIT)


def _resident(shape):
    return pl.BlockSpec(shape, lambda *_: (0,) * len(shape), pipeline_mode=pl.Buffered(1))


def _dot(a, b):
    return jnp.dot(a, b, preferred_element_type=F32)


def _dot_nt(a, b):
    return lax.dot_general(a, b, (((1,), (1,)), ((), ())), preferred_element_type=F32)


def _split_bf16(a):
    hi = a.astype(BF16)
    lo = (a - hi.astype(F32)).astype(BF16)
    return hi, lo


def _ada_kernel(c_ref, w_ref, b_ref, o_ref):
    a = c_ref[...]
    a = a * jax.nn.sigmoid(a)
    a_hi, a_lo = _split_bf16(a)
    w_hi, w_lo = _split_bf16(w_ref[0])
    acc = _dot(a_hi, w_hi) + _dot(a_lo, w_hi) + _dot(a_hi, w_lo)
    o_ref[0] = acc + b_ref[0]


def _ada_call(cc, w_ada, b_ada):
    depth, d, n = w_ada.shape
    rows = cc.shape[0]
    return pl.pallas_call(
        _ada_kernel,
        grid=(depth, n // ADA_TN),
        in_specs=[pl.BlockSpec((rows, d), lambda l, j: (0, 0)),
                  pl.BlockSpec((1, d, ADA_TN), lambda l, j: (l, 0, j)),
                  pl.BlockSpec((1, 1, ADA_TN), lambda l, j: (l, 0, j))],
        out_specs=pl.BlockSpec((1, rows, ADA_TN), lambda l, j: (l, 0, j)),
        out_shape=jax.ShapeDtypeStruct((depth, rows, n), F32),
        compiler_params=_cparams(("parallel", "parallel")),
        name="ada_modulation",
    )(cc, w_ada, b_ada.reshape(depth, 1, n))


def _proj_kernel(xl_ref, xc_ref, mod_ref, w_ref, o_ref, u_scr, *, n_lat_tiles):
    @pl.when(pl.program_id(1) == 0)
    def _():
        def modulate(x):
            u_scr[...] = (x * (1.0 + mod_ref[0, 1:2, :]) + mod_ref[0, 0:1, :]).astype(BF16)

        @pl.when(pl.program_id(0) < n_lat_tiles)
        def _():
            modulate(xl_ref[...])

        @pl.when(pl.program_id(0) >= n_lat_tiles)
        def _():
            modulate(xc_ref[...])

    o_ref[...] = _dot(u_scr[...], w_ref[0]).astype(BF16)


def _stream_specs(stream, tm, n_lat):
    lat, ctx, ctx_row0 = stream
    d = lat.shape[1]
    n_lat_tiles = n_lat // tm
    ctx_tile0 = ctx_row0 // tm

    def lat_map(i, *_):
        return (jnp.minimum(i, n_lat_tiles - 1), 0)

    def ctx_map(i, *_):
        return (ctx_tile0 + jnp.maximum(i - n_lat_tiles, 0), 0)

    return [pl.BlockSpec((tm, d), lat_map), pl.BlockSpec((tm, d), ctx_map, pipeline_mode=pl.Buffered(1))]


def _proj_call(stream, n_lat, n_tok, mod, w_in_p, layer, mod_row):
    d = stream[0].shape[1]
    return pl.pallas_call(
        functools.partial(_proj_kernel, n_lat_tiles=n_lat // PROJ_TM),
        grid=(n_tok // PROJ_TM, PROJ_W // PROJ_TN),
        in_specs=_stream_specs(stream, PROJ_TM, n_lat) + [
            pl.BlockSpec((1, 6, d), lambda i, j: (mod_row(i, PROJ_TM), 0, 0)),
            pl.BlockSpec((1, d, PROJ_TN), lambda i, j: (layer, 0, j))],
        out_specs=pl.BlockSpec((PROJ_TM, PROJ_TN), lambda i, j: (i, j)),
        out_shape=jax.ShapeDtypeStruct((n_tok, PROJ_W), BF16),
        scratch_shapes=[pltpu.VMEM((PROJ_TM, d), BF16)],
        compiler_params=_cparams(("parallel", "arbitrary")),
        name="input_projection",
    )(stream[0], stream[1], mod, w_in_p)


def _rope128(y, cos, sin_signed, first_quarter):
    rot = jnp.where(first_quarter, pltpu.roll(y, 96, 1), pltpu.roll(y, 32, 1))
    return y * cos + rot * sin_signed


def _prep_a_kernel(p_ref, cos_ref, sin_ref, gq_ref, gk_ref, q_ref, k_ref, *, scale):
    cos = cos_ref[...]
    sin_signed = sin_ref[...]
    lane = lax.broadcasted_iota(jnp.int32, cos.shape, 1)
    first_quarter = (lane & 32) == 0
    for h in range(GQA_HEADS + GQA_KV_HEADS):
        xh = p_ref[:, h * HEAD_DIM:(h + 1) * HEAD_DIM].astype(F32)
        r = lax.rsqrt(jnp.mean(xh * xh, axis=1, keepdims=True) + RMS_EPS)
        if h < GQA_HEADS:
            y = _rope128(xh * r * gq_ref[...], cos, sin_signed, first_quarter) * scale
            q_ref[:, h * HEAD_DIM:(h + 1) * HEAD_DIM] = y.astype(BF16)
        else:
            hk = h - GQA_HEADS
            y = _rope128(xh * r * gk_ref[...], cos, sin_signed, first_quarter)
            k_ref[:, hk * HEAD_DIM:(hk + 1) * HEAD_DIM] = y.astype(BF16)


def _prep_a_call(p, cos_t, sin_t, gq, gk, rope_blk):
    t = p.shape[0]
    tm = ROW_TM
    wq, wk = GQA_HEADS * HEAD_DIM, GQA_KV_HEADS * HEAD_DIM
    return pl.pallas_call(
        functools.partial(_prep_a_kernel, scale=SCORE_LOG2E / math.sqrt(HEAD_DIM)),
        grid=(t // tm,),
        in_specs=[pl.BlockSpec((tm, wq + wk), lambda i: (i, 0)),
                  pl.BlockSpec((tm, HEAD_DIM), lambda i: (rope_blk(i, tm), 0)),
                  pl.BlockSpec((tm, HEAD_DIM), lambda i: (rope_blk(i, tm), 0)),
                  pl.BlockSpec((1, HEAD_DIM), lambda i: (0, 0)),
                  pl.BlockSpec((1, HEAD_DIM), lambda i: (0, 0))],
        out_specs=[pl.BlockSpec((tm, wq), lambda i: (i, 0)),
                   pl.BlockSpec((tm, wk), lambda i: (i, 0))],
        out_shape=[jax.ShapeDtypeStruct((t, wq), BF16), jax.ShapeDtypeStruct((t, wk), BF16)],
        compiler_params=_cparams(("parallel",)),
        name="gqa_qk_prep",
    )(p, cos_t, sin_t, gq, gk)


def _prep_b_kernel(ql_ref, kvl_ref, kr_ref, cos_ref, sin_ref, gq_ref, gkv_ref, wq_ref, wkv_ref,
                   q_ref, k_ref, v_ref, *, scale):
    nh, hp = MLA_HEADS, MLA_PAD
    wn = nh * MLA_NOPE
    cos = cos_ref[...]
    sin = sin_ref[...]
    ql = ql_ref[...].astype(F32)
    ql = ql * lax.rsqrt(jnp.mean(ql * ql, axis=1, keepdims=True) + RMS_EPS) * gq_ref[...]
    qf = _dot(ql.astype(BF16), wq_ref[...])
    kvl = kvl_ref[...].astype(F32)
    kvl = kvl * lax.rsqrt(jnp.mean(kvl * kvl, axis=1, keepdims=True) + RMS_EPS) * gkv_ref[...]
    kvf = _dot(kvl.astype(BF16), wkv_ref[...])
    lane = lax.broadcasted_iota(jnp.int32, cos.shape, 1)
    low = lane < MLA_ROPE
    tk = kr_ref[...].astype(F32) * jnp.where(low, cos, sin)
    kr = jnp.where(low, tk + pltpu.roll(tk, MLA_ROPE, 1), 0.0).astype(BF16)
    for h in range(nh):
        q_ref[:, h * hp:h * hp + LANES] = (qf[:, h * LANES:(h + 1) * LANES] * scale).astype(BF16)
        qr = qf[:, wn + h * LANES:wn + (h + 1) * LANES] * cos + qf[:, 2 * wn + h * LANES:2 * wn + (h + 1) * LANES] * sin
        q_ref[:, h * hp + LANES:(h + 1) * hp] = (qr * scale).astype(BF16)
        k_ref[:, h * hp:h * hp + LANES] = kvf[:, h * LANES:(h + 1) * LANES].astype(BF16)
        k_ref[:, h * hp + LANES:(h + 1) * hp] = kr
    v_ref[...] = kvf[:, wn:].astype(BF16)


def _prep_b_call(p, cos_t, sin_t, gq, gkv, wq, wkv, rope_blk):
    t = p.shape[0]
    tm = ROW_TM
    nh = MLA_HEADS
    return pl.pallas_call(
        functools.partial(_prep_b_kernel, scale=SCORE_LOG2E / math.sqrt(MLA_QK)),
        grid=(t // tm,),
        in_specs=[pl.BlockSpec((tm, MLA_Q_RANK), lambda i: (i, COL_BQ // MLA_Q_RANK)),
                  pl.BlockSpec((tm, MLA_KV_RANK), lambda i: (i, COL_BKV // MLA_KV_RANK)),
                  pl.BlockSpec((tm, LANES), lambda i: (i, COL_BKR // LANES)),
                  pl.BlockSpec((tm, LANES), lambda i: (rope_blk(i, tm), 0)),
                  pl.BlockSpec((tm, LANES), lambda i: (rope_blk(i, tm), 0)),
                  pl.BlockSpec((1, MLA_Q_RANK), lambda i: (0, 0)),
                  pl.BlockSpec((1, MLA_KV_RANK), lambda i: (0, 0)),
                  _resident(wq.shape), _resident(wkv.shape)],
        out_specs=[pl.BlockSpec((tm, nh * MLA_PAD), lambda i: (i, 0)),
                   pl.BlockSpec((tm, nh * MLA_PAD), lambda i: (i, 0)),
                   pl.BlockSpec((tm, nh * MLA_V), lambda i: (i, 0))],
        out_shape=[jax.ShapeDtypeStruct((t, nh * MLA_PAD), BF16),
                   jax.ShapeDtypeStruct((t, nh * MLA_PAD), BF16),
                   jax.ShapeDtypeStruct((t, nh * MLA_V), BF16)],
        compiler_params=_cparams(("parallel",)),
        name="mla_prep",
    )(p, p, p, cos_t, sin_t, gq, gkv, wq, wkv)


def _online_attend(q, chunks):
    m = z = acc = None
    for k, v, bias in chunks:
        s = _dot_nt(q, k)
        if bias is not None:
            s = s + bias
        cm = jnp.max(s, axis=1, keepdims=True)
        if m is None:
            m = cm
            p = jnp.exp2(s - m)
            z = jnp.sum(p, axis=1, keepdims=True)
            acc = _dot(p.astype(BF16), v)
        else:
            m_new = jnp.maximum(m, cm)
            corr = jnp.exp2(m - m_new)
            p = jnp.exp2(s - m_new)
            z = z * corr + jnp.sum(p, axis=1, keepdims=True)
            acc = acc * corr + _dot(p.astype(BF16), v)
            m = m_new
    return acc / z


def _online_attend_ones(q, chunks, dv):
    m = acc = None
    for k, v_ones, bias in chunks:
        s = _dot_nt(q, k)
        if bias is not None:
            s = s + bias
        cm = jnp.max(s, axis=1, keepdims=True)
        if m is None:
            m = cm
            acc = _dot(jnp.exp2((s - m).astype(BF16)), v_ones)
        else:
            m_new = jnp.maximum(m, cm)
            corr = jnp.exp2(m - m_new)
            acc = acc * corr + _dot(jnp.exp2((s - m_new).astype(BF16)), v_ones)
            m = m_new
    return acc[:, :dv] / acc[:, dv:dv + 1]


def _dense_attn_kernel(q_ref, kl_ref, vl_ref, kc_ref, vc_ref, o_ref, v_ones):
    n_lat = kl_ref.shape[0]
    dv = vl_ref.shape[1]

    @pl.when(pl.program_id(2) == 0)
    def _():
        v_ones[0:n_lat, 0:dv] = vl_ref[...]
        v_ones[n_lat:, 0:dv] = vc_ref[...]
        v_ones[:, dv:] = jnp.ones((v_ones.shape[0], dv), BF16)

    chunks = [(kl_ref[c:c + ATTN_KC, :], v_ones[c:c + ATTN_KC, :], None) for c in range(0, n_lat, ATTN_KC)]
    chunks.append((kc_ref[...], v_ones[n_lat:, :], None))
    for r in range(0, q_ref.shape[0], ATTN_TQ):
        o_ref[r:r + ATTN_TQ, :] = _online_attend_ones(q_ref[r:r + ATTN_TQ, :], chunks, dv).astype(o_ref.dtype)


def _dense_attn_call(q, k, v, *, n_heads, dq, q_col, k_col, v_col, bsz, seq, n_ctx):
    tq = ATTN_TQ * ATTN_SUB
    tpb = seq // tq
    ctx0 = bsz * seq // n_ctx
    dv = HEAD_DIM
    return pl.pallas_call(
        _dense_attn_kernel,
        grid=(bsz, n_heads, tpb),
        in_specs=[pl.BlockSpec((tq, dq), lambda b, h, i: (b * tpb + i, q_col(h))),
                  pl.BlockSpec((seq, dq), lambda b, h, i: (b, k_col(h))),
                  pl.BlockSpec((seq, dv), lambda b, h, i: (b, v_col(h))),
                  pl.BlockSpec((n_ctx, dq), lambda b, h, i: (ctx0 + b, k_col(h))),
                  pl.BlockSpec((n_ctx, dv), lambda b, h, i: (ctx0 + b, v_col(h)))],
        out_specs=pl.BlockSpec((tq, dv), lambda b, h, i: (b * tpb + i, h)),
        out_shape=jax.ShapeDtypeStruct((bsz * seq, n_heads * dv), BF16),
        scratch_shapes=[pltpu.VMEM((seq + n_ctx, 2 * dv), BF16)],
        compiler_params=_cparams(("parallel", "parallel", "arbitrary")),
        name="dense_attention",
    )(q, k, v, k, v)


def _ctx_attn_kernel(q_ref, kc_ref, vc_ref, o_ref):
    o_ref[...] = _online_attend(q_ref[...], [(kc_ref[...], vc_ref[...], None)]).astype(o_ref.dtype)


def _ctx_attn_call(q, k, v, *, n_heads, dq, q_col, k_col, v_col, bsz, seq, n_ctx):
    ctx0 = bsz * seq // n_ctx
    dv = HEAD_DIM
    return pl.pallas_call(
        _ctx_attn_kernel,
        grid=(bsz, n_heads),
        in_specs=[pl.BlockSpec((n_ctx, dq), lambda b, h: (ctx0 + b, q_col(h))),
                  pl.BlockSpec((n_ctx, dq), lambda b, h: (ctx0 + b, k_col(h))),
                  pl.BlockSpec((n_ctx, dv), lambda b, h: (ctx0 + b, v_col(h)))],
        out_specs=pl.BlockSpec((n_ctx, dv), lambda b, h: (b, h)),
        out_shape=jax.ShapeDtypeStruct((bsz * n_ctx, n_heads * dv), BF16),
        compiler_params=_cparams(("parallel", "parallel")),
        name="context_attention",
    )(q, k, v)


def _na_window_offsets(variant, n_variants):
    shift = NA_BAND - NA_WIN
    return ((0, 0), (0, shift), (shift, shift))[0 if variant == 0 else (2 if variant == n_variants - 1 else 1)]


def _na_kernel(q_ref, k_ref, v_ref, kc_ref, vc_ref, *rest, rows_total, n_tiles):
    bias_refs, o_ref, v_ones = rest[:NA_SUB], rest[NA_SUB], rest[NA_SUB + 1]
    n_lat = k_ref.shape[0]
    dv = v_ref.shape[1]
    tile = NA_ROWS * GRID_W
    half = tile // 2
    win = NA_WIN * GRID_W
    shift = (NA_BAND - NA_WIN) * GRID_W

    @pl.when(pl.program_id(2) == 0)
    def _():
        v_ones[0:n_lat, 0:dv] = v_ref[...]
        v_ones[n_lat:, 0:dv] = vc_ref[...]
        v_ones[:, dv:] = jnp.ones((v_ones.shape[0], dv), BF16)

    ctx = (kc_ref[...], v_ones[n_lat:, :], None)
    for t in range(NA_SUB):
        i = pl.program_id(2) * NA_SUB + t
        start_blk = jnp.clip(2 * i - 1, 0, (rows_total - NA_BAND) // 4)
        band0 = start_blk * (4 * GRID_W)
        starts = (band0 + jnp.where(i == n_tiles - 1, shift, 0), band0 + jnp.where(i == 0, 0, shift))
        for hq, start in enumerate(starts):
            start = pl.multiple_of(start, 4 * GRID_W)
            rows = slice(t * tile + hq * half, t * tile + (hq + 1) * half)
            bias = bias_refs[t][0, 0, hq * half:(hq + 1) * half, :]
            chunks = [(k_ref[pl.ds(start, win), :], v_ones[pl.ds(start, win), :], bias), ctx]
            o_ref[rows, :] = _online_attend_ones(q_ref[rows, :], chunks, dv).astype(o_ref.dtype)


def _na_call(p, bias_tab, *, bsz, seq, n_ctx):
    tile = NA_ROWS * GRID_W
    n_tiles = seq // tile
    tq = tile * NA_SUB
    tpb = seq // tq
    rows_total = seq // GRID_W
    ctx0 = bsz * seq // n_ctx
    d = HEAD_DIM
    qc, kc, vc = COL_CQ // d, COL_CK // d, COL_CV // d

    def bias_spec(t):
        def index(b, h, i):
            tile_idx = i * NA_SUB + t
            return (h, jnp.where(tile_idx == 0, 0, jnp.where(tile_idx == n_tiles - 1, 2, 1)), 0, 0)
        return pl.BlockSpec((1, 1, tile, NA_WIN * GRID_W), index)

    return pl.pallas_call(
        functools.partial(_na_kernel, rows_total=rows_total, n_tiles=n_tiles),
        grid=(bsz, NA_HEADS, tpb),
        in_specs=[pl.BlockSpec((tq, d), lambda b, h, i: (b * tpb + i, qc + h)),
                  pl.BlockSpec((seq, d), lambda b, h, i: (b, kc + h)),
                  pl.BlockSpec((seq, d), lambda b, h, i: (b, vc + h)),
                  pl.BlockSpec((n_ctx, d), lambda b, h, i: (ctx0 + b, kc + h)),
                  pl.BlockSpec((n_ctx, d), lambda b, h, i: (ctx0 + b, vc + h))]
                 + [bias_spec(t) for t in range(NA_SUB)],
        out_specs=pl.BlockSpec((tq, d), lambda b, h, i: (b * tpb + i, h)),
        out_shape=jax.ShapeDtypeStruct((bsz * seq, NA_HEADS * d), BF16),
        scratch_shapes=[pltpu.VMEM((seq + n_ctx, 2 * d), BF16)],
        compiler_params=_cparams(("parallel", "parallel", "arbitrary")),
        name="neighbourhood_attention",
    )(p, p, p, p, p, *([bias_tab] * NA_SUB))


def _branch_kernel(oa_ref, ob_ref, oc_ref, oac_ref, obc_ref, occ_ref, ga_ref, gb_ref, gc_ref,
                   wa_ref, wb_ref, wc_ref, o_ref, *, n_lat_tiles):
    def gated(g_ref, w_ref, o_tile):
        return jax.nn.sigmoid(g_ref[...].astype(F32)) * _dot(o_tile, w_ref[...])

    @pl.when(pl.program_id(0) < n_lat_tiles)
    def _():
        acc = gated(ga_ref, wa_ref, oa_ref[...]) + gated(gb_ref, wb_ref, ob_ref[...]) + gated(gc_ref, wc_ref, oc_ref[...])
        o_ref[...] = acc.astype(BF16)

    @pl.when(pl.program_id(0) >= n_lat_tiles)
    def _():
        acc = gated(ga_ref, wa_ref, oac_ref[...]) + gated(gb_ref, wb_ref, obc_ref[...]) + gated(gc_ref, wc_ref, occ_ref[...])
        o_ref[...] = acc.astype(BF16)


def _branch_call(o_lat, o_ctx, p, wa, wb, wc, n_rows):
    d = wa.shape[1]
    tm = MM_TM
    g0 = COL_GATE // d
    n_lat_tiles = o_lat[0].shape[0] // tm
    lat_specs = [pl.BlockSpec((tm, o.shape[1]), lambda i: (jnp.minimum(i, n_lat_tiles - 1), 0)) for o in o_lat]
    ctx_specs = [pl.BlockSpec((tm, o.shape[1]), lambda i: (jnp.maximum(i - n_lat_tiles, 0), 0)) for o in o_ctx]
    return pl.pallas_call(
        functools.partial(_branch_kernel, n_lat_tiles=n_lat_tiles),
        grid=(n_rows // tm,),
        in_specs=lat_specs + ctx_specs + [
            pl.BlockSpec((tm, d), lambda i: (i, g0)),
            pl.BlockSpec((tm, d), lambda i: (i, g0 + 1)),
            pl.BlockSpec((tm, d), lambda i: (i, g0 + 2)),
            _resident(wa.shape), _resident(wb.shape), _resident(wc.shape)],
        out_specs=pl.BlockSpec((tm, d), lambda i: (i, 0)),
        out_shape=jax.ShapeDtypeStruct((n_rows, d), BF16),
        compiler_params=_cparams(("parallel",)),
        name="branch_merge",
    )(*o_lat, *o_ctx, p, p, p, wa, wb, wc)


def _layer_norm(z, g, b):
    mu = jnp.mean(z, axis=1, keepdims=True)
    zc = z - mu
    var = jnp.mean(zc * zc, axis=1, keepdims=True)
    return zc * lax.rsqrt(var + LN_EPS) * g + b


def _route(logits):
    lane = lax.broadcasted_iota(jnp.int32, logits.shape, 1)
    lane_f = lane.astype(F32)
    is_g = lane < N_GROUPS
    gl = jnp.where(is_g, logits, NEG_BIG)
    mg = jnp.max(gl, axis=1, keepdims=True)
    gsel = jnp.min(jnp.where(gl == mg, lane_f, float(LANES)), axis=1, keepdims=True)
    zg = jnp.sum(jnp.where(is_g, jnp.exp(gl - mg), 0.0), axis=1, keepdims=True)
    lo = N_GROUPS + gsel * EXPERTS_PER_GROUP
    is_e = (lane_f >= lo) & (lane_f < lo + EXPERTS_PER_GROUP)
    el = jnp.where(is_e, logits, NEG_BIG)
    t1 = jnp.max(el, axis=1, keepdims=True)
    i1 = jnp.min(jnp.where(el == t1, lane_f, float(LANES)), axis=1, keepdims=True)
    el2 = jnp.where(lane_f == i1, NEG_BIG, el)
    t2 = jnp.max(el2, axis=1, keepdims=True)
    i2 = jnp.min(jnp.where(el2 == t2, lane_f, float(LANES)), axis=1, keepdims=True)
    dd = jnp.exp(t2 - t1)
    g1 = 1.0 / (zg * (1.0 + dd))
    g2 = g1 * dd
    slab = jnp.where(lane == 0, i1 - N_GROUPS,
                     jnp.where(lane == 1, i2 - N_GROUPS,
                               jnp.where(lane == 2, g1, jnp.where(lane == 3, g2, 0.0))))
    return slab


def _out_kernel(m_ref, xl_ref, xc_ref, mod_ref, w_ref, lg_ref, lb_ref, wr_ref, br_ref,
                xo_ref, u_ref, r_ref, *, alpha, n_lat_tiles):
    @pl.when(pl.program_id(0) < n_lat_tiles)
    def _():
        xo_ref[...] = xl_ref[...]

    @pl.when(pl.program_id(0) >= n_lat_tiles)
    def _():
        xo_ref[...] = xc_ref[...]

    y = _dot(m_ref[...], w_ref[...])
    gate_m = mod_ref[0, 2:3, :]
    shift_f = mod_ref[0, 3:4, :]
    scale_f = mod_ref[0, 4:5, :]
    xn = _layer_norm(alpha * xo_ref[...] + gate_m * y, lg_ref[...], lb_ref[...])
    xo_ref[...] = xn
    u = xn * (1.0 + scale_f) + shift_f
    u_hi, u_lo = _split_bf16(u)
    u_ref[...] = u_hi
    both = _dot(u_hi, wr_ref[...])
    logits = both[:, :LANES] + both[:, LANES:] + _dot(u_lo, wr_ref[:, :LANES]) + br_ref[...]
    r_ref[...] = _route(logits)


def _out_call(mrg, stream, n_lat, mod, w_out, ln_g, ln_b, wr_cat, br, mod_row, n_rows, alpha):
    d = stream[0].shape[1]
    tm = MM_TM
    return pl.pallas_call(
        functools.partial(_out_kernel, alpha=alpha, n_lat_tiles=n_lat // tm),
        grid=(n_rows // tm,),
        in_specs=[pl.BlockSpec((tm, d), lambda i: (i, 0))] + _stream_specs(stream, tm, n_lat) + [
                  pl.BlockSpec((1, 6, d), lambda i: (mod_row(i, tm), 0, 0)),
                  _resident((d, d)), _resident((1, d)), _resident((1, d)),
                  _resident((d, 2 * LANES)), _resident((1, LANES))],
        out_specs=[pl.BlockSpec((tm, d), lambda i: (i, 0)),
                   pl.BlockSpec((tm, d), lambda i: (i, 0)),
                   pl.BlockSpec((tm, LANES), lambda i: (i, 0))],
        out_shape=[jax.ShapeDtypeStruct((n_rows, d), F32),
                   jax.ShapeDtypeStruct((n_rows, d), BF16),
                   jax.ShapeDtypeStruct((n_rows, LANES), F32)],
        compiler_params=_cparams(("parallel",)),
        name="out_proj_ln_route",
    )(mrg, stream[0], stream[1], mod, w_out, ln_g, ln_b, wr_cat, br)


def _rank_kernel(r_ref, rank_ref, cnt_ref, carry):
    @pl.when(pl.program_id(0) == 0)
    def _():
        carry[...] = jnp.zeros_like(carry)

    slab = r_ref[...]
    tm = slab.shape[0]
    lane = lax.broadcasted_iota(jnp.int32, slab.shape, 1)
    lane_f = lane.astype(F32)
    e1 = slab[:, 0:1]
    e2 = slab[:, 1:2]
    hit1 = lane_f == e1
    hit2 = lane_f == e2
    onehot = jnp.where(hit1 | hit2, 1.0, 0.0)
    row = lax.broadcasted_iota(jnp.int32, (tm, tm), 0)
    col = lax.broadcasted_iota(jnp.int32, (tm, tm), 1)
    lower = jnp.where(col < row, 1.0, 0.0).astype(BF16)
    before = _dot(lower, onehot.astype(BF16)) + carry[0:1, :]
    r1 = jnp.sum(jnp.where(hit1, before, 0.0), axis=1, keepdims=True)
    r2 = jnp.sum(jnp.where(hit2, before, 0.0), axis=1, keepdims=True)
    rank_ref[...] = jnp.where(lane == 0, r1, jnp.where(lane == 1, r2, 0.0))
    carry[...] = carry[...] + jnp.sum(onehot, axis=0, keepdims=True)
    cnt_ref[...] = carry[...]


def _rank_call(route):
    n = route.shape[0]
    tm = ROW_TM
    return pl.pallas_call(
        _rank_kernel,
        grid=(n // tm,),
        in_specs=[pl.BlockSpec((tm, LANES), lambda i: (i, 0))],
        out_specs=[pl.BlockSpec((tm, LANES), lambda i: (i, 0)),
                   pl.BlockSpec((8, LANES), lambda i: (0, 0))],
        out_shape=[jax.ShapeDtypeStruct((n, LANES), F32), jax.ShapeDtypeStruct((8, LANES), F32)],
        scratch_shapes=[pltpu.VMEM((8, LANES), F32)],
        compiler_params=_cparams(("arbitrary",)),
        name="expert_ranks",
    )(route)


def _dest_kernel(r_ref, rank_ref, ps_ref, o_ref):
    slab = r_ref[...]
    rank = rank_ref[...]
    lane = lax.broadcasted_iota(jnp.int32, slab.shape, 1)
    lane_f = lane.astype(F32)
    ps = ps_ref[...]
    s1 = jnp.sum(jnp.where(lane_f == slab[:, 0:1], ps, 0.0), axis=1, keepdims=True) + rank[:, 0:1]
    s2 = jnp.sum(jnp.where(lane_f == slab[:, 1:2], ps, 0.0), axis=1, keepdims=True) + rank[:, 1:2]
    o_ref[...] = jnp.where(lane == 0, s1, jnp.where(lane == 1, s2, 0.0))


def _dest_call(route, rank_slab, pstarts_row):
    n = route.shape[0]
    tm = n // 8
    return pl.pallas_call(
        _dest_kernel,
        grid=(n // tm,),
        in_specs=[pl.BlockSpec((tm, LANES), lambda i: (i, 0)),
                  pl.BlockSpec((tm, LANES), lambda i: (i, 0)),
                  _resident((1, LANES))],
        out_specs=pl.BlockSpec((tm, LANES), lambda i: (i, 0)),
        out_shape=jax.ShapeDtypeStruct((n, LANES), F32),
        compiler_params=_cparams(("parallel",)),
        name="expert_slots",
    )(route, rank_slab, pstarts_row)


def _moe_kernel(be_ref, first_ref, nxt_ref, par_ref, nu_ref, x_ref, wg_hbm, wu_hbm, wd_hbm, y_ref,
                gbuf, ubuf, dbuf, sems, *, layer):
    j = pl.program_id(0)

    def weight_copies(e, slot):
        copies = []
        for m, (hbm, buf) in enumerate(((wg_hbm, gbuf), (wu_hbm, ubuf), (wd_hbm, dbuf))):
            band = buf.shape[1] // MOE_DMA_CHUNKS
            for c in range(MOE_DMA_CHUNKS):
                rows = pl.ds(c * band, band)
                copies.append(pltpu.make_async_copy(hbm.at[layer, e, rows], buf.at[slot, rows], sems.at[slot, m]))
        return copies

    @pl.when(j < nu_ref[0])
    def _():
        slot = par_ref[j]

        @pl.when(j == 0)
        def _():
            for cp in weight_copies(be_ref[0], 0):
                cp.start()

        @pl.when(first_ref[j] == 1)
        def _():
            @pl.when(nxt_ref[j] >= 0)
            def _():
                for cp in weight_copies(nxt_ref[j], 1 - slot):
                    cp.start()

            for cp in weight_copies(be_ref[j], slot):
                cp.wait()

        xb = x_ref[...]
        hg = _dot(xb, gbuf[slot].astype(BF16))
        hu = _dot(xb, ubuf[slot].astype(BF16))
        hb = (hg * jax.nn.sigmoid(hg) * hu).astype(BF16)
        y_ref[...] = _dot(hb, dbuf[slot].astype(BF16)).astype(y_ref.dtype)

    @pl.when(j >= nu_ref[0])
    def _():
        y_ref[...] = jnp.zeros_like(y_ref)


def _moe_call(block_e, first, nxt, parity, n_used, u_sorted, w_gate, w_up, w_down, layer):
    n_slots, d = u_sorted.shape
    bm = MOE_BM
    nb = n_slots // bm
    de = w_gate.shape[-1]

    def xmap(j, be, fi, nx_, pa, nu):
        return (jnp.minimum(j, nu[0] - 1), 0)

    grid_spec = pltpu.PrefetchScalarGridSpec(
        num_scalar_prefetch=5,
        grid=(nb,),
        in_specs=[pl.BlockSpec((bm, d), xmap),
                  pl.BlockSpec(memory_space=pl.ANY),
                  pl.BlockSpec(memory_space=pl.ANY),
                  pl.BlockSpec(memory_space=pl.ANY)],
        out_specs=pl.BlockSpec((bm, d), lambda j, *_: (j, 0)),
        scratch_shapes=[pltpu.VMEM((2, d, de), F32), pltpu.VMEM((2, d, de), F32), pltpu.VMEM((2, de, d), F32),
                        pltpu.SemaphoreType.DMA((2, 3))],
    )
    return pl.pallas_call(
        functools.partial(_moe_kernel, layer=layer),
        grid_spec=grid_spec,
        out_shape=jax.ShapeDtypeStruct((n_slots, d), BF16),
        compiler_params=_cparams(("arbitrary",)),
        name="expert_mlp",
    )(block_e, first, nxt, parity, n_used, u_sorted, w_gate, w_up, w_down)


def _combine_kernel(x_ref, y0_ref, y1_ref, r_ref, mod_ref, lg_ref, lb_ref, o_ref, *, alpha):
    slab = r_ref[...]
    g1 = slab[:, 2:3]
    g2 = slab[:, 3:4]
    mx = g1 * y0_ref[...].astype(F32) + g2 * y1_ref[...].astype(F32)
    gate_f = mod_ref[0, 5:6, :]
    o_ref[...] = _layer_norm(alpha * x_ref[...] + gate_f * mx, lg_ref[...], lb_ref[...])


def _combine_call(xs, y0, y1, route, mod, ln_g, ln_b, mod_row, alpha):
    n, d = xs.shape
    tm = ROW_TM
    return pl.pallas_call(
        functools.partial(_combine_kernel, alpha=alpha),
        grid=(n // tm,),
        in_specs=[pl.BlockSpec((tm, d), lambda i: (i, 0)),
                  pl.BlockSpec((tm, d), lambda i: (i, 0)),
                  pl.BlockSpec((tm, d), lambda i: (i, 0)),
                  pl.BlockSpec((tm, LANES), lambda i: (i, 0)),
                  pl.BlockSpec((1, 6, d), lambda i: (mod_row(i, tm), 0, 0)),
                  pl.BlockSpec((1, d), lambda i: (0, 0)),
                  pl.BlockSpec((1, d), lambda i: (0, 0))],
        out_specs=pl.BlockSpec((tm, d), lambda i: (i, 0)),
        out_shape=jax.ShapeDtypeStruct((n, d), F32),
        compiler_params=_cparams(("parallel",)),
        name="moe_combine_ln",
    )(xs, y0, y1, route, mod, ln_g, ln_b)


def _rope_angles(seq, dim):
    tpos = np.arange(seq)
    row = (tpos // GRID_W).astype(np.float32)
    col = (tpos % GRID_W).astype(np.float32)
    quarter = dim // 4
    inv_freq = jnp.asarray(ROPE_THETA, F32) ** (-jnp.arange(quarter, dtype=F32) / quarter)
    ang_r = jnp.asarray(row)[:, None] * inv_freq
    ang_c = jnp.asarray(col)[:, None] * inv_freq
    return jnp.concatenate([ang_r, ang_r, ang_c, ang_c], axis=-1)


def _rot_sign(dim):
    l = np.arange(dim)
    return np.where((l & (dim // 4)) == 0, -1.0, 1.0).astype(np.float32), l ^ (dim // 4)


def _rope_tables(seq, pad_rows):
    ang_a = _rope_angles(seq, HEAD_DIM)
    sign_a, _ = _rot_sign(HEAD_DIM)
    cos_a = jnp.concatenate([jnp.cos(ang_a), jnp.ones((pad_rows, HEAD_DIM), F32)], 0)
    sin_a = jnp.concatenate([jnp.sin(ang_a) * sign_a, jnp.zeros((pad_rows, HEAD_DIM), F32)], 0)
    ang_b = _rope_angles(seq, MLA_ROPE)
    cos_b = jnp.concatenate([jnp.cos(ang_b), jnp.ones((pad_rows, MLA_ROPE), F32)], 0)
    sin_b = jnp.concatenate([jnp.sin(ang_b), jnp.zeros((pad_rows, MLA_ROPE), F32)], 0)
    return cos_a, sin_a, jnp.tile(cos_b, (1, 2)), jnp.tile(sin_b, (1, 2))


ORIG_AV, ORIG_BQ, ORIG_BKR, ORIG_CQ, ORIG_W = 1024, 1280, 2048, 2112, 10176
W_PREP_ROWS = 128


def _w_in_prep_kernel(w_ref, tail_ref, o_ref, *, cq_scale):
    rows = w_ref.shape[1]
    lane = lax.broadcasted_iota(jnp.int32, (rows, LANES), 1)
    low = lane < MLA_ROPE
    quarter = MLA_ROPE // 4

    o_ref[0, :, COL_AQ:COL_BQ] = w_ref[0, :, 0:ORIG_AV].astype(BF16)
    o_ref[0, :, COL_BQ:COL_BKR] = w_ref[0, :, ORIG_BQ:ORIG_BKR].astype(BF16)
    o_ref[0, :, COL_AV:COL_CQ] = w_ref[0, :, ORIG_AV:ORIG_BQ].astype(BF16)
    kb = w_ref[0, :, ORIG_BKR:ORIG_BKR + LANES]
    rot = jnp.where((lane & quarter) == 0, -pltpu.roll(kb, MLA_ROPE - quarter, 1), pltpu.roll(kb, MLA_ROPE + quarter, 1))
    o_ref[0, :, COL_BKR:COL_AV] = jnp.where(low, kb, rot).astype(BF16)
    n_shift = (PROJ_W - COL_CQ) // LANES
    for b in range(n_shift):
        src = ORIG_BKR + LANES * b
        first = w_ref[0, :, src:src + LANES]
        second = w_ref[0, :, src + LANES:src + 2 * LANES] if b < n_shift - 1 else tail_ref[0]
        v = jnp.where(low, pltpu.roll(first, MLA_ROPE, 1), pltpu.roll(second, MLA_ROPE, 1))
        if COL_CQ + LANES * b < COL_CK:
            v = v * cq_scale
        o_ref[0, :, COL_CQ + LANES * b:COL_CQ + LANES * (b + 1)] = v.astype(BF16)


def _permute_w_in(w):
    depth, d, width = w.shape
    assert width == ORIG_W and ORIG_W - ORIG_CQ == PROJ_W - COL_CQ
    tail = jnp.pad(w[:, :, ORIG_W - MLA_ROPE:], ((0, 0), (0, 0), (0, LANES - MLA_ROPE)))
    return pl.pallas_call(
        functools.partial(_w_in_prep_kernel, cq_scale=SCORE_LOG2E / math.sqrt(HEAD_DIM)),
        grid=(depth, d // W_PREP_ROWS),
        in_specs=[pl.BlockSpec((1, W_PREP_ROWS, width), lambda l, i: (l, i, 0)),
                  pl.BlockSpec((1, W_PREP_ROWS, LANES), lambda l, i: (l, i, 0))],
        out_specs=pl.BlockSpec((1, W_PREP_ROWS, PROJ_W), lambda l, i: (l, i, 0)),
        out_shape=jax.ShapeDtypeStruct((depth, d, PROJ_W), BF16),
        compiler_params=_cparams(("parallel", "parallel")),
        name="w_in_prep",
    )(w, tail)


def _rot_cols(w):
    dim = w.shape[-1]
    wr = w.reshape(w.shape[:-1] + (2, 2, dim // 4))
    return jnp.stack([-wr[..., 1, :], wr[..., 0, :]], axis=-2).reshape(w.shape)


def _permute_w_uq(w):
    r = w.shape[0]
    w3 = w.reshape(r, MLA_HEADS, MLA_QK)
    nope = w3[:, :, :MLA_NOPE].reshape(r, MLA_HEADS * MLA_NOPE)
    rope = w3[:, :, MLA_NOPE:]
    rot = _rot_cols(rope)
    zpad = jnp.zeros((r, MLA_HEADS, LANES - MLA_ROPE), w.dtype)
    rope_p = jnp.concatenate([rope, zpad], -1).reshape(r, MLA_HEADS * LANES)
    rot_p = jnp.concatenate([rot, zpad], -1).reshape(r, MLA_HEADS * LANES)
    return jnp.concatenate([nope, rope_p, rot_p], axis=1).astype(BF16)


def _permute_w_ukv(w):
    r = w.shape[0]
    w3 = w.reshape(r, MLA_HEADS, MLA_NOPE + MLA_V)
    kn = w3[:, :, :MLA_NOPE].reshape(r, MLA_HEADS * MLA_NOPE)
    vv = w3[:, :, MLA_NOPE:].reshape(r, MLA_HEADS * MLA_V)
    return jnp.concatenate([kn, vv], axis=1).astype(BF16)


def _na_bias_tables(rpb, seq):
    w, kh, kw = GRID_W, NA_KH, NA_KW
    rows = seq // w
    nh, n_dr, n_dc = rpb.shape
    line = jnp.full((nh, n_dr, 2 * w), NEG_BIG, F32).at[:, :, w - kw:w - kw + n_dc].set(rpb.astype(F32) * SCORE_LOG2E)
    skew = jnp.broadcast_to(line[:, :, None, :], (nh, n_dr, w, 2 * w)).reshape(nh, n_dr, 2 * w * w)
    skew = skew[:, :, :w * (2 * w - 1)].reshape(nh, n_dr, w, 2 * w - 1)
    tc = skew[:, :, :, w - 1:2 * w - 1]
    cq = np.arange(w)[:, None]
    ck = np.arange(w)[None, :]
    cs = np.clip(cq - kw // 2, 0, w - kw)
    col_ok = (ck >= cs) & (ck < cs + kw)
    tc = jnp.where(col_ok[None, None], tc, NEG_BIG)
    tc = jnp.concatenate([tc, jnp.full((nh, 1, w, w), NEG_BIG, F32)], axis=1)
    tc2 = jnp.concatenate([tc, tc], axis=-1)
    blks = []
    tile_starts = (0, NA_ROWS, rows - NA_ROWS)
    for variant, r0 in enumerate(tile_starts):
        rb = int(np.clip(r0 - kh // 2, 0, rows - NA_BAND))
        rq = r0 + np.arange(NA_ROWS)[:, None]
        rk = rb + np.arange(NA_BAND)[None, :]
        rs = np.clip(rq - kh // 2, 0, rows - kh)
        row_ok = (rk >= rs) & (rk < rs + kh)
        band_blk = np.where(row_ok, rk - rq + kh - 1, n_dr)
        win_blk = []
        for hq, off in enumerate(_na_window_offsets(variant, len(tile_starts))):
            part = band_blk[hq * NA_ROWS // 2:(hq + 1) * NA_ROWS // 2]
            assert (np.delete(part, np.s_[off:off + NA_WIN], axis=1) == n_dr).all()
            win_blk.extend(part[:, off:off + NA_WIN].tolist())
        blks.append(win_blk)
    return pl.pallas_call(
        functools.partial(_na_bias_kernel, blks=blks),
        grid=(nh,),
        in_specs=[pl.BlockSpec((1, n_dr + 1, w, 2 * w), lambda h: (h, 0, 0, 0))],
        out_specs=pl.BlockSpec((1, len(blks), NA_ROWS * w, NA_WIN * w), lambda h: (h, 0, 0, 0)),
        out_shape=jax.ShapeDtypeStruct((nh, len(blks), NA_ROWS * w, NA_WIN * w), F32),
        compiler_params=_cparams(("parallel",)),
        name="na_bias_table",
    )(tc2)


def _na_bias_kernel(tc_ref, o_ref, *, blks):
    w = GRID_W
    low = lax.broadcasted_iota(jnp.int32, (w, 2 * w), 1) < w
    for v, blk in enumerate(blks):
        for rq, row in enumerate(blk):
            for pr in range(len(row) // 2):
                pair = jnp.where(low, tc_ref[0, row[2 * pr]], tc_ref[0, row[2 * pr + 1]])
                o_ref[0, v, rq * w:(rq + 1) * w, pr * 2 * w:(pr + 1) * 2 * w] = pair


def kernel(x, c, ctx, c_ctx, w_ada, b_ada, w_in, gqa_q_norm, gqa_k_norm, mla_q_norm, mla_kv_norm, mla_w_uq, mla_w_ukv, na_rpb, w_branch_a, w_branch_b, w_branch_c, w_out, ln1_g, ln1_b, w_router_group, b_router_group, w_router_expert, b_router_expert, w_expert_gate, w_expert_up, w_expert_down, ln2_g, ln2_b):
    bsz, seq, d = x.shape
    n_ctx = ctx.shape[1]
    depth = w_ada.shape[0]
    nx, nc = bsz * seq, bsz * n_ctx
    t = nx + nc
    assert seq % PROJ_TM == 0 and nc == PROJ_TM and seq % (NA_BAND * GRID_W) == 0 and nx % n_ctx == 0
    alpha = (2 * depth) ** 0.25

    def mod_row(i, tm):
        return jnp.minimum(i // (seq // tm), bsz)

    def rope_blk(i, tm):
        return jnp.where(i < nx // tm, i % (seq // tm), seq // tm)

    stream = (x.reshape(nx, d), ctx.reshape(nc, d), 0)
    cc = jnp.concatenate([c, c_ctx[None], jnp.zeros((8 - bsz - 1, d), F32)], axis=0)
    mod_all = _ada_call(cc, w_ada, b_ada).reshape(depth, 8, 6, d)
    cos_a, sin_a, cos_b, sin_b = _rope_tables(seq, ROW_TM)
    w_in_p = _permute_w_in(w_in)

    attn_kw = dict(bsz=bsz, seq=seq, n_ctx=n_ctx)
    for i in range(depth):
        last = i == depth - 1
        n_rows = nx if last else t
        mod = mod_all[i]
        p = _proj_call(stream, nx, t, mod, w_in_p, i, mod_row)

        qa, ka = _prep_a_call(p, cos_a, sin_a, gqa_q_norm[i][None], gqa_k_norm[i][None], rope_blk)
        a_kw = dict(n_heads=GQA_HEADS, dq=HEAD_DIM, q_col=lambda h: h, k_col=lambda h: h // GQA_GROUP,
                    v_col=lambda h: COL_AV // HEAD_DIM + h // GQA_GROUP, **attn_kw)
        o_a = _dense_attn_call(qa, ka, p, **a_kw)
        qb, kb, vb = _prep_b_call(p, cos_b, sin_b, mla_q_norm[i][None], mla_kv_norm[i][None],
                                  _permute_w_uq(mla_w_uq[i]), _permute_w_ukv(mla_w_ukv[i]), rope_blk)
        b_kw = dict(n_heads=MLA_HEADS, dq=MLA_PAD, q_col=lambda h: h, k_col=lambda h: h, v_col=lambda h: h, **attn_kw)
        o_b = _dense_attn_call(qb, kb, vb, **b_kw)
        o_c = _na_call(p, _na_bias_tables(na_rpb[i], seq), **attn_kw)
        o_lat = (o_a, o_b, o_c)
        if last:
            o_ctx = o_lat
        else:
            c_kw = dict(n_heads=NA_HEADS, dq=HEAD_DIM, q_col=lambda h: COL_CQ // HEAD_DIM + h,
                        k_col=lambda h: COL_CK // HEAD_DIM + h, v_col=lambda h: COL_CV // HEAD_DIM + h, **attn_kw)
            o_ctx = (_ctx_attn_call(qa, ka, p, **a_kw), _ctx_attn_call(qb, kb, vb, **b_kw),
                     _ctx_attn_call(p, p, p, **c_kw))

        mrg = _branch_call(o_lat, o_ctx, p, w_branch_a[i].astype(BF16), w_branch_b[i].astype(BF16),
                           w_branch_c[i].astype(BF16), n_rows)
        wr = jnp.concatenate([w_router_group[i], w_router_expert[i],
                              jnp.zeros((d, LANES - N_GROUPS - N_EXPERTS), F32)], axis=1)
        br = jnp.concatenate([b_router_group[i], b_router_expert[i],
                              jnp.zeros((LANES - N_GROUPS - N_EXPERTS,), F32)])[None]
        wr_cat = jnp.concatenate(_split_bf16(wr), axis=1)
        xs, u_f, route = _out_call(mrg, stream, nx, mod, w_out[i].astype(BF16), ln1_g[i][None], ln1_b[i][None],
                                   wr_cat, br, mod_row, n_rows, alpha)

        rank_slab, cnt = _rank_call(route)
        eid = route[:, :TOP_K].astype(jnp.int32)
        counts = cnt[0, :N_EXPERTS].astype(jnp.int32)
        padded = ((counts + MOE_BM - 1) // MOE_BM) * MOE_BM
        pends = jnp.cumsum(padded)
        pstarts = pends - padded
        pstarts_row = jnp.concatenate([pstarts.astype(F32), jnp.zeros((LANES - N_EXPERTS,), F32)])[None]
        dest = _dest_call(route, rank_slab, pstarts_row)[:, :TOP_K].astype(jnp.int32)
        nb = -(-(n_rows * TOP_K) // MOE_BM) + N_EXPERTS
        n_used = (pends[-1] // MOE_BM).astype(jnp.int32)
        blk = jnp.minimum(jnp.arange(nb, dtype=jnp.int32), n_used - 1)
        block_e = jnp.clip(jnp.searchsorted(pends, blk * MOE_BM, side="right"), 0, N_EXPERTS - 1).astype(jnp.int32)
        order = jnp.argsort(eid.reshape(-1), stable=True).astype(jnp.int32)
        starts = jnp.cumsum(counts) - counts
        slot = jnp.arange(nb * MOE_BM, dtype=jnp.int32).reshape(nb, MOE_BM)
        blk_shift = (starts - pstarts)[block_e][:, None]
        blk_limit = (pstarts + counts)[block_e][:, None]
        pos = jnp.clip(slot + blk_shift, 0, n_rows * TOP_K - 1)
        slot_tok = jnp.where(slot < blk_limit, order.at[pos].get(mode="promise_in_bounds") // TOP_K,
                             slot % n_rows).reshape(-1)
        u_sorted = u_f.at[slot_tok].get(mode="promise_in_bounds")
        jj = jnp.arange(nb, dtype=jnp.int32)
        first = ((jj == 0) | (block_e != jnp.roll(block_e, 1))).astype(jnp.int32)
        parity = (jnp.cumsum(first) - 1) % 2
        live = jnp.where(counts > 0, jnp.arange(N_EXPERTS, dtype=jnp.int32), N_EXPERTS)
        next_live = jnp.concatenate([lax.cummin(live, reverse=True)[1:], jnp.full((1,), N_EXPERTS, jnp.int32)])
        nxt = jnp.where(next_live < N_EXPERTS, next_live, -1)[block_e]
        y_sorted = _moe_call(block_e, first, nxt.astype(jnp.int32), parity.astype(jnp.int32), n_used[None],
                             u_sorted, w_expert_gate, w_expert_up, w_expert_down, i)
        y0 = y_sorted.at[dest[:, 0]].get(mode="promise_in_bounds")
        y1 = y_sorted.at[dest[:, 1]].get(mode="promise_in_bounds")
        xs = _combine_call(xs, y0, y1, route, mod, ln2_g[i][None], ln2_b[i][None], mod_row, alpha)
        stream = (xs, xs, nx)
    return xs.reshape(bsz, seq, d)
```

```python
import functools
import math

import jax
import jax.numpy as jnp
import numpy as np
from jax import lax
from jax.experimental import pallas as pl
from jax.experimental.pallas import tpu as pltpu

F32 = jnp.float32
BF16 = jnp.bfloat16

HEAD_DIM = 128
GRID_W = 64
ROPE_THETA = 10000.0
GQA_HEADS, GQA_KV_HEADS = 6, 2
GQA_GROUP = GQA_HEADS // GQA_KV_HEADS
MLA_HEADS, MLA_Q_RANK, MLA_KV_RANK = 5, 512, 256
MLA_NOPE, MLA_ROPE, MLA_V = 128, 64, 128
MLA_QK = MLA_NOPE + MLA_ROPE
MLA_PAD = 256
NA_HEADS, NA_KH, NA_KW = 5, 8, 16
N_GROUPS, EXPERTS_PER_GROUP, TOP_K, D_EXPERT = 8, 8, 2, 512
N_EXPERTS = N_GROUPS * EXPERTS_PER_GROUP
LN_EPS = 1e-6
RMS_EPS = 1e-6
NEG_BIG = -1e30
SCORE_LOG2E = math.log2(math.e)

LANES = 128
VMEM_LIMIT = 56 * 1024 * 1024
PROJ_TM, PROJ_TN = 1024, 1024
ROW_TM = 512
MM_TM = 512
ATTN_TQ = 512
ATTN_SUB = 4
ATTN_KC = 1024
NA_ROWS = 8
NA_SUB = 4
NA_BAND = 16
NA_WIN = 12
MOE_BM = 256
MOE_DMA_CHUNKS = 4
ADA_TN = 512

COL_AQ, COL_AK = 0, 768
COL_BQ, COL_BKV, COL_BKR = 1024, 1536, 1792
COL_AV, COL_CQ, COL_CK, COL_CV = 1920, 2176, 2816, 3456
COL_GATE = 4096
PROJ_W = 10240


def _cparams(sem):
    return pltpu.CompilerParams(dimension_semantics=sem, vmem_limit_bytes=VMEM_LIMIT)


def _resident(shape):
    return pl.BlockSpec(shape, lambda *_: (0,) * len(shape), pipeline_mode=pl.Buffered(1))


def _dot(a, b):
    return jnp.dot(a, b, preferred_element_type=F32)


def _dot_nt(a, b):
    return lax.dot_general(a, b, (((1,), (1,)), ((), ())), preferred_element_type=F32)


def _split_bf16(a):
    hi = a.astype(BF16)
    lo = (a - hi.astype(F32)).astype(BF16)
    return hi, lo


def _ada_kernel(c_ref, w_ref, b_ref, o_ref):
    a = c_ref[...]
    a = a * jax.nn.sigmoid(a)
    a_hi, a_lo = _split_bf16(a)
    w_hi, w_lo = _split_bf16(w_ref[0])
    acc = _dot(a_hi, w_hi) + _dot(a_lo, w_hi) + _dot(a_hi, w_lo)
    o_ref[0] = acc + b_ref[0]


def _ada_call(cc, w_ada, b_ada):
    depth, d, n = w_ada.shape
    rows = cc.shape[0]
    return pl.pallas_call(
        _ada_kernel,
        grid=(depth, n // ADA_TN),
        in_specs=[pl.BlockSpec((rows, d), lambda l, j: (0, 0)),
                  pl.BlockSpec((1, d, ADA_TN), lambda l, j: (l, 0, j)),
                  pl.BlockSpec((1, 1, ADA_TN), lambda l, j: (l, 0, j))],
        out_specs=pl.BlockSpec((1, rows, ADA_TN), lambda l, j: (l, 0, j)),
        out_shape=jax.ShapeDtypeStruct((depth, rows, n), F32),
        compiler_params=_cparams(("parallel", "parallel")),
        name="ada_modulation",
    )(cc, w_ada, b_ada.reshape(depth, 1, n))


def _proj_kernel(xl_ref, xc_ref, mod_ref, w_ref, o_ref, u_scr, *, n_lat_tiles):
    @pl.when(pl.program_id(1) == 0)
    def _():
        def modulate(x):
            u_scr[...] = (x * (1.0 + mod_ref[0, 1:2, :]) + mod_ref[0, 0:1, :]).astype(BF16)

        @pl.when(pl.program_id(0) < n_lat_tiles)
        def _():
            modulate(xl_ref[...])

        @pl.when(pl.program_id(0) >= n_lat_tiles)
        def _():
            modulate(xc_ref[...])

    o_ref[...] = _dot_nt(u_scr[...], w_ref[0]).astype(BF16)


def _stream_specs(stream, tm, n_lat):
    lat, ctx, ctx_row0 = stream
    d = lat.shape[1]
    n_lat_tiles = n_lat // tm
    ctx_tile0 = ctx_row0 // tm

    def lat_map(i, *_):
        return (jnp.minimum(i, n_lat_tiles - 1), 0)

    def ctx_map(i, *_):
        return (ctx_tile0 + jnp.maximum(i - n_lat_tiles, 0), 0)

    return [pl.BlockSpec((tm, d), lat_map), pl.BlockSpec((tm, d), ctx_map, pipeline_mode=pl.Buffered(1))]


def _proj_call(stream, n_lat, n_tok, mod, w_in_p, layer, mod_row):
    d = stream[0].shape[1]
    return pl.pallas_call(
        functools.partial(_proj_kernel, n_lat_tiles=n_lat // PROJ_TM),
        grid=(n_tok // PROJ_TM, PROJ_W // PROJ_TN),
        in_specs=_stream_specs(stream, PROJ_TM, n_lat) + [
            pl.BlockSpec((1, 6, d), lambda i, j: (mod_row(i, PROJ_TM), 0, 0)),
            pl.BlockSpec((1, PROJ_TN, d), lambda i, j: (layer, j, 0))],
        out_specs=pl.BlockSpec((PROJ_TM, PROJ_TN), lambda i, j: (i, j)),
        out_shape=jax.ShapeDtypeStruct((n_tok, PROJ_W), BF16),
        scratch_shapes=[pltpu.VMEM((PROJ_TM, d), BF16)],
        compiler_params=_cparams(("parallel", "arbitrary")),
        name="input_projection",
    )(stream[0], stream[1], mod, w_in_p)


def _rope128(y, cos, sin_signed, first_quarter):
    rot = jnp.where(first_quarter, pltpu.roll(y, 96, 1), pltpu.roll(y, 32, 1))
    return y * cos + rot * sin_signed


def _prep_a_kernel(p_ref, cos_ref, sin_ref, gq_ref, gk_ref, q_ref, k_ref, *, scale):
    cos = cos_ref[...]
    sin_signed = sin_ref[...]
    lane = lax.broadcasted_iota(jnp.int32, cos.shape, 1)
    first_quarter = (lane & 32) == 0
    for h in range(GQA_HEADS + GQA_KV_HEADS):
        xh = p_ref[:, h * HEAD_DIM:(h + 1) * HEAD_DIM].astype(F32)
        r = lax.rsqrt(jnp.mean(xh * xh, axis=1, keepdims=True) + RMS_EPS)
        if h < GQA_HEADS:
            y = _rope128(xh * r * gq_ref[...], cos, sin_signed, first_quarter) * scale
            q_ref[:, h * HEAD_DIM:(h + 1) * HEAD_DIM] = y.astype(BF16)
        else:
            hk = h - GQA_HEADS
            y = _rope128(xh * r * gk_ref[...], cos, sin_signed, first_quarter)
            k_ref[:, hk * HEAD_DIM:(hk + 1) * HEAD_DIM] = y.astype(BF16)


def _prep_a_call(p, cos_t, sin_t, gq, gk, rope_blk):
    t = p.shape[0]
    tm = ROW_TM
    wq, wk = GQA_HEADS * HEAD_DIM, GQA_KV_HEADS * HEAD_DIM
    return pl.pallas_call(
        functools.partial(_prep_a_kernel, scale=SCORE_LOG2E / math.sqrt(HEAD_DIM)),
        grid=(t // tm,),
        in_specs=[pl.BlockSpec((tm, wq + wk), lambda i: (i, 0)),
                  pl.BlockSpec((tm, HEAD_DIM), lambda i: (rope_blk(i, tm), 0)),
                  pl.BlockSpec((tm, HEAD_DIM), lambda i: (rope_blk(i, tm), 0)),
                  pl.BlockSpec((1, HEAD_DIM), lambda i: (0, 0)),
                  pl.BlockSpec((1, HEAD_DIM), lambda i: (0, 0))],
        out_specs=[pl.BlockSpec((tm, wq), lambda i: (i, 0)),
                   pl.BlockSpec((tm, wk), lambda i: (i, 0))],
        out_shape=[jax.ShapeDtypeStruct((t, wq), BF16), jax.ShapeDtypeStruct((t, wk), BF16)],
        compiler_params=_cparams(("parallel",)),
        name="gqa_qk_prep",
    )(p, cos_t, sin_t, gq, gk)


def _prep_b_kernel(ql_ref, kvl_ref, kr_ref, cos_ref, sin_ref, gq_ref, gkv_ref, wq_ref, wkv_ref,
                   q_ref, k_ref, v_ref, *, scale):
    nh, hp = MLA_HEADS, MLA_PAD
    wn = nh * MLA_NOPE
    cos = cos_ref[...]
    sin = sin_ref[...]
    ql = ql_ref[...].astype(F32)
    ql = ql * lax.rsqrt(jnp.mean(ql * ql, axis=1, keepdims=True) + RMS_EPS) * gq_ref[...]
    qf = _dot(ql.astype(BF16), wq_ref[...])
    kvl = kvl_ref[...].astype(F32)
    kvl = kvl * lax.rsqrt(jnp.mean(kvl * kvl, axis=1, keepdims=True) + RMS_EPS) * gkv_ref[...]
    kvf = _dot(kvl.astype(BF16), wkv_ref[...])
    lane = lax.broadcasted_iota(jnp.int32, cos.shape, 1)
    low = lane < MLA_ROPE
    tk = kr_ref[...].astype(F32) * jnp.where(low, cos, sin)
    kr = jnp.where(low, tk + pltpu.roll(tk, MLA_ROPE, 1), 0.0).astype(BF16)
    for h in range(nh):
        q_ref[:, h * hp:h * hp + LANES] = (qf[:, h * LANES:(h + 1) * LANES] * scale).astype(BF16)
        qr = qf[:, wn + h * LANES:wn + (h + 1) * LANES] * cos + qf[:, 2 * wn + h * LANES:2 * wn + (h + 1) * LANES] * sin
        q_ref[:, h * hp + LANES:(h + 1) * hp] = (qr * scale).astype(BF16)
        k_ref[:, h * hp:h * hp + LANES] = kvf[:, h * LANES:(h + 1) * LANES].astype(BF16)
        k_ref[:, h * hp + LANES:(h + 1) * hp] = kr
    v_ref[...] = kvf[:, wn:].astype(BF16)


def _prep_b_call(p, cos_t, sin_t, gq, gkv, wq, wkv, rope_blk):
    t = p.shape[0]
    tm = ROW_TM
    nh = MLA_HEADS
    return pl.pallas_call(
        functools.partial(_prep_b_kernel, scale=SCORE_LOG2E / math.sqrt(MLA_QK)),
        grid=(t // tm,),
        in_specs=[pl.BlockSpec((tm, MLA_Q_RANK), lambda i: (i, COL_BQ // MLA_Q_RANK)),
                  pl.BlockSpec((tm, MLA_KV_RANK), lambda i: (i, COL_BKV // MLA_KV_RANK)),
                  pl.BlockSpec((tm, LANES), lambda i: (i, COL_BKR // LANES)),
                  pl.BlockSpec((tm, LANES), lambda i: (rope_blk(i, tm), 0)),
                  pl.BlockSpec((tm, LANES), lambda i: (rope_blk(i, tm), 0)),
                  pl.BlockSpec((1, MLA_Q_RANK), lambda i: (0, 0)),
                  pl.BlockSpec((1, MLA_KV_RANK), lambda i: (0, 0)),
                  _resident(wq.shape), _resident(wkv.shape)],
        out_specs=[pl.BlockSpec((tm, nh * MLA_PAD), lambda i: (i, 0)),
                   pl.BlockSpec((tm, nh * MLA_PAD), lambda i: (i, 0)),
                   pl.BlockSpec((tm, nh * MLA_V), lambda i: (i, 0))],
        out_shape=[jax.ShapeDtypeStruct((t, nh * MLA_PAD), BF16),
                   jax.ShapeDtypeStruct((t, nh * MLA_PAD), BF16),
                   jax.ShapeDtypeStruct((t, nh * MLA_V), BF16)],
        compiler_params=_cparams(("parallel",)),
        name="mla_prep",
    )(p, p, p, cos_t, sin_t, gq, gkv, wq, wkv)


def _online_attend(q, chunks):
    m = z = acc = None
    for k, v, bias in chunks:
        s = _dot_nt(q, k)
        if bias is not None:
            s = s + bias
        cm = jnp.max(s, axis=1, keepdims=True)
        if m is None:
            m = cm
            p = jnp.exp2(s - m)
            z = jnp.sum(p, axis=1, keepdims=True)
            acc = _dot(p.astype(BF16), v)
        else:
            m_new = jnp.maximum(m, cm)
            corr = jnp.exp2(m - m_new)
            p = jnp.exp2(s - m_new)
            z = z * corr + jnp.sum(p, axis=1, keepdims=True)
            acc = acc * corr + _dot(p.astype(BF16), v)
            m = m_new
    return acc / z


def _online_attend_ones(q, chunks, dv):
    m = acc = None
    for k, v_ones, bias in chunks:
        s = _dot_nt(q, k)
        if bias is not None:
            s = s + bias
        cm = jnp.max(s, axis=1, keepdims=True)
        if m is None:
            m = cm
            acc = _dot(jnp.exp2((s - m).astype(BF16)), v_ones)
        else:
            m_new = jnp.maximum(m, cm)
            corr = jnp.exp2(m - m_new)
            acc = acc * corr + _dot(jnp.exp2((s - m_new).astype(BF16)), v_ones)
            m = m_new
    return acc[:, :dv] / acc[:, dv:dv + 1]


def _dense_attn_kernel(q_ref, kl_ref, vl_ref, kc_ref, vc_ref, o_ref, v_ones):
    n_lat = kl_ref.shape[0]
    dv = vl_ref.shape[1]

    @pl.when(pl.program_id(2) == 0)
    def _():
        v_ones[0:n_lat, 0:dv] = vl_ref[...]
        v_ones[n_lat:, 0:dv] = vc_ref[...]
        v_ones[:, dv:] = jnp.ones((v_ones.shape[0], dv), BF16)

    chunks = [(kl_ref[c:c + ATTN_KC, :], v_ones[c:c + ATTN_KC, :], None) for c in range(0, n_lat, ATTN_KC)]
    chunks.append((kc_ref[...], v_ones[n_lat:, :], None))
    for r in range(0, q_ref.shape[0], ATTN_TQ):
        o_ref[r:r + ATTN_TQ, :] = _online_attend_ones(q_ref[r:r + ATTN_TQ, :], chunks, dv).astype(o_ref.dtype)


def _dense_attn_call(q, k, v, *, n_heads, dq, q_col, k_col, v_col, bsz, seq, n_ctx):
    tq = ATTN_TQ * ATTN_SUB
    tpb = seq // tq
    ctx0 = bsz * seq // n_ctx
    dv = HEAD_DIM
    return pl.pallas_call(
        _dense_attn_kernel,
        grid=(bsz, n_heads, tpb),
        in_specs=[pl.BlockSpec((tq, dq), lambda b, h, i: (b * tpb + i, q_col(h))),
                  pl.BlockSpec((seq, dq), lambda b, h, i: (b, k_col(h))),
                  pl.BlockSpec((seq, dv), lambda b, h, i: (b, v_col(h))),
                  pl.BlockSpec((n_ctx, dq), lambda b, h, i: (ctx0 + b, k_col(h))),
                  pl.BlockSpec((n_ctx, dv), lambda b, h, i: (ctx0 + b, v_col(h)))],
        out_specs=pl.BlockSpec((tq, dv), lambda b, h, i: (b * tpb + i, h)),
        out_shape=jax.ShapeDtypeStruct((bsz * seq, n_heads * dv), BF16),
        scratch_shapes=[pltpu.VMEM((seq + n_ctx, 2 * dv), BF16)],
        compiler_params=_cparams(("parallel", "parallel", "arbitrary")),
        name="dense_attention",
    )(q, k, v, k, v)


def _ctx_attn_kernel(q_ref, kc_ref, vc_ref, o_ref):
    o_ref[...] = _online_attend(q_ref[...], [(kc_ref[...], vc_ref[...], None)]).astype(o_ref.dtype)


def _ctx_attn_call(q, k, v, *, n_heads, dq, q_col, k_col, v_col, bsz, seq, n_ctx):
    ctx0 = bsz * seq // n_ctx
    dv = HEAD_DIM
    return pl.pallas_call(
        _ctx_attn_kernel,
        grid=(bsz, n_heads),
        in_specs=[pl.BlockSpec((n_ctx, dq), lambda b, h: (ctx0 + b, q_col(h))),
                  pl.BlockSpec((n_ctx, dq), lambda b, h: (ctx0 + b, k_col(h))),
                  pl.BlockSpec((n_ctx, dv), lambda b, h: (ctx0 + b, v_col(h)))],
        out_specs=pl.BlockSpec((n_ctx, dv), lambda b, h: (b, h)),
        out_shape=jax.ShapeDtypeStruct((bsz * n_ctx, n_heads * dv), BF16),
        compiler_params=_cparams(("parallel", "parallel")),
        name="context_attention",
    )(q, k, v)


def _na_window_offsets(variant, n_variants):
    shift = NA_BAND - NA_WIN
    return ((0, 0), (0, shift), (shift, shift))[0 if variant == 0 else (2 if variant == n_variants - 1 else 1)]


def _na_kernel(q_ref, k_ref, v_ref, kc_ref, vc_ref, *rest, rows_total, n_tiles):
    bias_refs, o_ref, v_ones = rest[:NA_SUB], rest[NA_SUB], rest[NA_SUB + 1]
    n_lat = k_ref.shape[0]
    dv = v_ref.shape[1]
    tile = NA_ROWS * GRID_W
    half = tile // 2
    win = NA_WIN * GRID_W
    shift = (NA_BAND - NA_WIN) * GRID_W

    @pl.when(pl.program_id(2) == 0)
    def _():
        v_ones[0:n_lat, 0:dv] = v_ref[...]
        v_ones[n_lat:, 0:dv] = vc_ref[...]
        v_ones[:, dv:] = jnp.ones((v_ones.shape[0], dv), BF16)

    ctx = (kc_ref[...], v_ones[n_lat:, :], None)
    for t in range(NA_SUB):
        i = pl.program_id(2) * NA_SUB + t
        start_blk = jnp.clip(2 * i - 1, 0, (rows_total - NA_BAND) // 4)
        band0 = start_blk * (4 * GRID_W)
        starts = (band0 + jnp.where(i == n_tiles - 1, shift, 0), band0 + jnp.where(i == 0, 0, shift))
        for hq, start in enumerate(starts):
            start = pl.multiple_of(start, 4 * GRID_W)
            rows = slice(t * tile + hq * half, t * tile + (hq + 1) * half)
            bias = bias_refs[t][0, 0, hq * half:(hq + 1) * half, :]
            chunks = [(k_ref[pl.ds(start, win), :], v_ones[pl.ds(start, win), :], bias), ctx]
            o_ref[rows, :] = _online_attend_ones(q_ref[rows, :], chunks, dv).astype(o_ref.dtype)


def _na_call(p, bias_tab, *, bsz, seq, n_ctx):
    tile = NA_ROWS * GRID_W
    n_tiles = seq // tile
    tq = tile * NA_SUB
    tpb = seq // tq
    rows_total = seq // GRID_W
    ctx0 = bsz * seq // n_ctx
    d = HEAD_DIM
    qc, kc, vc = COL_CQ // d, COL_CK // d, COL_CV // d

    def bias_spec(t):
        def index(b, h, i):
            tile_idx = i * NA_SUB + t
            return (h, jnp.where(tile_idx == 0, 0, jnp.where(tile_idx == n_tiles - 1, 2, 1)), 0, 0)
        return pl.BlockSpec((1, 1, tile, NA_WIN * GRID_W), index)

    return pl.pallas_call(
        functools.partial(_na_kernel, rows_total=rows_total, n_tiles=n_tiles),
        grid=(bsz, NA_HEADS, tpb),
        in_specs=[pl.BlockSpec((tq, d), lambda b, h, i: (b * tpb + i, qc + h)),
                  pl.BlockSpec((seq, d), lambda b, h, i: (b, kc + h)),
                  pl.BlockSpec((seq, d), lambda b, h, i: (b, vc + h)),
                  pl.BlockSpec((n_ctx, d), lambda b, h, i: (ctx0 + b, kc + h)),
                  pl.BlockSpec((n_ctx, d), lambda b, h, i: (ctx0 + b, vc + h))]
                 + [bias_spec(t) for t in range(NA_SUB)],
        out_specs=pl.BlockSpec((tq, d), lambda b, h, i: (b * tpb + i, h)),
        out_shape=jax.ShapeDtypeStruct((bsz * seq, NA_HEADS * d), BF16),
        scratch_shapes=[pltpu.VMEM((seq + n_ctx, 2 * d), BF16)],
        compiler_params=_cparams(("parallel", "parallel", "arbitrary")),
        name="neighbourhood_attention",
    )(p, p, p, p, p, *([bias_tab] * NA_SUB))


def _branch_kernel(oa_ref, ob_ref, oc_ref, oac_ref, obc_ref, occ_ref, ga_ref, gb_ref, gc_ref,
                   wa_ref, wb_ref, wc_ref, o_ref, *, n_lat_tiles):
    def gated(g_ref, w_ref, o_tile):
        return jax.nn.sigmoid(g_ref[...].astype(F32)) * _dot(o_tile, w_ref[...])

    @pl.when(pl.program_id(0) < n_lat_tiles)
    def _():
        acc = gated(ga_ref, wa_ref, oa_ref[...]) + gated(gb_ref, wb_ref, ob_ref[...]) + gated(gc_ref, wc_ref, oc_ref[...])
        o_ref[...] = acc.astype(BF16)

    @pl.when(pl.program_id(0) >= n_lat_tiles)
    def _():
        acc = gated(ga_ref, wa_ref, oac_ref[...]) + gated(gb_ref, wb_ref, obc_ref[...]) + gated(gc_ref, wc_ref, occ_ref[...])
        o_ref[...] = acc.astype(BF16)


def _branch_call(o_lat, o_ctx, p, wa, wb, wc, n_rows):
    d = wa.shape[1]
    tm = MM_TM
    g0 = COL_GATE // d
    n_lat_tiles = o_lat[0].shape[0] // tm
    lat_specs = [pl.BlockSpec((tm, o.shape[1]), lambda i: (jnp.minimum(i, n_lat_tiles - 1), 0)) for o in o_lat]
    ctx_specs = [pl.BlockSpec((tm, o.shape[1]), lambda i: (jnp.maximum(i - n_lat_tiles, 0), 0)) for o in o_ctx]
    return pl.pallas_call(
        functools.partial(_branch_kernel, n_lat_tiles=n_lat_tiles),
        grid=(n_rows // tm,),
        in_specs=lat_specs + ctx_specs + [
            pl.BlockSpec((tm, d), lambda i: (i, g0)),
            pl.BlockSpec((tm, d), lambda i: (i, g0 + 1)),
            pl.BlockSpec((tm, d), lambda i: (i, g0 + 2)),
            _resident(wa.shape), _resident(wb.shape), _resident(wc.shape)],
        out_specs=pl.BlockSpec((tm, d), lambda i: (i, 0)),
        out_shape=jax.ShapeDtypeStruct((n_rows, d), BF16),
        compiler_params=_cparams(("parallel",)),
        name="branch_merge",
    )(*o_lat, *o_ctx, p, p, p, wa, wb, wc)


def _layer_norm(z, g, b):
    mu = jnp.mean(z, axis=1, keepdims=True)
    zc = z - mu
    var = jnp.mean(zc * zc, axis=1, keepdims=True)
    return zc * lax.rsqrt(var + LN_EPS) * g + b


def _route(logits):
    lane = lax.broadcasted_iota(jnp.int32, logits.shape, 1)
    lane_f = lane.astype(F32)
    is_g = lane < N_GROUPS
    gl = jnp.where(is_g, logits, NEG_BIG)
    mg = jnp.max(gl, axis=1, keepdims=True)
    gsel = jnp.min(jnp.where(gl == mg, lane_f, float(LANES)), axis=1, keepdims=True)
    zg = jnp.sum(jnp.where(is_g, jnp.exp(gl - mg), 0.0), axis=1, keepdims=True)
    lo = N_GROUPS + gsel * EXPERTS_PER_GROUP
    is_e = (lane_f >= lo) & (lane_f < lo + EXPERTS_PER_GROUP)
    el = jnp.where(is_e, logits, NEG_BIG)
    t1 = jnp.max(el, axis=1, keepdims=True)
    i1 = jnp.min(jnp.where(el == t1, lane_f, float(LANES)), axis=1, keepdims=True)
    el2 = jnp.where(lane_f == i1, NEG_BIG, el)
    t2 = jnp.max(el2, axis=1, keepdims=True)
    i2 = jnp.min(jnp.where(el2 == t2, lane_f, float(LANES)), axis=1, keepdims=True)
    dd = jnp.exp(t2 - t1)
    g1 = 1.0 / (zg * (1.0 + dd))
    g2 = g1 * dd
    slab = jnp.where(lane == 0, i1 - N_GROUPS,
                     jnp.where(lane == 1, i2 - N_GROUPS,
                               jnp.where(lane == 2, g1, jnp.where(lane == 3, g2, 0.0))))
    return slab


def _out_kernel(m_ref, xl_ref, xc_ref, mod_ref, w_ref, lg_ref, lb_ref, wr_ref, br_ref,
                xo_ref, u_ref, r_ref, *, alpha, n_lat_tiles):
    @pl.when(pl.program_id(0) < n_lat_tiles)
    def _():
        xo_ref[...] = xl_ref[...]

    @pl.when(pl.program_id(0) >= n_lat_tiles)
    def _():
        xo_ref[...] = xc_ref[...]

    y = _dot(m_ref[...], w_ref[...])
    gate_m = mod_ref[0, 2:3, :]
    shift_f = mod_ref[0, 3:4, :]
    scale_f = mod_ref[0, 4:5, :]
    xn = _layer_norm(alpha * xo_ref[...] + gate_m * y, lg_ref[...], lb_ref[...])
    xo_ref[...] = xn
    u = xn * (1.0 + scale_f) + shift_f
    u_hi, u_lo = _split_bf16(u)
    u_ref[...] = u_hi
    both = _dot(u_hi, wr_ref[...])
    logits = both[:, :LANES] + both[:, LANES:] + _dot(u_lo, wr_ref[:, :LANES]) + br_ref[...]
    r_ref[...] = _route(logits)


def _out_call(mrg, stream, n_lat, mod, w_out, ln_g, ln_b, wr_cat, br, mod_row, n_rows, alpha):
    d = stream[0].shape[1]
    tm = MM_TM
    return pl.pallas_call(
        functools.partial(_out_kernel, alpha=alpha, n_lat_tiles=n_lat // tm),
        grid=(n_rows // tm,),
        in_specs=[pl.BlockSpec((tm, d), lambda i: (i, 0))] + _stream_specs(stream, tm, n_lat) + [
                  pl.BlockSpec((1, 6, d), lambda i: (mod_row(i, tm), 0, 0)),
                  _resident((d, d)), _resident((1, d)), _resident((1, d)),
                  _resident((d, 2 * LANES)), _resident((1, LANES))],
        out_specs=[pl.BlockSpec((tm, d), lambda i: (i, 0)),
                   pl.BlockSpec((tm, d), lambda i: (i, 0)),
                   pl.BlockSpec((tm, LANES), lambda i: (i, 0))],
        out_shape=[jax.ShapeDtypeStruct((n_rows, d), F32),
                   jax.ShapeDtypeStruct((n_rows, d), BF16),
                   jax.ShapeDtypeStruct((n_rows, LANES), F32)],
        compiler_params=_cparams(("parallel",)),
        name="out_proj_ln_route",
    )(mrg, stream[0], stream[1], mod, w_out, ln_g, ln_b, wr_cat, br)


def _rank_kernel(r_ref, rank_ref, cnt_ref, carry):
    @pl.when(pl.program_id(0) == 0)
    def _():
        carry[...] = jnp.zeros_like(carry)

    slab = r_ref[...]
    tm = slab.shape[0]
    lane = lax.broadcasted_iota(jnp.int32, slab.shape, 1)
    lane_f = lane.astype(F32)
    e1 = slab[:, 0:1]
    e2 = slab[:, 1:2]
    hit1 = lane_f == e1
    hit2 = lane_f == e2
    onehot = jnp.where(hit1 | hit2, 1.0, 0.0)
    row = lax.broadcasted_iota(jnp.int32, (tm, tm), 0)
    col = lax.broadcasted_iota(jnp.int32, (tm, tm), 1)
    lower = jnp.where(col < row, 1.0, 0.0).astype(BF16)
    before = _dot(lower, onehot.astype(BF16)) + carry[0:1, :]
    r1 = jnp.sum(jnp.where(hit1, before, 0.0), axis=1, keepdims=True)
    r2 = jnp.sum(jnp.where(hit2, before, 0.0), axis=1, keepdims=True)
    rank_ref[...] = jnp.where(lane == 0, r1, jnp.where(lane == 1, r2, 0.0))
    carry[...] = carry[...] + jnp.sum(onehot, axis=0, keepdims=True)
    cnt_ref[...] = carry[...]


def _rank_call(route):
    n = route.shape[0]
    tm = ROW_TM
    return pl.pallas_call(
        _rank_kernel,
        grid=(n // tm,),
        in_specs=[pl.BlockSpec((tm, LANES), lambda i: (i, 0))],
        out_specs=[pl.BlockSpec((tm, LANES), lambda i: (i, 0)),
                   pl.BlockSpec((8, LANES), lambda i: (0, 0))],
        out_shape=[jax.ShapeDtypeStruct((n, LANES), F32), jax.ShapeDtypeStruct((8, LANES), F32)],
        scratch_shapes=[pltpu.VMEM((8, LANES), F32)],
        compiler_params=_cparams(("arbitrary",)),
        name="expert_ranks",
    )(route)


def _dest_kernel(r_ref, rank_ref, ps_ref, o_ref):
    slab = r_ref[...]
    rank = rank_ref[...]
    lane = lax.broadcasted_iota(jnp.int32, slab.shape, 1)
    lane_f = lane.astype(F32)
    ps = ps_ref[...]
    s1 = jnp.sum(jnp.where(lane_f == slab[:, 0:1], ps, 0.0), axis=1, keepdims=True) + rank[:, 0:1]
    s2 = jnp.sum(jnp.where(lane_f == slab[:, 1:2], ps, 0.0), axis=1, keepdims=True) + rank[:, 1:2]
    o_ref[...] = jnp.where(lane == 0, s1, jnp.where(lane == 1, s2, 0.0))


def _dest_call(route, rank_slab, pstarts_row):
    n = route.shape[0]
    tm = n // 8
    return pl.pallas_call(
        _dest_kernel,
        grid=(n // tm,),
        in_specs=[pl.BlockSpec((tm, LANES), lambda i: (i, 0)),
                  pl.BlockSpec((tm, LANES), lambda i: (i, 0)),
                  _resident((1, LANES))],
        out_specs=pl.BlockSpec((tm, LANES), lambda i: (i, 0)),
        out_shape=jax.ShapeDtypeStruct((n, LANES), F32),
        compiler_params=_cparams(("parallel",)),
        name="expert_slots",
    )(route, rank_slab, pstarts_row)


def _moe_kernel(be_ref, first_ref, nxt_ref, par_ref, nu_ref, x_ref, wg_hbm, wu_hbm, wd_hbm, y_ref,
                gbuf, ubuf, dbuf, sems, *, layer):
    j = pl.program_id(0)

    def weight_copies(e, slot):
        copies = []
        for m, (hbm, buf) in enumerate(((wg_hbm, gbuf), (wu_hbm, ubuf), (wd_hbm, dbuf))):
            band = buf.shape[1] // MOE_DMA_CHUNKS
            for c in range(MOE_DMA_CHUNKS):
                rows = pl.ds(c * band, band)
                copies.append(pltpu.make_async_copy(hbm.at[layer, e, rows], buf.at[slot, rows], sems.at[slot, m]))
        return copies

    @pl.when(j < nu_ref[0])
    def _():
        slot = par_ref[j]

        @pl.when(j == 0)
        def _():
            for cp in weight_copies(be_ref[0], 0):
                cp.start()

        @pl.when(first_ref[j] == 1)
        def _():
            @pl.when(nxt_ref[j] >= 0)
            def _():
                for cp in weight_copies(nxt_ref[j], 1 - slot):
                    cp.start()

            for cp in weight_copies(be_ref[j], slot):
                cp.wait()

        xb = x_ref[...]
        hg = _dot(xb, gbuf[slot].astype(BF16))
        hu = _dot(xb, ubuf[slot].astype(BF16))
        hb = (hg * jax.nn.sigmoid(hg) * hu).astype(BF16)
        y_ref[...] = _dot(hb, dbuf[slot].astype(BF16)).astype(y_ref.dtype)

    @pl.when(j >= nu_ref[0])
    def _():
        y_ref[...] = jnp.zeros_like(y_ref)


def _moe_call(block_e, first, nxt, parity, n_used, u_sorted, w_gate, w_up, w_down, layer):
    n_slots, d = u_sorted.shape
    bm = MOE_BM
    nb = n_slots // bm
    de = w_gate.shape[-1]

    def xmap(j, be, fi, nx_, pa, nu):
        return (jnp.minimum(j, nu[0] - 1), 0)

    grid_spec = pltpu.PrefetchScalarGridSpec(
        num_scalar_prefetch=5,
        grid=(nb,),
        in_specs=[pl.BlockSpec((bm, d), xmap),
                  pl.BlockSpec(memory_space=pl.ANY),
                  pl.BlockSpec(memory_space=pl.ANY),
                  pl.BlockSpec(memory_space=pl.ANY)],
        out_specs=pl.BlockSpec((bm, d), lambda j, *_: (j, 0)),
        scratch_shapes=[pltpu.VMEM((2, d, de), F32), pltpu.VMEM((2, d, de), F32), pltpu.VMEM((2, de, d), F32),
                        pltpu.SemaphoreType.DMA((2, 3))],
    )
    return pl.pallas_call(
        functools.partial(_moe_kernel, layer=layer),
        grid_spec=grid_spec,
        out_shape=jax.ShapeDtypeStruct((n_slots, d), BF16),
        compiler_params=_cparams(("arbitrary",)),
        name="expert_mlp",
    )(block_e, first, nxt, parity, n_used, u_sorted, w_gate, w_up, w_down)


def _combine_kernel(x_ref, y0_ref, y1_ref, r_ref, mod_ref, lg_ref, lb_ref, o_ref, *, alpha):
    slab = r_ref[...]
    g1 = slab[:, 2:3]
    g2 = slab[:, 3:4]
    mx = g1 * y0_ref[...].astype(F32) + g2 * y1_ref[...].astype(F32)
    gate_f = mod_ref[0, 5:6, :]
    o_ref[...] = _layer_norm(alpha * x_ref[...] + gate_f * mx, lg_ref[...], lb_ref[...])


def _combine_call(xs, y_pair, route, mod, ln_g, ln_b, mod_row, alpha):
    n, d = xs.shape
    tm = ROW_TM
    return pl.pallas_call(
        functools.partial(_combine_kernel, alpha=alpha),
        grid=(n // tm,),
        in_specs=[pl.BlockSpec((tm, d), lambda i: (i, 0)),
                  pl.BlockSpec((tm, d), lambda i: (i, 0)),
                  pl.BlockSpec((tm, d), lambda i: (n // tm + i, 0)),
                  pl.BlockSpec((tm, LANES), lambda i: (i, 0)),
                  pl.BlockSpec((1, 6, d), lambda i: (mod_row(i, tm), 0, 0)),
                  pl.BlockSpec((1, d), lambda i: (0, 0)),
                  pl.BlockSpec((1, d), lambda i: (0, 0))],
        out_specs=pl.BlockSpec((tm, d), lambda i: (i, 0)),
        out_shape=jax.ShapeDtypeStruct((n, d), F32),
        compiler_params=_cparams(("parallel",)),
        name="moe_combine_ln",
    )(xs, y_pair, y_pair, route, mod, ln_g, ln_b)


def _rope_angles(seq, dim):
    tpos = np.arange(seq)
    row = (tpos // GRID_W).astype(np.float32)
    col = (tpos % GRID_W).astype(np.float32)
    quarter = dim // 4
    inv_freq = jnp.asarray(ROPE_THETA, F32) ** (-jnp.arange(quarter, dtype=F32) / quarter)
    ang_r = jnp.asarray(row)[:, None] * inv_freq
    ang_c = jnp.asarray(col)[:, None] * inv_freq
    return jnp.concatenate([ang_r, ang_r, ang_c, ang_c], axis=-1)


def _rot_sign(dim):
    l = np.arange(dim)
    return np.where((l & (dim // 4)) == 0, -1.0, 1.0).astype(np.float32), l ^ (dim // 4)


def _rope_tables(seq, pad_rows):
    ang_a = _rope_angles(seq, HEAD_DIM)
    sign_a, _ = _rot_sign(HEAD_DIM)
    cos_a = jnp.concatenate([jnp.cos(ang_a), jnp.ones((pad_rows, HEAD_DIM), F32)], 0)
    sin_a = jnp.concatenate([jnp.sin(ang_a) * sign_a, jnp.zeros((pad_rows, HEAD_DIM), F32)], 0)
    ang_b = _rope_angles(seq, MLA_ROPE)
    cos_b = jnp.concatenate([jnp.cos(ang_b), jnp.ones((pad_rows, MLA_ROPE), F32)], 0)
    sin_b = jnp.concatenate([jnp.sin(ang_b), jnp.zeros((pad_rows, MLA_ROPE), F32)], 0)
    return cos_a, sin_a, jnp.tile(cos_b, (1, 2)), jnp.tile(sin_b, (1, 2))


ORIG_AV, ORIG_BQ, ORIG_BKR, ORIG_CQ, ORIG_W = 1024, 1280, 2048, 2112, 10176
W_PREP_ROWS = MLA_ROPE
PREP_COPY, PREP_SCALE, PREP_ROT = 0, 1, 2


def _w_in_prep_kernel(src_ref, kind_ref, w_ref, o_ref, *, cq_scale):
    del src_ref
    kind = kind_ref[pl.program_id(1)]
    quarter = MLA_ROPE // 4

    @pl.when(kind == PREP_COPY)
    def _():
        o_ref[0] = w_ref[0].astype(BF16)

    @pl.when(kind == PREP_SCALE)
    def _():
        o_ref[0] = (w_ref[0] * cq_scale).astype(BF16)

    @pl.when(kind == PREP_ROT)
    def _():
        for r0 in range(0, MLA_ROPE, 2 * quarter):
            o_ref[0, r0:r0 + quarter] = (-w_ref[0, r0 + quarter:r0 + 2 * quarter]).astype(BF16)
            o_ref[0, r0 + quarter:r0 + 2 * quarter] = w_ref[0, r0:r0 + quarter].astype(BF16)


def _permute_w_in(w):
    depth, d, width = w.shape
    assert width == ORIG_W and ORIG_W - ORIG_CQ == PROJ_W - COL_CQ
    rb = W_PREP_ROWS
    src = np.zeros(PROJ_W // rb, np.int32)
    kind = np.full(PROJ_W // rb, PREP_COPY, np.int32)
    for dst0, src0, n in ((COL_AQ, 0, ORIG_AV), (COL_BQ, ORIG_BQ, ORIG_BKR - ORIG_BQ), (COL_BKR, ORIG_BKR, rb),
                          (COL_BKR + rb, ORIG_BKR, rb), (COL_AV, ORIG_AV, ORIG_BQ - ORIG_AV),
                          (COL_CQ, ORIG_CQ, ORIG_W - ORIG_CQ)):
        src[dst0 // rb:(dst0 + n) // rb] = np.arange(src0 // rb, (src0 + n) // rb)
    kind[(COL_BKR + rb) // rb] = PREP_ROT
    kind[COL_CQ // rb:COL_CK // rb] = PREP_SCALE
    wt = jnp.swapaxes(w, 1, 2)
    grid_spec = pltpu.PrefetchScalarGridSpec(
        num_scalar_prefetch=2,
        grid=(depth, PROJ_W // rb),
        in_specs=[pl.BlockSpec((1, rb, d), lambda l, r, src_ref, kind_ref: (l, src_ref[r], 0))],
        out_specs=pl.BlockSpec((1, rb, d), lambda l, r, src_ref, kind_ref: (l, r, 0)),
    )
    return pl.pallas_call(
        functools.partial(_w_in_prep_kernel, cq_scale=SCORE_LOG2E / math.sqrt(HEAD_DIM)),
        grid_spec=grid_spec,
        out_shape=jax.ShapeDtypeStruct((depth, PROJ_W, d), BF16),
        compiler_params=_cparams(("parallel", "parallel")),
        name="w_in_prep",
    )(jnp.asarray(src), jnp.asarray(kind), wt)


def _rot_cols(w):
    dim = w.shape[-1]
    wr = w.reshape(w.shape[:-1] + (2, 2, dim // 4))
    return jnp.stack([-wr[..., 1, :], wr[..., 0, :]], axis=-2).reshape(w.shape)


def _permute_w_uq(w):
    r = w.shape[0]
    w3 = w.reshape(r, MLA_HEADS, MLA_QK)
    nope = w3[:, :, :MLA_NOPE].reshape(r, MLA_HEADS * MLA_NOPE)
    rope = w3[:, :, MLA_NOPE:]
    rot = _rot_cols(rope)
    zpad = jnp.zeros((r, MLA_HEADS, LANES - MLA_ROPE), w.dtype)
    rope_p = jnp.concatenate([rope, zpad], -1).reshape(r, MLA_HEADS * LANES)
    rot_p = jnp.concatenate([rot, zpad], -1).reshape(r, MLA_HEADS * LANES)
    return jnp.concatenate([nope, rope_p, rot_p], axis=1).astype(BF16)


def _permute_w_ukv(w):
    r = w.shape[0]
    w3 = w.reshape(r, MLA_HEADS, MLA_NOPE + MLA_V)
    kn = w3[:, :, :MLA_NOPE].reshape(r, MLA_HEADS * MLA_NOPE)
    vv = w3[:, :, MLA_NOPE:].reshape(r, MLA_HEADS * MLA_V)
    return jnp.concatenate([kn, vv], axis=1).astype(BF16)


def _na_bias_tables(rpb, seq):
    w, kh, kw = GRID_W, NA_KH, NA_KW
    rows = seq // w
    nh, n_dr, n_dc = rpb.shape
    line = jnp.full((nh, n_dr, 2 * w), NEG_BIG, F32).at[:, :, w - kw:w - kw + n_dc].set(rpb.astype(F32) * SCORE_LOG2E)
    skew = jnp.broadcast_to(line[:, :, None, :], (nh, n_dr, w, 2 * w)).reshape(nh, n_dr, 2 * w * w)
    skew = skew[:, :, :w * (2 * w - 1)].reshape(nh, n_dr, w, 2 * w - 1)
    tc = skew[:, :, :, w - 1:2 * w - 1]
    cq = np.arange(w)[:, None]
    ck = np.arange(w)[None, :]
    cs = np.clip(cq - kw // 2, 0, w - kw)
    col_ok = (ck >= cs) & (ck < cs + kw)
    tc = jnp.where(col_ok[None, None], tc, NEG_BIG)
    tc = jnp.concatenate([tc, jnp.full((nh, 1, w, w), NEG_BIG, F32)], axis=1)
    tc2 = jnp.concatenate([tc, tc], axis=-1)
    blks = []
    tile_starts = (0, NA_ROWS, rows - NA_ROWS)
    for variant, r0 in enumerate(tile_starts):
        rb = int(np.clip(r0 - kh // 2, 0, rows - NA_BAND))
        rq = r0 + np.arange(NA_ROWS)[:, None]
        rk = rb + np.arange(NA_BAND)[None, :]
        rs = np.clip(rq - kh // 2, 0, rows - kh)
        row_ok = (rk >= rs) & (rk < rs + kh)
        band_blk = np.where(row_ok, rk - rq + kh - 1, n_dr)
        win_blk = []
        for hq, off in enumerate(_na_window_offsets(variant, len(tile_starts))):
            part = band_blk[hq * NA_ROWS // 2:(hq + 1) * NA_ROWS // 2]
            assert (np.delete(part, np.s_[off:off + NA_WIN], axis=1) == n_dr).all()
            win_blk.extend(part[:, off:off + NA_WIN].tolist())
        blks.append(win_blk)
    return pl.pallas_call(
        functools.partial(_na_bias_kernel, blks=blks),
        grid=(nh,),
        in_specs=[pl.BlockSpec((1, n_dr + 1, w, 2 * w), lambda h: (h, 0, 0, 0))],
        out_specs=pl.BlockSpec((1, len(blks), NA_ROWS * w, NA_WIN * w), lambda h: (h, 0, 0, 0)),
        out_shape=jax.ShapeDtypeStruct((nh, len(blks), NA_ROWS * w, NA_WIN * w), F32),
        compiler_params=_cparams(("parallel",)),
        name="na_bias_table",
    )(tc2)


def _na_bias_kernel(tc_ref, o_ref, *, blks):
    w = GRID_W
    low = lax.broadcasted_iota(jnp.int32, (w, 2 * w), 1) < w
    for v, blk in enumerate(blks):
        for rq, row in enumerate(blk):
            for pr in range(len(row) // 2):
                pair = jnp.where(low, tc_ref[0, row[2 * pr]], tc_ref[0, row[2 * pr + 1]])
                o_ref[0, v, rq * w:(rq + 1) * w, pr * 2 * w:(pr + 1) * 2 * w] = pair


def kernel(x, c, ctx, c_ctx, w_ada, b_ada, w_in, gqa_q_norm, gqa_k_norm, mla_q_norm, mla_kv_norm, mla_w_uq, mla_w_ukv, na_rpb, w_branch_a, w_branch_b, w_branch_c, w_out, ln1_g, ln1_b, w_router_group, b_router_group, w_router_expert, b_router_expert, w_expert_gate, w_expert_up, w_expert_down, ln2_g, ln2_b):
    bsz, seq, d = x.shape
    n_ctx = ctx.shape[1]
    depth = w_ada.shape[0]
    nx, nc = bsz * seq, bsz * n_ctx
    t = nx + nc
    assert seq % PROJ_TM == 0 and nc == PROJ_TM and seq % (NA_BAND * GRID_W) == 0 and nx % n_ctx == 0
    alpha = (2 * depth) ** 0.25

    def mod_row(i, tm):
        return jnp.minimum(i // (seq // tm), bsz)

    def rope_blk(i, tm):
        return jnp.where(i < nx // tm, i % (seq // tm), seq // tm)

    stream = (x.reshape(nx, d), ctx.reshape(nc, d), 0)
    cc = jnp.concatenate([c, c_ctx[None], jnp.zeros((8 - bsz - 1, d), F32)], axis=0)
    mod_all = _ada_call(cc, w_ada, b_ada).reshape(depth, 8, 6, d)
    cos_a, sin_a, cos_b, sin_b = _rope_tables(seq, ROW_TM)
    w_in_p = _permute_w_in(w_in)

    attn_kw = dict(bsz=bsz, seq=seq, n_ctx=n_ctx)
    for i in range(depth):
        last = i == depth - 1
        n_rows = nx if last else t
        mod = mod_all[i]
        p = _proj_call(stream, nx, t, mod, w_in_p, i, mod_row)

        qa, ka = _prep_a_call(p, cos_a, sin_a, gqa_q_norm[i][None], gqa_k_norm[i][None], rope_blk)
        a_kw = dict(n_heads=GQA_HEADS, dq=HEAD_DIM, q_col=lambda h: h, k_col=lambda h: h // GQA_GROUP,
                    v_col=lambda h: COL_AV // HEAD_DIM + h // GQA_GROUP, **attn_kw)
        o_a = _dense_attn_call(qa, ka, p, **a_kw)
        qb, kb, vb = _prep_b_call(p, cos_b, sin_b, mla_q_norm[i][None], mla_kv_norm[i][None],
                                  _permute_w_uq(mla_w_uq[i]), _permute_w_ukv(mla_w_ukv[i]), rope_blk)
        b_kw = dict(n_heads=MLA_HEADS, dq=MLA_PAD, q_col=lambda h: h, k_col=lambda h: h, v_col=lambda h: h, **attn_kw)
        o_b = _dense_attn_call(qb, kb, vb, **b_kw)
        o_c = _na_call(p, _na_bias_tables(na_rpb[i], seq), **attn_kw)
        o_lat = (o_a, o_b, o_c)
        if last:
            o_ctx = o_lat
        else:
            c_kw = dict(n_heads=NA_HEADS, dq=HEAD_DIM, q_col=lambda h: COL_CQ // HEAD_DIM + h,
                        k_col=lambda h: COL_CK // HEAD_DIM + h, v_col=lambda h: COL_CV // HEAD_DIM + h, **attn_kw)
            o_ctx = (_ctx_attn_call(qa, ka, p, **a_kw), _ctx_attn_call(qb, kb, vb, **b_kw),
                     _ctx_attn_call(p, p, p, **c_kw))

        mrg = _branch_call(o_lat, o_ctx, p, w_branch_a[i].astype(BF16), w_branch_b[i].astype(BF16),
                           w_branch_c[i].astype(BF16), n_rows)
        wr = jnp.concatenate([w_router_group[i], w_router_expert[i],
                              jnp.zeros((d, LANES - N_GROUPS - N_EXPERTS), F32)], axis=1)
        br = jnp.concatenate([b_router_group[i], b_router_expert[i],
                              jnp.zeros((LANES - N_GROUPS - N_EXPERTS,), F32)])[None]
        wr_cat = jnp.concatenate(_split_bf16(wr), axis=1)
        xs, u_f, route = _out_call(mrg, stream, nx, mod, w_out[i].astype(BF16), ln1_g[i][None], ln1_b[i][None],
                                   wr_cat, br, mod_row, n_rows, alpha)

        rank_slab, cnt = _rank_call(route)
        eid = route[:, :TOP_K].astype(jnp.int32)
        counts = cnt[0, :N_EXPERTS].astype(jnp.int32)
        padded = ((counts + MOE_BM - 1) // MOE_BM) * MOE_BM
        pends = jnp.cumsum(padded)
        pstarts = pends - padded
        pstarts_row = jnp.concatenate([pstarts.astype(F32), jnp.zeros((LANES - N_EXPERTS,), F32)])[None]
        dest = _dest_call(route, rank_slab, pstarts_row)[:, :TOP_K].astype(jnp.int32)
        nb = -(-(n_rows * TOP_K) // MOE_BM) + N_EXPERTS
        n_used = (pends[-1] // MOE_BM).astype(jnp.int32)
        blk = jnp.minimum(jnp.arange(nb, dtype=jnp.int32), n_used - 1)
        block_e = jnp.clip(jnp.searchsorted(pends, blk * MOE_BM, side="right"), 0, N_EXPERTS - 1).astype(jnp.int32)
        order = jnp.argsort(eid.reshape(-1), stable=True).astype(jnp.int32)
        starts = jnp.cumsum(counts) - counts
        slot = jnp.arange(nb * MOE_BM, dtype=jnp.int32).reshape(nb, MOE_BM)
        blk_shift = (starts - pstarts)[block_e][:, None]
        blk_limit = (pstarts + counts)[block_e][:, None]
        pos = jnp.clip(slot + blk_shift, 0, n_rows * TOP_K - 1)
        slot_tok = jnp.where(slot < blk_limit, order.at[pos].get(mode="promise_in_bounds") // TOP_K,
                             slot % n_rows).reshape(-1)
        u_sorted = u_f.at[slot_tok].get(mode="promise_in_bounds")
        jj = jnp.arange(nb, dtype=jnp.int32)
        first = ((jj == 0) | (block_e != jnp.roll(block_e, 1))).astype(jnp.int32)
        parity = (jnp.cumsum(first) - 1) % 2
        live = jnp.where(counts > 0, jnp.arange(N_EXPERTS, dtype=jnp.int32), N_EXPERTS)
        next_live = jnp.concatenate([lax.cummin(live, reverse=True)[1:], jnp.full((1,), N_EXPERTS, jnp.int32)])
        nxt = jnp.where(next_live < N_EXPERTS, next_live, -1)[block_e]
        y_sorted = _moe_call(block_e, first, nxt.astype(jnp.int32), parity.astype(jnp.int32), n_used[None],
                             u_sorted, w_expert_gate, w_expert_up, w_expert_down, i)
        y_pair = y_sorted.at[dest.T.reshape(-1)].get(mode="promise_in_bounds")
        xs = _combine_call(xs, y_pair, route, mod, ln2_g[i][None], ln2_b[i][None], mod_row, alpha)
        stream = (xs, xs, nx)
    return xs.reshape(bsz, seq, d)
```

```python
import functools
import math

import jax
import jax.numpy as jnp
import numpy as np
from jax import lax
from jax.experimental import pallas as pl
from jax.experimental.pallas import tpu as pltpu

F32 = jnp.float32
BF16 = jnp.bfloat16

HEAD_DIM = 128
GRID_W = 64
ROPE_THETA = 10000.0
GQA_HEADS, GQA_KV_HEADS = 6, 2
GQA_GROUP = GQA_HEADS // GQA_KV_HEADS
MLA_HEADS, MLA_Q_RANK, MLA_KV_RANK = 5, 512, 256
MLA_NOPE, MLA_ROPE, MLA_V = 128, 64, 128
MLA_QK = MLA_NOPE + MLA_ROPE
MLA_PAD = 256
NA_HEADS, NA_KH, NA_KW = 5, 8, 16
N_GROUPS, EXPERTS_PER_GROUP, TOP_K, D_EXPERT = 8, 8, 2, 512
N_EXPERTS = N_GROUPS * EXPERTS_PER_GROUP
LN_EPS = 1e-6
RMS_EPS = 1e-6
NEG_BIG = -1e30
SCORE_LOG2E = math.log2(math.e)

LANES = 128
VMEM_LIMIT = 56 * 1024 * 1024
PROJ_TM, PROJ_TN = 1024, 1024
ROW_TM = 512
MM_TM = 512
ATTN_TQ = 512
ATTN_SUB = 4
ATTN_KC = 1024
NA_ROWS = 8
NA_SUB = 4
NA_BAND = 16
NA_WIN = 12
MOE_BM = 256
MOE_DMA_CHUNKS = 4
ADA_TN = 512

COL_AQ, COL_AK = 0, 768
COL_BQ, COL_BKV, COL_BKR = 1024, 1536, 1792
COL_AV, COL_CQ, COL_CK, COL_CV = 1920, 2176, 2816, 3456
COL_GATE = 4096
PROJ_W = 10240


def _cparams(sem):
    return pltpu.CompilerParams(dimension_semantics=sem, vmem_limit_bytes=VMEM_LIMIT)


def _resident(shape):
    return pl.BlockSpec(shape, lambda *_: (0,) * len(shape), pipeline_mode=pl.Buffered(1))


def _dot(a, b):
    return jnp.dot(a, b, preferred_element_type=F32)


def _dot_nt(a, b):
    return lax.dot_general(a, b, (((1,), (1,)), ((), ())), preferred_element_type=F32)


def _split_bf16(a):
    hi = a.astype(BF16)
    lo = (a - hi.astype(F32)).astype(BF16)
    return hi, lo


def _ada_kernel(c_ref, w_ref, b_ref, o_ref):
    a = c_ref[...]
    a = a * jax.nn.sigmoid(a)
    a_hi, a_lo = _split_bf16(a)
    w_hi, w_lo = _split_bf16(w_ref[0])
    acc = _dot(a_hi, w_hi) + _dot(a_lo, w_hi) + _dot(a_hi, w_lo)
    o_ref[0] = acc + b_ref[0]


def _ada_call(cc, w_ada, b_ada):
    depth, d, n = w_ada.shape
    rows = cc.shape[0]
    return pl.pallas_call(
        _ada_kernel,
        grid=(depth, n // ADA_TN),
        in_specs=[pl.BlockSpec((rows, d), lambda l, j: (0, 0)),
                  pl.BlockSpec((1, d, ADA_TN), lambda l, j: (l, 0, j)),
                  pl.BlockSpec((1, 1, ADA_TN), lambda l, j: (l, 0, j))],
        out_specs=pl.BlockSpec((1, rows, ADA_TN), lambda l, j: (l, 0, j)),
        out_shape=jax.ShapeDtypeStruct((depth, rows, n), F32),
        compiler_params=_cparams(("parallel", "parallel")),
        name="ada_modulation",
    )(cc, w_ada, b_ada.reshape(depth, 1, n))


def _proj_kernel(xl_ref, xc_ref, mod_ref, w_ref, o_ref, u_scr, *, n_lat_tiles):
    @pl.when(pl.program_id(1) == 0)
    def _():
        def modulate(x):
            u_scr[...] = (x * (1.0 + mod_ref[0, 1:2, :]) + mod_ref[0, 0:1, :]).astype(BF16)

        @pl.when(pl.program_id(0) < n_lat_tiles)
        def _():
            modulate(xl_ref[...])

        @pl.when(pl.program_id(0) >= n_lat_tiles)
        def _():
            modulate(xc_ref[...])

    o_ref[...] = _dot_nt(u_scr[...], w_ref[0]).astype(BF16)


def _stream_specs(stream, tm, n_lat):
    lat, ctx, ctx_row0 = stream
    d = lat.shape[1]
    n_lat_tiles = n_lat // tm
    ctx_tile0 = ctx_row0 // tm

    def lat_map(i, *_):
        return (jnp.minimum(i, n_lat_tiles - 1), 0)

    def ctx_map(i, *_):
        return (ctx_tile0 + jnp.maximum(i - n_lat_tiles, 0), 0)

    return [pl.BlockSpec((tm, d), lat_map), pl.BlockSpec((tm, d), ctx_map, pipeline_mode=pl.Buffered(1))]


def _proj_call(stream, n_lat, n_tok, mod, w_in_p, layer, mod_row):
    d = stream[0].shape[1]
    return pl.pallas_call(
        functools.partial(_proj_kernel, n_lat_tiles=n_lat // PROJ_TM),
        grid=(n_tok // PROJ_TM, PROJ_W // PROJ_TN),
        in_specs=_stream_specs(stream, PROJ_TM, n_lat) + [
            pl.BlockSpec((1, 6, d), lambda i, j: (mod_row(i, PROJ_TM), 0, 0)),
            pl.BlockSpec((1, PROJ_TN, d), lambda i, j: (layer, j, 0))],
        out_specs=pl.BlockSpec((PROJ_TM, PROJ_TN), lambda i, j: (i, j)),
        out_shape=jax.ShapeDtypeStruct((n_tok, PROJ_W), BF16),
        scratch_shapes=[pltpu.VMEM((PROJ_TM, d), BF16)],
        compiler_params=_cparams(("parallel", "arbitrary")),
        name="input_projection",
    )(stream[0], stream[1], mod, w_in_p)


def _rope128(y, cos, sin_signed, first_quarter):
    rot = jnp.where(first_quarter, pltpu.roll(y, 96, 1), pltpu.roll(y, 32, 1))
    return y * cos + rot * sin_signed


def _prep_a_kernel(p_ref, cos_ref, sin_ref, gq_ref, gk_ref, q_ref, k_ref, *, scale):
    cos = cos_ref[...]
    sin_signed = sin_ref[...]
    lane = lax.broadcasted_iota(jnp.int32, cos.shape, 1)
    first_quarter = (lane & 32) == 0
    for h in range(GQA_HEADS + GQA_KV_HEADS):
        xh = p_ref[:, h * HEAD_DIM:(h + 1) * HEAD_DIM].astype(F32)
        r = lax.rsqrt(jnp.mean(xh * xh, axis=1, keepdims=True) + RMS_EPS)
        if h < GQA_HEADS:
            y = _rope128(xh * r * gq_ref[...], cos, sin_signed, first_quarter) * scale
            q_ref[:, h * HEAD_DIM:(h + 1) * HEAD_DIM] = y.astype(BF16)
        else:
            hk = h - GQA_HEADS
            y = _rope128(xh * r * gk_ref[...], cos, sin_signed, first_quarter)
            k_ref[:, hk * HEAD_DIM:(hk + 1) * HEAD_DIM] = y.astype(BF16)


def _prep_a_call(p, cos_t, sin_t, gq, gk, rope_blk):
    t = p.shape[0]
    tm = ROW_TM
    wq, wk = GQA_HEADS * HEAD_DIM, GQA_KV_HEADS * HEAD_DIM
    return pl.pallas_call(
        functools.partial(_prep_a_kernel, scale=SCORE_LOG2E / math.sqrt(HEAD_DIM)),
        grid=(t // tm,),
        in_specs=[pl.BlockSpec((tm, wq + wk), lambda i: (i, 0)),
                  pl.BlockSpec((tm, HEAD_DIM), lambda i: (rope_blk(i, tm), 0)),
                  pl.BlockSpec((tm, HEAD_DIM), lambda i: (rope_blk(i, tm), 0)),
                  pl.BlockSpec((1, HEAD_DIM), lambda i: (0, 0)),
                  pl.BlockSpec((1, HEAD_DIM), lambda i: (0, 0))],
        out_specs=[pl.BlockSpec((tm, wq), lambda i: (i, 0)),
                   pl.BlockSpec((tm, wk), lambda i: (i, 0))],
        out_shape=[jax.ShapeDtypeStruct((t, wq), BF16), jax.ShapeDtypeStruct((t, wk), BF16)],
        compiler_params=_cparams(("parallel",)),
        name="gqa_qk_prep",
    )(p, cos_t, sin_t, gq, gk)


def _prep_b_kernel(ql_ref, kvl_ref, kr_ref, cos_ref, sin_ref, gq_ref, gkv_ref, wq_ref, wkv_ref,
                   q_ref, k_ref, v_ref, *, scale):
    nh, hp = MLA_HEADS, MLA_PAD
    wn = nh * MLA_NOPE
    cos = cos_ref[...]
    sin = sin_ref[...]
    ql = ql_ref[...].astype(F32)
    ql = ql * lax.rsqrt(jnp.mean(ql * ql, axis=1, keepdims=True) + RMS_EPS) * gq_ref[...]
    qf = _dot(ql.astype(BF16), wq_ref[...])
    kvl = kvl_ref[...].astype(F32)
    kvl = kvl * lax.rsqrt(jnp.mean(kvl * kvl, axis=1, keepdims=True) + RMS_EPS) * gkv_ref[...]
    kvf = _dot(kvl.astype(BF16), wkv_ref[...])
    lane = lax.broadcasted_iota(jnp.int32, cos.shape, 1)
    low = lane < MLA_ROPE
    tk = kr_ref[...].astype(F32) * jnp.where(low, cos, sin)
    kr = jnp.where(low, tk + pltpu.roll(tk, MLA_ROPE, 1), 0.0).astype(BF16)
    for h in range(nh):
        q_ref[:, h * hp:h * hp + LANES] = (qf[:, h * LANES:(h + 1) * LANES] * scale).astype(BF16)
        qr = qf[:, wn + h * LANES:wn + (h + 1) * LANES] * cos + qf[:, 2 * wn + h * LANES:2 * wn + (h + 1) * LANES] * sin
        q_ref[:, h * hp + LANES:(h + 1) * hp] = (qr * scale).astype(BF16)
        k_ref[:, h * hp:h * hp + LANES] = kvf[:, h * LANES:(h + 1) * LANES].astype(BF16)
        k_ref[:, h * hp + LANES:(h + 1) * hp] = kr
    v_ref[...] = kvf[:, wn:].astype(BF16)


def _prep_b_call(p, cos_t, sin_t, gq, gkv, wq, wkv, rope_blk):
    t = p.shape[0]
    tm = ROW_TM
    nh = MLA_HEADS
    return pl.pallas_call(
        functools.partial(_prep_b_kernel, scale=SCORE_LOG2E / math.sqrt(MLA_QK)),
        grid=(t // tm,),
        in_specs=[pl.BlockSpec((tm, MLA_Q_RANK), lambda i: (i, COL_BQ // MLA_Q_RANK)),
                  pl.BlockSpec((tm, MLA_KV_RANK), lambda i: (i, COL_BKV // MLA_KV_RANK)),
                  pl.BlockSpec((tm, LANES), lambda i: (i, COL_BKR // LANES)),
                  pl.BlockSpec((tm, LANES), lambda i: (rope_blk(i, tm), 0)),
                  pl.BlockSpec((tm, LANES), lambda i: (rope_blk(i, tm), 0)),
                  pl.BlockSpec((1, MLA_Q_RANK), lambda i: (0, 0)),
                  pl.BlockSpec((1, MLA_KV_RANK), lambda i: (0, 0)),
                  _resident(wq.shape), _resident(wkv.shape)],
        out_specs=[pl.BlockSpec((tm, nh * MLA_PAD), lambda i: (i, 0)),
                   pl.BlockSpec((tm, nh * MLA_PAD), lambda i: (i, 0)),
                   pl.BlockSpec((tm, nh * MLA_V), lambda i: (i, 0))],
        out_shape=[jax.ShapeDtypeStruct((t, nh * MLA_PAD), BF16),
                   jax.ShapeDtypeStruct((t, nh * MLA_PAD), BF16),
                   jax.ShapeDtypeStruct((t, nh * MLA_V), BF16)],
        compiler_params=_cparams(("parallel",)),
        name="mla_prep",
    )(p, p, p, cos_t, sin_t, gq, gkv, wq, wkv)


def _online_attend(q, chunks):
    m = z = acc = None
    for k, v, bias in chunks:
        s = _dot_nt(q, k)
        if bias is not None:
            s = s + bias
        cm = jnp.max(s, axis=1, keepdims=True)
        if m is None:
            m = cm
            p = jnp.exp2(s - m)
            z = jnp.sum(p, axis=1, keepdims=True)
            acc = _dot(p.astype(BF16), v)
        else:
            m_new = jnp.maximum(m, cm)
            corr = jnp.exp2(m - m_new)
            p = jnp.exp2(s - m_new)
            z = z * corr + jnp.sum(p, axis=1, keepdims=True)
            acc = acc * corr + _dot(p.astype(BF16), v)
            m = m_new
    return acc / z


def _online_attend_ones(q, chunks, dv):
    m = acc = None
    for k, v_ones, bias in chunks:
        s = _dot_nt(q, k)
        if bias is not None:
            s = s + bias
        cm = jnp.max(s, axis=1, keepdims=True)
        if m is None:
            m = cm
            acc = _dot(jnp.exp2((s - m).astype(BF16)), v_ones)
        else:
            m_new = jnp.maximum(m, cm)
            corr = jnp.exp2(m - m_new)
            acc = acc * corr + _dot(jnp.exp2((s - m_new).astype(BF16)), v_ones)
            m = m_new
    return acc[:, :dv] / acc[:, dv:dv + 1]


def _dense_attn_kernel(q_ref, kl_ref, vl_ref, kc_ref, vc_ref, o_ref, v_ones):
    n_lat = kl_ref.shape[0]
    dv = vl_ref.shape[1]

    @pl.when(pl.program_id(2) == 0)
    def _():
        v_ones[0:n_lat, 0:dv] = vl_ref[...]
        v_ones[n_lat:, 0:dv] = vc_ref[...]
        v_ones[:, dv:] = jnp.ones((v_ones.shape[0], dv), BF16)

    chunks = [(kl_ref[c:c + ATTN_KC, :], v_ones[c:c + ATTN_KC, :], None) for c in range(0, n_lat, ATTN_KC)]
    chunks.append((kc_ref[...], v_ones[n_lat:, :], None))
    for r in range(0, q_ref.shape[0], ATTN_TQ):
        o_ref[r:r + ATTN_TQ, :] = _online_attend_ones(q_ref[r:r + ATTN_TQ, :], chunks, dv).astype(o_ref.dtype)


def _dense_attn_call(q, k, v, *, n_heads, dq, q_col, k_col, v_col, bsz, seq, n_ctx):
    tq = ATTN_TQ * ATTN_SUB
    tpb = seq // tq
    ctx0 = bsz * seq // n_ctx
    dv = HEAD_DIM
    return pl.pallas_call(
        _dense_attn_kernel,
        grid=(bsz, n_heads, tpb),
        in_specs=[pl.BlockSpec((tq, dq), lambda b, h, i: (b * tpb + i, q_col(h))),
                  pl.BlockSpec((seq, dq), lambda b, h, i: (b, k_col(h))),
                  pl.BlockSpec((seq, dv), lambda b, h, i: (b, v_col(h))),
                  pl.BlockSpec((n_ctx, dq), lambda b, h, i: (ctx0 + b, k_col(h))),
                  pl.BlockSpec((n_ctx, dv), lambda b, h, i: (ctx0 + b, v_col(h)))],
        out_specs=pl.BlockSpec((tq, dv), lambda b, h, i: (b * tpb + i, h)),
        out_shape=jax.ShapeDtypeStruct((bsz * seq, n_heads * dv), BF16),
        scratch_shapes=[pltpu.VMEM((seq + n_ctx, 2 * dv), BF16)],
        compiler_params=_cparams(("parallel", "parallel", "arbitrary")),
        name="dense_attention",
    )(q, k, v, k, v)


def _ctx_attn_kernel(q_ref, kc_ref, vc_ref, o_ref):
    o_ref[...] = _online_attend(q_ref[...], [(kc_ref[...], vc_ref[...], None)]).astype(o_ref.dtype)


def _ctx_attn_call(q, k, v, *, n_heads, dq, q_col, k_col, v_col, bsz, seq, n_ctx):
    ctx0 = bsz * seq // n_ctx
    dv = HEAD_DIM
    return pl.pallas_call(
        _ctx_attn_kernel,
        grid=(bsz, n_heads),
        in_specs=[pl.BlockSpec((n_ctx, dq), lambda b, h: (ctx0 + b, q_col(h))),
                  pl.BlockSpec((n_ctx, dq), lambda b, h: (ctx0 + b, k_col(h))),
                  pl.BlockSpec((n_ctx, dv), lambda b, h: (ctx0 + b, v_col(h)))],
        out_specs=pl.BlockSpec((n_ctx, dv), lambda b, h: (b, h)),
        out_shape=jax.ShapeDtypeStruct((bsz * n_ctx, n_heads * dv), BF16),
        compiler_params=_cparams(("parallel", "parallel")),
        name="context_attention",
    )(q, k, v)


def _na_window_offsets(variant, n_variants):
    shift = NA_BAND - NA_WIN
    return ((0, 0), (0, shift), (shift, shift))[0 if variant == 0 else (2 if variant == n_variants - 1 else 1)]


def _na_kernel(q_ref, k_ref, v_ref, kc_ref, vc_ref, *rest, rows_total, n_tiles):
    bias_refs, o_ref, v_ones = rest[:NA_SUB], rest[NA_SUB], rest[NA_SUB + 1]
    n_lat = k_ref.shape[0]
    dv = v_ref.shape[1]
    tile = NA_ROWS * GRID_W
    half = tile // 2
    win = NA_WIN * GRID_W
    shift = (NA_BAND - NA_WIN) * GRID_W

    @pl.when(pl.program_id(2) == 0)
    def _():
        v_ones[0:n_lat, 0:dv] = v_ref[...]
        v_ones[n_lat:, 0:dv] = vc_ref[...]
        v_ones[:, dv:] = jnp.ones((v_ones.shape[0], dv), BF16)

    ctx = (kc_ref[...], v_ones[n_lat:, :], None)
    for t in range(NA_SUB):
        i = pl.program_id(2) * NA_SUB + t
        start_blk = jnp.clip(2 * i - 1, 0, (rows_total - NA_BAND) // 4)
        band0 = start_blk * (4 * GRID_W)
        starts = (band0 + jnp.where(i == n_tiles - 1, shift, 0), band0 + jnp.where(i == 0, 0, shift))
        for hq, start in enumerate(starts):
            start = pl.multiple_of(start, 4 * GRID_W)
            rows = slice(t * tile + hq * half, t * tile + (hq + 1) * half)
            bias = bias_refs[t][0, 0, hq * half:(hq + 1) * half, :]
            chunks = [(k_ref[pl.ds(start, win), :], v_ones[pl.ds(start, win), :], bias), ctx]
            o_ref[rows, :] = _online_attend_ones(q_ref[rows, :], chunks, dv).astype(o_ref.dtype)


def _na_call(p, bias_tab, *, bsz, seq, n_ctx):
    tile = NA_ROWS * GRID_W
    n_tiles = seq // tile
    tq = tile * NA_SUB
    tpb = seq // tq
    rows_total = seq // GRID_W
    ctx0 = bsz * seq // n_ctx
    d = HEAD_DIM
    qc, kc, vc = COL_CQ // d, COL_CK // d, COL_CV // d

    def bias_spec(t):
        def index(b, h, i):
            tile_idx = i * NA_SUB + t
            return (h, jnp.where(tile_idx == 0, 0, jnp.where(tile_idx == n_tiles - 1, 2, 1)), 0, 0)
        return pl.BlockSpec((1, 1, tile, NA_WIN * GRID_W), index)

    return pl.pallas_call(
        functools.partial(_na_kernel, rows_total=rows_total, n_tiles=n_tiles),
        grid=(bsz, NA_HEADS, tpb),
        in_specs=[pl.BlockSpec((tq, d), lambda b, h, i: (b * tpb + i, qc + h)),
                  pl.BlockSpec((seq, d), lambda b, h, i: (b, kc + h)),
                  pl.BlockSpec((seq, d), lambda b, h, i: (b, vc + h)),
                  pl.BlockSpec((n_ctx, d), lambda b, h, i: (ctx0 + b, kc + h)),
                  pl.BlockSpec((n_ctx, d), lambda b, h, i: (ctx0 + b, vc + h))]
                 + [bias_spec(t) for t in range(NA_SUB)],
        out_specs=pl.BlockSpec((tq, d), lambda b, h, i: (b * tpb + i, h)),
        out_shape=jax.ShapeDtypeStruct((bsz * seq, NA_HEADS * d), BF16),
        scratch_shapes=[pltpu.VMEM((seq + n_ctx, 2 * d), BF16)],
        compiler_params=_cparams(("parallel", "parallel", "arbitrary")),
        name="neighbourhood_attention",
    )(p, p, p, p, p, *([bias_tab] * NA_SUB))


def _branch_kernel(oa_ref, ob_ref, oc_ref, oac_ref, obc_ref, occ_ref, ga_ref, gb_ref, gc_ref,
                   wa_ref, wb_ref, wc_ref, o_ref, *, n_lat_tiles):
    def gated(g_ref, w_ref, o_tile):
        return jax.nn.sigmoid(g_ref[...].astype(F32)) * _dot(o_tile, w_ref[...])

    @pl.when(pl.program_id(0) < n_lat_tiles)
    def _():
        acc = gated(ga_ref, wa_ref, oa_ref[...]) + gated(gb_ref, wb_ref, ob_ref[...]) + gated(gc_ref, wc_ref, oc_ref[...])
        o_ref[...] = acc.astype(BF16)

    @pl.when(pl.program_id(0) >= n_lat_tiles)
    def _():
        acc = gated(ga_ref, wa_ref, oac_ref[...]) + gated(gb_ref, wb_ref, obc_ref[...]) + gated(gc_ref, wc_ref, occ_ref[...])
        o_ref[...] = acc.astype(BF16)


def _branch_call(o_lat, o_ctx, p, wa, wb, wc, n_rows):
    d = wa.shape[1]
    tm = MM_TM
    g0 = COL_GATE // d
    n_lat_tiles = o_lat[0].shape[0] // tm
    lat_specs = [pl.BlockSpec((tm, o.shape[1]), lambda i: (jnp.minimum(i, n_lat_tiles - 1), 0)) for o in o_lat]
    ctx_specs = [pl.BlockSpec((tm, o.shape[1]), lambda i: (jnp.maximum(i - n_lat_tiles, 0), 0)) for o in o_ctx]
    return pl.pallas_call(
        functools.partial(_branch_kernel, n_lat_tiles=n_lat_tiles),
        grid=(n_rows // tm,),
        in_specs=lat_specs + ctx_specs + [
            pl.BlockSpec((tm, d), lambda i: (i, g0)),
            pl.BlockSpec((tm, d), lambda i: (i, g0 + 1)),
            pl.BlockSpec((tm, d), lambda i: (i, g0 + 2)),
            _resident(wa.shape), _resident(wb.shape), _resident(wc.shape)],
        out_specs=pl.BlockSpec((tm, d), lambda i: (i, 0)),
        out_shape=jax.ShapeDtypeStruct((n_rows, d), BF16),
        compiler_params=_cparams(("parallel",)),
        name="branch_merge",
    )(*o_lat, *o_ctx, p, p, p, wa, wb, wc)


def _layer_norm(z, g, b):
    mu = jnp.mean(z, axis=1, keepdims=True)
    zc = z - mu
    var = jnp.mean(zc * zc, axis=1, keepdims=True)
    return zc * lax.rsqrt(var + LN_EPS) * g + b


def _route(logits):
    lane = lax.broadcasted_iota(jnp.int32, logits.shape, 1)
    lane_f = lane.astype(F32)
    is_g = lane < N_GROUPS
    gl = jnp.where(is_g, logits, NEG_BIG)
    mg = jnp.max(gl, axis=1, keepdims=True)
    gsel = jnp.min(jnp.where(gl == mg, lane_f, float(LANES)), axis=1, keepdims=True)
    zg = jnp.sum(jnp.where(is_g, jnp.exp(gl - mg), 0.0), axis=1, keepdims=True)
    lo = N_GROUPS + gsel * EXPERTS_PER_GROUP
    is_e = (lane_f >= lo) & (lane_f < lo + EXPERTS_PER_GROUP)
    el = jnp.where(is_e, logits, NEG_BIG)
    t1 = jnp.max(el, axis=1, keepdims=True)
    i1 = jnp.min(jnp.where(el == t1, lane_f, float(LANES)), axis=1, keepdims=True)
    el2 = jnp.where(lane_f == i1, NEG_BIG, el)
    t2 = jnp.max(el2, axis=1, keepdims=True)
    i2 = jnp.min(jnp.where(el2 == t2, lane_f, float(LANES)), axis=1, keepdims=True)
    dd = jnp.exp(t2 - t1)
    g1 = 1.0 / (zg * (1.0 + dd))
    g2 = g1 * dd
    slab = jnp.where(lane == 0, i1 - N_GROUPS,
                     jnp.where(lane == 1, i2 - N_GROUPS,
                               jnp.where(lane == 2, g1, jnp.where(lane == 3, g2, 0.0))))
    return slab


def _out_kernel(m_ref, xl_ref, xc_ref, mod_ref, w_ref, lg_ref, lb_ref, wr_ref, br_ref,
                xo_ref, u_ref, r_ref, *, alpha, n_lat_tiles):
    @pl.when(pl.program_id(0) < n_lat_tiles)
    def _():
        xo_ref[...] = xl_ref[...]

    @pl.when(pl.program_id(0) >= n_lat_tiles)
    def _():
        xo_ref[...] = xc_ref[...]

    y = _dot(m_ref[...], w_ref[...])
    gate_m = mod_ref[0, 2:3, :]
    shift_f = mod_ref[0, 3:4, :]
    scale_f = mod_ref[0, 4:5, :]
    xn = _layer_norm(alpha * xo_ref[...] + gate_m * y, lg_ref[...], lb_ref[...])
    xo_ref[...] = xn
    u = xn * (1.0 + scale_f) + shift_f
    u_hi, u_lo = _split_bf16(u)
    u_ref[...] = u_hi
    both = _dot(u_hi, wr_ref[...])
    logits = both[:, :LANES] + both[:, LANES:] + _dot(u_lo, wr_ref[:, :LANES]) + br_ref[...]
    r_ref[...] = _route(logits)


def _out_call(mrg, stream, n_lat, mod, w_out, ln_g, ln_b, wr_cat, br, mod_row, n_rows, alpha):
    d = stream[0].shape[1]
    tm = MM_TM
    return pl.pallas_call(
        functools.partial(_out_kernel, alpha=alpha, n_lat_tiles=n_lat // tm),
        grid=(n_rows // tm,),
        in_specs=[pl.BlockSpec((tm, d), lambda i: (i, 0))] + _stream_specs(stream, tm, n_lat) + [
                  pl.BlockSpec((1, 6, d), lambda i: (mod_row(i, tm), 0, 0)),
                  _resident((d, d)), _resident((1, d)), _resident((1, d)),
                  _resident((d, 2 * LANES)), _resident((1, LANES))],
        out_specs=[pl.BlockSpec((tm, d), lambda i: (i, 0)),
                   pl.BlockSpec((tm, d), lambda i: (i, 0)),
                   pl.BlockSpec((tm, LANES), lambda i: (i, 0))],
        out_shape=[jax.ShapeDtypeStruct((n_rows, d), F32),
                   jax.ShapeDtypeStruct((n_rows, d), BF16),
                   jax.ShapeDtypeStruct((n_rows, LANES), F32)],
        compiler_params=_cparams(("parallel",)),
        name="out_proj_ln_route",
    )(mrg, stream[0], stream[1], mod, w_out, ln_g, ln_b, wr_cat, br)


def _rank_kernel(r_ref, rank_ref, cnt_ref, carry):
    @pl.when(pl.program_id(0) == 0)
    def _():
        carry[...] = jnp.zeros_like(carry)

    slab = r_ref[...]
    tm = slab.shape[0]
    lane = lax.broadcasted_iota(jnp.int32, slab.shape, 1)
    lane_f = lane.astype(F32)
    e1 = slab[:, 0:1]
    e2 = slab[:, 1:2]
    hit1 = lane_f == e1
    hit2 = lane_f == e2
    onehot = jnp.where(hit1 | hit2, 1.0, 0.0)
    row = lax.broadcasted_iota(jnp.int32, (tm, tm), 0)
    col = lax.broadcasted_iota(jnp.int32, (tm, tm), 1)
    lower = jnp.where(col < row, 1.0, 0.0).astype(BF16)
    before = _dot(lower, onehot.astype(BF16)) + carry[0:1, :]
    r1 = jnp.sum(jnp.where(hit1, before, 0.0), axis=1, keepdims=True)
    r2 = jnp.sum(jnp.where(hit2, before, 0.0), axis=1, keepdims=True)
    rank_ref[...] = jnp.where(lane == 0, r1, jnp.where(lane == 1, r2, 0.0))
    carry[...] = carry[...] + jnp.sum(onehot, axis=0, keepdims=True)
    cnt_ref[...] = carry[...]


def _rank_call(route):
    n = route.shape[0]
    tm = ROW_TM
    return pl.pallas_call(
        _rank_kernel,
        grid=(n // tm,),
        in_specs=[pl.BlockSpec((tm, LANES), lambda i: (i, 0))],
        out_specs=[pl.BlockSpec((tm, LANES), lambda i: (i, 0)),
                   pl.BlockSpec((8, LANES), lambda i: (0, 0))],
        out_shape=[jax.ShapeDtypeStruct((n, LANES), F32), jax.ShapeDtypeStruct((8, LANES), F32)],
        scratch_shapes=[pltpu.VMEM((8, LANES), F32)],
        compiler_params=_cparams(("arbitrary",)),
        name="expert_ranks",
    )(route)


def _dest_kernel(r_ref, rank_ref, ps_ref, o_ref):
    slab = r_ref[...]
    rank = rank_ref[...]
    lane = lax.broadcasted_iota(jnp.int32, slab.shape, 1)
    lane_f = lane.astype(F32)
    ps = ps_ref[...]
    s1 = jnp.sum(jnp.where(lane_f == slab[:, 0:1], ps, 0.0), axis=1, keepdims=True) + rank[:, 0:1]
    s2 = jnp.sum(jnp.where(lane_f == slab[:, 1:2], ps, 0.0), axis=1, keepdims=True) + rank[:, 1:2]
    o_ref[...] = jnp.where(lane == 0, s1, jnp.where(lane == 1, s2, 0.0))


def _dest_call(route, rank_slab, pstarts_row):
    n = route.shape[0]
    tm = n // 8
    return pl.pallas_call(
        _dest_kernel,
        grid=(n // tm,),
        in_specs=[pl.BlockSpec((tm, LANES), lambda i: (i, 0)),
                  pl.BlockSpec((tm, LANES), lambda i: (i, 0)),
                  _resident((1, LANES))],
        out_specs=pl.BlockSpec((tm, LANES), lambda i: (i, 0)),
        out_shape=jax.ShapeDtypeStruct((n, LANES), F32),
        compiler_params=_cparams(("parallel",)),
        name="expert_slots",
    )(route, rank_slab, pstarts_row)


def _moe_kernel(be_ref, first_ref, nxt_ref, par_ref, nu_ref, x_ref, wg_hbm, wu_hbm, wd_hbm, y_ref,
                gbuf, ubuf, dbuf, sems, *, layer):
    j = pl.program_id(0)

    def weight_copies(e, slot):
        copies = []
        for m, (hbm, buf) in enumerate(((wg_hbm, gbuf), (wu_hbm, ubuf), (wd_hbm, dbuf))):
            band = buf.shape[1] // MOE_DMA_CHUNKS
            for c in range(MOE_DMA_CHUNKS):
                rows = pl.ds(c * band, band)
                copies.append(pltpu.make_async_copy(hbm.at[layer, e, rows], buf.at[slot, rows], sems.at[slot, m]))
        return copies

    @pl.when(j < nu_ref[0])
    def _():
        slot = par_ref[j]

        @pl.when(j == 0)
        def _():
            for cp in weight_copies(be_ref[0], 0):
                cp.start()

        @pl.when(first_ref[j] == 1)
        def _():
            @pl.when(nxt_ref[j] >= 0)
            def _():
                for cp in weight_copies(nxt_ref[j], 1 - slot):
                    cp.start()

            for cp in weight_copies(be_ref[j], slot):
                cp.wait()

        xb = x_ref[...]
        hg = _dot(xb, gbuf[slot].astype(BF16))
        hu = _dot(xb, ubuf[slot].astype(BF16))
        hb = (hg * jax.nn.sigmoid(hg) * hu).astype(BF16)
        y_ref[...] = _dot(hb, dbuf[slot].astype(BF16)).astype(y_ref.dtype)

    @pl.when(j >= nu_ref[0])
    def _():
        y_ref[...] = jnp.zeros_like(y_ref)


def _moe_call(block_e, first, nxt, parity, n_used, u_sorted, w_gate, w_up, w_down, layer):
    n_slots, d = u_sorted.shape
    bm = MOE_BM
    nb = n_slots // bm
    de = w_gate.shape[-1]

    def xmap(j, be, fi, nx_, pa, nu):
        return (jnp.minimum(j, nu[0] - 1), 0)

    grid_spec = pltpu.PrefetchScalarGridSpec(
        num_scalar_prefetch=5,
        grid=(nb,),
        in_specs=[pl.BlockSpec((bm, d), xmap),
                  pl.BlockSpec(memory_space=pl.ANY),
                  pl.BlockSpec(memory_space=pl.ANY),
                  pl.BlockSpec(memory_space=pl.ANY)],
        out_specs=pl.BlockSpec((bm, d), lambda j, *_: (j, 0)),
        scratch_shapes=[pltpu.VMEM((2, d, de), F32), pltpu.VMEM((2, d, de), F32), pltpu.VMEM((2, de, d), F32),
                        pltpu.SemaphoreType.DMA((2, 3))],
    )
    return pl.pallas_call(
        functools.partial(_moe_kernel, layer=layer),
        grid_spec=grid_spec,
        out_shape=jax.ShapeDtypeStruct((n_slots, d), BF16),
        compiler_params=_cparams(("arbitrary",)),
        name="expert_mlp",
    )(block_e, first, nxt, parity, n_used, u_sorted, w_gate, w_up, w_down)


def _combine_kernel(x_ref, y0_ref, y1_ref, r_ref, mod_ref, lg_ref, lb_ref, o_ref, *, alpha):
    slab = r_ref[...]
    g1 = slab[:, 2:3]
    g2 = slab[:, 3:4]
    mx = g1 * y0_ref[...].astype(F32) + g2 * y1_ref[...].astype(F32)
    gate_f = mod_ref[0, 5:6, :]
    o_ref[...] = _layer_norm(alpha * x_ref[...] + gate_f * mx, lg_ref[...], lb_ref[...])


def _combine_call(xs, y_pair, route, mod, ln_g, ln_b, mod_row, alpha):
    n, d = xs.shape
    tm = ROW_TM
    return pl.pallas_call(
        functools.partial(_combine_kernel, alpha=alpha),
        grid=(n // tm,),
        in_specs=[pl.BlockSpec((tm, d), lambda i: (i, 0)),
                  pl.BlockSpec((tm, d), lambda i: (i, 0)),
                  pl.BlockSpec((tm, d), lambda i: (n // tm + i, 0)),
                  pl.BlockSpec((tm, LANES), lambda i: (i, 0)),
                  pl.BlockSpec((1, 6, d), lambda i: (mod_row(i, tm), 0, 0)),
                  pl.BlockSpec((1, d), lambda i: (0, 0)),
                  pl.BlockSpec((1, d), lambda i: (0, 0))],
        out_specs=pl.BlockSpec((tm, d), lambda i: (i, 0)),
        out_shape=jax.ShapeDtypeStruct((n, d), F32),
        compiler_params=_cparams(("parallel",)),
        name="moe_combine_ln",
    )(xs, y_pair, y_pair, route, mod, ln_g, ln_b)


def _rope_angles(seq, dim):
    tpos = np.arange(seq)
    row = (tpos // GRID_W).astype(np.float32)
    col = (tpos % GRID_W).astype(np.float32)
    quarter = dim // 4
    inv_freq = jnp.asarray(ROPE_THETA, F32) ** (-jnp.arange(quarter, dtype=F32) / quarter)
    ang_r = jnp.asarray(row)[:, None] * inv_freq
    ang_c = jnp.asarray(col)[:, None] * inv_freq
    return jnp.concatenate([ang_r, ang_r, ang_c, ang_c], axis=-1)


def _rot_sign(dim):
    l = np.arange(dim)
    return np.where((l & (dim // 4)) == 0, -1.0, 1.0).astype(np.float32), l ^ (dim // 4)


def _rope_tables(seq, pad_rows):
    ang_a = _rope_angles(seq, HEAD_DIM)
    sign_a, _ = _rot_sign(HEAD_DIM)
    cos_a = jnp.concatenate([jnp.cos(ang_a), jnp.ones((pad_rows, HEAD_DIM), F32)], 0)
    sin_a = jnp.concatenate([jnp.sin(ang_a) * sign_a, jnp.zeros((pad_rows, HEAD_DIM), F32)], 0)
    ang_b = _rope_angles(seq, MLA_ROPE)
    cos_b = jnp.concatenate([jnp.cos(ang_b), jnp.ones((pad_rows, MLA_ROPE), F32)], 0)
    sin_b = jnp.concatenate([jnp.sin(ang_b), jnp.zeros((pad_rows, MLA_ROPE), F32)], 0)
    return cos_a, sin_a, jnp.tile(cos_b, (1, 2)), jnp.tile(sin_b, (1, 2))


ORIG_AV, ORIG_BQ, ORIG_BKR, ORIG_CQ, ORIG_W = 1024, 1280, 2048, 2112, 10176
W_PREP_ROWS = MLA_ROPE
W_PREP_GROUP = 8
PREP_COPY, PREP_SCALE, PREP_ROT = 0, 1, 2


def _w_in_prep_kernel(src_ref, kind_ref, *refs, cq_scale):
    del src_ref
    w_refs, o_ref = refs[:W_PREP_GROUP], refs[W_PREP_GROUP]
    quarter = MLA_ROPE // 4
    rb = W_PREP_ROWS
    for k, w_ref in enumerate(w_refs):
        kind = kind_ref[pl.program_id(1) * W_PREP_GROUP + k]
        lo = k * rb

        @pl.when(kind == PREP_COPY)
        def _(w_ref=w_ref, lo=lo):
            o_ref[0, lo:lo + rb] = w_ref[0].astype(BF16)

        @pl.when(kind == PREP_SCALE)
        def _(w_ref=w_ref, lo=lo):
            o_ref[0, lo:lo + rb] = (w_ref[0] * cq_scale).astype(BF16)

        @pl.when(kind == PREP_ROT)
        def _(w_ref=w_ref, lo=lo):
            for r0 in range(0, rb, 2 * quarter):
                o_ref[0, lo + r0:lo + r0 + quarter] = (-w_ref[0, r0 + quarter:r0 + 2 * quarter]).astype(BF16)
                o_ref[0, lo + r0 + quarter:lo + r0 + 2 * quarter] = w_ref[0, r0:r0 + quarter].astype(BF16)


def _permute_w_in(w):
    depth, d, width = w.shape
    assert width == ORIG_W and ORIG_W - ORIG_CQ == PROJ_W - COL_CQ
    rb = W_PREP_ROWS
    src = np.zeros(PROJ_W // rb, np.int32)
    kind = np.full(PROJ_W // rb, PREP_COPY, np.int32)
    for dst0, src0, n in ((COL_AQ, 0, ORIG_AV), (COL_BQ, ORIG_BQ, ORIG_BKR - ORIG_BQ), (COL_BKR, ORIG_BKR, rb),
                          (COL_BKR + rb, ORIG_BKR, rb), (COL_AV, ORIG_AV, ORIG_BQ - ORIG_AV),
                          (COL_CQ, ORIG_CQ, ORIG_W - ORIG_CQ)):
        src[dst0 // rb:(dst0 + n) // rb] = np.arange(src0 // rb, (src0 + n) // rb)
    kind[(COL_BKR + rb) // rb] = PREP_ROT
    kind[COL_CQ // rb:COL_CK // rb] = PREP_SCALE
    wt = jnp.swapaxes(w, 1, 2)
    grp = W_PREP_GROUP

    def src_spec(k):
        return pl.BlockSpec((1, rb, d), lambda l, r, src_ref, kind_ref: (l, src_ref[r * grp + k], 0))

    grid_spec = pltpu.PrefetchScalarGridSpec(
        num_scalar_prefetch=2,
        grid=(depth, PROJ_W // (rb * grp)),
        in_specs=[src_spec(k) for k in range(grp)],
        out_specs=pl.BlockSpec((1, rb * grp, d), lambda l, r, src_ref, kind_ref: (l, r, 0)),
    )
    return pl.pallas_call(
        functools.partial(_w_in_prep_kernel, cq_scale=SCORE_LOG2E / math.sqrt(HEAD_DIM)),
        grid_spec=grid_spec,
        out_shape=jax.ShapeDtypeStruct((depth, PROJ_W, d), BF16),
        compiler_params=_cparams(("parallel", "parallel")),
        name="w_in_prep",
    )(jnp.asarray(src), jnp.asarray(kind), *([wt] * grp))


def _rot_cols(w):
    dim = w.shape[-1]
    wr = w.reshape(w.shape[:-1] + (2, 2, dim // 4))
    return jnp.stack([-wr[..., 1, :], wr[..., 0, :]], axis=-2).reshape(w.shape)


def _permute_w_uq(w):
    r = w.shape[0]
    w3 = w.reshape(r, MLA_HEADS, MLA_QK)
    nope = w3[:, :, :MLA_NOPE].reshape(r, MLA_HEADS * MLA_NOPE)
    rope = w3[:, :, MLA_NOPE:]
    rot = _rot_cols(rope)
    zpad = jnp.zeros((r, MLA_HEADS, LANES - MLA_ROPE), w.dtype)
    rope_p = jnp.concatenate([rope, zpad], -1).reshape(r, MLA_HEADS * LANES)
    rot_p = jnp.concatenate([rot, zpad], -1).reshape(r, MLA_HEADS * LANES)
    return jnp.concatenate([nope, rope_p, rot_p], axis=1).astype(BF16)


def _permute_w_ukv(w):
    r = w.shape[0]
    w3 = w.reshape(r, MLA_HEADS, MLA_NOPE + MLA_V)
    kn = w3[:, :, :MLA_NOPE].reshape(r, MLA_HEADS * MLA_NOPE)
    vv = w3[:, :, MLA_NOPE:].reshape(r, MLA_HEADS * MLA_V)
    return jnp.concatenate([kn, vv], axis=1).astype(BF16)


def _na_bias_tables(rpb, seq):
    w, kh, kw = GRID_W, NA_KH, NA_KW
    rows = seq // w
    nh, n_dr, n_dc = rpb.shape
    line = jnp.full((nh, n_dr, 2 * w), NEG_BIG, F32).at[:, :, w - kw:w - kw + n_dc].set(rpb.astype(F32) * SCORE_LOG2E)
    skew = jnp.broadcast_to(line[:, :, None, :], (nh, n_dr, w, 2 * w)).reshape(nh, n_dr, 2 * w * w)
    skew = skew[:, :, :w * (2 * w - 1)].reshape(nh, n_dr, w, 2 * w - 1)
    tc = skew[:, :, :, w - 1:2 * w - 1]
    cq = np.arange(w)[:, None]
    ck = np.arange(w)[None, :]
    cs = np.clip(cq - kw // 2, 0, w - kw)
    col_ok = (ck >= cs) & (ck < cs + kw)
    tc = jnp.where(col_ok[None, None], tc, NEG_BIG)
    tc = jnp.concatenate([tc, jnp.full((nh, 1, w, w), NEG_BIG, F32)], axis=1)
    tc2 = jnp.concatenate([tc, tc], axis=-1)
    blks = []
    tile_starts = (0, NA_ROWS, rows - NA_ROWS)
    for variant, r0 in enumerate(tile_starts):
        rb = int(np.clip(r0 - kh // 2, 0, rows - NA_BAND))
        rq = r0 + np.arange(NA_ROWS)[:, None]
        rk = rb + np.arange(NA_BAND)[None, :]
        rs = np.clip(rq - kh // 2, 0, rows - kh)
        row_ok = (rk >= rs) & (rk < rs + kh)
        band_blk = np.where(row_ok, rk - rq + kh - 1, n_dr)
        win_blk = []
        for hq, off in enumerate(_na_window_offsets(variant, len(tile_starts))):
            part = band_blk[hq * NA_ROWS // 2:(hq + 1) * NA_ROWS // 2]
            assert (np.delete(part, np.s_[off:off + NA_WIN], axis=1) == n_dr).all()
            win_blk.extend(part[:, off:off + NA_WIN].tolist())
        blks.append(win_blk)
    return pl.pallas_call(
        functools.partial(_na_bias_kernel, blks=blks),
        grid=(nh,),
        in_specs=[pl.BlockSpec((1, n_dr + 1, w, 2 * w), lambda h: (h, 0, 0, 0))],
        out_specs=pl.BlockSpec((1, len(blks), NA_ROWS * w, NA_WIN * w), lambda h: (h, 0, 0, 0)),
        out_shape=jax.ShapeDtypeStruct((nh, len(blks), NA_ROWS * w, NA_WIN * w), F32),
        compiler_params=_cparams(("parallel",)),
        name="na_bias_table",
    )(tc2)


def _na_bias_kernel(tc_ref, o_ref, *, blks):
    w = GRID_W
    low = lax.broadcasted_iota(jnp.int32, (w, 2 * w), 1) < w
    for v, blk in enumerate(blks):
        for rq, row in enumerate(blk):
            for pr in range(len(row) // 2):
                pair = jnp.where(low, tc_ref[0, row[2 * pr]], tc_ref[0, row[2 * pr + 1]])
                o_ref[0, v, rq * w:(rq + 1) * w, pr * 2 * w:(pr + 1) * 2 * w] = pair


def kernel(x, c, ctx, c_ctx, w_ada, b_ada, w_in, gqa_q_norm, gqa_k_norm, mla_q_norm, mla_kv_norm, mla_w_uq, mla_w_ukv, na_rpb, w_branch_a, w_branch_b, w_branch_c, w_out, ln1_g, ln1_b, w_router_group, b_router_group, w_router_expert, b_router_expert, w_expert_gate, w_expert_up, w_expert_down, ln2_g, ln2_b):
    bsz, seq, d = x.shape
    n_ctx = ctx.shape[1]
    depth = w_ada.shape[0]
    nx, nc = bsz * seq, bsz * n_ctx
    t = nx + nc
    assert seq % PROJ_TM == 0 and nc == PROJ_TM and seq % (NA_BAND * GRID_W) == 0 and nx % n_ctx == 0
    alpha = (2 * depth) ** 0.25

    def mod_row(i, tm):
        return jnp.minimum(i // (seq // tm), bsz)

    def rope_blk(i, tm):
        return jnp.where(i < nx // tm, i % (seq // tm), seq // tm)

    stream = (x.reshape(nx, d), ctx.reshape(nc, d), 0)
    cc = jnp.concatenate([c, c_ctx[None], jnp.zeros((8 - bsz - 1, d), F32)], axis=0)
    mod_all = _ada_call(cc, w_ada, b_ada).reshape(depth, 8, 6, d)
    cos_a, sin_a, cos_b, sin_b = _rope_tables(seq, ROW_TM)
    w_in_p = _permute_w_in(w_in)

    attn_kw = dict(bsz=bsz, seq=seq, n_ctx=n_ctx)
    for i in range(depth):
        last = i == depth - 1
        n_rows = nx if last else t
        mod = mod_all[i]
        p = _proj_call(stream, nx, t, mod, w_in_p, i, mod_row)

        qa, ka = _prep_a_call(p, cos_a, sin_a, gqa_q_norm[i][None], gqa_k_norm[i][None], rope_blk)
        a_kw = dict(n_heads=GQA_HEADS, dq=HEAD_DIM, q_col=lambda h: h, k_col=lambda h: h // GQA_GROUP,
                    v_col=lambda h: COL_AV // HEAD_DIM + h // GQA_GROUP, **attn_kw)
        o_a = _dense_attn_call(qa, ka, p, **a_kw)
        qb, kb, vb = _prep_b_call(p, cos_b, sin_b, mla_q_norm[i][None], mla_kv_norm[i][None],
                                  _permute_w_uq(mla_w_uq[i]), _permute_w_ukv(mla_w_ukv[i]), rope_blk)
        b_kw = dict(n_heads=MLA_HEADS, dq=MLA_PAD, q_col=lambda h: h, k_col=lambda h: h, v_col=lambda h: h, **attn_kw)
        o_b = _dense_attn_call(qb, kb, vb, **b_kw)
        o_c = _na_call(p, _na_bias_tables(na_rpb[i], seq), **attn_kw)
        o_lat = (o_a, o_b, o_c)
        if last:
            o_ctx = o_lat
        else:
            c_kw = dict(n_heads=NA_HEADS, dq=HEAD_DIM, q_col=lambda h: COL_CQ // HEAD_DIM + h,
                        k_col=lambda h: COL_CK // HEAD_DIM + h, v_col=lambda h: COL_CV // HEAD_DIM + h, **attn_kw)
            o_ctx = (_ctx_attn_call(qa, ka, p, **a_kw), _ctx_attn_call(qb, kb, vb, **b_kw),
                     _ctx_attn_call(p, p, p, **c_kw))

        mrg = _branch_call(o_lat, o_ctx, p, w_branch_a[i].astype(BF16), w_branch_b[i].astype(BF16),
                           w_branch_c[i].astype(BF16), n_rows)
        wr = jnp.concatenate([w_router_group[i], w_router_expert[i],
                              jnp.zeros((d, LANES - N_GROUPS - N_EXPERTS), F32)], axis=1)
        br = jnp.concatenate([b_router_group[i], b_router_expert[i],
                              jnp.zeros((LANES - N_GROUPS - N_EXPERTS,), F32)])[None]
        wr_cat = jnp.concatenate(_split_bf16(wr), axis=1)
        xs, u_f, route = _out_call(mrg, stream, nx, mod, w_out[i].astype(BF16), ln1_g[i][None], ln1_b[i][None],
                                   wr_cat, br, mod_row, n_rows, alpha)

        rank_slab, cnt = _rank_call(route)
        eid = route[:, :TOP_K].astype(jnp.int32)
        counts = cnt[0, :N_EXPERTS].astype(jnp.int32)
        padded = ((counts + MOE_BM - 1) // MOE_BM) * MOE_BM
        pends = jnp.cumsum(padded)
        pstarts = pends - padded
        pstarts_row = jnp.concatenate([pstarts.astype(F32), jnp.zeros((LANES - N_EXPERTS,), F32)])[None]
        dest = _dest_call(route, rank_slab, pstarts_row)[:, :TOP_K].astype(jnp.int32)
        nb = -(-(n_rows * TOP_K) // MOE_BM) + N_EXPERTS
        n_used = (pends[-1] // MOE_BM).astype(jnp.int32)
        blk = jnp.minimum(jnp.arange(nb, dtype=jnp.int32), n_used - 1)
        block_e = jnp.clip(jnp.searchsorted(pends, blk * MOE_BM, side="right"), 0, N_EXPERTS - 1).astype(jnp.int32)
        order = jnp.argsort(eid.reshape(-1), stable=True).astype(jnp.int32)
        starts = jnp.cumsum(counts) - counts
        slot = jnp.arange(nb * MOE_BM, dtype=jnp.int32).reshape(nb, MOE_BM)
        blk_shift = (starts - pstarts)[block_e][:, None]
        blk_limit = (pstarts + counts)[block_e][:, None]
        pos = jnp.clip(slot + blk_shift, 0, n_rows * TOP_K - 1)
        slot_tok = jnp.where(slot < blk_limit, order.at[pos].get(mode="promise_in_bounds") // TOP_K,
                             slot % n_rows).reshape(-1)
        u_sorted = u_f.at[slot_tok].get(mode="promise_in_bounds")
        jj = jnp.arange(nb, dtype=jnp.int32)
        first = ((jj == 0) | (block_e != jnp.roll(block_e, 1))).astype(jnp.int32)
        parity = (jnp.cumsum(first) - 1) % 2
        live = jnp.where(counts > 0, jnp.arange(N_EXPERTS, dtype=jnp.int32), N_EXPERTS)
        next_live = jnp.concatenate([lax.cummin(live, reverse=True)[1:], jnp.full((1,), N_EXPERTS, jnp.int32)])
        nxt = jnp.where(next_live < N_EXPERTS, next_live, -1)[block_e]
        y_sorted = _moe_call(block_e, first, nxt.astype(jnp.int32), parity.astype(jnp.int32), n_used[None],
                             u_sorted, w_expert_gate, w_expert_up, w_expert_down, i)
        y_pair = y_sorted.at[dest.T.reshape(-1)].get(mode="promise_in_bounds")
        xs = _combine_call(xs, y_pair, route, mod, ln2_g[i][None], ln2_b[i][None], mod_row, alpha)
        stream = (xs, xs, nx)
    return xs.reshape(bsz, seq, d)
```

```python
import functools
import math

import jax
import jax.numpy as jnp
import numpy as np
from jax import lax
from jax.experimental import pallas as pl
from jax.experimental.pallas import tpu as pltpu

F32 = jnp.float32
BF16 = jnp.bfloat16

HEAD_DIM = 128
GRID_W = 64
ROPE_THETA = 10000.0
GQA_HEADS, GQA_KV_HEADS = 6, 2
GQA_GROUP = GQA_HEADS // GQA_KV_HEADS
MLA_HEADS, MLA_Q_RANK, MLA_KV_RANK = 5, 512, 256
MLA_NOPE, MLA_ROPE, MLA_V = 128, 64, 128
MLA_QK = MLA_NOPE + MLA_ROPE
MLA_PAD = 256
NA_HEADS, NA_KH, NA_KW = 5, 8, 16
N_GROUPS, EXPERTS_PER_GROUP, TOP_K, D_EXPERT = 8, 8, 2, 512
N_EXPERTS = N_GROUPS * EXPERTS_PER_GROUP
LN_EPS = 1e-6
RMS_EPS = 1e-6
NEG_BIG = -1e30
SCORE_LOG2E = math.log2(math.e)

LANES = 128
VMEM_LIMIT = 56 * 1024 * 1024
PROJ_TM, PROJ_TN = 1024, 1024
ROW_TM = 512
MM_TM = 512
ATTN_TQ = 512
ATTN_SUB = 4
ATTN_KC = 1024
NA_ROWS = 8
NA_SUB = 4
NA_BAND = 16
NA_WIN = 12
MOE_BM = 256
MOE_DMA_CHUNKS = 4
ADA_TN = 512

COL_AQ, COL_AK = 0, 768
COL_BQ, COL_BKV, COL_BKR = 1024, 1536, 1792
COL_AV, COL_CQ, COL_CK, COL_CV = 1920, 2176, 2816, 3456
COL_GATE = 4096
PROJ_W = 10240


def _cparams(sem):
    return pltpu.CompilerParams(dimension_semantics=sem, vmem_limit_bytes=VMEM_LIMIT)


def _resident(shape):
    return pl.BlockSpec(shape, lambda *_: (0,) * len(shape), pipeline_mode=pl.Buffered(1))


def _dot(a, b):
    return jnp.dot(a, b, preferred_element_type=F32)


def _dot_nt(a, b):
    return lax.dot_general(a, b, (((1,), (1,)), ((), ())), preferred_element_type=F32)


def _split_bf16(a):
    hi = a.astype(BF16)
    lo = (a - hi.astype(F32)).astype(BF16)
    return hi, lo


def _ada_kernel(c_ref, w_ref, b_ref, o_ref):
    a = c_ref[...]
    a = a * jax.nn.sigmoid(a)
    a_hi, a_lo = _split_bf16(a)
    w_hi, w_lo = _split_bf16(w_ref[0])
    acc = _dot(a_hi, w_hi) + _dot(a_lo, w_hi) + _dot(a_hi, w_lo)
    o_ref[0] = acc + b_ref[0]


def _ada_call(cc, w_ada, b_ada):
    depth, d, n = w_ada.shape
    rows = cc.shape[0]
    return pl.pallas_call(
        _ada_kernel,
        grid=(depth, n // ADA_TN),
        in_specs=[pl.BlockSpec((rows, d), lambda l, j: (0, 0)),
                  pl.BlockSpec((1, d, ADA_TN), lambda l, j: (l, 0, j)),
                  pl.BlockSpec((1, 1, ADA_TN), lambda l, j: (l, 0, j))],
        out_specs=pl.BlockSpec((1, rows, ADA_TN), lambda l, j: (l, 0, j)),
        out_shape=jax.ShapeDtypeStruct((depth, rows, n), F32),
        compiler_params=_cparams(("parallel", "parallel")),
        name="ada_modulation",
    )(cc, w_ada, b_ada.reshape(depth, 1, n))


def _proj_kernel(xl_ref, xc_ref, mod_ref, w_ref, o_ref, u_scr, *, n_lat_tiles):
    @pl.when(pl.program_id(1) == 0)
    def _():
        def modulate(x):
            u_scr[...] = (x * (1.0 + mod_ref[0, 1:2, :]) + mod_ref[0, 0:1, :]).astype(BF16)

        @pl.when(pl.program_id(0) < n_lat_tiles)
        def _():
            modulate(xl_ref[...])

        @pl.when(pl.program_id(0) >= n_lat_tiles)
        def _():
            modulate(xc_ref[...])

    o_ref[...] = _dot_nt(u_scr[...], w_ref[0]).astype(BF16)


def _stream_specs(stream, tm, n_lat):
    lat, ctx, ctx_row0 = stream
    d = lat.shape[1]
    n_lat_tiles = n_lat // tm
    ctx_tile0 = ctx_row0 // tm

    def lat_map(i, *_):
        return (jnp.minimum(i, n_lat_tiles - 1), 0)

    def ctx_map(i, *_):
        return (ctx_tile0 + jnp.maximum(i - n_lat_tiles, 0), 0)

    return [pl.BlockSpec((tm, d), lat_map), pl.BlockSpec((tm, d), ctx_map, pipeline_mode=pl.Buffered(1))]


def _proj_call(stream, n_lat, n_tok, mod, w_in_p, layer, mod_row):
    d = stream[0].shape[1]
    return pl.pallas_call(
        functools.partial(_proj_kernel, n_lat_tiles=n_lat // PROJ_TM),
        grid=(n_tok // PROJ_TM, PROJ_W // PROJ_TN),
        in_specs=_stream_specs(stream, PROJ_TM, n_lat) + [
            pl.BlockSpec((1, 6, d), lambda i, j: (mod_row(i, PROJ_TM), 0, 0)),
            pl.BlockSpec((1, PROJ_TN, d), lambda i, j: (layer, j, 0))],
        out_specs=pl.BlockSpec((PROJ_TM, PROJ_TN), lambda i, j: (i, j)),
        out_shape=jax.ShapeDtypeStruct((n_tok, PROJ_W), BF16),
        scratch_shapes=[pltpu.VMEM((PROJ_TM, d), BF16)],
        compiler_params=_cparams(("parallel", "arbitrary")),
        name="input_projection",
    )(stream[0], stream[1], mod, w_in_p)


def _rope128(y, cos, sin_signed, first_quarter):
    rot = jnp.where(first_quarter, pltpu.roll(y, 96, 1), pltpu.roll(y, 32, 1))
    return y * cos + rot * sin_signed


def _prep_a_kernel(p_ref, cos_ref, sin_ref, gq_ref, gk_ref, q_ref, k_ref, *, scale):
    cos = cos_ref[...]
    sin_signed = sin_ref[...]
    lane = lax.broadcasted_iota(jnp.int32, cos.shape, 1)
    first_quarter = (lane & 32) == 0
    for h in range(GQA_HEADS + GQA_KV_HEADS):
        xh = p_ref[:, h * HEAD_DIM:(h + 1) * HEAD_DIM].astype(F32)
        r = lax.rsqrt(jnp.mean(xh * xh, axis=1, keepdims=True) + RMS_EPS)
        if h < GQA_HEADS:
            y = _rope128(xh * r * gq_ref[...], cos, sin_signed, first_quarter) * scale
            q_ref[:, h * HEAD_DIM:(h + 1) * HEAD_DIM] = y.astype(BF16)
        else:
            hk = h - GQA_HEADS
            y = _rope128(xh * r * gk_ref[...], cos, sin_signed, first_quarter)
            k_ref[:, hk * HEAD_DIM:(hk + 1) * HEAD_DIM] = y.astype(BF16)


def _prep_b_kernel(ql_ref, kvl_ref, kr_ref, cos_ref, sin_ref, gq_ref, gkv_ref, wq_ref, wkv_ref,
                   q_ref, k_ref, v_ref, *, scale):
    nh, hp = MLA_HEADS, MLA_PAD
    wn = nh * MLA_NOPE
    cos = cos_ref[...]
    sin = sin_ref[...]
    ql = ql_ref[...].astype(F32)
    ql = ql * lax.rsqrt(jnp.mean(ql * ql, axis=1, keepdims=True) + RMS_EPS) * gq_ref[...]
    qf = _dot(ql.astype(BF16), wq_ref[...])
    kvl = kvl_ref[...].astype(F32)
    kvl = kvl * lax.rsqrt(jnp.mean(kvl * kvl, axis=1, keepdims=True) + RMS_EPS) * gkv_ref[...]
    kvf = _dot(kvl.astype(BF16), wkv_ref[...])
    lane = lax.broadcasted_iota(jnp.int32, cos.shape, 1)
    low = lane < MLA_ROPE
    tk = kr_ref[...].astype(F32) * jnp.where(low, cos, sin)
    kr = jnp.where(low, tk + pltpu.roll(tk, MLA_ROPE, 1), 0.0).astype(BF16)
    for h in range(nh):
        q_ref[:, h * hp:h * hp + LANES] = (qf[:, h * LANES:(h + 1) * LANES] * scale).astype(BF16)
        qr = qf[:, wn + h * LANES:wn + (h + 1) * LANES] * cos + qf[:, 2 * wn + h * LANES:2 * wn + (h + 1) * LANES] * sin
        q_ref[:, h * hp + LANES:(h + 1) * hp] = (qr * scale).astype(BF16)
        k_ref[:, h * hp:h * hp + LANES] = kvf[:, h * LANES:(h + 1) * LANES].astype(BF16)
        k_ref[:, h * hp + LANES:(h + 1) * hp] = kr
    v_ref[...] = kvf[:, wn:].astype(BF16)


def _prep_ab_kernel(pa_ref, cos_a_ref, sin_a_ref, gqa_ref, gka_ref,
                    ql_ref, kvl_ref, kr_ref, cos_b_ref, sin_b_ref, gqb_ref, gkvb_ref, wq_ref, wkv_ref,
                    qa_ref, ka_ref, qb_ref, kb_ref, vb_ref, *, scale_a, scale_b):
    _prep_a_kernel(pa_ref, cos_a_ref, sin_a_ref, gqa_ref, gka_ref, qa_ref, ka_ref, scale=scale_a)
    _prep_b_kernel(ql_ref, kvl_ref, kr_ref, cos_b_ref, sin_b_ref, gqb_ref, gkvb_ref, wq_ref, wkv_ref,
                   qb_ref, kb_ref, vb_ref, scale=scale_b)


def _prep_ab_call(p, tabs, gqa, gka, gqb, gkvb, wq, wkv, rope_blk):
    t = p.shape[0]
    tm = ROW_TM
    nh = MLA_HEADS
    wqa, wka = GQA_HEADS * HEAD_DIM, GQA_KV_HEADS * HEAD_DIM
    cos_a, sin_a, cos_b, sin_b = tabs

    def rope_spec():
        return pl.BlockSpec((tm, LANES), lambda i: (rope_blk(i, tm), 0))

    return pl.pallas_call(
        functools.partial(_prep_ab_kernel, scale_a=SCORE_LOG2E / math.sqrt(HEAD_DIM),
                          scale_b=SCORE_LOG2E / math.sqrt(MLA_QK)),
        grid=(t // tm,),
        in_specs=[pl.BlockSpec((tm, wqa + wka), lambda i: (i, 0)), rope_spec(), rope_spec(),
                  _resident((1, HEAD_DIM)), _resident((1, HEAD_DIM)),
                  pl.BlockSpec((tm, MLA_Q_RANK), lambda i: (i, COL_BQ // MLA_Q_RANK)),
                  pl.BlockSpec((tm, MLA_KV_RANK), lambda i: (i, COL_BKV // MLA_KV_RANK)),
                  pl.BlockSpec((tm, LANES), lambda i: (i, COL_BKR // LANES)), rope_spec(), rope_spec(),
                  _resident((1, MLA_Q_RANK)), _resident((1, MLA_KV_RANK)),
                  _resident(wq.shape), _resident(wkv.shape)],
        out_specs=[pl.BlockSpec((tm, wqa), lambda i: (i, 0)),
                   pl.BlockSpec((tm, wka), lambda i: (i, 0)),
                   pl.BlockSpec((tm, nh * MLA_PAD), lambda i: (i, 0)),
                   pl.BlockSpec((tm, nh * MLA_PAD), lambda i: (i, 0)),
                   pl.BlockSpec((tm, nh * MLA_V), lambda i: (i, 0))],
        out_shape=[jax.ShapeDtypeStruct((t, wqa), BF16), jax.ShapeDtypeStruct((t, wka), BF16),
                   jax.ShapeDtypeStruct((t, nh * MLA_PAD), BF16),
                   jax.ShapeDtypeStruct((t, nh * MLA_PAD), BF16),
                   jax.ShapeDtypeStruct((t, nh * MLA_V), BF16)],
        compiler_params=_cparams(("parallel",)),
        name="qk_prep",
    )(p, cos_a, sin_a, gqa, gka, p, p, p, cos_b, sin_b, gqb, gkvb, wq, wkv)


def _online_attend(q, chunks):
    m = z = acc = None
    for k, v, bias in chunks:
        s = _dot_nt(q, k)
        if bias is not None:
            s = s + bias
        cm = jnp.max(s, axis=1, keepdims=True)
        if m is None:
            m = cm
            p = jnp.exp2(s - m)
            z = jnp.sum(p, axis=1, keepdims=True)
            acc = _dot(p.astype(BF16), v)
        else:
            m_new = jnp.maximum(m, cm)
            corr = jnp.exp2(m - m_new)
            p = jnp.exp2(s - m_new)
            z = z * corr + jnp.sum(p, axis=1, keepdims=True)
            acc = acc * corr + _dot(p.astype(BF16), v)
            m = m_new
    return acc / z


def _online_attend_ones(q, chunks, dv):
    m = acc = None
    for k, v_ones, bias in chunks:
        s = _dot_nt(q, k)
        if bias is not None:
            s = s + bias
        cm = jnp.max(s, axis=1, keepdims=True)
        if m is None:
            m = cm
            acc = _dot(jnp.exp2((s - m).astype(BF16)), v_ones)
        else:
            m_new = jnp.maximum(m, cm)
            corr = jnp.exp2(m - m_new)
            acc = acc * corr + _dot(jnp.exp2((s - m_new).astype(BF16)), v_ones)
            m = m_new
    return acc[:, :dv] / acc[:, dv:dv + 1]


def _dense_attn_kernel(q_ref, kl_ref, vl_ref, kc_ref, vc_ref, o_ref, v_ones):
    n_lat = kl_ref.shape[0]
    dv = vl_ref.shape[1]

    @pl.when(pl.program_id(2) == 0)
    def _():
        v_ones[0:n_lat, 0:dv] = vl_ref[...]
        v_ones[n_lat:, 0:dv] = vc_ref[...]
        v_ones[:, dv:] = jnp.ones((v_ones.shape[0], dv), BF16)

    chunks = [(kl_ref[c:c + ATTN_KC, :], v_ones[c:c + ATTN_KC, :], None) for c in range(0, n_lat, ATTN_KC)]
    chunks.append((kc_ref[...], v_ones[n_lat:, :], None))
    for r in range(0, q_ref.shape[0], ATTN_TQ):
        o_ref[r:r + ATTN_TQ, :] = _online_attend_ones(q_ref[r:r + ATTN_TQ, :], chunks, dv).astype(o_ref.dtype)


def _dense_attn_call(q, k, v, *, n_heads, dq, q_col, k_col, v_col, bsz, seq, n_ctx):
    tq = ATTN_TQ * ATTN_SUB
    tpb = seq // tq
    ctx0 = bsz * seq // n_ctx
    dv = HEAD_DIM
    return pl.pallas_call(
        _dense_attn_kernel,
        grid=(bsz, n_heads, tpb),
        in_specs=[pl.BlockSpec((tq, dq), lambda b, h, i: (b * tpb + i, q_col(h))),
                  pl.BlockSpec((seq, dq), lambda b, h, i: (b, k_col(h))),
                  pl.BlockSpec((seq, dv), lambda b, h, i: (b, v_col(h))),
                  pl.BlockSpec((n_ctx, dq), lambda b, h, i: (ctx0 + b, k_col(h))),
                  pl.BlockSpec((n_ctx, dv), lambda b, h, i: (ctx0 + b, v_col(h)))],
        out_specs=pl.BlockSpec((tq, dv), lambda b, h, i: (b * tpb + i, h)),
        out_shape=jax.ShapeDtypeStruct((bsz * seq, n_heads * dv), BF16),
        scratch_shapes=[pltpu.VMEM((seq + n_ctx, 2 * dv), BF16)],
        compiler_params=_cparams(("parallel", "parallel", "arbitrary")),
        name="dense_attention",
    )(q, k, v, k, v)


def _ctx_attn_kernel(q_ref, kc_ref, vc_ref, o_ref):
    o_ref[...] = _online_attend(q_ref[...], [(kc_ref[...], vc_ref[...], None)]).astype(o_ref.dtype)


def _ctx_attn_call(q, k, v, *, n_heads, dq, q_col, k_col, v_col, bsz, seq, n_ctx):
    ctx0 = bsz * seq // n_ctx
    dv = HEAD_DIM
    return pl.pallas_call(
        _ctx_attn_kernel,
        grid=(bsz, n_heads),
        in_specs=[pl.BlockSpec((n_ctx, dq), lambda b, h: (ctx0 + b, q_col(h))),
                  pl.BlockSpec((n_ctx, dq), lambda b, h: (ctx0 + b, k_col(h))),
                  pl.BlockSpec((n_ctx, dv), lambda b, h: (ctx0 + b, v_col(h)))],
        out_specs=pl.BlockSpec((n_ctx, dv), lambda b, h: (b, h)),
        out_shape=jax.ShapeDtypeStruct((bsz * n_ctx, n_heads * dv), BF16),
        compiler_params=_cparams(("parallel", "parallel")),
        name="context_attention",
    )(q, k, v)


def _na_window_offsets(variant, n_variants):
    shift = NA_BAND - NA_WIN
    return ((0, 0), (0, shift), (shift, shift))[0 if variant == 0 else (2 if variant == n_variants - 1 else 1)]


def _na_kernel(q_ref, k_ref, v_ref, kc_ref, vc_ref, *rest, rows_total, n_tiles):
    bias_refs, o_ref, v_ones = rest[:NA_SUB], rest[NA_SUB], rest[NA_SUB + 1]
    n_lat = k_ref.shape[0]
    dv = v_ref.shape[1]
    tile = NA_ROWS * GRID_W
    half = tile // 2
    win = NA_WIN * GRID_W
    shift = (NA_BAND - NA_WIN) * GRID_W

    @pl.when(pl.program_id(2) == 0)
    def _():
        v_ones[0:n_lat, 0:dv] = v_ref[...]
        v_ones[n_lat:, 0:dv] = vc_ref[...]
        v_ones[:, dv:] = jnp.ones((v_ones.shape[0], dv), BF16)

    ctx = (kc_ref[...], v_ones[n_lat:, :], None)
    for t in range(NA_SUB):
        i = pl.program_id(2) * NA_SUB + t
        start_blk = jnp.clip(2 * i - 1, 0, (rows_total - NA_BAND) // 4)
        band0 = start_blk * (4 * GRID_W)
        starts = (band0 + jnp.where(i == n_tiles - 1, shift, 0), band0 + jnp.where(i == 0, 0, shift))
        for hq, start in enumerate(starts):
            start = pl.multiple_of(start, 4 * GRID_W)
            rows = slice(t * tile + hq * half, t * tile + (hq + 1) * half)
            bias = bias_refs[t][0, 0, hq * half:(hq + 1) * half, :]
            chunks = [(k_ref[pl.ds(start, win), :], v_ones[pl.ds(start, win), :], bias), ctx]
            o_ref[rows, :] = _online_attend_ones(q_ref[rows, :], chunks, dv).astype(o_ref.dtype)


def _na_call(p, bias_tab, *, bsz, seq, n_ctx):
    tile = NA_ROWS * GRID_W
    n_tiles = seq // tile
    tq = tile * NA_SUB
    tpb = seq // tq
    rows_total = seq // GRID_W
    ctx0 = bsz * seq // n_ctx
    d = HEAD_DIM
    qc, kc, vc = COL_CQ // d, COL_CK // d, COL_CV // d

    def bias_spec(t):
        def index(b, h, i):
            tile_idx = i * NA_SUB + t
            return (h, jnp.where(tile_idx == 0, 0, jnp.where(tile_idx == n_tiles - 1, 2, 1)), 0, 0)
        return pl.BlockSpec((1, 1, tile, NA_WIN * GRID_W), index)

    return pl.pallas_call(
        functools.partial(_na_kernel, rows_total=rows_total, n_tiles=n_tiles),
        grid=(bsz, NA_HEADS, tpb),
        in_specs=[pl.BlockSpec((tq, d), lambda b, h, i: (b * tpb + i, qc + h)),
                  pl.BlockSpec((seq, d), lambda b, h, i: (b, kc + h)),
                  pl.BlockSpec((seq, d), lambda b, h, i: (b, vc + h)),
                  pl.BlockSpec((n_ctx, d), lambda b, h, i: (ctx0 + b, kc + h)),
                  pl.BlockSpec((n_ctx, d), lambda b, h, i: (ctx0 + b, vc + h))]
                 + [bias_spec(t) for t in range(NA_SUB)],
        out_specs=pl.BlockSpec((tq, d), lambda b, h, i: (b * tpb + i, h)),
        out_shape=jax.ShapeDtypeStruct((bsz * seq, NA_HEADS * d), BF16),
        scratch_shapes=[pltpu.VMEM((seq + n_ctx, 2 * d), BF16)],
        compiler_params=_cparams(("parallel", "parallel", "arbitrary")),
        name="neighbourhood_attention",
    )(p, p, p, p, p, *([bias_tab] * NA_SUB))


def _branch_kernel(oa_ref, ob_ref, oc_ref, oac_ref, obc_ref, occ_ref, ga_ref, gb_ref, gc_ref,
                   wa_ref, wb_ref, wc_ref, o_ref, *, n_lat_tiles):
    def gated(g_ref, w_ref, o_tile):
        return jax.nn.sigmoid(g_ref[...].astype(F32)) * _dot(o_tile, w_ref[...])

    @pl.when(pl.program_id(0) < n_lat_tiles)
    def _():
        acc = gated(ga_ref, wa_ref, oa_ref[...]) + gated(gb_ref, wb_ref, ob_ref[...]) + gated(gc_ref, wc_ref, oc_ref[...])
        o_ref[...] = acc.astype(BF16)

    @pl.when(pl.program_id(0) >= n_lat_tiles)
    def _():
        acc = gated(ga_ref, wa_ref, oac_ref[...]) + gated(gb_ref, wb_ref, obc_ref[...]) + gated(gc_ref, wc_ref, occ_ref[...])
        o_ref[...] = acc.astype(BF16)


def _branch_call(o_lat, o_ctx, p, wa, wb, wc, n_rows):
    d = wa.shape[1]
    tm = MM_TM
    g0 = COL_GATE // d
    n_lat_tiles = o_lat[0].shape[0] // tm
    lat_specs = [pl.BlockSpec((tm, o.shape[1]), lambda i: (jnp.minimum(i, n_lat_tiles - 1), 0)) for o in o_lat]
    ctx_specs = [pl.BlockSpec((tm, o.shape[1]), lambda i: (jnp.maximum(i - n_lat_tiles, 0), 0)) for o in o_ctx]
    return pl.pallas_call(
        functools.partial(_branch_kernel, n_lat_tiles=n_lat_tiles),
        grid=(n_rows // tm,),
        in_specs=lat_specs + ctx_specs + [
            pl.BlockSpec((tm, d), lambda i: (i, g0)),
            pl.BlockSpec((tm, d), lambda i: (i, g0 + 1)),
            pl.BlockSpec((tm, d), lambda i: (i, g0 + 2)),
            _resident(wa.shape), _resident(wb.shape), _resident(wc.shape)],
        out_specs=pl.BlockSpec((tm, d), lambda i: (i, 0)),
        out_shape=jax.ShapeDtypeStruct((n_rows, d), BF16),
        compiler_params=_cparams(("parallel",)),
        name="branch_merge",
    )(*o_lat, *o_ctx, p, p, p, wa, wb, wc)


def _layer_norm(z, g, b):
    mu = jnp.mean(z, axis=1, keepdims=True)
    zc = z - mu
    var = jnp.mean(zc * zc, axis=1, keepdims=True)
    return zc * lax.rsqrt(var + LN_EPS) * g + b


def _route(logits):
    lane = lax.broadcasted_iota(jnp.int32, logits.shape, 1)
    lane_f = lane.astype(F32)
    is_g = lane < N_GROUPS
    gl = jnp.where(is_g, logits, NEG_BIG)
    mg = jnp.max(gl, axis=1, keepdims=True)
    gsel = jnp.min(jnp.where(gl == mg, lane_f, float(LANES)), axis=1, keepdims=True)
    zg = jnp.sum(jnp.where(is_g, jnp.exp(gl - mg), 0.0), axis=1, keepdims=True)
    lo = N_GROUPS + gsel * EXPERTS_PER_GROUP
    is_e = (lane_f >= lo) & (lane_f < lo + EXPERTS_PER_GROUP)
    el = jnp.where(is_e, logits, NEG_BIG)
    t1 = jnp.max(el, axis=1, keepdims=True)
    i1 = jnp.min(jnp.where(el == t1, lane_f, float(LANES)), axis=1, keepdims=True)
    el2 = jnp.where(lane_f == i1, NEG_BIG, el)
    t2 = jnp.max(el2, axis=1, keepdims=True)
    i2 = jnp.min(jnp.where(el2 == t2, lane_f, float(LANES)), axis=1, keepdims=True)
    dd = jnp.exp(t2 - t1)
    g1 = 1.0 / (zg * (1.0 + dd))
    g2 = g1 * dd
    slab = jnp.where(lane == 0, i1 - N_GROUPS,
                     jnp.where(lane == 1, i2 - N_GROUPS,
                               jnp.where(lane == 2, g1, jnp.where(lane == 3, g2, 0.0))))
    return slab


def _out_kernel(m_ref, xl_ref, xc_ref, mod_ref, w_ref, lg_ref, lb_ref, wr_ref, br_ref,
                xo_ref, u_ref, r_ref, *, alpha, n_lat_tiles):
    @pl.when(pl.program_id(0) < n_lat_tiles)
    def _():
        xo_ref[...] = xl_ref[...]

    @pl.when(pl.program_id(0) >= n_lat_tiles)
    def _():
        xo_ref[...] = xc_ref[...]

    gate_m = mod_ref[0, 2:3, :]
    shift_f = mod_ref[0, 3:4, :]
    scale_f = mod_ref[0, 4:5, :]
    y = _dot(m_ref[...], w_ref[...])
    xn = _layer_norm(alpha * xo_ref[...] + gate_m * y, lg_ref[...], lb_ref[...])
    xo_ref[...] = xn
    u = xn * (1.0 + scale_f) + shift_f
    u_hi, u_lo = _split_bf16(u)
    u_ref[...] = u_hi
    both = _dot(u_hi, wr_ref[...])
    logits = both[:, :LANES] + both[:, LANES:] + _dot(u_lo, wr_ref[:, :LANES]) + br_ref[...]
    r_ref[...] = _route(logits)


def _out_call(mrg, stream, n_lat, mod, w_out, ln_g, ln_b, wr_cat, br, mod_row, n_rows, alpha):
    d = stream[0].shape[1]
    tm = MM_TM
    return pl.pallas_call(
        functools.partial(_out_kernel, alpha=alpha, n_lat_tiles=n_lat // tm),
        grid=(n_rows // tm,),
        in_specs=[pl.BlockSpec((tm, d), lambda i: (i, 0))] + _stream_specs(stream, tm, n_lat) + [
                  pl.BlockSpec((1, 6, d), lambda i: (mod_row(i, tm), 0, 0)),
                  _resident((d, d)), _resident((1, d)), _resident((1, d)),
                  _resident((d, 2 * LANES)), _resident((1, LANES))],
        out_specs=[pl.BlockSpec((tm, d), lambda i: (i, 0)),
                   pl.BlockSpec((tm, d), lambda i: (i, 0)),
                   pl.BlockSpec((tm, LANES), lambda i: (i, 0))],
        out_shape=[jax.ShapeDtypeStruct((n_rows, d), F32),
                   jax.ShapeDtypeStruct((n_rows, d), BF16),
                   jax.ShapeDtypeStruct((n_rows, LANES), F32)],
        compiler_params=_cparams(("parallel",)),
        name="out_proj_ln_route",
    )(mrg, stream[0], stream[1], mod, w_out, ln_g, ln_b, wr_cat, br)


def _rank_kernel(r_ref, rank_ref, cnt_ref, carry):
    @pl.when(pl.program_id(0) == 0)
    def _():
        carry[...] = jnp.zeros_like(carry)

    slab = r_ref[...]
    tm = slab.shape[0]
    lane = lax.broadcasted_iota(jnp.int32, slab.shape, 1)
    lane_f = lane.astype(F32)
    e1 = slab[:, 0:1]
    e2 = slab[:, 1:2]
    hit1 = lane_f == e1
    hit2 = lane_f == e2
    onehot = jnp.where(hit1 | hit2, 1.0, 0.0)
    row = lax.broadcasted_iota(jnp.int32, (tm, tm), 0)
    col = lax.broadcasted_iota(jnp.int32, (tm, tm), 1)
    lower = jnp.where(col < row, 1.0, 0.0).astype(BF16)
    before = _dot(lower, onehot.astype(BF16)) + carry[0:1, :]
    r1 = jnp.sum(jnp.where(hit1, before, 0.0), axis=1, keepdims=True)
    r2 = jnp.sum(jnp.where(hit2, before, 0.0), axis=1, keepdims=True)
    rank_ref[...] = jnp.where(lane == 0, r1, jnp.where(lane == 1, r2, 0.0))
    carry[...] = carry[...] + jnp.sum(onehot, axis=0, keepdims=True)
    cnt_ref[...] = carry[...]


def _rank_call(route):
    n = route.shape[0]
    tm = ROW_TM
    return pl.pallas_call(
        _rank_kernel,
        grid=(n // tm,),
        in_specs=[pl.BlockSpec((tm, LANES), lambda i: (i, 0))],
        out_specs=[pl.BlockSpec((tm, LANES), lambda i: (i, 0)),
                   pl.BlockSpec((8, LANES), lambda i: (0, 0))],
        out_shape=[jax.ShapeDtypeStruct((n, LANES), F32), jax.ShapeDtypeStruct((8, LANES), F32)],
        scratch_shapes=[pltpu.VMEM((8, LANES), F32)],
        compiler_params=_cparams(("arbitrary",)),
        name="expert_ranks",
    )(route)


def _dest_kernel(r_ref, rank_ref, ps_ref, o_ref):
    slab = r_ref[...]
    rank = rank_ref[...]
    lane = lax.broadcasted_iota(jnp.int32, slab.shape, 1)
    lane_f = lane.astype(F32)
    ps = ps_ref[...]
    s1 = jnp.sum(jnp.where(lane_f == slab[:, 0:1], ps, 0.0), axis=1, keepdims=True) + rank[:, 0:1]
    s2 = jnp.sum(jnp.where(lane_f == slab[:, 1:2], ps, 0.0), axis=1, keepdims=True) + rank[:, 1:2]
    o_ref[...] = jnp.where(lane == 0, s1, jnp.where(lane == 1, s2, 0.0))


def _dest_call(route, rank_slab, pstarts_row):
    n = route.shape[0]
    tm = n // 8
    return pl.pallas_call(
        _dest_kernel,
        grid=(n // tm,),
        in_specs=[pl.BlockSpec((tm, LANES), lambda i: (i, 0)),
                  pl.BlockSpec((tm, LANES), lambda i: (i, 0)),
                  _resident((1, LANES))],
        out_specs=pl.BlockSpec((tm, LANES), lambda i: (i, 0)),
        out_shape=jax.ShapeDtypeStruct((n, LANES), F32),
        compiler_params=_cparams(("parallel",)),
        name="expert_slots",
    )(route, rank_slab, pstarts_row)


def _moe_kernel(be_ref, first_ref, nxt_ref, par_ref, nu_ref, x_ref, wg_hbm, wu_hbm, wd_hbm, y_ref,
                gbuf, ubuf, dbuf, sems, *, layer):
    j = pl.program_id(0)

    def weight_copies(e, slot):
        copies = []
        for m, (hbm, buf) in enumerate(((wg_hbm, gbuf), (wu_hbm, ubuf), (wd_hbm, dbuf))):
            band = buf.shape[1] // MOE_DMA_CHUNKS
            for c in range(MOE_DMA_CHUNKS):
                rows = pl.ds(c * band, band)
                copies.append(pltpu.make_async_copy(hbm.at[layer, e, rows], buf.at[slot, rows], sems.at[slot, m]))
        return copies

    @pl.when(j < nu_ref[0])
    def _():
        slot = par_ref[j]

        @pl.when(j == 0)
        def _():
            for cp in weight_copies(be_ref[0], 0):
                cp.start()

        @pl.when(first_ref[j] == 1)
        def _():
            @pl.when(nxt_ref[j] >= 0)
            def _():
                for cp in weight_copies(nxt_ref[j], 1 - slot):
                    cp.start()

            for cp in weight_copies(be_ref[j], slot):
                cp.wait()

        xb = x_ref[...]
        hg = _dot(xb, gbuf[slot].astype(BF16))
        hu = _dot(xb, ubuf[slot].astype(BF16))
        hb = (hg * jax.nn.sigmoid(hg) * hu).astype(BF16)
        y_ref[...] = _dot(hb, dbuf[slot].astype(BF16)).astype(y_ref.dtype)

    @pl.when(j >= nu_ref[0])
    def _():
        y_ref[...] = jnp.zeros_like(y_ref)


def _moe_call(block_e, first, nxt, parity, n_used, u_sorted, w_gate, w_up, w_down, layer):
    n_slots, d = u_sorted.shape
    bm = MOE_BM
    nb = n_slots // bm
    de = w_gate.shape[-1]

    def xmap(j, be, fi, nx_, pa, nu):
        return (jnp.minimum(j, nu[0] - 1), 0)

    grid_spec = pltpu.PrefetchScalarGridSpec(
        num_scalar_prefetch=5,
        grid=(nb,),
        in_specs=[pl.BlockSpec((bm, d), xmap),
                  pl.BlockSpec(memory_space=pl.ANY),
                  pl.BlockSpec(memory_space=pl.ANY),
                  pl.BlockSpec(memory_space=pl.ANY)],
        out_specs=pl.BlockSpec((bm, d), lambda j, *_: (j, 0)),
        scratch_shapes=[pltpu.VMEM((2, d, de), F32), pltpu.VMEM((2, d, de), F32), pltpu.VMEM((2, de, d), F32),
                        pltpu.SemaphoreType.DMA((2, 3))],
    )
    return pl.pallas_call(
        functools.partial(_moe_kernel, layer=layer),
        grid_spec=grid_spec,
        out_shape=jax.ShapeDtypeStruct((n_slots, d), BF16),
        compiler_params=_cparams(("arbitrary",)),
        name="expert_mlp",
    )(block_e, first, nxt, parity, n_used, u_sorted, w_gate, w_up, w_down)


def _combine_kernel(x_ref, y0_ref, y1_ref, r_ref, mod_ref, lg_ref, lb_ref, o_ref, *, alpha):
    slab = r_ref[...]
    g1 = slab[:, 2:3]
    g2 = slab[:, 3:4]
    mx = g1 * y0_ref[...].astype(F32) + g2 * y1_ref[...].astype(F32)
    gate_f = mod_ref[0, 5:6, :]
    o_ref[...] = _layer_norm(alpha * x_ref[...] + gate_f * mx, lg_ref[...], lb_ref[...])


def _combine_call(xs, y_pair, route, mod, ln_g, ln_b, mod_row, alpha):
    n, d = xs.shape
    tm = ROW_TM
    return pl.pallas_call(
        functools.partial(_combine_kernel, alpha=alpha),
        grid=(n // tm,),
        in_specs=[pl.BlockSpec((tm, d), lambda i: (i, 0)),
                  pl.BlockSpec((tm, d), lambda i: (i, 0)),
                  pl.BlockSpec((tm, d), lambda i: (n // tm + i, 0)),
                  pl.BlockSpec((tm, LANES), lambda i: (i, 0)),
                  pl.BlockSpec((1, 6, d), lambda i: (mod_row(i, tm), 0, 0)),
                  pl.BlockSpec((1, d), lambda i: (0, 0)),
                  pl.BlockSpec((1, d), lambda i: (0, 0))],
        out_specs=pl.BlockSpec((tm, d), lambda i: (i, 0)),
        out_shape=jax.ShapeDtypeStruct((n, d), F32),
        compiler_params=_cparams(("parallel",)),
        name="moe_combine_ln",
    )(xs, y_pair, y_pair, route, mod, ln_g, ln_b)


def _rope_angles(seq, dim):
    tpos = np.arange(seq)
    row = (tpos // GRID_W).astype(np.float32)
    col = (tpos % GRID_W).astype(np.float32)
    quarter = dim // 4
    inv_freq = jnp.asarray(ROPE_THETA, F32) ** (-jnp.arange(quarter, dtype=F32) / quarter)
    ang_r = jnp.asarray(row)[:, None] * inv_freq
    ang_c = jnp.asarray(col)[:, None] * inv_freq
    return jnp.concatenate([ang_r, ang_r, ang_c, ang_c], axis=-1)


def _rot_sign(dim):
    l = np.arange(dim)
    return np.where((l & (dim // 4)) == 0, -1.0, 1.0).astype(np.float32), l ^ (dim // 4)


def _rope_tables(seq, pad_rows):
    ang_a = _rope_angles(seq, HEAD_DIM)
    sign_a, _ = _rot_sign(HEAD_DIM)
    cos_a = jnp.concatenate([jnp.cos(ang_a), jnp.ones((pad_rows, HEAD_DIM), F32)], 0)
    sin_a = jnp.concatenate([jnp.sin(ang_a) * sign_a, jnp.zeros((pad_rows, HEAD_DIM), F32)], 0)
    ang_b = _rope_angles(seq, MLA_ROPE)
    cos_b = jnp.concatenate([jnp.cos(ang_b), jnp.ones((pad_rows, MLA_ROPE), F32)], 0)
    sin_b = jnp.concatenate([jnp.sin(ang_b), jnp.zeros((pad_rows, MLA_ROPE), F32)], 0)
    return cos_a, sin_a, jnp.tile(cos_b, (1, 2)), jnp.tile(sin_b, (1, 2))


ORIG_AV, ORIG_BQ, ORIG_BKR, ORIG_CQ, ORIG_W = 1024, 1280, 2048, 2112, 10176
W_PREP_ROWS = MLA_ROPE
W_PREP_GROUP = 8
PREP_COPY, PREP_SCALE, PREP_ROT = 0, 1, 2


def _w_in_prep_kernel(src_ref, kind_ref, *refs, cq_scale):
    del src_ref
    w_refs, o_ref = refs[:W_PREP_GROUP], refs[W_PREP_GROUP]
    quarter = MLA_ROPE // 4
    rb = W_PREP_ROWS
    for k, w_ref in enumerate(w_refs):
        kind = kind_ref[pl.program_id(1) * W_PREP_GROUP + k]
        lo = k * rb

        @pl.when(kind == PREP_COPY)
        def _(w_ref=w_ref, lo=lo):
            o_ref[0, lo:lo + rb] = w_ref[0].astype(BF16)

        @pl.when(kind == PREP_SCALE)
        def _(w_ref=w_ref, lo=lo):
            o_ref[0, lo:lo + rb] = (w_ref[0] * cq_scale).astype(BF16)

        @pl.when(kind == PREP_ROT)
        def _(w_ref=w_ref, lo=lo):
            for r0 in range(0, rb, 2 * quarter):
                o_ref[0, lo + r0:lo + r0 + quarter] = (-w_ref[0, r0 + quarter:r0 + 2 * quarter]).astype(BF16)
                o_ref[0, lo + r0 + quarter:lo + r0 + 2 * quarter] = w_ref[0, r0:r0 + quarter].astype(BF16)


def _permute_w_in(w):
    depth, d, width = w.shape
    assert width == ORIG_W and ORIG_W - ORIG_CQ == PROJ_W - COL_CQ
    rb = W_PREP_ROWS
    src = np.zeros(PROJ_W // rb, np.int32)
    kind = np.full(PROJ_W // rb, PREP_COPY, np.int32)
    for dst0, src0, n in ((COL_AQ, 0, ORIG_AV), (COL_BQ, ORIG_BQ, ORIG_BKR - ORIG_BQ), (COL_BKR, ORIG_BKR, rb),
                          (COL_BKR + rb, ORIG_BKR, rb), (COL_AV, ORIG_AV, ORIG_BQ - ORIG_AV),
                          (COL_CQ, ORIG_CQ, ORIG_W - ORIG_CQ)):
        src[dst0 // rb:(dst0 + n) // rb] = np.arange(src0 // rb, (src0 + n) // rb)
    kind[(COL_BKR + rb) // rb] = PREP_ROT
    kind[COL_CQ // rb:COL_CK // rb] = PREP_SCALE
    wt = jnp.swapaxes(w, 1, 2)
    grp = W_PREP_GROUP

    def src_spec(k):
        return pl.BlockSpec((1, rb, d), lambda l, r, src_ref, kind_ref: (l, src_ref[r * grp + k], 0))

    grid_spec = pltpu.PrefetchScalarGridSpec(
        num_scalar_prefetch=2,
        grid=(depth, PROJ_W // (rb * grp)),
        in_specs=[src_spec(k) for k in range(grp)],
        out_specs=pl.BlockSpec((1, rb * grp, d), lambda l, r, src_ref, kind_ref: (l, r, 0)),
    )
    return pl.pallas_call(
        functools.partial(_w_in_prep_kernel, cq_scale=SCORE_LOG2E / math.sqrt(HEAD_DIM)),
        grid_spec=grid_spec,
        out_shape=jax.ShapeDtypeStruct((depth, PROJ_W, d), BF16),
        compiler_params=_cparams(("parallel", "parallel")),
        name="w_in_prep",
    )(jnp.asarray(src), jnp.asarray(kind), *([wt] * grp))


def _rot_cols(w):
    dim = w.shape[-1]
    wr = w.reshape(w.shape[:-1] + (2, 2, dim // 4))
    return jnp.stack([-wr[..., 1, :], wr[..., 0, :]], axis=-2).reshape(w.shape)


def _permute_w_uq(w):
    r = w.shape[0]
    w3 = w.reshape(r, MLA_HEADS, MLA_QK)
    nope = w3[:, :, :MLA_NOPE].reshape(r, MLA_HEADS * MLA_NOPE)
    rope = w3[:, :, MLA_NOPE:]
    rot = _rot_cols(rope)
    zpad = jnp.zeros((r, MLA_HEADS, LANES - MLA_ROPE), w.dtype)
    rope_p = jnp.concatenate([rope, zpad], -1).reshape(r, MLA_HEADS * LANES)
    rot_p = jnp.concatenate([rot, zpad], -1).reshape(r, MLA_HEADS * LANES)
    return jnp.concatenate([nope, rope_p, rot_p], axis=1).astype(BF16)


def _permute_w_ukv(w):
    r = w.shape[0]
    w3 = w.reshape(r, MLA_HEADS, MLA_NOPE + MLA_V)
    kn = w3[:, :, :MLA_NOPE].reshape(r, MLA_HEADS * MLA_NOPE)
    vv = w3[:, :, MLA_NOPE:].reshape(r, MLA_HEADS * MLA_V)
    return jnp.concatenate([kn, vv], axis=1).astype(BF16)


def _na_bias_tables(rpb, seq):
    w, kh, kw = GRID_W, NA_KH, NA_KW
    rows = seq // w
    nh, n_dr, n_dc = rpb.shape
    line = jnp.full((nh, n_dr, 2 * w), NEG_BIG, F32).at[:, :, w - kw:w - kw + n_dc].set(rpb.astype(F32) * SCORE_LOG2E)
    skew = jnp.broadcast_to(line[:, :, None, :], (nh, n_dr, w, 2 * w)).reshape(nh, n_dr, 2 * w * w)
    skew = skew[:, :, :w * (2 * w - 1)].reshape(nh, n_dr, w, 2 * w - 1)
    tc = skew[:, :, :, w - 1:2 * w - 1]
    cq = np.arange(w)[:, None]
    ck = np.arange(w)[None, :]
    cs = np.clip(cq - kw // 2, 0, w - kw)
    col_ok = (ck >= cs) & (ck < cs + kw)
    tc = jnp.where(col_ok[None, None], tc, NEG_BIG)
    tc = jnp.concatenate([tc, jnp.full((nh, 1, w, w), NEG_BIG, F32)], axis=1)
    tc2 = jnp.concatenate([tc, tc], axis=-1)
    blks = []
    tile_starts = (0, NA_ROWS, rows - NA_ROWS)
    for variant, r0 in enumerate(tile_starts):
        rb = int(np.clip(r0 - kh // 2, 0, rows - NA_BAND))
        rq = r0 + np.arange(NA_ROWS)[:, None]
        rk = rb + np.arange(NA_BAND)[None, :]
        rs = np.clip(rq - kh // 2, 0, rows - kh)
        row_ok = (rk >= rs) & (rk < rs + kh)
        band_blk = np.where(row_ok, rk - rq + kh - 1, n_dr)
        win_blk = []
        for hq, off in enumerate(_na_window_offsets(variant, len(tile_starts))):
            part = band_blk[hq * NA_ROWS // 2:(hq + 1) * NA_ROWS // 2]
            assert (np.delete(part, np.s_[off:off + NA_WIN], axis=1) == n_dr).all()
            win_blk.extend(part[:, off:off + NA_WIN].tolist())
        blks.append(win_blk)
    return pl.pallas_call(
        functools.partial(_na_bias_kernel, blks=blks),
        grid=(nh,),
        in_specs=[pl.BlockSpec((1, n_dr + 1, w, 2 * w), lambda h: (h, 0, 0, 0))],
        out_specs=pl.BlockSpec((1, len(blks), NA_ROWS * w, NA_WIN * w), lambda h: (h, 0, 0, 0)),
        out_shape=jax.ShapeDtypeStruct((nh, len(blks), NA_ROWS * w, NA_WIN * w), F32),
        compiler_params=_cparams(("parallel",)),
        name="na_bias_table",
    )(tc2)


def _na_bias_kernel(tc_ref, o_ref, *, blks):
    w = GRID_W
    low = lax.broadcasted_iota(jnp.int32, (w, 2 * w), 1) < w
    for v, blk in enumerate(blks):
        for rq, row in enumerate(blk):
            for pr in range(len(row) // 2):
                pair = jnp.where(low, tc_ref[0, row[2 * pr]], tc_ref[0, row[2 * pr + 1]])
                o_ref[0, v, rq * w:(rq + 1) * w, pr * 2 * w:(pr + 1) * 2 * w] = pair


def kernel(x, c, ctx, c_ctx, w_ada, b_ada, w_in, gqa_q_norm, gqa_k_norm, mla_q_norm, mla_kv_norm, mla_w_uq, mla_w_ukv, na_rpb, w_branch_a, w_branch_b, w_branch_c, w_out, ln1_g, ln1_b, w_router_group, b_router_group, w_router_expert, b_router_expert, w_expert_gate, w_expert_up, w_expert_down, ln2_g, ln2_b):
    bsz, seq, d = x.shape
    n_ctx = ctx.shape[1]
    depth = w_ada.shape[0]
    nx, nc = bsz * seq, bsz * n_ctx
    t = nx + nc
    assert seq % PROJ_TM == 0 and nc == PROJ_TM and seq % (NA_BAND * GRID_W) == 0 and nx % n_ctx == 0
    alpha = (2 * depth) ** 0.25

    def mod_row(i, tm):
        return jnp.minimum(i // (seq // tm), bsz)

    def rope_blk(i, tm):
        return jnp.where(i < nx // tm, i % (seq // tm), seq // tm)

    stream = (x.reshape(nx, d), ctx.reshape(nc, d), 0)
    cc = jnp.concatenate([c, c_ctx[None], jnp.zeros((8 - bsz - 1, d), F32)], axis=0)
    mod_all = _ada_call(cc, w_ada, b_ada).reshape(depth, 8, 6, d)
    cos_a, sin_a, cos_b, sin_b = _rope_tables(seq, ROW_TM)
    w_in_p = _permute_w_in(w_in)

    attn_kw = dict(bsz=bsz, seq=seq, n_ctx=n_ctx)
    for i in range(depth):
        last = i == depth - 1
        n_rows = nx if last else t
        mod = mod_all[i]
        p = _proj_call(stream, nx, t, mod, w_in_p, i, mod_row)

        qa, ka, qb, kb, vb = _prep_ab_call(
            p, (cos_a, sin_a, cos_b, sin_b), gqa_q_norm[i][None], gqa_k_norm[i][None],
            mla_q_norm[i][None], mla_kv_norm[i][None],
            _permute_w_uq(mla_w_uq[i]), _permute_w_ukv(mla_w_ukv[i]), rope_blk)
        a_kw = dict(n_heads=GQA_HEADS, dq=HEAD_DIM, q_col=lambda h: h, k_col=lambda h: h // GQA_GROUP,
                    v_col=lambda h: COL_AV // HEAD_DIM + h // GQA_GROUP, **attn_kw)
        o_a = _dense_attn_call(qa, ka, p, **a_kw)
        b_kw = dict(n_heads=MLA_HEADS, dq=MLA_PAD, q_col=lambda h: h, k_col=lambda h: h, v_col=lambda h: h, **attn_kw)
        o_b = _dense_attn_call(qb, kb, vb, **b_kw)
        o_c = _na_call(p, _na_bias_tables(na_rpb[i], seq), **attn_kw)
        o_lat = (o_a, o_b, o_c)
        if last:
            o_ctx = o_lat
        else:
            c_kw = dict(n_heads=NA_HEADS, dq=HEAD_DIM, q_col=lambda h: COL_CQ // HEAD_DIM + h,
                        k_col=lambda h: COL_CK // HEAD_DIM + h, v_col=lambda h: COL_CV // HEAD_DIM + h, **attn_kw)
            o_ctx = (_ctx_attn_call(qa, ka, p, **a_kw), _ctx_attn_call(qb, kb, vb, **b_kw),
                     _ctx_attn_call(p, p, p, **c_kw))

        mrg = _branch_call(o_lat, o_ctx, p, w_branch_a[i].astype(BF16), w_branch_b[i].astype(BF16),
                           w_branch_c[i].astype(BF16), n_rows)
        wr = jnp.concatenate([w_router_group[i], w_router_expert[i],
                              jnp.zeros((d, LANES - N_GROUPS - N_EXPERTS), F32)], axis=1)
        br = jnp.concatenate([b_router_group[i], b_router_expert[i],
                              jnp.zeros((LANES - N_GROUPS - N_EXPERTS,), F32)])[None]
        wr_cat = jnp.concatenate(_split_bf16(wr), axis=1)
        xs, u_f, route = _out_call(mrg, stream, nx, mod, w_out[i].astype(BF16), ln1_g[i][None], ln1_b[i][None],
                                   wr_cat, br, mod_row, n_rows, alpha)

        rank_slab, cnt = _rank_call(route)
        eid = route[:, :TOP_K].astype(jnp.int32)
        counts = cnt[0, :N_EXPERTS].astype(jnp.int32)
        padded = ((counts + MOE_BM - 1) // MOE_BM) * MOE_BM
        pends = jnp.cumsum(padded)
        pstarts = pends - padded
        pstarts_row = jnp.concatenate([pstarts.astype(F32), jnp.zeros((LANES - N_EXPERTS,), F32)])[None]
        dest = _dest_call(route, rank_slab, pstarts_row)[:, :TOP_K].astype(jnp.int32)
        nb = -(-(n_rows * TOP_K) // MOE_BM) + N_EXPERTS
        n_used = (pends[-1] // MOE_BM).astype(jnp.int32)
        blk = jnp.minimum(jnp.arange(nb, dtype=jnp.int32), n_used - 1)
        block_e = jnp.clip(jnp.searchsorted(pends, blk * MOE_BM, side="right"), 0, N_EXPERTS - 1).astype(jnp.int32)
        order = jnp.argsort(eid.reshape(-1), stable=True).astype(jnp.int32)
        starts = jnp.cumsum(counts) - counts
        slot = jnp.arange(nb * MOE_BM, dtype=jnp.int32).reshape(nb, MOE_BM)
        blk_shift = (starts - pstarts)[block_e][:, None]
        blk_limit = (pstarts + counts)[block_e][:, None]
        pos = jnp.clip(slot + blk_shift, 0, n_rows * TOP_K - 1)
        slot_tok = jnp.where(slot < blk_limit, order.at[pos].get(mode="promise_in_bounds") // TOP_K,
                             slot % n_rows).reshape(-1)
        u_sorted = u_f.at[slot_tok].get(mode="promise_in_bounds")
        jj = jnp.arange(nb, dtype=jnp.int32)
        first = ((jj == 0) | (block_e != jnp.roll(block_e, 1))).astype(jnp.int32)
        parity = (jnp.cumsum(first) - 1) % 2
        live = jnp.where(counts > 0, jnp.arange(N_EXPERTS, dtype=jnp.int32), N_EXPERTS)
        next_live = jnp.concatenate([lax.cummin(live, reverse=True)[1:], jnp.full((1,), N_EXPERTS, jnp.int32)])
        nxt = jnp.where(next_live < N_EXPERTS, next_live, -1)[block_e]
        y_sorted = _moe_call(block_e, first, nxt.astype(jnp.int32), parity.astype(jnp.int32), n_used[None],
                             u_sorted, w_expert_gate, w_expert_up, w_expert_down, i)
        y_pair = y_sorted.at[dest.T.reshape(-1)].get(mode="promise_in_bounds")
        xs = _combine_call(xs, y_pair, route, mod, ln2_g[i][None], ln2_b[i][None], mod_row, alpha)
        stream = (xs, xs, nx)
    return xs.reshape(bsz, seq, d)
```

```python
import functools
import math

import jax
import jax.numpy as jnp
import numpy as np
from jax import lax
from jax.experimental import pallas as pl
from jax.experimental.pallas import tpu as pltpu

F32 = jnp.float32
BF16 = jnp.bfloat16

HEAD_DIM = 128
GRID_W = 64
ROPE_THETA = 10000.0
GQA_HEADS, GQA_KV_HEADS = 6, 2
GQA_GROUP = GQA_HEADS // GQA_KV_HEADS
MLA_HEADS, MLA_Q_RANK, MLA_KV_RANK = 5, 512, 256
MLA_NOPE, MLA_ROPE, MLA_V = 128, 64, 128
MLA_QK = MLA_NOPE + MLA_ROPE
MLA_PAD = 256
NA_HEADS, NA_KH, NA_KW = 5, 8, 16
N_GROUPS, EXPERTS_PER_GROUP, TOP_K, D_EXPERT = 8, 8, 2, 512
N_EXPERTS = N_GROUPS * EXPERTS_PER_GROUP
LN_EPS = 1e-6
RMS_EPS = 1e-6
NEG_BIG = -1e30
SCORE_LOG2E = math.log2(math.e)

LANES = 128
VMEM_LIMIT = 56 * 1024 * 1024
PROJ_TM, PROJ_TN = 1024, 1024
ROW_TM = 512
MM_TM = 512
ATTN_TQ = 512
ATTN_SUB = 8
ATTN_KC = 1024
NA_ROWS = 8
NA_SUB = 4
NA_BAND = 16
NA_WIN = 12
MOE_BM = 256
MOE_DMA_CHUNKS = 4
ADA_TN = 512

COL_AQ, COL_AK = 0, 768
COL_BQ, COL_BKV, COL_BKR = 1024, 1536, 1792
COL_AV, COL_CQ, COL_CK, COL_CV = 1920, 2176, 2816, 3456
COL_GATE = 4096
PROJ_W = 10240


def _cparams(sem):
    return pltpu.CompilerParams(dimension_semantics=sem, vmem_limit_bytes=VMEM_LIMIT)


def _resident(shape):
    return pl.BlockSpec(shape, lambda *_: (0,) * len(shape), pipeline_mode=pl.Buffered(1))


def _dot(a, b):
    return jnp.dot(a, b, preferred_element_type=F32)


def _dot_nt(a, b):
    return lax.dot_general(a, b, (((1,), (1,)), ((), ())), preferred_element_type=F32)


def _split_bf16(a):
    hi = a.astype(BF16)
    lo = (a - hi.astype(F32)).astype(BF16)
    return hi, lo


def _ada_kernel(c_ref, w_ref, b_ref, o_ref):
    a = c_ref[...]
    a = a * jax.nn.sigmoid(a)
    a_hi, a_lo = _split_bf16(a)
    w_hi, w_lo = _split_bf16(w_ref[0])
    acc = _dot(a_hi, w_hi) + _dot(a_lo, w_hi) + _dot(a_hi, w_lo)
    o_ref[0] = acc + b_ref[0]


def _ada_call(cc, w_ada, b_ada):
    depth, d, n = w_ada.shape
    rows = cc.shape[0]
    return pl.pallas_call(
        _ada_kernel,
        grid=(depth, n // ADA_TN),
        in_specs=[pl.BlockSpec((rows, d), lambda l, j: (0, 0)),
                  pl.BlockSpec((1, d, ADA_TN), lambda l, j: (l, 0, j)),
                  pl.BlockSpec((1, 1, ADA_TN), lambda l, j: (l, 0, j))],
        out_specs=pl.BlockSpec((1, rows, ADA_TN), lambda l, j: (l, 0, j)),
        out_shape=jax.ShapeDtypeStruct((depth, rows, n), F32),
        compiler_params=_cparams(("parallel", "parallel")),
        name="ada_modulation",
    )(cc, w_ada, b_ada.reshape(depth, 1, n))


def _proj_kernel(xl_ref, xc_ref, mod_ref, w_ref, o_ref, u_scr, *, n_lat_tiles):
    @pl.when(pl.program_id(1) == 0)
    def _():
        def modulate(x):
            u_scr[...] = (x * (1.0 + mod_ref[0, 1:2, :]) + mod_ref[0, 0:1, :]).astype(BF16)

        @pl.when(pl.program_id(0) < n_lat_tiles)
        def _():
            modulate(xl_ref[...])

        @pl.when(pl.program_id(0) >= n_lat_tiles)
        def _():
            modulate(xc_ref[...])

    o_ref[...] = _dot_nt(u_scr[...], w_ref[0]).astype(BF16)


def _stream_specs(stream, tm, n_lat):
    lat, ctx, ctx_row0 = stream
    d = lat.shape[1]
    n_lat_tiles = n_lat // tm
    ctx_tile0 = ctx_row0 // tm

    def lat_map(i, *_):
        return (jnp.minimum(i, n_lat_tiles - 1), 0)

    def ctx_map(i, *_):
        return (ctx_tile0 + jnp.maximum(i - n_lat_tiles, 0), 0)

    return [pl.BlockSpec((tm, d), lat_map), pl.BlockSpec((tm, d), ctx_map, pipeline_mode=pl.Buffered(1))]


def _proj_call(stream, n_lat, n_tok, mod, w_in_p, layer, mod_row):
    d = stream[0].shape[1]
    return pl.pallas_call(
        functools.partial(_proj_kernel, n_lat_tiles=n_lat // PROJ_TM),
        grid=(n_tok // PROJ_TM, PROJ_W // PROJ_TN),
        in_specs=_stream_specs(stream, PROJ_TM, n_lat) + [
            pl.BlockSpec((1, 6, d), lambda i, j: (mod_row(i, PROJ_TM), 0, 0)),
            pl.BlockSpec((1, PROJ_TN, d), lambda i, j: (layer, j, 0))],
        out_specs=pl.BlockSpec((PROJ_TM, PROJ_TN), lambda i, j: (i, j)),
        out_shape=jax.ShapeDtypeStruct((n_tok, PROJ_W), BF16),
        scratch_shapes=[pltpu.VMEM((PROJ_TM, d), BF16)],
        compiler_params=_cparams(("parallel", "arbitrary")),
        name="input_projection",
    )(stream[0], stream[1], mod, w_in_p)


def _rope128(y, cos, sin_signed, first_quarter):
    rot = jnp.where(first_quarter, pltpu.roll(y, 96, 1), pltpu.roll(y, 32, 1))
    return y * cos + rot * sin_signed


def _prep_a_kernel(p_ref, cos_ref, sin_ref, gq_ref, gk_ref, q_ref, k_ref, *, scale):
    cos = cos_ref[...]
    sin_signed = sin_ref[...]
    lane = lax.broadcasted_iota(jnp.int32, cos.shape, 1)
    first_quarter = (lane & 32) == 0
    for h in range(GQA_HEADS + GQA_KV_HEADS):
        xh = p_ref[:, h * HEAD_DIM:(h + 1) * HEAD_DIM].astype(F32)
        r = lax.rsqrt(jnp.mean(xh * xh, axis=1, keepdims=True) + RMS_EPS)
        if h < GQA_HEADS:
            y = _rope128(xh * r * gq_ref[...], cos, sin_signed, first_quarter) * scale
            q_ref[:, h * HEAD_DIM:(h + 1) * HEAD_DIM] = y.astype(BF16)
        else:
            hk = h - GQA_HEADS
            y = _rope128(xh * r * gk_ref[...], cos, sin_signed, first_quarter)
            k_ref[:, hk * HEAD_DIM:(hk + 1) * HEAD_DIM] = y.astype(BF16)


def _prep_b_kernel(ql_ref, kvl_ref, kr_ref, cos_ref, sin_ref, gq_ref, gkv_ref, wq_ref, wkv_ref,
                   q_ref, k_ref, v_ref, *, scale):
    nh, hp = MLA_HEADS, MLA_PAD
    wn = nh * MLA_NOPE
    cos = cos_ref[...]
    sin = sin_ref[...]
    ql = ql_ref[...].astype(F32)
    ql = ql * lax.rsqrt(jnp.mean(ql * ql, axis=1, keepdims=True) + RMS_EPS) * gq_ref[...]
    qf = _dot(ql.astype(BF16), wq_ref[...])
    kvl = kvl_ref[...].astype(F32)
    kvl = kvl * lax.rsqrt(jnp.mean(kvl * kvl, axis=1, keepdims=True) + RMS_EPS) * gkv_ref[...]
    kvf = _dot(kvl.astype(BF16), wkv_ref[...])
    lane = lax.broadcasted_iota(jnp.int32, cos.shape, 1)
    low = lane < MLA_ROPE
    tk = kr_ref[...].astype(F32) * jnp.where(low, cos, sin)
    kr = jnp.where(low, tk + pltpu.roll(tk, MLA_ROPE, 1), 0.0).astype(BF16)
    for h in range(nh):
        q_ref[:, h * hp:h * hp + LANES] = (qf[:, h * LANES:(h + 1) * LANES] * scale).astype(BF16)
        qr = qf[:, wn + h * LANES:wn + (h + 1) * LANES] * cos + qf[:, 2 * wn + h * LANES:2 * wn + (h + 1) * LANES] * sin
        q_ref[:, h * hp + LANES:(h + 1) * hp] = (qr * scale).astype(BF16)
        k_ref[:, h * hp:h * hp + LANES] = kvf[:, h * LANES:(h + 1) * LANES].astype(BF16)
        k_ref[:, h * hp + LANES:(h + 1) * hp] = kr
    v_ref[...] = kvf[:, wn:].astype(BF16)


def _prep_ab_kernel(pa_ref, cos_a_ref, sin_a_ref, gqa_ref, gka_ref,
                    ql_ref, kvl_ref, kr_ref, cos_b_ref, sin_b_ref, gqb_ref, gkvb_ref, wq_ref, wkv_ref,
                    qa_ref, ka_ref, qb_ref, kb_ref, vb_ref, *, scale_a, scale_b):
    _prep_a_kernel(pa_ref, cos_a_ref, sin_a_ref, gqa_ref, gka_ref, qa_ref, ka_ref, scale=scale_a)
    _prep_b_kernel(ql_ref, kvl_ref, kr_ref, cos_b_ref, sin_b_ref, gqb_ref, gkvb_ref, wq_ref, wkv_ref,
                   qb_ref, kb_ref, vb_ref, scale=scale_b)


def _prep_ab_call(p, tabs, gqa, gka, gqb, gkvb, wq, wkv, rope_blk):
    t = p.shape[0]
    tm = ROW_TM
    nh = MLA_HEADS
    wqa, wka = GQA_HEADS * HEAD_DIM, GQA_KV_HEADS * HEAD_DIM
    cos_a, sin_a, cos_b, sin_b = tabs

    def rope_spec():
        return pl.BlockSpec((tm, LANES), lambda i: (rope_blk(i, tm), 0))

    return pl.pallas_call(
        functools.partial(_prep_ab_kernel, scale_a=SCORE_LOG2E / math.sqrt(HEAD_DIM),
                          scale_b=SCORE_LOG2E / math.sqrt(MLA_QK)),
        grid=(t // tm,),
        in_specs=[pl.BlockSpec((tm, wqa + wka), lambda i: (i, 0)), rope_spec(), rope_spec(),
                  _resident((1, HEAD_DIM)), _resident((1, HEAD_DIM)),
                  pl.BlockSpec((tm, MLA_Q_RANK), lambda i: (i, COL_BQ // MLA_Q_RANK)),
                  pl.BlockSpec((tm, MLA_KV_RANK), lambda i: (i, COL_BKV // MLA_KV_RANK)),
                  pl.BlockSpec((tm, LANES), lambda i: (i, COL_BKR // LANES)), rope_spec(), rope_spec(),
                  _resident((1, MLA_Q_RANK)), _resident((1, MLA_KV_RANK)),
                  _resident(wq.shape), _resident(wkv.shape)],
        out_specs=[pl.BlockSpec((tm, wqa), lambda i: (i, 0)),
                   pl.BlockSpec((tm, wka), lambda i: (i, 0)),
                   pl.BlockSpec((tm, nh * MLA_PAD), lambda i: (i, 0)),
                   pl.BlockSpec((tm, nh * MLA_PAD), lambda i: (i, 0)),
                   pl.BlockSpec((tm, nh * MLA_V), lambda i: (i, 0))],
        out_shape=[jax.ShapeDtypeStruct((t, wqa), BF16), jax.ShapeDtypeStruct((t, wka), BF16),
                   jax.ShapeDtypeStruct((t, nh * MLA_PAD), BF16),
                   jax.ShapeDtypeStruct((t, nh * MLA_PAD), BF16),
                   jax.ShapeDtypeStruct((t, nh * MLA_V), BF16)],
        compiler_params=_cparams(("parallel",)),
        name="qk_prep",
    )(p, cos_a, sin_a, gqa, gka, p, p, p, cos_b, sin_b, gqb, gkvb, wq, wkv)


def _online_attend(q, chunks):
    m = z = acc = None
    for k, v, bias in chunks:
        s = _dot_nt(q, k)
        if bias is not None:
            s = s + bias
        cm = jnp.max(s, axis=1, keepdims=True)
        if m is None:
            m = cm
            p = jnp.exp2(s - m)
            z = jnp.sum(p, axis=1, keepdims=True)
            acc = _dot(p.astype(BF16), v)
        else:
            m_new = jnp.maximum(m, cm)
            corr = jnp.exp2(m - m_new)
            p = jnp.exp2(s - m_new)
            z = z * corr + jnp.sum(p, axis=1, keepdims=True)
            acc = acc * corr + _dot(p.astype(BF16), v)
            m = m_new
    return acc / z


def _online_attend_ones(q, chunks, dv):
    m = acc = None
    for k, v_ones, bias in chunks:
        s = _dot_nt(q, k)
        if bias is not None:
            s = s + bias
        cm = jnp.max(s, axis=1, keepdims=True)
        if m is None:
            m = cm
            acc = _dot(jnp.exp2((s - m).astype(BF16)), v_ones)
        else:
            m_new = jnp.maximum(m, cm)
            corr = jnp.exp2(m - m_new)
            acc = acc * corr + _dot(jnp.exp2((s - m_new).astype(BF16)), v_ones)
            m = m_new
    return acc[:, :dv] / acc[:, dv:dv + 1]


def _dense_attn_kernel(q_ref, kl_ref, vl_ref, kc_ref, vc_ref, o_ref, v_ones):
    n_lat = kl_ref.shape[0]
    dv = vl_ref.shape[1]

    @pl.when(pl.program_id(2) == 0)
    def _():
        v_ones[0:n_lat, 0:dv] = vl_ref[...]
        v_ones[n_lat:, 0:dv] = vc_ref[...]
        v_ones[:, dv:] = jnp.ones((v_ones.shape[0], dv), BF16)

    chunks = [(kl_ref[c:c + ATTN_KC, :], v_ones[c:c + ATTN_KC, :], None) for c in range(0, n_lat, ATTN_KC)]
    chunks.append((kc_ref[...], v_ones[n_lat:, :], None))
    for r in range(0, q_ref.shape[0], ATTN_TQ):
        o_ref[r:r + ATTN_TQ, :] = _online_attend_ones(q_ref[r:r + ATTN_TQ, :], chunks, dv).astype(o_ref.dtype)


def _dense_attn_call(q, k, v, *, n_heads, dq, q_col, k_col, v_col, bsz, seq, n_ctx):
    tq = ATTN_TQ * ATTN_SUB
    tpb = seq // tq
    ctx0 = bsz * seq // n_ctx
    dv = HEAD_DIM
    return pl.pallas_call(
        _dense_attn_kernel,
        grid=(bsz, n_heads, tpb),
        in_specs=[pl.BlockSpec((tq, dq), lambda b, h, i: (b * tpb + i, q_col(h))),
                  pl.BlockSpec((seq, dq), lambda b, h, i: (b, k_col(h))),
                  pl.BlockSpec((seq, dv), lambda b, h, i: (b, v_col(h))),
                  pl.BlockSpec((n_ctx, dq), lambda b, h, i: (ctx0 + b, k_col(h))),
                  pl.BlockSpec((n_ctx, dv), lambda b, h, i: (ctx0 + b, v_col(h)))],
        out_specs=pl.BlockSpec((tq, dv), lambda b, h, i: (b * tpb + i, h)),
        out_shape=jax.ShapeDtypeStruct((bsz * seq, n_heads * dv), BF16),
        scratch_shapes=[pltpu.VMEM((seq + n_ctx, 2 * dv), BF16)],
        compiler_params=_cparams(("parallel", "parallel", "arbitrary")),
        name="dense_attention",
    )(q, k, v, k, v)


def _ctx_attn_kernel(q_ref, kc_ref, vc_ref, o_ref):
    o_ref[...] = _online_attend(q_ref[...], [(kc_ref[...], vc_ref[...], None)]).astype(o_ref.dtype)


def _ctx_attn_call(q, k, v, *, n_heads, dq, q_col, k_col, v_col, bsz, seq, n_ctx):
    ctx0 = bsz * seq // n_ctx
    dv = HEAD_DIM
    return pl.pallas_call(
        _ctx_attn_kernel,
        grid=(bsz, n_heads),
        in_specs=[pl.BlockSpec((n_ctx, dq), lambda b, h: (ctx0 + b, q_col(h))),
                  pl.BlockSpec((n_ctx, dq), lambda b, h: (ctx0 + b, k_col(h))),
                  pl.BlockSpec((n_ctx, dv), lambda b, h: (ctx0 + b, v_col(h)))],
        out_specs=pl.BlockSpec((n_ctx, dv), lambda b, h: (b, h)),
        out_shape=jax.ShapeDtypeStruct((bsz * n_ctx, n_heads * dv), BF16),
        compiler_params=_cparams(("parallel", "parallel")),
        name="context_attention",
    )(q, k, v)


def _na_window_offsets(variant, n_variants):
    shift = NA_BAND - NA_WIN
    return ((0, 0), (0, shift), (shift, shift))[0 if variant == 0 else (2 if variant == n_variants - 1 else 1)]


def _na_kernel(q_ref, k_ref, v_ref, kc_ref, vc_ref, *rest, rows_total, n_tiles):
    bias_refs, o_ref, v_ones = rest[:NA_SUB], rest[NA_SUB], rest[NA_SUB + 1]
    n_lat = k_ref.shape[0]
    dv = v_ref.shape[1]
    tile = NA_ROWS * GRID_W
    half = tile // 2
    win = NA_WIN * GRID_W
    shift = (NA_BAND - NA_WIN) * GRID_W

    @pl.when(pl.program_id(2) == 0)
    def _():
        v_ones[0:n_lat, 0:dv] = v_ref[...]
        v_ones[n_lat:, 0:dv] = vc_ref[...]
        v_ones[:, dv:] = jnp.ones((v_ones.shape[0], dv), BF16)

    ctx = (kc_ref[...], v_ones[n_lat:, :], None)
    for t in range(NA_SUB):
        i = pl.program_id(2) * NA_SUB + t
        start_blk = jnp.clip(2 * i - 1, 0, (rows_total - NA_BAND) // 4)
        band0 = start_blk * (4 * GRID_W)
        starts = (band0 + jnp.where(i == n_tiles - 1, shift, 0), band0 + jnp.where(i == 0, 0, shift))
        for hq, start in enumerate(starts):
            start = pl.multiple_of(start, 4 * GRID_W)
            rows = slice(t * tile + hq * half, t * tile + (hq + 1) * half)
            bias = bias_refs[t][0, 0, hq * half:(hq + 1) * half, :]
            chunks = [(k_ref[pl.ds(start, win), :], v_ones[pl.ds(start, win), :], bias), ctx]
            o_ref[rows, :] = _online_attend_ones(q_ref[rows, :], chunks, dv).astype(o_ref.dtype)


def _na_call(p, bias_tab, *, bsz, seq, n_ctx):
    tile = NA_ROWS * GRID_W
    n_tiles = seq // tile
    tq = tile * NA_SUB
    tpb = seq // tq
    rows_total = seq // GRID_W
    ctx0 = bsz * seq // n_ctx
    d = HEAD_DIM
    qc, kc, vc = COL_CQ // d, COL_CK // d, COL_CV // d

    def bias_spec(t):
        def index(b, h, i):
            tile_idx = i * NA_SUB + t
            return (h, jnp.where(tile_idx == 0, 0, jnp.where(tile_idx == n_tiles - 1, 2, 1)), 0, 0)
        return pl.BlockSpec((1, 1, tile, NA_WIN * GRID_W), index)

    return pl.pallas_call(
        functools.partial(_na_kernel, rows_total=rows_total, n_tiles=n_tiles),
        grid=(bsz, NA_HEADS, tpb),
        in_specs=[pl.BlockSpec((tq, d), lambda b, h, i: (b * tpb + i, qc + h)),
                  pl.BlockSpec((seq, d), lambda b, h, i: (b, kc + h)),
                  pl.BlockSpec((seq, d), lambda b, h, i: (b, vc + h)),
                  pl.BlockSpec((n_ctx, d), lambda b, h, i: (ctx0 + b, kc + h)),
                  pl.BlockSpec((n_ctx, d), lambda b, h, i: (ctx0 + b, vc + h))]
                 + [bias_spec(t) for t in range(NA_SUB)],
        out_specs=pl.BlockSpec((tq, d), lambda b, h, i: (b * tpb + i, h)),
        out_shape=jax.ShapeDtypeStruct((bsz * seq, NA_HEADS * d), BF16),
        scratch_shapes=[pltpu.VMEM((seq + n_ctx, 2 * d), BF16)],
        compiler_params=_cparams(("parallel", "parallel", "arbitrary")),
        name="neighbourhood_attention",
    )(p, p, p, p, p, *([bias_tab] * NA_SUB))


def _branch_kernel(oa_ref, ob_ref, oc_ref, oac_ref, obc_ref, occ_ref, ga_ref, gb_ref, gc_ref,
                   wa_ref, wb_ref, wc_ref, o_ref, *, n_lat_tiles):
    def gated(g_ref, w_ref, o_tile):
        return jax.nn.sigmoid(g_ref[...].astype(F32)) * _dot(o_tile, w_ref[...])

    @pl.when(pl.program_id(0) < n_lat_tiles)
    def _():
        acc = gated(ga_ref, wa_ref, oa_ref[...]) + gated(gb_ref, wb_ref, ob_ref[...]) + gated(gc_ref, wc_ref, oc_ref[...])
        o_ref[...] = acc.astype(BF16)

    @pl.when(pl.program_id(0) >= n_lat_tiles)
    def _():
        acc = gated(ga_ref, wa_ref, oac_ref[...]) + gated(gb_ref, wb_ref, obc_ref[...]) + gated(gc_ref, wc_ref, occ_ref[...])
        o_ref[...] = acc.astype(BF16)


def _branch_call(o_lat, o_ctx, p, wa, wb, wc, n_rows):
    d = wa.shape[1]
    tm = MM_TM
    g0 = COL_GATE // d
    n_lat_tiles = o_lat[0].shape[0] // tm
    lat_specs = [pl.BlockSpec((tm, o.shape[1]), lambda i: (jnp.minimum(i, n_lat_tiles - 1), 0)) for o in o_lat]
    ctx_specs = [pl.BlockSpec((tm, o.shape[1]), lambda i: (jnp.maximum(i - n_lat_tiles, 0), 0)) for o in o_ctx]
    return pl.pallas_call(
        functools.partial(_branch_kernel, n_lat_tiles=n_lat_tiles),
        grid=(n_rows // tm,),
        in_specs=lat_specs + ctx_specs + [
            pl.BlockSpec((tm, d), lambda i: (i, g0)),
            pl.BlockSpec((tm, d), lambda i: (i, g0 + 1)),
            pl.BlockSpec((tm, d), lambda i: (i, g0 + 2)),
            _resident(wa.shape), _resident(wb.shape), _resident(wc.shape)],
        out_specs=pl.BlockSpec((tm, d), lambda i: (i, 0)),
        out_shape=jax.ShapeDtypeStruct((n_rows, d), BF16),
        compiler_params=_cparams(("parallel",)),
        name="branch_merge",
    )(*o_lat, *o_ctx, p, p, p, wa, wb, wc)


def _layer_norm(z, g, b):
    mu = jnp.mean(z, axis=1, keepdims=True)
    zc = z - mu
    var = jnp.mean(zc * zc, axis=1, keepdims=True)
    return zc * lax.rsqrt(var + LN_EPS) * g + b


def _route(logits):
    lane = lax.broadcasted_iota(jnp.int32, logits.shape, 1)
    lane_f = lane.astype(F32)
    is_g = lane < N_GROUPS
    gl = jnp.where(is_g, logits, NEG_BIG)
    mg = jnp.max(gl, axis=1, keepdims=True)
    gsel = jnp.min(jnp.where(gl == mg, lane_f, float(LANES)), axis=1, keepdims=True)
    zg = jnp.sum(jnp.where(is_g, jnp.exp(gl - mg), 0.0), axis=1, keepdims=True)
    lo = N_GROUPS + gsel * EXPERTS_PER_GROUP
    is_e = (lane_f >= lo) & (lane_f < lo + EXPERTS_PER_GROUP)
    el = jnp.where(is_e, logits, NEG_BIG)
    t1 = jnp.max(el, axis=1, keepdims=True)
    i1 = jnp.min(jnp.where(el == t1, lane_f, float(LANES)), axis=1, keepdims=True)
    el2 = jnp.where(lane_f == i1, NEG_BIG, el)
    t2 = jnp.max(el2, axis=1, keepdims=True)
    i2 = jnp.min(jnp.where(el2 == t2, lane_f, float(LANES)), axis=1, keepdims=True)
    dd = jnp.exp(t2 - t1)
    g1 = 1.0 / (zg * (1.0 + dd))
    g2 = g1 * dd
    slab = jnp.where(lane == 0, i1 - N_GROUPS,
                     jnp.where(lane == 1, i2 - N_GROUPS,
                               jnp.where(lane == 2, g1, jnp.where(lane == 3, g2, 0.0))))
    return slab


def _out_kernel(m_ref, xl_ref, xc_ref, mod_ref, w_ref, lg_ref, lb_ref, wr_ref, br_ref,
                xo_ref, u_ref, r_ref, *, alpha, n_lat_tiles):
    @pl.when(pl.program_id(0) < n_lat_tiles)
    def _():
        xo_ref[...] = xl_ref[...]

    @pl.when(pl.program_id(0) >= n_lat_tiles)
    def _():
        xo_ref[...] = xc_ref[...]

    gate_m = mod_ref[0, 2:3, :]
    shift_f = mod_ref[0, 3:4, :]
    scale_f = mod_ref[0, 4:5, :]
    y = _dot(m_ref[...], w_ref[...])
    xn = _layer_norm(alpha * xo_ref[...] + gate_m * y, lg_ref[...], lb_ref[...])
    xo_ref[...] = xn
    u = xn * (1.0 + scale_f) + shift_f
    u_hi, u_lo = _split_bf16(u)
    u_ref[...] = u_hi
    both = _dot(u_hi, wr_ref[...])
    logits = both[:, :LANES] + both[:, LANES:] + _dot(u_lo, wr_ref[:, :LANES]) + br_ref[...]
    r_ref[...] = _route(logits)


def _out_call(mrg, stream, n_lat, mod, w_out, ln_g, ln_b, wr_cat, br, mod_row, n_rows, alpha):
    d = stream[0].shape[1]
    tm = MM_TM
    return pl.pallas_call(
        functools.partial(_out_kernel, alpha=alpha, n_lat_tiles=n_lat // tm),
        grid=(n_rows // tm,),
        in_specs=[pl.BlockSpec((tm, d), lambda i: (i, 0))] + _stream_specs(stream, tm, n_lat) + [
                  pl.BlockSpec((1, 6, d), lambda i: (mod_row(i, tm), 0, 0)),
                  _resident((d, d)), _resident((1, d)), _resident((1, d)),
                  _resident((d, 2 * LANES)), _resident((1, LANES))],
        out_specs=[pl.BlockSpec((tm, d), lambda i: (i, 0)),
                   pl.BlockSpec((tm, d), lambda i: (i, 0)),
                   pl.BlockSpec((tm, LANES), lambda i: (i, 0))],
        out_shape=[jax.ShapeDtypeStruct((n_rows, d), F32),
                   jax.ShapeDtypeStruct((n_rows, d), BF16),
                   jax.ShapeDtypeStruct((n_rows, LANES), F32)],
        compiler_params=_cparams(("parallel",)),
        name="out_proj_ln_route",
    )(mrg, stream[0], stream[1], mod, w_out, ln_g, ln_b, wr_cat, br)


def _rank_kernel(r_ref, rank_ref, cnt_ref, carry):
    @pl.when(pl.program_id(0) == 0)
    def _():
        carry[...] = jnp.zeros_like(carry)

    slab = r_ref[...]
    tm = slab.shape[0]
    lane = lax.broadcasted_iota(jnp.int32, slab.shape, 1)
    lane_f = lane.astype(F32)
    e1 = slab[:, 0:1]
    e2 = slab[:, 1:2]
    hit1 = lane_f == e1
    hit2 = lane_f == e2
    onehot = jnp.where(hit1 | hit2, 1.0, 0.0)
    row = lax.broadcasted_iota(jnp.int32, (tm, tm), 0)
    col = lax.broadcasted_iota(jnp.int32, (tm, tm), 1)
    lower = jnp.where(col < row, 1.0, 0.0).astype(BF16)
    before = _dot(lower, onehot.astype(BF16)) + carry[0:1, :]
    r1 = jnp.sum(jnp.where(hit1, before, 0.0), axis=1, keepdims=True)
    r2 = jnp.sum(jnp.where(hit2, before, 0.0), axis=1, keepdims=True)
    rank_ref[...] = jnp.where(lane == 0, r1, jnp.where(lane == 1, r2, 0.0))
    carry[...] = carry[...] + jnp.sum(onehot, axis=0, keepdims=True)
    cnt_ref[...] = carry[...]


def _rank_call(route):
    n = route.shape[0]
    tm = ROW_TM
    return pl.pallas_call(
        _rank_kernel,
        grid=(n // tm,),
        in_specs=[pl.BlockSpec((tm, LANES), lambda i: (i, 0))],
        out_specs=[pl.BlockSpec((tm, LANES), lambda i: (i, 0)),
                   pl.BlockSpec((8, LANES), lambda i: (0, 0))],
        out_shape=[jax.ShapeDtypeStruct((n, LANES), F32), jax.ShapeDtypeStruct((8, LANES), F32)],
        scratch_shapes=[pltpu.VMEM((8, LANES), F32)],
        compiler_params=_cparams(("arbitrary",)),
        name="expert_ranks",
    )(route)


def _dest_kernel(r_ref, rank_ref, ps_ref, o_ref):
    slab = r_ref[...]
    rank = rank_ref[...]
    lane = lax.broadcasted_iota(jnp.int32, slab.shape, 1)
    lane_f = lane.astype(F32)
    ps = ps_ref[...]
    s1 = jnp.sum(jnp.where(lane_f == slab[:, 0:1], ps, 0.0), axis=1, keepdims=True) + rank[:, 0:1]
    s2 = jnp.sum(jnp.where(lane_f == slab[:, 1:2], ps, 0.0), axis=1, keepdims=True) + rank[:, 1:2]
    o_ref[...] = jnp.where(lane == 0, s1, jnp.where(lane == 1, s2, 0.0))


def _dest_call(route, rank_slab, pstarts_row):
    n = route.shape[0]
    tm = n // 8
    return pl.pallas_call(
        _dest_kernel,
        grid=(n // tm,),
        in_specs=[pl.BlockSpec((tm, LANES), lambda i: (i, 0)),
                  pl.BlockSpec((tm, LANES), lambda i: (i, 0)),
                  _resident((1, LANES))],
        out_specs=pl.BlockSpec((tm, LANES), lambda i: (i, 0)),
        out_shape=jax.ShapeDtypeStruct((n, LANES), F32),
        compiler_params=_cparams(("parallel",)),
        name="expert_slots",
    )(route, rank_slab, pstarts_row)


def _moe_kernel(be_ref, first_ref, nxt_ref, par_ref, nu_ref, x_ref, wg_hbm, wu_hbm, wd_hbm, y_ref,
                gbuf, ubuf, dbuf, sems, *, layer):
    j = pl.program_id(0)

    def weight_copies(e, slot):
        copies = []
        for m, (hbm, buf) in enumerate(((wg_hbm, gbuf), (wu_hbm, ubuf), (wd_hbm, dbuf))):
            band = buf.shape[1] // MOE_DMA_CHUNKS
            for c in range(MOE_DMA_CHUNKS):
                rows = pl.ds(c * band, band)
                copies.append(pltpu.make_async_copy(hbm.at[layer, e, rows], buf.at[slot, rows], sems.at[slot, m]))
        return copies

    @pl.when(j < nu_ref[0])
    def _():
        slot = par_ref[j]

        @pl.when(j == 0)
        def _():
            for cp in weight_copies(be_ref[0], 0):
                cp.start()

        @pl.when(first_ref[j] == 1)
        def _():
            @pl.when(nxt_ref[j] >= 0)
            def _():
                for cp in weight_copies(nxt_ref[j], 1 - slot):
                    cp.start()

            for cp in weight_copies(be_ref[j], slot):
                cp.wait()

        xb = x_ref[...]
        hg = _dot(xb, gbuf[slot].astype(BF16))
        hu = _dot(xb, ubuf[slot].astype(BF16))
        hb = (hg * jax.nn.sigmoid(hg) * hu).astype(BF16)
        y_ref[...] = _dot(hb, dbuf[slot].astype(BF16)).astype(y_ref.dtype)

    @pl.when(j >= nu_ref[0])
    def _():
        y_ref[...] = jnp.zeros_like(y_ref)


def _moe_call(block_e, first, nxt, parity, n_used, u_sorted, w_gate, w_up, w_down, layer):
    n_slots, d = u_sorted.shape
    bm = MOE_BM
    nb = n_slots // bm
    de = w_gate.shape[-1]

    def xmap(j, be, fi, nx_, pa, nu):
        return (jnp.minimum(j, nu[0] - 1), 0)

    grid_spec = pltpu.PrefetchScalarGridSpec(
        num_scalar_prefetch=5,
        grid=(nb,),
        in_specs=[pl.BlockSpec((bm, d), xmap),
                  pl.BlockSpec(memory_space=pl.ANY),
                  pl.BlockSpec(memory_space=pl.ANY),
                  pl.BlockSpec(memory_space=pl.ANY)],
        out_specs=pl.BlockSpec((bm, d), lambda j, *_: (j, 0)),
        scratch_shapes=[pltpu.VMEM((2, d, de), F32), pltpu.VMEM((2, d, de), F32), pltpu.VMEM((2, de, d), F32),
                        pltpu.SemaphoreType.DMA((2, 3))],
    )
    return pl.pallas_call(
        functools.partial(_moe_kernel, layer=layer),
        grid_spec=grid_spec,
        out_shape=jax.ShapeDtypeStruct((n_slots, d), BF16),
        compiler_params=_cparams(("arbitrary",)),
        name="expert_mlp",
    )(block_e, first, nxt, parity, n_used, u_sorted, w_gate, w_up, w_down)


def _combine_kernel(x_ref, y0_ref, y1_ref, r_ref, mod_ref, lg_ref, lb_ref, o_ref, *, alpha):
    slab = r_ref[...]
    g1 = slab[:, 2:3]
    g2 = slab[:, 3:4]
    mx = g1 * y0_ref[...].astype(F32) + g2 * y1_ref[...].astype(F32)
    gate_f = mod_ref[0, 5:6, :]
    o_ref[...] = _layer_norm(alpha * x_ref[...] + gate_f * mx, lg_ref[...], lb_ref[...])


def _combine_call(xs, y_pair, route, mod, ln_g, ln_b, mod_row, alpha):
    n, d = xs.shape
    tm = ROW_TM
    return pl.pallas_call(
        functools.partial(_combine_kernel, alpha=alpha),
        grid=(n // tm,),
        in_specs=[pl.BlockSpec((tm, d), lambda i: (i, 0)),
                  pl.BlockSpec((tm, d), lambda i: (i, 0)),
                  pl.BlockSpec((tm, d), lambda i: (n // tm + i, 0)),
                  pl.BlockSpec((tm, LANES), lambda i: (i, 0)),
                  pl.BlockSpec((1, 6, d), lambda i: (mod_row(i, tm), 0, 0)),
                  pl.BlockSpec((1, d), lambda i: (0, 0)),
                  pl.BlockSpec((1, d), lambda i: (0, 0))],
        out_specs=pl.BlockSpec((tm, d), lambda i: (i, 0)),
        out_shape=jax.ShapeDtypeStruct((n, d), F32),
        compiler_params=_cparams(("parallel",)),
        name="moe_combine_ln",
    )(xs, y_pair, y_pair, route, mod, ln_g, ln_b)


def _rope_angles(seq, dim):
    tpos = np.arange(seq)
    row = (tpos // GRID_W).astype(np.float32)
    col = (tpos % GRID_W).astype(np.float32)
    quarter = dim // 4
    inv_freq = jnp.asarray(ROPE_THETA, F32) ** (-jnp.arange(quarter, dtype=F32) / quarter)
    ang_r = jnp.asarray(row)[:, None] * inv_freq
    ang_c = jnp.asarray(col)[:, None] * inv_freq
    return jnp.concatenate([ang_r, ang_r, ang_c, ang_c], axis=-1)


def _rot_sign(dim):
    l = np.arange(dim)
    return np.where((l & (dim // 4)) == 0, -1.0, 1.0).astype(np.float32), l ^ (dim // 4)


def _rope_tables(seq, pad_rows):
    ang_a = _rope_angles(seq, HEAD_DIM)
    sign_a, _ = _rot_sign(HEAD_DIM)
    cos_a = jnp.concatenate([jnp.cos(ang_a), jnp.ones((pad_rows, HEAD_DIM), F32)], 0)
    sin_a = jnp.concatenate([jnp.sin(ang_a) * sign_a, jnp.zeros((pad_rows, HEAD_DIM), F32)], 0)
    ang_b = _rope_angles(seq, MLA_ROPE)
    cos_b = jnp.concatenate([jnp.cos(ang_b), jnp.ones((pad_rows, MLA_ROPE), F32)], 0)
    sin_b = jnp.concatenate([jnp.sin(ang_b), jnp.zeros((pad_rows, MLA_ROPE), F32)], 0)
    return cos_a, sin_a, jnp.tile(cos_b, (1, 2)), jnp.tile(sin_b, (1, 2))


ORIG_AV, ORIG_BQ, ORIG_BKR, ORIG_CQ, ORIG_W = 1024, 1280, 2048, 2112, 10176
W_PREP_ROWS = MLA_ROPE
W_PREP_GROUP = 8
PREP_COPY, PREP_SCALE, PREP_ROT = 0, 1, 2


def _w_in_prep_kernel(src_ref, kind_ref, *refs, cq_scale):
    del src_ref
    w_refs, o_ref = refs[:W_PREP_GROUP], refs[W_PREP_GROUP]
    quarter = MLA_ROPE // 4
    rb = W_PREP_ROWS
    for k, w_ref in enumerate(w_refs):
        kind = kind_ref[pl.program_id(1) * W_PREP_GROUP + k]
        lo = k * rb

        @pl.when(kind == PREP_COPY)
        def _(w_ref=w_ref, lo=lo):
            o_ref[0, lo:lo + rb] = w_ref[0].astype(BF16)

        @pl.when(kind == PREP_SCALE)
        def _(w_ref=w_ref, lo=lo):
            o_ref[0, lo:lo + rb] = (w_ref[0] * cq_scale).astype(BF16)

        @pl.when(kind == PREP_ROT)
        def _(w_ref=w_ref, lo=lo):
            for r0 in range(0, rb, 2 * quarter):
                o_ref[0, lo + r0:lo + r0 + quarter] = (-w_ref[0, r0 + quarter:r0 + 2 * quarter]).astype(BF16)
                o_ref[0, lo + r0 + quarter:lo + r0 + 2 * quarter] = w_ref[0, r0:r0 + quarter].astype(BF16)


def _permute_w_in(w):
    depth, d, width = w.shape
    assert width == ORIG_W and ORIG_W - ORIG_CQ == PROJ_W - COL_CQ
    rb = W_PREP_ROWS
    src = np.zeros(PROJ_W // rb, np.int32)
    kind = np.full(PROJ_W // rb, PREP_COPY, np.int32)
    for dst0, src0, n in ((COL_AQ, 0, ORIG_AV), (COL_BQ, ORIG_BQ, ORIG_BKR - ORIG_BQ), (COL_BKR, ORIG_BKR, rb),
                          (COL_BKR + rb, ORIG_BKR, rb), (COL_AV, ORIG_AV, ORIG_BQ - ORIG_AV),
                          (COL_CQ, ORIG_CQ, ORIG_W - ORIG_CQ)):
        src[dst0 // rb:(dst0 + n) // rb] = np.arange(src0 // rb, (src0 + n) // rb)
    kind[(COL_BKR + rb) // rb] = PREP_ROT
    kind[COL_CQ // rb:COL_CK // rb] = PREP_SCALE
    wt = jnp.swapaxes(w, 1, 2)
    grp = W_PREP_GROUP

    def src_spec(k):
        return pl.BlockSpec((1, rb, d), lambda l, r, src_ref, kind_ref: (l, src_ref[r * grp + k], 0))

    grid_spec = pltpu.PrefetchScalarGridSpec(
        num_scalar_prefetch=2,
        grid=(depth, PROJ_W // (rb * grp)),
        in_specs=[src_spec(k) for k in range(grp)],
        out_specs=pl.BlockSpec((1, rb * grp, d), lambda l, r, src_ref, kind_ref: (l, r, 0)),
    )
    return pl.pallas_call(
        functools.partial(_w_in_prep_kernel, cq_scale=SCORE_LOG2E / math.sqrt(HEAD_DIM)),
        grid_spec=grid_spec,
        out_shape=jax.ShapeDtypeStruct((depth, PROJ_W, d), BF16),
        compiler_params=_cparams(("parallel", "parallel")),
        name="w_in_prep",
    )(jnp.asarray(src), jnp.asarray(kind), *([wt] * grp))


def _rot_cols(w):
    dim = w.shape[-1]
    wr = w.reshape(w.shape[:-1] + (2, 2, dim // 4))
    return jnp.stack([-wr[..., 1, :], wr[..., 0, :]], axis=-2).reshape(w.shape)


def _permute_w_uq(w):
    r = w.shape[0]
    w3 = w.reshape(r, MLA_HEADS, MLA_QK)
    nope = w3[:, :, :MLA_NOPE].reshape(r, MLA_HEADS * MLA_NOPE)
    rope = w3[:, :, MLA_NOPE:]
    rot = _rot_cols(rope)
    zpad = jnp.zeros((r, MLA_HEADS, LANES - MLA_ROPE), w.dtype)
    rope_p = jnp.concatenate([rope, zpad], -1).reshape(r, MLA_HEADS * LANES)
    rot_p = jnp.concatenate([rot, zpad], -1).reshape(r, MLA_HEADS * LANES)
    return jnp.concatenate([nope, rope_p, rot_p], axis=1).astype(BF16)


def _permute_w_ukv(w):
    r = w.shape[0]
    w3 = w.reshape(r, MLA_HEADS, MLA_NOPE + MLA_V)
    kn = w3[:, :, :MLA_NOPE].reshape(r, MLA_HEADS * MLA_NOPE)
    vv = w3[:, :, MLA_NOPE:].reshape(r, MLA_HEADS * MLA_V)
    return jnp.concatenate([kn, vv], axis=1).astype(BF16)


def _na_bias_tables(rpb, seq):
    w, kh, kw = GRID_W, NA_KH, NA_KW
    rows = seq // w
    nh, n_dr, n_dc = rpb.shape
    line = jnp.full((nh, n_dr, 2 * w), NEG_BIG, F32).at[:, :, w - kw:w - kw + n_dc].set(rpb.astype(F32) * SCORE_LOG2E)
    skew = jnp.broadcast_to(line[:, :, None, :], (nh, n_dr, w, 2 * w)).reshape(nh, n_dr, 2 * w * w)
    skew = skew[:, :, :w * (2 * w - 1)].reshape(nh, n_dr, w, 2 * w - 1)
    tc = skew[:, :, :, w - 1:2 * w - 1]
    cq = np.arange(w)[:, None]
    ck = np.arange(w)[None, :]
    cs = np.clip(cq - kw // 2, 0, w - kw)
    col_ok = (ck >= cs) & (ck < cs + kw)
    tc = jnp.where(col_ok[None, None], tc, NEG_BIG)
    tc = jnp.concatenate([tc, jnp.full((nh, 1, w, w), NEG_BIG, F32)], axis=1)
    tc2 = jnp.concatenate([tc, tc], axis=-1)
    blks = []
    tile_starts = (0, NA_ROWS, rows - NA_ROWS)
    for variant, r0 in enumerate(tile_starts):
        rb = int(np.clip(r0 - kh // 2, 0, rows - NA_BAND))
        rq = r0 + np.arange(NA_ROWS)[:, None]
        rk = rb + np.arange(NA_BAND)[None, :]
        rs = np.clip(rq - kh // 2, 0, rows - kh)
        row_ok = (rk >= rs) & (rk < rs + kh)
        band_blk = np.where(row_ok, rk - rq + kh - 1, n_dr)
        win_blk = []
        for hq, off in enumerate(_na_window_offsets(variant, len(tile_starts))):
            part = band_blk[hq * NA_ROWS // 2:(hq + 1) * NA_ROWS // 2]
            assert (np.delete(part, np.s_[off:off + NA_WIN], axis=1) == n_dr).all()
            win_blk.extend(part[:, off:off + NA_WIN].tolist())
        blks.append(win_blk)
    return pl.pallas_call(
        functools.partial(_na_bias_kernel, blks=blks),
        grid=(nh,),
        in_specs=[pl.BlockSpec((1, n_dr + 1, w, 2 * w), lambda h: (h, 0, 0, 0))],
        out_specs=pl.BlockSpec((1, len(blks), NA_ROWS * w, NA_WIN * w), lambda h: (h, 0, 0, 0)),
        out_shape=jax.ShapeDtypeStruct((nh, len(blks), NA_ROWS * w, NA_WIN * w), F32),
        compiler_params=_cparams(("parallel",)),
        name="na_bias_table",
    )(tc2)


def _na_bias_kernel(tc_ref, o_ref, *, blks):
    w = GRID_W
    low = lax.broadcasted_iota(jnp.int32, (w, 2 * w), 1) < w
    for v, blk in enumerate(blks):
        for rq, row in enumerate(blk):
            for pr in range(len(row) // 2):
                pair = jnp.where(low, tc_ref[0, row[2 * pr]], tc_ref[0, row[2 * pr + 1]])
                o_ref[0, v, rq * w:(rq + 1) * w, pr * 2 * w:(pr + 1) * 2 * w] = pair


def kernel(x, c, ctx, c_ctx, w_ada, b_ada, w_in, gqa_q_norm, gqa_k_norm, mla_q_norm, mla_kv_norm, mla_w_uq, mla_w_ukv, na_rpb, w_branch_a, w_branch_b, w_branch_c, w_out, ln1_g, ln1_b, w_router_group, b_router_group, w_router_expert, b_router_expert, w_expert_gate, w_expert_up, w_expert_down, ln2_g, ln2_b):
    bsz, seq, d = x.shape
    n_ctx = ctx.shape[1]
    depth = w_ada.shape[0]
    nx, nc = bsz * seq, bsz * n_ctx
    t = nx + nc
    assert seq % PROJ_TM == 0 and nc == PROJ_TM and seq % (NA_BAND * GRID_W) == 0 and nx % n_ctx == 0
    alpha = (2 * depth) ** 0.25

    def mod_row(i, tm):
        return jnp.minimum(i // (seq // tm), bsz)

    def rope_blk(i, tm):
        return jnp.where(i < nx // tm, i % (seq // tm), seq // tm)

    stream = (x.reshape(nx, d), ctx.reshape(nc, d), 0)
    cc = jnp.concatenate([c, c_ctx[None], jnp.zeros((8 - bsz - 1, d), F32)], axis=0)
    mod_all = _ada_call(cc, w_ada, b_ada).reshape(depth, 8, 6, d)
    cos_a, sin_a, cos_b, sin_b = _rope_tables(seq, ROW_TM)
    w_in_p = _permute_w_in(w_in)

    attn_kw = dict(bsz=bsz, seq=seq, n_ctx=n_ctx)
    for i in range(depth):
        last = i == depth - 1
        n_rows = nx if last else t
        mod = mod_all[i]
        p = _proj_call(stream, nx, t, mod, w_in_p, i, mod_row)

        qa, ka, qb, kb, vb = _prep_ab_call(
            p, (cos_a, sin_a, cos_b, sin_b), gqa_q_norm[i][None], gqa_k_norm[i][None],
            mla_q_norm[i][None], mla_kv_norm[i][None],
            _permute_w_uq(mla_w_uq[i]), _permute_w_ukv(mla_w_ukv[i]), rope_blk)
        a_kw = dict(n_heads=GQA_HEADS, dq=HEAD_DIM, q_col=lambda h: h, k_col=lambda h: h // GQA_GROUP,
                    v_col=lambda h: COL_AV // HEAD_DIM + h // GQA_GROUP, **attn_kw)
        o_a = _dense_attn_call(qa, ka, p, **a_kw)
        b_kw = dict(n_heads=MLA_HEADS, dq=MLA_PAD, q_col=lambda h: h, k_col=lambda h: h, v_col=lambda h: h, **attn_kw)
        o_b = _dense_attn_call(qb, kb, vb, **b_kw)
        o_c = _na_call(p, _na_bias_tables(na_rpb[i], seq), **attn_kw)
        o_lat = (o_a, o_b, o_c)
        if last:
            o_ctx = o_lat
        else:
            c_kw = dict(n_heads=NA_HEADS, dq=HEAD_DIM, q_col=lambda h: COL_CQ // HEAD_DIM + h,
                        k_col=lambda h: COL_CK // HEAD_DIM + h, v_col=lambda h: COL_CV // HEAD_DIM + h, **attn_kw)
            o_ctx = (_ctx_attn_call(qa, ka, p, **a_kw), _ctx_attn_call(qb, kb, vb, **b_kw),
                     _ctx_attn_call(p, p, p, **c_kw))

        mrg = _branch_call(o_lat, o_ctx, p, w_branch_a[i].astype(BF16), w_branch_b[i].astype(BF16),
                           w_branch_c[i].astype(BF16), n_rows)
        wr = jnp.concatenate([w_router_group[i], w_router_expert[i],
                              jnp.zeros((d, LANES - N_GROUPS - N_EXPERTS), F32)], axis=1)
        br = jnp.concatenate([b_router_group[i], b_router_expert[i],
                              jnp.zeros((LANES - N_GROUPS - N_EXPERTS,), F32)])[None]
        wr_cat = jnp.concatenate(_split_bf16(wr), axis=1)
        xs, u_f, route = _out_call(mrg, stream, nx, mod, w_out[i].astype(BF16), ln1_g[i][None], ln1_b[i][None],
                                   wr_cat, br, mod_row, n_rows, alpha)

        rank_slab, cnt = _rank_call(route)
        eid = route[:, :TOP_K].astype(jnp.int32)
        counts = cnt[0, :N_EXPERTS].astype(jnp.int32)
        padded = ((counts + MOE_BM - 1) // MOE_BM) * MOE_BM
        pends = jnp.cumsum(padded)
        pstarts = pends - padded
        pstarts_row = jnp.concatenate([pstarts.astype(F32), jnp.zeros((LANES - N_EXPERTS,), F32)])[None]
        dest = _dest_call(route, rank_slab, pstarts_row)[:, :TOP_K].astype(jnp.int32)
        nb = -(-(n_rows * TOP_K) // MOE_BM) + N_EXPERTS
        n_used = (pends[-1] // MOE_BM).astype(jnp.int32)
        blk = jnp.minimum(jnp.arange(nb, dtype=jnp.int32), n_used - 1)
        block_e = jnp.clip(jnp.searchsorted(pends, blk * MOE_BM, side="right"), 0, N_EXPERTS - 1).astype(jnp.int32)
        order = jnp.argsort(eid.reshape(-1), stable=True).astype(jnp.int32)
        starts = jnp.cumsum(counts) - counts
        slot = jnp.arange(nb * MOE_BM, dtype=jnp.int32).reshape(nb, MOE_BM)
        blk_shift = (starts - pstarts)[block_e][:, None]
        blk_limit = (pstarts + counts)[block_e][:, None]
        pos = jnp.clip(slot + blk_shift, 0, n_rows * TOP_K - 1)
        slot_tok = jnp.where(slot < blk_limit, order.at[pos].get(mode="promise_in_bounds") // TOP_K,
                             slot % n_rows).reshape(-1)
        u_sorted = u_f.at[slot_tok].get(mode="promise_in_bounds")
        jj = jnp.arange(nb, dtype=jnp.int32)
        first = ((jj == 0) | (block_e != jnp.roll(block_e, 1))).astype(jnp.int32)
        parity = (jnp.cumsum(first) - 1) % 2
        live = jnp.where(counts > 0, jnp.arange(N_EXPERTS, dtype=jnp.int32), N_EXPERTS)
        next_live = jnp.concatenate([lax.cummin(live, reverse=True)[1:], jnp.full((1,), N_EXPERTS, jnp.int32)])
        nxt = jnp.where(next_live < N_EXPERTS, next_live, -1)[block_e]
        y_sorted = _moe_call(block_e, first, nxt.astype(jnp.int32), parity.astype(jnp.int32), n_used[None],
                             u_sorted, w_expert_gate, w_expert_up, w_expert_down, i)
        y_pair = y_sorted.at[dest.T.reshape(-1)].get(mode="promise_in_bounds")
        xs = _combine_call(xs, y_pair, route, mod, ln2_g[i][None], ln2_b[i][None], mod_row, alpha)
        stream = (xs, xs, nx)
    return xs.reshape(bsz, seq, d)
```

```python
import functools
import math

import jax
import jax.numpy as jnp
import numpy as np
from jax import lax
from jax.experimental import pallas as pl
from jax.experimental.pallas import tpu as pltpu

F32 = jnp.float32
BF16 = jnp.bfloat16

HEAD_DIM = 128
GRID_W = 64
ROPE_THETA = 10000.0
GQA_HEADS, GQA_KV_HEADS = 6, 2
GQA_GROUP = GQA_HEADS // GQA_KV_HEADS
MLA_HEADS, MLA_Q_RANK, MLA_KV_RANK = 5, 512, 256
MLA_NOPE, MLA_ROPE, MLA_V = 128, 64, 128
MLA_QK = MLA_NOPE + MLA_ROPE
MLA_PAD = 256
NA_HEADS, NA_KH, NA_KW = 5, 8, 16
N_GROUPS, EXPERTS_PER_GROUP, TOP_K, D_EXPERT = 8, 8, 2, 512
N_EXPERTS = N_GROUPS * EXPERTS_PER_GROUP
LN_EPS = 1e-6
RMS_EPS = 1e-6
NEG_BIG = -1e30
SCORE_LOG2E = math.log2(math.e)

LANES = 128
VMEM_LIMIT = 56 * 1024 * 1024
PROJ_TM, PROJ_TN = 1024, 1024
ROW_TM = 512
MM_TM = 512
ATTN_TQ = 512
ATTN_SUB = 8
ATTN_KC = 1024
NA_ROWS = 8
NA_SUB = 8
NA_BAND = 16
NA_WIN = 12
MOE_BM = 256
MOE_DMA_CHUNKS = 4
ADA_TN = 512

COL_AQ, COL_AK = 0, 768
COL_BQ, COL_BKV, COL_BKR = 1024, 1536, 1792
COL_AV, COL_CQ, COL_CK, COL_CV = 1920, 2176, 2816, 3456
COL_GATE = 4096
PROJ_W = 10240


def _cparams(sem):
    return pltpu.CompilerParams(dimension_semantics=sem, vmem_limit_bytes=VMEM_LIMIT)


def _resident(shape):
    return pl.BlockSpec(shape, lambda *_: (0,) * len(shape), pipeline_mode=pl.Buffered(1))


def _dot(a, b):
    return jnp.dot(a, b, preferred_element_type=F32)


def _dot_nt(a, b):
    return lax.dot_general(a, b, (((1,), (1,)), ((), ())), preferred_element_type=F32)


def _split_bf16(a):
    hi = a.astype(BF16)
    lo = (a - hi.astype(F32)).astype(BF16)
    return hi, lo


def _ada_kernel(c_ref, w_ref, b_ref, o_ref):
    a = c_ref[...]
    a = a * jax.nn.sigmoid(a)
    a_hi, a_lo = _split_bf16(a)
    w_hi, w_lo = _split_bf16(w_ref[0])
    rows = a.shape[0]
    both = _dot(jnp.concatenate([a_hi, a_lo], axis=0), w_hi)
    acc = both[:rows] + both[rows:] + _dot(a_hi, w_lo)
    o_ref[0] = acc + b_ref[0]


def _ada_call(cc, w_ada, b_ada):
    depth, d, n = w_ada.shape
    rows = cc.shape[0]
    return pl.pallas_call(
        _ada_kernel,
        grid=(depth, n // ADA_TN),
        in_specs=[pl.BlockSpec((rows, d), lambda l, j: (0, 0)),
                  pl.BlockSpec((1, d, ADA_TN), lambda l, j: (l, 0, j)),
                  pl.BlockSpec((1, 1, ADA_TN), lambda l, j: (l, 0, j))],
        out_specs=pl.BlockSpec((1, rows, ADA_TN), lambda l, j: (l, 0, j)),
        out_shape=jax.ShapeDtypeStruct((depth, rows, n), F32),
        compiler_params=_cparams(("parallel", "parallel")),
        name="ada_modulation",
    )(cc, w_ada, b_ada.reshape(depth, 1, n))


def _proj_kernel(xl_ref, xc_ref, mod_ref, w_ref, o_ref, u_scr, *, n_lat_tiles):
    @pl.when(pl.program_id(1) == 0)
    def _():
        def modulate(x):
            u_scr[...] = (x * (1.0 + mod_ref[0, 1:2, :]) + mod_ref[0, 0:1, :]).astype(BF16)

        @pl.when(pl.program_id(0) < n_lat_tiles)
        def _():
            modulate(xl_ref[...])

        @pl.when(pl.program_id(0) >= n_lat_tiles)
        def _():
            modulate(xc_ref[...])

    o_ref[...] = _dot_nt(u_scr[...], w_ref[0]).astype(BF16)


def _stream_specs(stream, tm, n_lat):
    lat, ctx, ctx_row0 = stream
    d = lat.shape[1]
    n_lat_tiles = n_lat // tm
    ctx_tile0 = ctx_row0 // tm

    def lat_map(i, *_):
        return (jnp.minimum(i, n_lat_tiles - 1), 0)

    def ctx_map(i, *_):
        return (ctx_tile0 + jnp.maximum(i - n_lat_tiles, 0), 0)

    return [pl.BlockSpec((tm, d), lat_map), pl.BlockSpec((tm, d), ctx_map, pipeline_mode=pl.Buffered(1))]


def _proj_call(stream, n_lat, n_tok, mod, w_in_p, layer, mod_row):
    d = stream[0].shape[1]
    return pl.pallas_call(
        functools.partial(_proj_kernel, n_lat_tiles=n_lat // PROJ_TM),
        grid=(n_tok // PROJ_TM, PROJ_W // PROJ_TN),
        in_specs=_stream_specs(stream, PROJ_TM, n_lat) + [
            pl.BlockSpec((1, 6, d), lambda i, j: (mod_row(i, PROJ_TM), 0, 0)),
            pl.BlockSpec((1, PROJ_TN, d), lambda i, j: (layer, j, 0))],
        out_specs=pl.BlockSpec((PROJ_TM, PROJ_TN), lambda i, j: (i, j)),
        out_shape=jax.ShapeDtypeStruct((n_tok, PROJ_W), BF16),
        scratch_shapes=[pltpu.VMEM((PROJ_TM, d), BF16)],
        compiler_params=_cparams(("parallel", "arbitrary")),
        name="input_projection",
    )(stream[0], stream[1], mod, w_in_p)


def _rope128(y, cos, sin_signed, first_quarter):
    rot = jnp.where(first_quarter, pltpu.roll(y, 96, 1), pltpu.roll(y, 32, 1))
    return y * cos + rot * sin_signed


def _prep_a_kernel(p_ref, cos_ref, sin_ref, gq_ref, gk_ref, q_ref, k_ref, *, scale):
    cos = cos_ref[...]
    sin_signed = sin_ref[...]
    lane = lax.broadcasted_iota(jnp.int32, cos.shape, 1)
    first_quarter = (lane & 32) == 0
    for h in range(GQA_HEADS + GQA_KV_HEADS):
        xh = p_ref[:, h * HEAD_DIM:(h + 1) * HEAD_DIM].astype(F32)
        r = lax.rsqrt(jnp.mean(xh * xh, axis=1, keepdims=True) + RMS_EPS)
        if h < GQA_HEADS:
            y = _rope128(xh * r * gq_ref[...], cos, sin_signed, first_quarter) * scale
            q_ref[:, h * HEAD_DIM:(h + 1) * HEAD_DIM] = y.astype(BF16)
        else:
            hk = h - GQA_HEADS
            y = _rope128(xh * r * gk_ref[...], cos, sin_signed, first_quarter)
            k_ref[:, hk * HEAD_DIM:(hk + 1) * HEAD_DIM] = y.astype(BF16)


def _prep_b_kernel(ql_ref, kvl_ref, kr_ref, cos_ref, sin_ref, gq_ref, gkv_ref, wq_ref, wkv_ref,
                   q_ref, k_ref, v_ref, *, scale):
    nh, hp = MLA_HEADS, MLA_PAD
    wn = nh * MLA_NOPE
    cos = cos_ref[...]
    sin = sin_ref[...]
    ql = ql_ref[...].astype(F32)
    ql = ql * lax.rsqrt(jnp.mean(ql * ql, axis=1, keepdims=True) + RMS_EPS) * gq_ref[...]
    qf = _dot(ql.astype(BF16), wq_ref[...])
    kvl = kvl_ref[...].astype(F32)
    kvl = kvl * lax.rsqrt(jnp.mean(kvl * kvl, axis=1, keepdims=True) + RMS_EPS) * gkv_ref[...]
    kvf = _dot(kvl.astype(BF16), wkv_ref[...])
    lane = lax.broadcasted_iota(jnp.int32, cos.shape, 1)
    low = lane < MLA_ROPE
    tk = kr_ref[...].astype(F32) * jnp.where(low, cos, sin)
    kr = jnp.where(low, tk + pltpu.roll(tk, MLA_ROPE, 1), 0.0).astype(BF16)
    for h in range(nh):
        q_ref[:, h * hp:h * hp + LANES] = (qf[:, h * LANES:(h + 1) * LANES] * scale).astype(BF16)
        qr = qf[:, wn + h * LANES:wn + (h + 1) * LANES] * cos + qf[:, 2 * wn + h * LANES:2 * wn + (h + 1) * LANES] * sin
        q_ref[:, h * hp + LANES:(h + 1) * hp] = (qr * scale).astype(BF16)
        k_ref[:, h * hp:h * hp + LANES] = kvf[:, h * LANES:(h + 1) * LANES].astype(BF16)
        k_ref[:, h * hp + LANES:(h + 1) * hp] = kr
    v_ref[...] = kvf[:, wn:].astype(BF16)


def _prep_ab_kernel(pa_ref, cos_a_ref, sin_a_ref, gqa_ref, gka_ref,
                    ql_ref, kvl_ref, kr_ref, cos_b_ref, sin_b_ref, gqb_ref, gkvb_ref, wq_ref, wkv_ref,
                    qa_ref, ka_ref, qb_ref, kb_ref, vb_ref, *, scale_a, scale_b):
    _prep_a_kernel(pa_ref, cos_a_ref, sin_a_ref, gqa_ref, gka_ref, qa_ref, ka_ref, scale=scale_a)
    _prep_b_kernel(ql_ref, kvl_ref, kr_ref, cos_b_ref, sin_b_ref, gqb_ref, gkvb_ref, wq_ref, wkv_ref,
                   qb_ref, kb_ref, vb_ref, scale=scale_b)


def _prep_ab_call(p, tabs, gqa, gka, gqb, gkvb, wq, wkv, rope_blk):
    t = p.shape[0]
    tm = ROW_TM
    nh = MLA_HEADS
    wqa, wka = GQA_HEADS * HEAD_DIM, GQA_KV_HEADS * HEAD_DIM
    cos_a, sin_a, cos_b, sin_b = tabs

    def rope_spec():
        return pl.BlockSpec((tm, LANES), lambda i: (rope_blk(i, tm), 0))

    return pl.pallas_call(
        functools.partial(_prep_ab_kernel, scale_a=SCORE_LOG2E / math.sqrt(HEAD_DIM),
                          scale_b=SCORE_LOG2E / math.sqrt(MLA_QK)),
        grid=(t // tm,),
        in_specs=[pl.BlockSpec((tm, wqa + wka), lambda i: (i, 0)), rope_spec(), rope_spec(),
                  _resident((1, HEAD_DIM)), _resident((1, HEAD_DIM)),
                  pl.BlockSpec((tm, MLA_Q_RANK), lambda i: (i, COL_BQ // MLA_Q_RANK)),
                  pl.BlockSpec((tm, MLA_KV_RANK), lambda i: (i, COL_BKV // MLA_KV_RANK)),
                  pl.BlockSpec((tm, LANES), lambda i: (i, COL_BKR // LANES)), rope_spec(), rope_spec(),
                  _resident((1, MLA_Q_RANK)), _resident((1, MLA_KV_RANK)),
                  _resident(wq.shape), _resident(wkv.shape)],
        out_specs=[pl.BlockSpec((tm, wqa), lambda i: (i, 0)),
                   pl.BlockSpec((tm, wka), lambda i: (i, 0)),
                   pl.BlockSpec((tm, nh * MLA_PAD), lambda i: (i, 0)),
                   pl.BlockSpec((tm, nh * MLA_PAD), lambda i: (i, 0)),
                   pl.BlockSpec((tm, nh * MLA_V), lambda i: (i, 0))],
        out_shape=[jax.ShapeDtypeStruct((t, wqa), BF16), jax.ShapeDtypeStruct((t, wka), BF16),
                   jax.ShapeDtypeStruct((t, nh * MLA_PAD), BF16),
                   jax.ShapeDtypeStruct((t, nh * MLA_PAD), BF16),
                   jax.ShapeDtypeStruct((t, nh * MLA_V), BF16)],
        compiler_params=_cparams(("parallel",)),
        name="qk_prep",
    )(p, cos_a, sin_a, gqa, gka, p, p, p, cos_b, sin_b, gqb, gkvb, wq, wkv)


def _online_attend(q, chunks):
    m = z = acc = None
    for k, v, bias in chunks:
        s = _dot_nt(q, k)
        if bias is not None:
            s = s + bias
        cm = jnp.max(s, axis=1, keepdims=True)
        if m is None:
            m = cm
            p = jnp.exp2(s - m)
            z = jnp.sum(p, axis=1, keepdims=True)
            acc = _dot(p.astype(BF16), v)
        else:
            m_new = jnp.maximum(m, cm)
            corr = jnp.exp2(m - m_new)
            p = jnp.exp2(s - m_new)
            z = z * corr + jnp.sum(p, axis=1, keepdims=True)
            acc = acc * corr + _dot(p.astype(BF16), v)
            m = m_new
    return acc / z


def _online_attend_ones(q, chunks, dv):
    m = acc = None
    for k, v_ones, bias in chunks:
        s = _dot_nt(q, k)
        if bias is not None:
            s = s + bias
        cm = jnp.max(s, axis=1, keepdims=True)
        if m is None:
            m = cm
            acc = _dot(jnp.exp2((s - m).astype(BF16)), v_ones)
        else:
            m_new = jnp.maximum(m, cm)
            corr = jnp.exp2(m - m_new)
            acc = acc * corr + _dot(jnp.exp2((s - m_new).astype(BF16)), v_ones)
            m = m_new
    return acc[:, :dv] / acc[:, dv:dv + 1]


def _dense_attn_kernel(q_ref, kl_ref, vl_ref, kc_ref, vc_ref, o_ref, v_ones):
    n_lat = kl_ref.shape[0]
    dv = vl_ref.shape[1]

    @pl.when(pl.program_id(2) == 0)
    def _():
        v_ones[0:n_lat, 0:dv] = vl_ref[...]
        v_ones[n_lat:, 0:dv] = vc_ref[...]
        v_ones[:, dv:] = jnp.ones((v_ones.shape[0], dv), BF16)

    chunks = [(kl_ref[c:c + ATTN_KC, :], v_ones[c:c + ATTN_KC, :], None) for c in range(0, n_lat, ATTN_KC)]
    chunks.append((kc_ref[...], v_ones[n_lat:, :], None))
    for r in range(0, q_ref.shape[0], ATTN_TQ):
        o_ref[r:r + ATTN_TQ, :] = _online_attend_ones(q_ref[r:r + ATTN_TQ, :], chunks, dv).astype(o_ref.dtype)


def _dense_attn_call(q, k, v, *, n_heads, dq, q_col, k_col, v_col, bsz, seq, n_ctx):
    tq = ATTN_TQ * ATTN_SUB
    tpb = seq // tq
    ctx0 = bsz * seq // n_ctx
    dv = HEAD_DIM
    return pl.pallas_call(
        _dense_attn_kernel,
        grid=(bsz, n_heads, tpb),
        in_specs=[pl.BlockSpec((tq, dq), lambda b, h, i: (b * tpb + i, q_col(h))),
                  pl.BlockSpec((seq, dq), lambda b, h, i: (b, k_col(h))),
                  pl.BlockSpec((seq, dv), lambda b, h, i: (b, v_col(h))),
                  pl.BlockSpec((n_ctx, dq), lambda b, h, i: (ctx0 + b, k_col(h))),
                  pl.BlockSpec((n_ctx, dv), lambda b, h, i: (ctx0 + b, v_col(h)))],
        out_specs=pl.BlockSpec((tq, dv), lambda b, h, i: (b * tpb + i, h)),
        out_shape=jax.ShapeDtypeStruct((bsz * seq, n_heads * dv), BF16),
        scratch_shapes=[pltpu.VMEM((seq + n_ctx, 2 * dv), BF16)],
        compiler_params=_cparams(("parallel", "parallel", "arbitrary")),
        name="dense_attention",
    )(q, k, v, k, v)


def _ctx_attn_kernel(q_ref, kc_ref, vc_ref, o_ref):
    o_ref[...] = _online_attend(q_ref[...], [(kc_ref[...], vc_ref[...], None)]).astype(o_ref.dtype)


def _ctx_attn_call(q, k, v, *, n_heads, dq, q_col, k_col, v_col, bsz, seq, n_ctx):
    ctx0 = bsz * seq // n_ctx
    dv = HEAD_DIM
    return pl.pallas_call(
        _ctx_attn_kernel,
        grid=(bsz, n_heads),
        in_specs=[pl.BlockSpec((n_ctx, dq), lambda b, h: (ctx0 + b, q_col(h))),
                  pl.BlockSpec((n_ctx, dq), lambda b, h: (ctx0 + b, k_col(h))),
                  pl.BlockSpec((n_ctx, dv), lambda b, h: (ctx0 + b, v_col(h)))],
        out_specs=pl.BlockSpec((n_ctx, dv), lambda b, h: (b, h)),
        out_shape=jax.ShapeDtypeStruct((bsz * n_ctx, n_heads * dv), BF16),
        compiler_params=_cparams(("parallel", "parallel")),
        name="context_attention",
    )(q, k, v)


def _na_window_offsets(variant, n_variants):
    shift = NA_BAND - NA_WIN
    return ((0, 0), (0, shift), (shift, shift))[0 if variant == 0 else (2 if variant == n_variants - 1 else 1)]


def _na_kernel(q_ref, k_ref, v_ref, kc_ref, vc_ref, *rest, rows_total, n_tiles):
    bias_refs, o_ref, v_ones = rest[:NA_SUB], rest[NA_SUB], rest[NA_SUB + 1]
    n_lat = k_ref.shape[0]
    dv = v_ref.shape[1]
    tile = NA_ROWS * GRID_W
    half = tile // 2
    win = NA_WIN * GRID_W
    shift = (NA_BAND - NA_WIN) * GRID_W

    @pl.when(pl.program_id(2) == 0)
    def _():
        v_ones[0:n_lat, 0:dv] = v_ref[...]
        v_ones[n_lat:, 0:dv] = vc_ref[...]
        v_ones[:, dv:] = jnp.ones((v_ones.shape[0], dv), BF16)

    ctx = (kc_ref[...], v_ones[n_lat:, :], None)
    for t in range(NA_SUB):
        i = pl.program_id(2) * NA_SUB + t
        start_blk = jnp.clip(2 * i - 1, 0, (rows_total - NA_BAND) // 4)
        band0 = start_blk * (4 * GRID_W)
        starts = (band0 + jnp.where(i == n_tiles - 1, shift, 0), band0 + jnp.where(i == 0, 0, shift))
        for hq, start in enumerate(starts):
            start = pl.multiple_of(start, 4 * GRID_W)
            rows = slice(t * tile + hq * half, t * tile + (hq + 1) * half)
            bias = bias_refs[t][0, 0, hq * half:(hq + 1) * half, :]
            chunks = [(k_ref[pl.ds(start, win), :], v_ones[pl.ds(start, win), :], bias), ctx]
            o_ref[rows, :] = _online_attend_ones(q_ref[rows, :], chunks, dv).astype(o_ref.dtype)


def _na_call(p, bias_tab, *, bsz, seq, n_ctx):
    tile = NA_ROWS * GRID_W
    n_tiles = seq // tile
    tq = tile * NA_SUB
    tpb = seq // tq
    rows_total = seq // GRID_W
    ctx0 = bsz * seq // n_ctx
    d = HEAD_DIM
    qc, kc, vc = COL_CQ // d, COL_CK // d, COL_CV // d

    def bias_spec(t):
        def index(b, h, i):
            tile_idx = i * NA_SUB + t
            return (h, jnp.where(tile_idx == 0, 0, jnp.where(tile_idx == n_tiles - 1, 2, 1)), 0, 0)
        return pl.BlockSpec((1, 1, tile, NA_WIN * GRID_W), index)

    return pl.pallas_call(
        functools.partial(_na_kernel, rows_total=rows_total, n_tiles=n_tiles),
        grid=(bsz, NA_HEADS, tpb),
        in_specs=[pl.BlockSpec((tq, d), lambda b, h, i: (b * tpb + i, qc + h)),
                  pl.BlockSpec((seq, d), lambda b, h, i: (b, kc + h)),
                  pl.BlockSpec((seq, d), lambda b, h, i: (b, vc + h)),
                  pl.BlockSpec((n_ctx, d), lambda b, h, i: (ctx0 + b, kc + h)),
                  pl.BlockSpec((n_ctx, d), lambda b, h, i: (ctx0 + b, vc + h))]
                 + [bias_spec(t) for t in range(NA_SUB)],
        out_specs=pl.BlockSpec((tq, d), lambda b, h, i: (b * tpb + i, h)),
        out_shape=jax.ShapeDtypeStruct((bsz * seq, NA_HEADS * d), BF16),
        scratch_shapes=[pltpu.VMEM((seq + n_ctx, 2 * d), BF16)],
        compiler_params=_cparams(("parallel", "parallel", "arbitrary")),
        name="neighbourhood_attention",
    )(p, p, p, p, p, *([bias_tab] * NA_SUB))


def _branch_kernel(oa_ref, ob_ref, oc_ref, oac_ref, obc_ref, occ_ref, ga_ref, gb_ref, gc_ref,
                   wa_ref, wb_ref, wc_ref, o_ref, *, n_lat_tiles):
    def gated(g_ref, w_ref, o_tile):
        return jax.nn.sigmoid(g_ref[...].astype(F32)) * _dot(o_tile, w_ref[...])

    @pl.when(pl.program_id(0) < n_lat_tiles)
    def _():
        acc = gated(ga_ref, wa_ref, oa_ref[...]) + gated(gb_ref, wb_ref, ob_ref[...]) + gated(gc_ref, wc_ref, oc_ref[...])
        o_ref[...] = acc.astype(BF16)

    @pl.when(pl.program_id(0) >= n_lat_tiles)
    def _():
        acc = gated(ga_ref, wa_ref, oac_ref[...]) + gated(gb_ref, wb_ref, obc_ref[...]) + gated(gc_ref, wc_ref, occ_ref[...])
        o_ref[...] = acc.astype(BF16)


def _branch_call(o_lat, o_ctx, p, wa, wb, wc, n_rows):
    d = wa.shape[1]
    tm = MM_TM
    g0 = COL_GATE // d
    n_lat_tiles = o_lat[0].shape[0] // tm
    lat_specs = [pl.BlockSpec((tm, o.shape[1]), lambda i: (jnp.minimum(i, n_lat_tiles - 1), 0)) for o in o_lat]
    ctx_specs = [pl.BlockSpec((tm, o.shape[1]), lambda i: (jnp.maximum(i - n_lat_tiles, 0), 0)) for o in o_ctx]
    return pl.pallas_call(
        functools.partial(_branch_kernel, n_lat_tiles=n_lat_tiles),
        grid=(n_rows // tm,),
        in_specs=lat_specs + ctx_specs + [
            pl.BlockSpec((tm, d), lambda i: (i, g0)),
            pl.BlockSpec((tm, d), lambda i: (i, g0 + 1)),
            pl.BlockSpec((tm, d), lambda i: (i, g0 + 2)),
            _resident(wa.shape), _resident(wb.shape), _resident(wc.shape)],
        out_specs=pl.BlockSpec((tm, d), lambda i: (i, 0)),
        out_shape=jax.ShapeDtypeStruct((n_rows, d), BF16),
        compiler_params=_cparams(("parallel",)),
        name="branch_merge",
    )(*o_lat, *o_ctx, p, p, p, wa, wb, wc)


def _layer_norm(z, g, b):
    mu = jnp.mean(z, axis=1, keepdims=True)
    zc = z - mu
    var = jnp.mean(zc * zc, axis=1, keepdims=True)
    return zc * lax.rsqrt(var + LN_EPS) * g + b


def _route(logits):
    lane = lax.broadcasted_iota(jnp.int32, logits.shape, 1)
    lane_f = lane.astype(F32)
    is_g = lane < N_GROUPS
    gl = jnp.where(is_g, logits, NEG_BIG)
    mg = jnp.max(gl, axis=1, keepdims=True)
    gsel = jnp.min(jnp.where(gl == mg, lane_f, float(LANES)), axis=1, keepdims=True)
    zg = jnp.sum(jnp.where(is_g, jnp.exp(gl - mg), 0.0), axis=1, keepdims=True)
    lo = N_GROUPS + gsel * EXPERTS_PER_GROUP
    is_e = (lane_f >= lo) & (lane_f < lo + EXPERTS_PER_GROUP)
    el = jnp.where(is_e, logits, NEG_BIG)
    t1 = jnp.max(el, axis=1, keepdims=True)
    i1 = jnp.min(jnp.where(el == t1, lane_f, float(LANES)), axis=1, keepdims=True)
    el2 = jnp.where(lane_f == i1, NEG_BIG, el)
    t2 = jnp.max(el2, axis=1, keepdims=True)
    i2 = jnp.min(jnp.where(el2 == t2, lane_f, float(LANES)), axis=1, keepdims=True)
    dd = jnp.exp(t2 - t1)
    g1 = 1.0 / (zg * (1.0 + dd))
    g2 = g1 * dd
    slab = jnp.where(lane == 0, i1 - N_GROUPS,
                     jnp.where(lane == 1, i2 - N_GROUPS,
                               jnp.where(lane == 2, g1, jnp.where(lane == 3, g2, 0.0))))
    return slab


def _out_kernel(m_ref, xl_ref, xc_ref, mod_ref, w_ref, lg_ref, lb_ref, wr_ref, br_ref,
                xo_ref, u_ref, r_ref, *, alpha, n_lat_tiles):
    @pl.when(pl.program_id(0) < n_lat_tiles)
    def _():
        xo_ref[...] = xl_ref[...]

    @pl.when(pl.program_id(0) >= n_lat_tiles)
    def _():
        xo_ref[...] = xc_ref[...]

    gate_m = mod_ref[0, 2:3, :]
    shift_f = mod_ref[0, 3:4, :]
    scale_f = mod_ref[0, 4:5, :]
    y = _dot(m_ref[...], w_ref[...])
    xn = _layer_norm(alpha * xo_ref[...] + gate_m * y, lg_ref[...], lb_ref[...])
    xo_ref[...] = xn
    u = xn * (1.0 + scale_f) + shift_f
    u_hi, u_lo = _split_bf16(u)
    u_ref[...] = u_hi
    both = _dot(u_hi, wr_ref[...])
    logits = both[:, :LANES] + both[:, LANES:] + _dot(u_lo, wr_ref[:, :LANES]) + br_ref[...]
    r_ref[...] = _route(logits)


def _out_call(mrg, stream, n_lat, mod, w_out, ln_g, ln_b, wr_cat, br, mod_row, n_rows, alpha):
    d = stream[0].shape[1]
    tm = MM_TM
    return pl.pallas_call(
        functools.partial(_out_kernel, alpha=alpha, n_lat_tiles=n_lat // tm),
        grid=(n_rows // tm,),
        in_specs=[pl.BlockSpec((tm, d), lambda i: (i, 0))] + _stream_specs(stream, tm, n_lat) + [
                  pl.BlockSpec((1, 6, d), lambda i: (mod_row(i, tm), 0, 0)),
                  _resident((d, d)), _resident((1, d)), _resident((1, d)),
                  _resident((d, 2 * LANES)), _resident((1, LANES))],
        out_specs=[pl.BlockSpec((tm, d), lambda i: (i, 0)),
                   pl.BlockSpec((tm, d), lambda i: (i, 0)),
                   pl.BlockSpec((tm, LANES), lambda i: (i, 0))],
        out_shape=[jax.ShapeDtypeStruct((n_rows, d), F32),
                   jax.ShapeDtypeStruct((n_rows, d), BF16),
                   jax.ShapeDtypeStruct((n_rows, LANES), F32)],
        compiler_params=_cparams(("parallel",)),
        name="out_proj_ln_route",
    )(mrg, stream[0], stream[1], mod, w_out, ln_g, ln_b, wr_cat, br)


def _rank_kernel(r_ref, rank_ref, cnt_ref, carry):
    @pl.when(pl.program_id(0) == 0)
    def _():
        carry[...] = jnp.zeros_like(carry)

    slab = r_ref[...]
    tm = slab.shape[0]
    lane = lax.broadcasted_iota(jnp.int32, slab.shape, 1)
    lane_f = lane.astype(F32)
    e1 = slab[:, 0:1]
    e2 = slab[:, 1:2]
    hit1 = lane_f == e1
    hit2 = lane_f == e2
    onehot = jnp.where(hit1 | hit2, 1.0, 0.0)
    row = lax.broadcasted_iota(jnp.int32, (tm, tm), 0)
    col = lax.broadcasted_iota(jnp.int32, (tm, tm), 1)
    lower = jnp.where(col < row, 1.0, 0.0).astype(BF16)
    before = _dot(lower, onehot.astype(BF16)) + carry[0:1, :]
    r1 = jnp.sum(jnp.where(hit1, before, 0.0), axis=1, keepdims=True)
    r2 = jnp.sum(jnp.where(hit2, before, 0.0), axis=1, keepdims=True)
    rank_ref[...] = jnp.where(lane == 0, r1, jnp.where(lane == 1, r2, 0.0))
    carry[...] = carry[...] + jnp.sum(onehot, axis=0, keepdims=True)
    cnt_ref[...] = carry[...]


def _rank_call(route):
    n = route.shape[0]
    tm = ROW_TM
    return pl.pallas_call(
        _rank_kernel,
        grid=(n // tm,),
        in_specs=[pl.BlockSpec((tm, LANES), lambda i: (i, 0))],
        out_specs=[pl.BlockSpec((tm, LANES), lambda i: (i, 0)),
                   pl.BlockSpec((8, LANES), lambda i: (0, 0))],
        out_shape=[jax.ShapeDtypeStruct((n, LANES), F32), jax.ShapeDtypeStruct((8, LANES), F32)],
        scratch_shapes=[pltpu.VMEM((8, LANES), F32)],
        compiler_params=_cparams(("arbitrary",)),
        name="expert_ranks",
    )(route)


def _dest_kernel(r_ref, rank_ref, ps_ref, o_ref):
    slab = r_ref[...]
    rank = rank_ref[...]
    lane = lax.broadcasted_iota(jnp.int32, slab.shape, 1)
    lane_f = lane.astype(F32)
    ps = ps_ref[...]
    s1 = jnp.sum(jnp.where(lane_f == slab[:, 0:1], ps, 0.0), axis=1, keepdims=True) + rank[:, 0:1]
    s2 = jnp.sum(jnp.where(lane_f == slab[:, 1:2], ps, 0.0), axis=1, keepdims=True) + rank[:, 1:2]
    o_ref[...] = jnp.where(lane == 0, s1, jnp.where(lane == 1, s2, 0.0))


def _dest_call(route, rank_slab, pstarts_row):
    n = route.shape[0]
    tm = n // 8
    return pl.pallas_call(
        _dest_kernel,
        grid=(n // tm,),
        in_specs=[pl.BlockSpec((tm, LANES), lambda i: (i, 0)),
                  pl.BlockSpec((tm, LANES), lambda i: (i, 0)),
                  _resident((1, LANES))],
        out_specs=pl.BlockSpec((tm, LANES), lambda i: (i, 0)),
        out_shape=jax.ShapeDtypeStruct((n, LANES), F32),
        compiler_params=_cparams(("parallel",)),
        name="expert_slots",
    )(route, rank_slab, pstarts_row)


def _moe_kernel(be_ref, first_ref, nxt_ref, par_ref, nu_ref, x_ref, wg_hbm, wu_hbm, wd_hbm, y_ref,
                gbuf, ubuf, dbuf, sems, *, layer):
    j = pl.program_id(0)

    def weight_copies(e, slot):
        copies = []
        for m, (hbm, buf) in enumerate(((wg_hbm, gbuf), (wu_hbm, ubuf), (wd_hbm, dbuf))):
            band = buf.shape[1] // MOE_DMA_CHUNKS
            for c in range(MOE_DMA_CHUNKS):
                rows = pl.ds(c * band, band)
                copies.append(pltpu.make_async_copy(hbm.at[layer, e, rows], buf.at[slot, rows], sems.at[slot, m]))
        return copies

    @pl.when(j < nu_ref[0])
    def _():
        slot = par_ref[j]

        @pl.when(j == 0)
        def _():
            for cp in weight_copies(be_ref[0], 0):
                cp.start()

        @pl.when(first_ref[j] == 1)
        def _():
            @pl.when(nxt_ref[j] >= 0)
            def _():
                for cp in weight_copies(nxt_ref[j], 1 - slot):
                    cp.start()

            for cp in weight_copies(be_ref[j], slot):
                cp.wait()

        xb = x_ref[...]
        hg = _dot(xb, gbuf[slot].astype(BF16))
        hu = _dot(xb, ubuf[slot].astype(BF16))
        hb = (hg * jax.nn.sigmoid(hg) * hu).astype(BF16)
        y_ref[...] = _dot(hb, dbuf[slot].astype(BF16)).astype(y_ref.dtype)

    @pl.when(j >= nu_ref[0])
    def _():
        y_ref[...] = jnp.zeros_like(y_ref)


def _moe_call(block_e, first, nxt, parity, n_used, u_sorted, w_gate, w_up, w_down, layer):
    n_slots, d = u_sorted.shape
    bm = MOE_BM
    nb = n_slots // bm
    de = w_gate.shape[-1]

    def xmap(j, be, fi, nx_, pa, nu):
        return (jnp.minimum(j, nu[0] - 1), 0)

    grid_spec = pltpu.PrefetchScalarGridSpec(
        num_scalar_prefetch=5,
        grid=(nb,),
        in_specs=[pl.BlockSpec((bm, d), xmap),
                  pl.BlockSpec(memory_space=pl.ANY),
                  pl.BlockSpec(memory_space=pl.ANY),
                  pl.BlockSpec(memory_space=pl.ANY)],
        out_specs=pl.BlockSpec((bm, d), lambda j, *_: (j, 0)),
        scratch_shapes=[pltpu.VMEM((2, d, de), F32), pltpu.VMEM((2, d, de), F32), pltpu.VMEM((2, de, d), F32),
                        pltpu.SemaphoreType.DMA((2, 3))],
    )
    return pl.pallas_call(
        functools.partial(_moe_kernel, layer=layer),
        grid_spec=grid_spec,
        out_shape=jax.ShapeDtypeStruct((n_slots, d), BF16),
        compiler_params=_cparams(("arbitrary",)),
        name="expert_mlp",
    )(block_e, first, nxt, parity, n_used, u_sorted, w_gate, w_up, w_down)


def _combine_kernel(x_ref, y0_ref, y1_ref, r_ref, mod_ref, lg_ref, lb_ref, o_ref, *, alpha):
    slab = r_ref[...]
    g1 = slab[:, 2:3]
    g2 = slab[:, 3:4]
    mx = g1 * y0_ref[...].astype(F32) + g2 * y1_ref[...].astype(F32)
    gate_f = mod_ref[0, 5:6, :]
    o_ref[...] = _layer_norm(alpha * x_ref[...] + gate_f * mx, lg_ref[...], lb_ref[...])


def _combine_call(xs, y_pair, route, mod, ln_g, ln_b, mod_row, alpha):
    n, d = xs.shape
    tm = ROW_TM
    return pl.pallas_call(
        functools.partial(_combine_kernel, alpha=alpha),
        grid=(n // tm,),
        in_specs=[pl.BlockSpec((tm, d), lambda i: (i, 0)),
                  pl.BlockSpec((tm, d), lambda i: (i, 0)),
                  pl.BlockSpec((tm, d), lambda i: (n // tm + i, 0)),
                  pl.BlockSpec((tm, LANES), lambda i: (i, 0)),
                  pl.BlockSpec((1, 6, d), lambda i: (mod_row(i, tm), 0, 0)),
                  pl.BlockSpec((1, d), lambda i: (0, 0)),
                  pl.BlockSpec((1, d), lambda i: (0, 0))],
        out_specs=pl.BlockSpec((tm, d), lambda i: (i, 0)),
        out_shape=jax.ShapeDtypeStruct((n, d), F32),
        compiler_params=_cparams(("parallel",)),
        name="moe_combine_ln",
    )(xs, y_pair, y_pair, route, mod, ln_g, ln_b)


def _rope_angles(seq, dim):
    tpos = np.arange(seq)
    row = (tpos // GRID_W).astype(np.float32)
    col = (tpos % GRID_W).astype(np.float32)
    quarter = dim // 4
    inv_freq = jnp.asarray(ROPE_THETA, F32) ** (-jnp.arange(quarter, dtype=F32) / quarter)
    ang_r = jnp.asarray(row)[:, None] * inv_freq
    ang_c = jnp.asarray(col)[:, None] * inv_freq
    return jnp.concatenate([ang_r, ang_r, ang_c, ang_c], axis=-1)


def _rot_sign(dim):
    l = np.arange(dim)
    return np.where((l & (dim // 4)) == 0, -1.0, 1.0).astype(np.float32), l ^ (dim // 4)


def _rope_tables(seq, pad_rows):
    ang_a = _rope_angles(seq, HEAD_DIM)
    sign_a, _ = _rot_sign(HEAD_DIM)
    cos_a = jnp.concatenate([jnp.cos(ang_a), jnp.ones((pad_rows, HEAD_DIM), F32)], 0)
    sin_a = jnp.concatenate([jnp.sin(ang_a) * sign_a, jnp.zeros((pad_rows, HEAD_DIM), F32)], 0)
    ang_b = _rope_angles(seq, MLA_ROPE)
    cos_b = jnp.concatenate([jnp.cos(ang_b), jnp.ones((pad_rows, MLA_ROPE), F32)], 0)
    sin_b = jnp.concatenate([jnp.sin(ang_b), jnp.zeros((pad_rows, MLA_ROPE), F32)], 0)
    return cos_a, sin_a, jnp.tile(cos_b, (1, 2)), jnp.tile(sin_b, (1, 2))


ORIG_AV, ORIG_BQ, ORIG_BKR, ORIG_CQ, ORIG_W = 1024, 1280, 2048, 2112, 10176
W_PREP_ROWS = MLA_ROPE
W_PREP_GROUP = 8
PREP_COPY, PREP_SCALE, PREP_ROT = 0, 1, 2


def _w_in_prep_kernel(src_ref, kind_ref, *refs, cq_scale):
    del src_ref
    w_refs, o_ref = refs[:W_PREP_GROUP], refs[W_PREP_GROUP]
    quarter = MLA_ROPE // 4
    rb = W_PREP_ROWS
    for k, w_ref in enumerate(w_refs):
        kind = kind_ref[pl.program_id(1) * W_PREP_GROUP + k]
        lo = k * rb

        @pl.when(kind == PREP_COPY)
        def _(w_ref=w_ref, lo=lo):
            o_ref[0, lo:lo + rb] = w_ref[0].astype(BF16)

        @pl.when(kind == PREP_SCALE)
        def _(w_ref=w_ref, lo=lo):
            o_ref[0, lo:lo + rb] = (w_ref[0] * cq_scale).astype(BF16)

        @pl.when(kind == PREP_ROT)
        def _(w_ref=w_ref, lo=lo):
            for r0 in range(0, rb, 2 * quarter):
                o_ref[0, lo + r0:lo + r0 + quarter] = (-w_ref[0, r0 + quarter:r0 + 2 * quarter]).astype(BF16)
                o_ref[0, lo + r0 + quarter:lo + r0 + 2 * quarter] = w_ref[0, r0:r0 + quarter].astype(BF16)


def _permute_w_in(w):
    depth, d, width = w.shape
    assert width == ORIG_W and ORIG_W - ORIG_CQ == PROJ_W - COL_CQ
    rb = W_PREP_ROWS
    src = np.zeros(PROJ_W // rb, np.int32)
    kind = np.full(PROJ_W // rb, PREP_COPY, np.int32)
    for dst0, src0, n in ((COL_AQ, 0, ORIG_AV), (COL_BQ, ORIG_BQ, ORIG_BKR - ORIG_BQ), (COL_BKR, ORIG_BKR, rb),
                          (COL_BKR + rb, ORIG_BKR, rb), (COL_AV, ORIG_AV, ORIG_BQ - ORIG_AV),
                          (COL_CQ, ORIG_CQ, ORIG_W - ORIG_CQ)):
        src[dst0 // rb:(dst0 + n) // rb] = np.arange(src0 // rb, (src0 + n) // rb)
    kind[(COL_BKR + rb) // rb] = PREP_ROT
    kind[COL_CQ // rb:COL_CK // rb] = PREP_SCALE
    wt = jnp.swapaxes(w, 1, 2)
    grp = W_PREP_GROUP

    def src_spec(k):
        return pl.BlockSpec((1, rb, d), lambda l, r, src_ref, kind_ref: (l, src_ref[r * grp + k], 0))

    grid_spec = pltpu.PrefetchScalarGridSpec(
        num_scalar_prefetch=2,
        grid=(depth, PROJ_W // (rb * grp)),
        in_specs=[src_spec(k) for k in range(grp)],
        out_specs=pl.BlockSpec((1, rb * grp, d), lambda l, r, src_ref, kind_ref: (l, r, 0)),
    )
    return pl.pallas_call(
        functools.partial(_w_in_prep_kernel, cq_scale=SCORE_LOG2E / math.sqrt(HEAD_DIM)),
        grid_spec=grid_spec,
        out_shape=jax.ShapeDtypeStruct((depth, PROJ_W, d), BF16),
        compiler_params=_cparams(("parallel", "parallel")),
        name="w_in_prep",
    )(jnp.asarray(src), jnp.asarray(kind), *([wt] * grp))


def _rot_cols(w):
    dim = w.shape[-1]
    wr = w.reshape(w.shape[:-1] + (2, 2, dim // 4))
    return jnp.stack([-wr[..., 1, :], wr[..., 0, :]], axis=-2).reshape(w.shape)


def _permute_w_uq(w):
    r = w.shape[0]
    w3 = w.reshape(r, MLA_HEADS, MLA_QK)
    nope = w3[:, :, :MLA_NOPE].reshape(r, MLA_HEADS * MLA_NOPE)
    rope = w3[:, :, MLA_NOPE:]
    rot = _rot_cols(rope)
    zpad = jnp.zeros((r, MLA_HEADS, LANES - MLA_ROPE), w.dtype)
    rope_p = jnp.concatenate([rope, zpad], -1).reshape(r, MLA_HEADS * LANES)
    rot_p = jnp.concatenate([rot, zpad], -1).reshape(r, MLA_HEADS * LANES)
    return jnp.concatenate([nope, rope_p, rot_p], axis=1).astype(BF16)


def _permute_w_ukv(w):
    r = w.shape[0]
    w3 = w.reshape(r, MLA_HEADS, MLA_NOPE + MLA_V)
    kn = w3[:, :, :MLA_NOPE].reshape(r, MLA_HEADS * MLA_NOPE)
    vv = w3[:, :, MLA_NOPE:].reshape(r, MLA_HEADS * MLA_V)
    return jnp.concatenate([kn, vv], axis=1).astype(BF16)


def _na_bias_tables(rpb, seq):
    w, kh, kw = GRID_W, NA_KH, NA_KW
    rows = seq // w
    nh, n_dr, n_dc = rpb.shape
    line = jnp.full((nh, n_dr, 2 * w), NEG_BIG, F32).at[:, :, w - kw:w - kw + n_dc].set(rpb.astype(F32) * SCORE_LOG2E)
    skew = jnp.broadcast_to(line[:, :, None, :], (nh, n_dr, w, 2 * w)).reshape(nh, n_dr, 2 * w * w)
    skew = skew[:, :, :w * (2 * w - 1)].reshape(nh, n_dr, w, 2 * w - 1)
    tc = skew[:, :, :, w - 1:2 * w - 1]
    cq = np.arange(w)[:, None]
    ck = np.arange(w)[None, :]
    cs = np.clip(cq - kw // 2, 0, w - kw)
    col_ok = (ck >= cs) & (ck < cs + kw)
    tc = jnp.where(col_ok[None, None], tc, NEG_BIG)
    tc = jnp.concatenate([tc, jnp.full((nh, 1, w, w), NEG_BIG, F32)], axis=1)
    tc2 = jnp.concatenate([tc, tc], axis=-1)
    blks = []
    tile_starts = (0, NA_ROWS, rows - NA_ROWS)
    for variant, r0 in enumerate(tile_starts):
        rb = int(np.clip(r0 - kh // 2, 0, rows - NA_BAND))
        rq = r0 + np.arange(NA_ROWS)[:, None]
        rk = rb + np.arange(NA_BAND)[None, :]
        rs = np.clip(rq - kh // 2, 0, rows - kh)
        row_ok = (rk >= rs) & (rk < rs + kh)
        band_blk = np.where(row_ok, rk - rq + kh - 1, n_dr)
        win_blk = []
        for hq, off in enumerate(_na_window_offsets(variant, len(tile_starts))):
            part = band_blk[hq * NA_ROWS // 2:(hq + 1) * NA_ROWS // 2]
            assert (np.delete(part, np.s_[off:off + NA_WIN], axis=1) == n_dr).all()
            win_blk.extend(part[:, off:off + NA_WIN].tolist())
        blks.append(win_blk)
    return pl.pallas_call(
        functools.partial(_na_bias_kernel, blks=blks),
        grid=(nh,),
        in_specs=[pl.BlockSpec((1, n_dr + 1, w, 2 * w), lambda h: (h, 0, 0, 0))],
        out_specs=pl.BlockSpec((1, len(blks), NA_ROWS * w, NA_WIN * w), lambda h: (h, 0, 0, 0)),
        out_shape=jax.ShapeDtypeStruct((nh, len(blks), NA_ROWS * w, NA_WIN * w), F32),
        compiler_params=_cparams(("parallel",)),
        name="na_bias_table",
    )(tc2)


def _na_bias_kernel(tc_ref, o_ref, *, blks):
    w = GRID_W
    low = lax.broadcasted_iota(jnp.int32, (w, 2 * w), 1) < w
    for v, blk in enumerate(blks):
        for rq, row in enumerate(blk):
            for pr in range(len(row) // 2):
                pair = jnp.where(low, tc_ref[0, row[2 * pr]], tc_ref[0, row[2 * pr + 1]])
                o_ref[0, v, rq * w:(rq + 1) * w, pr * 2 * w:(pr + 1) * 2 * w] = pair


def kernel(x, c, ctx, c_ctx, w_ada, b_ada, w_in, gqa_q_norm, gqa_k_norm, mla_q_norm, mla_kv_norm, mla_w_uq, mla_w_ukv, na_rpb, w_branch_a, w_branch_b, w_branch_c, w_out, ln1_g, ln1_b, w_router_group, b_router_group, w_router_expert, b_router_expert, w_expert_gate, w_expert_up, w_expert_down, ln2_g, ln2_b):
    bsz, seq, d = x.shape
    n_ctx = ctx.shape[1]
    depth = w_ada.shape[0]
    nx, nc = bsz * seq, bsz * n_ctx
    t = nx + nc
    assert seq % PROJ_TM == 0 and nc == PROJ_TM and seq % (NA_BAND * GRID_W) == 0 and nx % n_ctx == 0
    alpha = (2 * depth) ** 0.25

    def mod_row(i, tm):
        return jnp.minimum(i // (seq // tm), bsz)

    def rope_blk(i, tm):
        return jnp.where(i < nx // tm, i % (seq // tm), seq // tm)

    stream = (x.reshape(nx, d), ctx.reshape(nc, d), 0)
    cc = jnp.concatenate([c, c_ctx[None], jnp.zeros((8 - bsz - 1, d), F32)], axis=0)
    mod_all = _ada_call(cc, w_ada, b_ada).reshape(depth, 8, 6, d)
    cos_a, sin_a, cos_b, sin_b = _rope_tables(seq, ROW_TM)
    w_in_p = _permute_w_in(w_in)

    attn_kw = dict(bsz=bsz, seq=seq, n_ctx=n_ctx)
    for i in range(depth):
        last = i == depth - 1
        n_rows = nx if last else t
        mod = mod_all[i]
        p = _proj_call(stream, nx, t, mod, w_in_p, i, mod_row)

        qa, ka, qb, kb, vb = _prep_ab_call(
            p, (cos_a, sin_a, cos_b, sin_b), gqa_q_norm[i][None], gqa_k_norm[i][None],
            mla_q_norm[i][None], mla_kv_norm[i][None],
            _permute_w_uq(mla_w_uq[i]), _permute_w_ukv(mla_w_ukv[i]), rope_blk)
        a_kw = dict(n_heads=GQA_HEADS, dq=HEAD_DIM, q_col=lambda h: h, k_col=lambda h: h // GQA_GROUP,
                    v_col=lambda h: COL_AV // HEAD_DIM + h // GQA_GROUP, **attn_kw)
        o_a = _dense_attn_call(qa, ka, p, **a_kw)
        b_kw = dict(n_heads=MLA_HEADS, dq=MLA_PAD, q_col=lambda h: h, k_col=lambda h: h, v_col=lambda h: h, **attn_kw)
        o_b = _dense_attn_call(qb, kb, vb, **b_kw)
        o_c = _na_call(p, _na_bias_tables(na_rpb[i], seq), **attn_kw)
        o_lat = (o_a, o_b, o_c)
        if last:
            o_ctx = o_lat
        else:
            c_kw = dict(n_heads=NA_HEADS, dq=HEAD_DIM, q_col=lambda h: COL_CQ // HEAD_DIM + h,
                        k_col=lambda h: COL_CK // HEAD_DIM + h, v_col=lambda h: COL_CV // HEAD_DIM + h, **attn_kw)
            o_ctx = (_ctx_attn_call(qa, ka, p, **a_kw), _ctx_attn_call(qb, kb, vb, **b_kw),
                     _ctx_attn_call(p, p, p, **c_kw))

        mrg = _branch_call(o_lat, o_ctx, p, w_branch_a[i].astype(BF16), w_branch_b[i].astype(BF16),
                           w_branch_c[i].astype(BF16), n_rows)
        wr = jnp.concatenate([w_router_group[i], w_router_expert[i],
                              jnp.zeros((d, LANES - N_GROUPS - N_EXPERTS), F32)], axis=1)
        br = jnp.concatenate([b_router_group[i], b_router_expert[i],
                              jnp.zeros((LANES - N_GROUPS - N_EXPERTS,), F32)])[None]
        wr_cat = jnp.concatenate(_split_bf16(wr), axis=1)
        xs, u_f, route = _out_call(mrg, stream, nx, mod, w_out[i].astype(BF16), ln1_g[i][None], ln1_b[i][None],
                                   wr_cat, br, mod_row, n_rows, alpha)

        rank_slab, cnt = _rank_call(route)
        eid = route[:, :TOP_K].astype(jnp.int32)
        counts = cnt[0, :N_EXPERTS].astype(jnp.int32)
        padded = ((counts + MOE_BM - 1) // MOE_BM) * MOE_BM
        pends = jnp.cumsum(padded)
        pstarts = pends - padded
        pstarts_row = jnp.concatenate([pstarts.astype(F32), jnp.zeros((LANES - N_EXPERTS,), F32)])[None]
        dest = _dest_call(route, rank_slab, pstarts_row)[:, :TOP_K].astype(jnp.int32)
        nb = -(-(n_rows * TOP_K) // MOE_BM) + N_EXPERTS
        n_used = (pends[-1] // MOE_BM).astype(jnp.int32)
        blk = jnp.minimum(jnp.arange(nb, dtype=jnp.int32), n_used - 1)
        block_e = jnp.clip(jnp.searchsorted(pends, blk * MOE_BM, side="right"), 0, N_EXPERTS - 1).astype(jnp.int32)
        order = jnp.argsort(eid.reshape(-1), stable=True).astype(jnp.int32)
        starts = jnp.cumsum(counts) - counts
        slot = jnp.arange(nb * MOE_BM, dtype=jnp.int32).reshape(nb, MOE_BM)
        blk_shift = (starts - pstarts)[block_e][:, None]
        blk_limit = (pstarts + counts)[block_e][:, None]
        pos = jnp.clip(slot + blk_shift, 0, n_rows * TOP_K - 1)
        slot_tok = jnp.where(slot < blk_limit, order.at[pos].get(mode="promise_in_bounds") // TOP_K,
                             slot % n_rows).reshape(-1)
        u_sorted = u_f.at[slot_tok].get(mode="promise_in_bounds")
        jj = jnp.arange(nb, dtype=jnp.int32)
        first = ((jj == 0) | (block_e != jnp.roll(block_e, 1))).astype(jnp.int32)
        parity = (jnp.cumsum(first) - 1) % 2
        live = jnp.where(counts > 0, jnp.arange(N_EXPERTS, dtype=jnp.int32), N_EXPERTS)
        next_live = jnp.concatenate([lax.cummin(live, reverse=True)[1:], jnp.full((1,), N_EXPERTS, jnp.int32)])
        nxt = jnp.where(next_live < N_EXPERTS, next_live, -1)[block_e]
        y_sorted = _moe_call(block_e, first, nxt.astype(jnp.int32), parity.astype(jnp.int32), n_used[None],
                             u_sorted, w_expert_gate, w_expert_up, w_expert_down, i)
        y_pair = y_sorted.at[dest.T.reshape(-1)].get(mode="promise_in_bounds")
        xs = _combine_call(xs, y_pair, route, mod, ln2_g[i][None], ln2_b[i][None], mod_row, alpha)
        stream = (xs, xs, nx)
    return xs.reshape(bsz, seq, d)
```

```python
import functools
import math

import jax
import jax.numpy as jnp
import numpy as np
from jax import lax
from jax.experimental import pallas as pl
from jax.experimental.pallas import tpu as pltpu

F32 = jnp.float32
BF16 = jnp.bfloat16

HEAD_DIM = 128
GRID_W = 64
ROPE_THETA = 10000.0
GQA_HEADS, GQA_KV_HEADS = 6, 2
GQA_GROUP = GQA_HEADS // GQA_KV_HEADS
MLA_HEADS, MLA_Q_RANK, MLA_KV_RANK = 5, 512, 256
MLA_NOPE, MLA_ROPE, MLA_V = 128, 64, 128
MLA_QK = MLA_NOPE + MLA_ROPE
MLA_PAD = 256
NA_HEADS, NA_KH, NA_KW = 5, 8, 16
N_GROUPS, EXPERTS_PER_GROUP, TOP_K, D_EXPERT = 8, 8, 2, 512
N_EXPERTS = N_GROUPS * EXPERTS_PER_GROUP
LN_EPS = 1e-6
RMS_EPS = 1e-6
NEG_BIG = -1e30
SCORE_LOG2E = math.log2(math.e)

LANES = 128
VMEM_LIMIT = 56 * 1024 * 1024
PROJ_TM, PROJ_TN = 1024, 1024
ROW_TM = 512
MM_TM = 512
ATTN_TQ = 512
ATTN_SUB = 8
ATTN_KC = 1024
NA_ROWS = 8
NA_SUB = 8
NA_BAND = 16
NA_WIN = 12
MOE_BM = 256
MOE_DMA_CHUNKS = 4
ADA_TN = 512

COL_AQ, COL_AK = 0, 768
COL_BQ, COL_BKV, COL_BKR = 1024, 1536, 1792
COL_AV, COL_CQ, COL_CK, COL_CV = 1920, 2176, 2816, 3456
COL_GATE = 4096
PROJ_W = 10240


def _cparams(sem):
    return pltpu.CompilerParams(dimension_semantics=sem, vmem_limit_bytes=VMEM_LIMIT)


def _resident(shape):
    return pl.BlockSpec(shape, lambda *_: (0,) * len(shape), pipeline_mode=pl.Buffered(1))


def _dot(a, b):
    return jnp.dot(a, b, preferred_element_type=F32)


def _dot_nt(a, b):
    return lax.dot_general(a, b, (((1,), (1,)), ((), ())), preferred_element_type=F32)


def _split_bf16(a):
    hi = a.astype(BF16)
    lo = (a - hi.astype(F32)).astype(BF16)
    return hi, lo


def _ada_kernel(c_ref, w_ref, b_ref, o_ref):
    a = c_ref[...]
    a = a * jax.nn.sigmoid(a)
    a_hi, a_lo = _split_bf16(a)
    w_hi, w_lo = _split_bf16(w_ref[0])
    rows = a.shape[0]
    both = _dot(jnp.concatenate([a_hi, a_lo], axis=0), w_hi)
    acc = both[:rows] + both[rows:] + _dot(a_hi, w_lo)
    o_ref[0] = acc + b_ref[0]


def _ada_call(cc, w_ada, b_ada):
    depth, d, n = w_ada.shape
    rows = cc.shape[0]
    return pl.pallas_call(
        _ada_kernel,
        grid=(depth, n // ADA_TN),
        in_specs=[pl.BlockSpec((rows, d), lambda l, j: (0, 0)),
                  pl.BlockSpec((1, d, ADA_TN), lambda l, j: (l, 0, j)),
                  pl.BlockSpec((1, 1, ADA_TN), lambda l, j: (l, 0, j))],
        out_specs=pl.BlockSpec((1, rows, ADA_TN), lambda l, j: (l, 0, j)),
        out_shape=jax.ShapeDtypeStruct((depth, rows, n), F32),
        compiler_params=_cparams(("parallel", "parallel")),
        name="ada_modulation",
    )(cc, w_ada, b_ada.reshape(depth, 1, n))


def _proj_kernel(xl_ref, xc_ref, mod_ref, w_ref, o_ref, u_scr, *, n_lat_tiles):
    @pl.when(pl.program_id(1) == 0)
    def _():
        def modulate(x):
            u_scr[...] = (x * (1.0 + mod_ref[0, 1:2, :]) + mod_ref[0, 0:1, :]).astype(BF16)

        @pl.when(pl.program_id(0) < n_lat_tiles)
        def _():
            modulate(xl_ref[...])

        @pl.when(pl.program_id(0) >= n_lat_tiles)
        def _():
            modulate(xc_ref[...])

    o_ref[...] = _dot_nt(u_scr[...], w_ref[0]).astype(BF16)


def _stream_specs(stream, tm, n_lat):
    lat, ctx, ctx_row0 = stream
    d = lat.shape[1]
    n_lat_tiles = n_lat // tm
    ctx_tile0 = ctx_row0 // tm

    def lat_map(i, *_):
        return (jnp.minimum(i, n_lat_tiles - 1), 0)

    def ctx_map(i, *_):
        return (ctx_tile0 + jnp.maximum(i - n_lat_tiles, 0), 0)

    return [pl.BlockSpec((tm, d), lat_map), pl.BlockSpec((tm, d), ctx_map, pipeline_mode=pl.Buffered(1))]


def _proj_call(stream, n_lat, n_tok, mod, w_in_p, layer, mod_row):
    d = stream[0].shape[1]
    return pl.pallas_call(
        functools.partial(_proj_kernel, n_lat_tiles=n_lat // PROJ_TM),
        grid=(n_tok // PROJ_TM, PROJ_W // PROJ_TN),
        in_specs=_stream_specs(stream, PROJ_TM, n_lat) + [
            pl.BlockSpec((1, 6, d), lambda i, j: (mod_row(i, PROJ_TM), 0, 0)),
            pl.BlockSpec((1, PROJ_TN, d), lambda i, j: (layer, j, 0))],
        out_specs=pl.BlockSpec((PROJ_TM, PROJ_TN), lambda i, j: (i, j)),
        out_shape=jax.ShapeDtypeStruct((n_tok, PROJ_W), BF16),
        scratch_shapes=[pltpu.VMEM((PROJ_TM, d), BF16)],
        compiler_params=_cparams(("parallel", "arbitrary")),
        name="input_projection",
    )(stream[0], stream[1], mod, w_in_p)


def _rope128(y, cos, sin_signed, first_quarter):
    rot = jnp.where(first_quarter, pltpu.roll(y, 96, 1), pltpu.roll(y, 32, 1))
    return y * cos + rot * sin_signed


def _prep_a_kernel(p_ref, cos_ref, sin_ref, gq_ref, gk_ref, q_ref, k_ref, *, scale):
    cos = cos_ref[...]
    sin_signed = sin_ref[...]
    lane = lax.broadcasted_iota(jnp.int32, cos.shape, 1)
    first_quarter = (lane & 32) == 0
    for h in range(GQA_HEADS + GQA_KV_HEADS):
        xh = p_ref[:, h * HEAD_DIM:(h + 1) * HEAD_DIM].astype(F32)
        r = lax.rsqrt(jnp.mean(xh * xh, axis=1, keepdims=True) + RMS_EPS)
        if h < GQA_HEADS:
            y = _rope128(xh * r * gq_ref[...], cos, sin_signed, first_quarter) * scale
            q_ref[:, h * HEAD_DIM:(h + 1) * HEAD_DIM] = y.astype(BF16)
        else:
            hk = h - GQA_HEADS
            y = _rope128(xh * r * gk_ref[...], cos, sin_signed, first_quarter)
            k_ref[:, hk * HEAD_DIM:(hk + 1) * HEAD_DIM] = y.astype(BF16)


def _prep_b_kernel(ql_ref, kvl_ref, kr_ref, cos_ref, sin_ref, gq_ref, gkv_ref, wq_ref, wkv_ref,
                   q_ref, k_ref, v_ref, *, scale):
    nh, hp = MLA_HEADS, MLA_PAD
    wn = nh * MLA_NOPE
    cos = cos_ref[...]
    sin = sin_ref[...]
    ql = ql_ref[...].astype(F32)
    ql = ql * lax.rsqrt(jnp.mean(ql * ql, axis=1, keepdims=True) + RMS_EPS) * gq_ref[...]
    qf = _dot(ql.astype(BF16), wq_ref[...])
    kvl = kvl_ref[...].astype(F32)
    kvl = kvl * lax.rsqrt(jnp.mean(kvl * kvl, axis=1, keepdims=True) + RMS_EPS) * gkv_ref[...]
    kvf = _dot(kvl.astype(BF16), wkv_ref[...])
    lane = lax.broadcasted_iota(jnp.int32, cos.shape, 1)
    low = lane < MLA_ROPE
    tk = kr_ref[...].astype(F32) * jnp.where(low, cos, sin)
    kr = jnp.where(low, tk + pltpu.roll(tk, MLA_ROPE, 1), 0.0).astype(BF16)
    for h in range(nh):
        q_ref[:, h * hp:h * hp + LANES] = (qf[:, h * LANES:(h + 1) * LANES] * scale).astype(BF16)
        qr = qf[:, wn + h * LANES:wn + (h + 1) * LANES] * cos + qf[:, 2 * wn + h * LANES:2 * wn + (h + 1) * LANES] * sin
        q_ref[:, h * hp + LANES:(h + 1) * hp] = (qr * scale).astype(BF16)
        k_ref[:, h * hp:h * hp + LANES] = kvf[:, h * LANES:(h + 1) * LANES].astype(BF16)
        k_ref[:, h * hp + LANES:(h + 1) * hp] = kr
    v_ref[...] = kvf[:, wn:].astype(BF16)


def _prep_ab_kernel(pa_ref, cos_a_ref, sin_a_ref, gqa_ref, gka_ref,
                    ql_ref, kvl_ref, kr_ref, cos_b_ref, sin_b_ref, gqb_ref, gkvb_ref, wq_ref, wkv_ref,
                    qa_ref, ka_ref, qb_ref, kb_ref, vb_ref, *, scale_a, scale_b):
    _prep_a_kernel(pa_ref, cos_a_ref, sin_a_ref, gqa_ref, gka_ref, qa_ref, ka_ref, scale=scale_a)
    _prep_b_kernel(ql_ref, kvl_ref, kr_ref, cos_b_ref, sin_b_ref, gqb_ref, gkvb_ref, wq_ref, wkv_ref,
                   qb_ref, kb_ref, vb_ref, scale=scale_b)


def _prep_ab_call(p, tabs, gqa, gka, gqb, gkvb, wq, wkv, rope_blk):
    t = p.shape[0]
    tm = ROW_TM
    nh = MLA_HEADS
    wqa, wka = GQA_HEADS * HEAD_DIM, GQA_KV_HEADS * HEAD_DIM
    cos_a, sin_a, cos_b, sin_b = tabs

    def rope_spec():
        return pl.BlockSpec((tm, LANES), lambda i: (rope_blk(i, tm), 0))

    return pl.pallas_call(
        functools.partial(_prep_ab_kernel, scale_a=SCORE_LOG2E / math.sqrt(HEAD_DIM),
                          scale_b=SCORE_LOG2E / math.sqrt(MLA_QK)),
        grid=(t // tm,),
        in_specs=[pl.BlockSpec((tm, wqa + wka), lambda i: (i, 0)), rope_spec(), rope_spec(),
                  _resident((1, HEAD_DIM)), _resident((1, HEAD_DIM)),
                  pl.BlockSpec((tm, MLA_Q_RANK), lambda i: (i, COL_BQ // MLA_Q_RANK)),
                  pl.BlockSpec((tm, MLA_KV_RANK), lambda i: (i, COL_BKV // MLA_KV_RANK)),
                  pl.BlockSpec((tm, LANES), lambda i: (i, COL_BKR // LANES)), rope_spec(), rope_spec(),
                  _resident((1, MLA_Q_RANK)), _resident((1, MLA_KV_RANK)),
                  _resident(wq.shape), _resident(wkv.shape)],
        out_specs=[pl.BlockSpec((tm, wqa), lambda i: (i, 0)),
                   pl.BlockSpec((tm, wka), lambda i: (i, 0)),
                   pl.BlockSpec((tm, nh * MLA_PAD), lambda i: (i, 0)),
                   pl.BlockSpec((tm, nh * MLA_PAD), lambda i: (i, 0)),
                   pl.BlockSpec((tm, nh * MLA_V), lambda i: (i, 0))],
        out_shape=[jax.ShapeDtypeStruct((t, wqa), BF16), jax.ShapeDtypeStruct((t, wka), BF16),
                   jax.ShapeDtypeStruct((t, nh * MLA_PAD), BF16),
                   jax.ShapeDtypeStruct((t, nh * MLA_PAD), BF16),
                   jax.ShapeDtypeStruct((t, nh * MLA_V), BF16)],
        compiler_params=_cparams(("parallel",)),
        name="qk_prep",
    )(p, cos_a, sin_a, gqa, gka, p, p, p, cos_b, sin_b, gqb, gkvb, wq, wkv)


def _online_attend(q, chunks):
    m = z = acc = None
    for k, v, bias in chunks:
        s = _dot_nt(q, k)
        if bias is not None:
            s = s + bias
        cm = jnp.max(s, axis=1, keepdims=True)
        if m is None:
            m = cm
            p = jnp.exp2(s - m)
            z = jnp.sum(p, axis=1, keepdims=True)
            acc = _dot(p.astype(BF16), v)
        else:
            m_new = jnp.maximum(m, cm)
            corr = jnp.exp2(m - m_new)
            p = jnp.exp2(s - m_new)
            z = z * corr + jnp.sum(p, axis=1, keepdims=True)
            acc = acc * corr + _dot(p.astype(BF16), v)
            m = m_new
    return acc / z


def _online_attend_ones(q, chunks, dv):
    m = acc = None
    for k, v_ones, bias in chunks:
        s = _dot_nt(q, k)
        if bias is not None:
            s = s + bias
        cm = jnp.max(s, axis=1, keepdims=True)
        if m is None:
            m = cm
            acc = _dot(jnp.exp2((s - m).astype(BF16)), v_ones)
        else:
            m_new = jnp.maximum(m, cm)
            corr = jnp.exp2(m - m_new)
            acc = acc * corr + _dot(jnp.exp2((s - m_new).astype(BF16)), v_ones)
            m = m_new
    return acc[:, :dv] / acc[:, dv:dv + 1]


def _dense_attn_kernel(q_ref, kl_ref, vl_ref, kc_ref, vc_ref, o_ref, v_ones):
    n_lat = kl_ref.shape[0]
    dv = vl_ref.shape[1]

    @pl.when(pl.program_id(2) == 0)
    def _():
        v_ones[0:n_lat, 0:dv] = vl_ref[...]
        v_ones[n_lat:, 0:dv] = vc_ref[...]
        v_ones[:, dv:] = jnp.ones((v_ones.shape[0], dv), BF16)

    chunks = [(kl_ref[c:c + ATTN_KC, :], v_ones[c:c + ATTN_KC, :], None) for c in range(0, n_lat, ATTN_KC)]
    chunks.append((kc_ref[...], v_ones[n_lat:, :], None))
    for r in range(0, q_ref.shape[0], ATTN_TQ):
        o_ref[r:r + ATTN_TQ, :] = _online_attend_ones(q_ref[r:r + ATTN_TQ, :], chunks, dv).astype(o_ref.dtype)


def _dense_attn_call(q, k, v, *, n_heads, dq, q_col, k_col, v_col, bsz, seq, n_ctx):
    tq = ATTN_TQ * ATTN_SUB
    tpb = seq // tq
    ctx0 = bsz * seq // n_ctx
    dv = HEAD_DIM
    return pl.pallas_call(
        _dense_attn_kernel,
        grid=(bsz, n_heads, tpb),
        in_specs=[pl.BlockSpec((tq, dq), lambda b, h, i: (b * tpb + i, q_col(h))),
                  pl.BlockSpec((seq, dq), lambda b, h, i: (b, k_col(h))),
                  pl.BlockSpec((seq, dv), lambda b, h, i: (b, v_col(h))),
                  pl.BlockSpec((n_ctx, dq), lambda b, h, i: (ctx0 + b, k_col(h))),
                  pl.BlockSpec((n_ctx, dv), lambda b, h, i: (ctx0 + b, v_col(h)))],
        out_specs=pl.BlockSpec((tq, dv), lambda b, h, i: (b * tpb + i, h)),
        out_shape=jax.ShapeDtypeStruct((bsz * seq, n_heads * dv), BF16),
        scratch_shapes=[pltpu.VMEM((seq + n_ctx, 2 * dv), BF16)],
        compiler_params=_cparams(("parallel", "parallel", "arbitrary")),
        name="dense_attention",
    )(q, k, v, k, v)


def _ctx_attn_kernel(q_ref, kc_ref, vc_ref, o_ref):
    o_ref[...] = _online_attend(q_ref[...], [(kc_ref[...], vc_ref[...], None)]).astype(o_ref.dtype)


def _ctx_attn_call(q, k, v, *, n_heads, dq, q_col, k_col, v_col, bsz, seq, n_ctx):
    ctx0 = bsz * seq // n_ctx
    dv = HEAD_DIM
    return pl.pallas_call(
        _ctx_attn_kernel,
        grid=(bsz, n_heads),
        in_specs=[pl.BlockSpec((n_ctx, dq), lambda b, h: (ctx0 + b, q_col(h))),
                  pl.BlockSpec((n_ctx, dq), lambda b, h: (ctx0 + b, k_col(h))),
                  pl.BlockSpec((n_ctx, dv), lambda b, h: (ctx0 + b, v_col(h)))],
        out_specs=pl.BlockSpec((n_ctx, dv), lambda b, h: (b, h)),
        out_shape=jax.ShapeDtypeStruct((bsz * n_ctx, n_heads * dv), BF16),
        compiler_params=_cparams(("parallel", "parallel")),
        name="context_attention",
    )(q, k, v)


def _na_window_offsets(variant, n_variants):
    shift = NA_BAND - NA_WIN
    return ((0, 0), (0, shift), (shift, shift))[0 if variant == 0 else (2 if variant == n_variants - 1 else 1)]


def _na_kernel(q_ref, k_ref, v_ref, kc_ref, vc_ref, *rest, rows_total, n_tiles):
    bias_refs, o_ref, v_ones = rest[:NA_SUB], rest[NA_SUB], rest[NA_SUB + 1]
    n_lat = k_ref.shape[0]
    dv = v_ref.shape[1]
    tile = NA_ROWS * GRID_W
    half = tile // 2
    win = NA_WIN * GRID_W
    shift = (NA_BAND - NA_WIN) * GRID_W

    @pl.when(pl.program_id(2) == 0)
    def _():
        v_ones[0:n_lat, 0:dv] = v_ref[...]
        v_ones[n_lat:, 0:dv] = vc_ref[...]
        v_ones[:, dv:] = jnp.ones((v_ones.shape[0], dv), BF16)

    ctx = (kc_ref[...], v_ones[n_lat:, :], None)
    for t in range(NA_SUB):
        i = pl.program_id(2) * NA_SUB + t
        start_blk = jnp.clip(2 * i - 1, 0, (rows_total - NA_BAND) // 4)
        band0 = start_blk * (4 * GRID_W)
        starts = (band0 + jnp.where(i == n_tiles - 1, shift, 0), band0 + jnp.where(i == 0, 0, shift))
        for hq, start in enumerate(starts):
            start = pl.multiple_of(start, 4 * GRID_W)
            rows = slice(t * tile + hq * half, t * tile + (hq + 1) * half)
            bias = bias_refs[t][0, 0, hq * half:(hq + 1) * half, :]
            chunks = [(k_ref[pl.ds(start, win), :], v_ones[pl.ds(start, win), :], bias), ctx]
            o_ref[rows, :] = _online_attend_ones(q_ref[rows, :], chunks, dv).astype(o_ref.dtype)


def _na_call(p, bias_tab, *, bsz, seq, n_ctx):
    tile = NA_ROWS * GRID_W
    n_tiles = seq // tile
    tq = tile * NA_SUB
    tpb = seq // tq
    rows_total = seq // GRID_W
    ctx0 = bsz * seq // n_ctx
    d = HEAD_DIM
    qc, kc, vc = COL_CQ // d, COL_CK // d, COL_CV // d

    def bias_spec(t):
        def index(b, h, i):
            tile_idx = i * NA_SUB + t
            return (h, jnp.where(tile_idx == 0, 0, jnp.where(tile_idx == n_tiles - 1, 2, 1)), 0, 0)
        return pl.BlockSpec((1, 1, tile, NA_WIN * GRID_W), index)

    return pl.pallas_call(
        functools.partial(_na_kernel, rows_total=rows_total, n_tiles=n_tiles),
        grid=(bsz, NA_HEADS, tpb),
        in_specs=[pl.BlockSpec((tq, d), lambda b, h, i: (b * tpb + i, qc + h)),
                  pl.BlockSpec((seq, d), lambda b, h, i: (b, kc + h)),
                  pl.BlockSpec((seq, d), lambda b, h, i: (b, vc + h)),
                  pl.BlockSpec((n_ctx, d), lambda b, h, i: (ctx0 + b, kc + h)),
                  pl.BlockSpec((n_ctx, d), lambda b, h, i: (ctx0 + b, vc + h))]
                 + [bias_spec(t) for t in range(NA_SUB)],
        out_specs=pl.BlockSpec((tq, d), lambda b, h, i: (b * tpb + i, h)),
        out_shape=jax.ShapeDtypeStruct((bsz * seq, NA_HEADS * d), BF16),
        scratch_shapes=[pltpu.VMEM((seq + n_ctx, 2 * d), BF16)],
        compiler_params=_cparams(("parallel", "parallel", "arbitrary")),
        name="neighbourhood_attention",
    )(p, p, p, p, p, *([bias_tab] * NA_SUB))


def _branch_kernel(oa_ref, ob_ref, oc_ref, oac_ref, obc_ref, occ_ref, ga_ref, gb_ref, gc_ref,
                   wa_ref, wb_ref, wc_ref, o_ref, *, n_lat_tiles):
    def gated(g_ref, w_ref, o_tile):
        return jax.nn.sigmoid(g_ref[...].astype(F32)) * _dot(o_tile, w_ref[...])

    @pl.when(pl.program_id(0) < n_lat_tiles)
    def _():
        acc = gated(ga_ref, wa_ref, oa_ref[...]) + gated(gb_ref, wb_ref, ob_ref[...]) + gated(gc_ref, wc_ref, oc_ref[...])
        o_ref[...] = acc.astype(BF16)

    @pl.when(pl.program_id(0) >= n_lat_tiles)
    def _():
        acc = gated(ga_ref, wa_ref, oac_ref[...]) + gated(gb_ref, wb_ref, obc_ref[...]) + gated(gc_ref, wc_ref, occ_ref[...])
        o_ref[...] = acc.astype(BF16)


def _branch_call(o_lat, o_ctx, p, wa, wb, wc, n_rows):
    d = wa.shape[1]
    tm = MM_TM
    g0 = COL_GATE // d
    n_lat_tiles = o_lat[0].shape[0] // tm
    lat_specs = [pl.BlockSpec((tm, o.shape[1]), lambda i: (jnp.minimum(i, n_lat_tiles - 1), 0)) for o in o_lat]
    ctx_specs = [pl.BlockSpec((tm, o.shape[1]), lambda i: (jnp.maximum(i - n_lat_tiles, 0), 0)) for o in o_ctx]
    return pl.pallas_call(
        functools.partial(_branch_kernel, n_lat_tiles=n_lat_tiles),
        grid=(n_rows // tm,),
        in_specs=lat_specs + ctx_specs + [
            pl.BlockSpec((tm, d), lambda i: (i, g0)),
            pl.BlockSpec((tm, d), lambda i: (i, g0 + 1)),
            pl.BlockSpec((tm, d), lambda i: (i, g0 + 2)),
            _resident(wa.shape), _resident(wb.shape), _resident(wc.shape)],
        out_specs=pl.BlockSpec((tm, d), lambda i: (i, 0)),
        out_shape=jax.ShapeDtypeStruct((n_rows, d), BF16),
        compiler_params=_cparams(("parallel",)),
        name="branch_merge",
    )(*o_lat, *o_ctx, p, p, p, wa, wb, wc)


def _layer_norm(z, g, b):
    mu = jnp.mean(z, axis=1, keepdims=True)
    zc = z - mu
    var = jnp.mean(zc * zc, axis=1, keepdims=True)
    return zc * lax.rsqrt(var + LN_EPS) * g + b


def _route(logits):
    lane = lax.broadcasted_iota(jnp.int32, logits.shape, 1)
    lane_f = lane.astype(F32)
    is_g = lane < N_GROUPS
    gl = jnp.where(is_g, logits, NEG_BIG)
    mg = jnp.max(gl, axis=1, keepdims=True)
    gsel = jnp.min(jnp.where(gl == mg, lane_f, float(LANES)), axis=1, keepdims=True)
    zg = jnp.sum(jnp.where(is_g, jnp.exp(gl - mg), 0.0), axis=1, keepdims=True)
    lo = N_GROUPS + gsel * EXPERTS_PER_GROUP
    is_e = (lane_f >= lo) & (lane_f < lo + EXPERTS_PER_GROUP)
    el = jnp.where(is_e, logits, NEG_BIG)
    t1 = jnp.max(el, axis=1, keepdims=True)
    i1 = jnp.min(jnp.where(el == t1, lane_f, float(LANES)), axis=1, keepdims=True)
    el2 = jnp.where(lane_f == i1, NEG_BIG, el)
    t2 = jnp.max(el2, axis=1, keepdims=True)
    i2 = jnp.min(jnp.where(el2 == t2, lane_f, float(LANES)), axis=1, keepdims=True)
    dd = jnp.exp(t2 - t1)
    g1 = 1.0 / (zg * (1.0 + dd))
    g2 = g1 * dd
    slab = jnp.where(lane == 0, i1 - N_GROUPS,
                     jnp.where(lane == 1, i2 - N_GROUPS,
                               jnp.where(lane == 2, g1, jnp.where(lane == 3, g2, 0.0))))
    return slab


def _out_kernel(m_ref, xl_ref, xc_ref, mod_ref, w_ref, lg_ref, lb_ref, wr_ref, br_ref,
                xo_ref, u_ref, r_ref, *, alpha, n_lat_tiles):
    @pl.when(pl.program_id(0) < n_lat_tiles)
    def _():
        xo_ref[...] = xl_ref[...]

    @pl.when(pl.program_id(0) >= n_lat_tiles)
    def _():
        xo_ref[...] = xc_ref[...]

    gate_m = mod_ref[0, 2:3, :]
    shift_f = mod_ref[0, 3:4, :]
    scale_f = mod_ref[0, 4:5, :]
    y = _dot(m_ref[...], w_ref[...])
    xn = _layer_norm(alpha * xo_ref[...] + gate_m * y, lg_ref[...], lb_ref[...])
    xo_ref[...] = xn
    u = xn * (1.0 + scale_f) + shift_f
    u_hi, u_lo = _split_bf16(u)
    u_ref[...] = u_hi
    both = _dot(u_hi, wr_ref[...])
    logits = both[:, :LANES] + both[:, LANES:] + _dot(u_lo, wr_ref[:, :LANES]) + br_ref[...]
    r_ref[...] = _route(logits)


def _out_call(mrg, stream, n_lat, mod, w_out, ln_g, ln_b, wr_cat, br, mod_row, n_rows, alpha):
    d = stream[0].shape[1]
    tm = MM_TM
    return pl.pallas_call(
        functools.partial(_out_kernel, alpha=alpha, n_lat_tiles=n_lat // tm),
        grid=(n_rows // tm,),
        in_specs=[pl.BlockSpec((tm, d), lambda i: (i, 0))] + _stream_specs(stream, tm, n_lat) + [
                  pl.BlockSpec((1, 6, d), lambda i: (mod_row(i, tm), 0, 0)),
                  _resident((d, d)), _resident((1, d)), _resident((1, d)),
                  _resident((d, 2 * LANES)), _resident((1, LANES))],
        out_specs=[pl.BlockSpec((tm, d), lambda i: (i, 0)),
                   pl.BlockSpec((tm, d), lambda i: (i, 0)),
                   pl.BlockSpec((tm, LANES), lambda i: (i, 0))],
        out_shape=[jax.ShapeDtypeStruct((n_rows, d), F32),
                   jax.ShapeDtypeStruct((n_rows, d), BF16),
                   jax.ShapeDtypeStruct((n_rows, LANES), F32)],
        compiler_params=_cparams(("parallel",)),
        name="out_proj_ln_route",
    )(mrg, stream[0], stream[1], mod, w_out, ln_g, ln_b, wr_cat, br)


def _rank_kernel(r_ref, rank_ref, cnt_ref, carry):
    @pl.when(pl.program_id(0) == 0)
    def _():
        carry[...] = jnp.zeros_like(carry)

    slab = r_ref[...]
    tm = slab.shape[0]
    lane = lax.broadcasted_iota(jnp.int32, slab.shape, 1)
    lane_f = lane.astype(F32)
    e1 = slab[:, 0:1]
    e2 = slab[:, 1:2]
    hit1 = lane_f == e1
    hit2 = lane_f == e2
    onehot = jnp.where(hit1 | hit2, 1.0, 0.0)
    row = lax.broadcasted_iota(jnp.int32, (tm, tm), 0)
    col = lax.broadcasted_iota(jnp.int32, (tm, tm), 1)
    lower = jnp.where(col < row, 1.0, 0.0).astype(BF16)
    before = _dot(lower, onehot.astype(BF16)) + carry[0:1, :]
    r1 = jnp.sum(jnp.where(hit1, before, 0.0), axis=1, keepdims=True)
    r2 = jnp.sum(jnp.where(hit2, before, 0.0), axis=1, keepdims=True)
    rank_ref[...] = jnp.where(lane == 0, r1, jnp.where(lane == 1, r2, 0.0))
    carry[...] = carry[...] + jnp.sum(onehot, axis=0, keepdims=True)
    cnt_ref[...] = carry[...]


def _rank_call(route):
    n = route.shape[0]
    tm = ROW_TM
    return pl.pallas_call(
        _rank_kernel,
        grid=(n // tm,),
        in_specs=[pl.BlockSpec((tm, LANES), lambda i: (i, 0))],
        out_specs=[pl.BlockSpec((tm, LANES), lambda i: (i, 0)),
                   pl.BlockSpec((8, LANES), lambda i: (0, 0))],
        out_shape=[jax.ShapeDtypeStruct((n, LANES), F32), jax.ShapeDtypeStruct((8, LANES), F32)],
        scratch_shapes=[pltpu.VMEM((8, LANES), F32)],
        compiler_params=_cparams(("arbitrary",)),
        name="expert_ranks",
    )(route)


def _dest_kernel(r_ref, rank_ref, ps_ref, o_ref):
    slab = r_ref[...]
    rank = rank_ref[...]
    lane = lax.broadcasted_iota(jnp.int32, slab.shape, 1)
    lane_f = lane.astype(F32)
    ps = ps_ref[...]
    s1 = jnp.sum(jnp.where(lane_f == slab[:, 0:1], ps, 0.0), axis=1, keepdims=True) + rank[:, 0:1]
    s2 = jnp.sum(jnp.where(lane_f == slab[:, 1:2], ps, 0.0), axis=1, keepdims=True) + rank[:, 1:2]
    o_ref[...] = jnp.where(lane == 0, s1, jnp.where(lane == 1, s2, 0.0))


def _dest_call(route, rank_slab, pstarts_row):
    n = route.shape[0]
    tm = n // 8
    return pl.pallas_call(
        _dest_kernel,
        grid=(n // tm,),
        in_specs=[pl.BlockSpec((tm, LANES), lambda i: (i, 0)),
                  pl.BlockSpec((tm, LANES), lambda i: (i, 0)),
                  _resident((1, LANES))],
        out_specs=pl.BlockSpec((tm, LANES), lambda i: (i, 0)),
        out_shape=jax.ShapeDtypeStruct((n, LANES), F32),
        compiler_params=_cparams(("parallel",)),
        name="expert_slots",
    )(route, rank_slab, pstarts_row)


def _moe_kernel(be_ref, first_ref, nxt_ref, par_ref, nu_ref, x_ref, wg_hbm, wu_hbm, wd_hbm, y_ref,
                gbuf, ubuf, dbuf, sems, *, layer):
    j = pl.program_id(0)

    def weight_copies(e, slot):
        copies = []
        for m, (hbm, buf) in enumerate(((wg_hbm, gbuf), (wu_hbm, ubuf), (wd_hbm, dbuf))):
            band = buf.shape[1] // MOE_DMA_CHUNKS
            for c in range(MOE_DMA_CHUNKS):
                rows = pl.ds(c * band, band)
                copies.append(pltpu.make_async_copy(hbm.at[layer, e, rows], buf.at[slot, rows], sems.at[slot, m]))
        return copies

    @pl.when(j < nu_ref[0])
    def _():
        slot = par_ref[j]

        @pl.when(j == 0)
        def _():
            for n, cp in enumerate(weight_copies(be_ref[0], 0)):
                cp.start(priority=n % 2)

        @pl.when(first_ref[j] == 1)
        def _():
            @pl.when(nxt_ref[j] >= 0)
            def _():
                for n, cp in enumerate(weight_copies(nxt_ref[j], 1 - slot)):
                    cp.start(priority=n % 2)

            for cp in weight_copies(be_ref[j], slot):
                cp.wait()

        xb = x_ref[...]
        hg = _dot(xb, gbuf[slot].astype(BF16))
        hu = _dot(xb, ubuf[slot].astype(BF16))
        hb = (hg * jax.nn.sigmoid(hg) * hu).astype(BF16)
        y_ref[...] = _dot(hb, dbuf[slot].astype(BF16)).astype(y_ref.dtype)

    @pl.when(j >= nu_ref[0])
    def _():
        y_ref[...] = jnp.zeros_like(y_ref)


def _moe_call(block_e, first, nxt, parity, n_used, u_sorted, w_gate, w_up, w_down, layer):
    n_slots, d = u_sorted.shape
    bm = MOE_BM
    nb = n_slots // bm
    de = w_gate.shape[-1]

    def xmap(j, be, fi, nx_, pa, nu):
        return (jnp.minimum(j, nu[0] - 1), 0)

    grid_spec = pltpu.PrefetchScalarGridSpec(
        num_scalar_prefetch=5,
        grid=(nb,),
        in_specs=[pl.BlockSpec((bm, d), xmap),
                  pl.BlockSpec(memory_space=pl.ANY),
                  pl.BlockSpec(memory_space=pl.ANY),
                  pl.BlockSpec(memory_space=pl.ANY)],
        out_specs=pl.BlockSpec((bm, d), lambda j, *_: (j, 0)),
        scratch_shapes=[pltpu.VMEM((2, d, de), F32), pltpu.VMEM((2, d, de), F32), pltpu.VMEM((2, de, d), F32),
                        pltpu.SemaphoreType.DMA((2, 3))],
    )
    return pl.pallas_call(
        functools.partial(_moe_kernel, layer=layer),
        grid_spec=grid_spec,
        out_shape=jax.ShapeDtypeStruct((n_slots, d), BF16),
        compiler_params=_cparams(("arbitrary",)),
        name="expert_mlp",
    )(block_e, first, nxt, parity, n_used, u_sorted, w_gate, w_up, w_down)


def _combine_kernel(x_ref, y0_ref, y1_ref, r_ref, mod_ref, lg_ref, lb_ref, o_ref, *, alpha):
    slab = r_ref[...]
    g1 = slab[:, 2:3]
    g2 = slab[:, 3:4]
    mx = g1 * y0_ref[...].astype(F32) + g2 * y1_ref[...].astype(F32)
    gate_f = mod_ref[0, 5:6, :]
    o_ref[...] = _layer_norm(alpha * x_ref[...] + gate_f * mx, lg_ref[...], lb_ref[...])


def _combine_call(xs, y_pair, route, mod, ln_g, ln_b, mod_row, alpha):
    n, d = xs.shape
    tm = ROW_TM
    return pl.pallas_call(
        functools.partial(_combine_kernel, alpha=alpha),
        grid=(n // tm,),
        in_specs=[pl.BlockSpec((tm, d), lambda i: (i, 0)),
                  pl.BlockSpec((tm, d), lambda i: (i, 0)),
                  pl.BlockSpec((tm, d), lambda i: (n // tm + i, 0)),
                  pl.BlockSpec((tm, LANES), lambda i: (i, 0)),
                  pl.BlockSpec((1, 6, d), lambda i: (mod_row(i, tm), 0, 0)),
                  pl.BlockSpec((1, d), lambda i: (0, 0)),
                  pl.BlockSpec((1, d), lambda i: (0, 0))],
        out_specs=pl.BlockSpec((tm, d), lambda i: (i, 0)),
        out_shape=jax.ShapeDtypeStruct((n, d), F32),
        compiler_params=_cparams(("parallel",)),
        name="moe_combine_ln",
    )(xs, y_pair, y_pair, route, mod, ln_g, ln_b)


def _rope_angles(seq, dim):
    tpos = np.arange(seq)
    row = (tpos // GRID_W).astype(np.float32)
    col = (tpos % GRID_W).astype(np.float32)
    quarter = dim // 4
    inv_freq = jnp.asarray(ROPE_THETA, F32) ** (-jnp.arange(quarter, dtype=F32) / quarter)
    ang_r = jnp.asarray(row)[:, None] * inv_freq
    ang_c = jnp.asarray(col)[:, None] * inv_freq
    return jnp.concatenate([ang_r, ang_r, ang_c, ang_c], axis=-1)


def _rot_sign(dim):
    l = np.arange(dim)
    return np.where((l & (dim // 4)) == 0, -1.0, 1.0).astype(np.float32), l ^ (dim // 4)


def _rope_tables(seq, pad_rows):
    ang_a = _rope_angles(seq, HEAD_DIM)
    sign_a, _ = _rot_sign(HEAD_DIM)
    cos_a = jnp.concatenate([jnp.cos(ang_a), jnp.ones((pad_rows, HEAD_DIM), F32)], 0)
    sin_a = jnp.concatenate([jnp.sin(ang_a) * sign_a, jnp.zeros((pad_rows, HEAD_DIM), F32)], 0)
    ang_b = _rope_angles(seq, MLA_ROPE)
    cos_b = jnp.concatenate([jnp.cos(ang_b), jnp.ones((pad_rows, MLA_ROPE), F32)], 0)
    sin_b = jnp.concatenate([jnp.sin(ang_b), jnp.zeros((pad_rows, MLA_ROPE), F32)], 0)
    return cos_a, sin_a, jnp.tile(cos_b, (1, 2)), jnp.tile(sin_b, (1, 2))


ORIG_AV, ORIG_BQ, ORIG_BKR, ORIG_CQ, ORIG_W = 1024, 1280, 2048, 2112, 10176
W_PREP_ROWS = MLA_ROPE
W_PREP_GROUP = 8
PREP_COPY, PREP_SCALE, PREP_ROT = 0, 1, 2


def _w_in_prep_kernel(src_ref, kind_ref, *refs, cq_scale):
    del src_ref
    w_refs, o_ref = refs[:W_PREP_GROUP], refs[W_PREP_GROUP]
    quarter = MLA_ROPE // 4
    rb = W_PREP_ROWS
    for k, w_ref in enumerate(w_refs):
        kind = kind_ref[pl.program_id(1) * W_PREP_GROUP + k]
        lo = k * rb

        @pl.when(kind == PREP_COPY)
        def _(w_ref=w_ref, lo=lo):
            o_ref[0, lo:lo + rb] = w_ref[0].astype(BF16)

        @pl.when(kind == PREP_SCALE)
        def _(w_ref=w_ref, lo=lo):
            o_ref[0, lo:lo + rb] = (w_ref[0] * cq_scale).astype(BF16)

        @pl.when(kind == PREP_ROT)
        def _(w_ref=w_ref, lo=lo):
            for r0 in range(0, rb, 2 * quarter):
                o_ref[0, lo + r0:lo + r0 + quarter] = (-w_ref[0, r0 + quarter:r0 + 2 * quarter]).astype(BF16)
                o_ref[0, lo + r0 + quarter:lo + r0 + 2 * quarter] = w_ref[0, r0:r0 + quarter].astype(BF16)


def _permute_w_in(w):
    depth, d, width = w.shape
    assert width == ORIG_W and ORIG_W - ORIG_CQ == PROJ_W - COL_CQ
    rb = W_PREP_ROWS
    src = np.zeros(PROJ_W // rb, np.int32)
    kind = np.full(PROJ_W // rb, PREP_COPY, np.int32)
    for dst0, src0, n in ((COL_AQ, 0, ORIG_AV), (COL_BQ, ORIG_BQ, ORIG_BKR - ORIG_BQ), (COL_BKR, ORIG_BKR, rb),
                          (COL_BKR + rb, ORIG_BKR, rb), (COL_AV, ORIG_AV, ORIG_BQ - ORIG_AV),
                          (COL_CQ, ORIG_CQ, ORIG_W - ORIG_CQ)):
        src[dst0 // rb:(dst0 + n) // rb] = np.arange(src0 // rb, (src0 + n) // rb)
    kind[(COL_BKR + rb) // rb] = PREP_ROT
    kind[COL_CQ // rb:COL_CK // rb] = PREP_SCALE
    wt = jnp.swapaxes(w, 1, 2)
    grp = W_PREP_GROUP

    def src_spec(k):
        return pl.BlockSpec((1, rb, d), lambda l, r, src_ref, kind_ref: (l, src_ref[r * grp + k], 0))

    grid_spec = pltpu.PrefetchScalarGridSpec(
        num_scalar_prefetch=2,
        grid=(depth, PROJ_W // (rb * grp)),
        in_specs=[src_spec(k) for k in range(grp)],
        out_specs=pl.BlockSpec((1, rb * grp, d), lambda l, r, src_ref, kind_ref: (l, r, 0)),
    )
    return pl.pallas_call(
        functools.partial(_w_in_prep_kernel, cq_scale=SCORE_LOG2E / math.sqrt(HEAD_DIM)),
        grid_spec=grid_spec,
        out_shape=jax.ShapeDtypeStruct((depth, PROJ_W, d), BF16),
        compiler_params=_cparams(("parallel", "parallel")),
        name="w_in_prep",
    )(jnp.asarray(src), jnp.asarray(kind), *([wt] * grp))


def _rot_cols(w):
    dim = w.shape[-1]
    wr = w.reshape(w.shape[:-1] + (2, 2, dim // 4))
    return jnp.stack([-wr[..., 1, :], wr[..., 0, :]], axis=-2).reshape(w.shape)


def _permute_w_uq(w):
    r = w.shape[0]
    w3 = w.reshape(r, MLA_HEADS, MLA_QK)
    nope = w3[:, :, :MLA_NOPE].reshape(r, MLA_HEADS * MLA_NOPE)
    rope = w3[:, :, MLA_NOPE:]
    rot = _rot_cols(rope)
    zpad = jnp.zeros((r, MLA_HEADS, LANES - MLA_ROPE), w.dtype)
    rope_p = jnp.concatenate([rope, zpad], -1).reshape(r, MLA_HEADS * LANES)
    rot_p = jnp.concatenate([rot, zpad], -1).reshape(r, MLA_HEADS * LANES)
    return jnp.concatenate([nope, rope_p, rot_p], axis=1).astype(BF16)


def _permute_w_ukv(w):
    r = w.shape[0]
    w3 = w.reshape(r, MLA_HEADS, MLA_NOPE + MLA_V)
    kn = w3[:, :, :MLA_NOPE].reshape(r, MLA_HEADS * MLA_NOPE)
    vv = w3[:, :, MLA_NOPE:].reshape(r, MLA_HEADS * MLA_V)
    return jnp.concatenate([kn, vv], axis=1).astype(BF16)


def _na_bias_tables(rpb, seq):
    w, kh, kw = GRID_W, NA_KH, NA_KW
    rows = seq // w
    nh, n_dr, n_dc = rpb.shape
    line = jnp.full((nh, n_dr, 2 * w), NEG_BIG, F32).at[:, :, w - kw:w - kw + n_dc].set(rpb.astype(F32) * SCORE_LOG2E)
    skew = jnp.broadcast_to(line[:, :, None, :], (nh, n_dr, w, 2 * w)).reshape(nh, n_dr, 2 * w * w)
    skew = skew[:, :, :w * (2 * w - 1)].reshape(nh, n_dr, w, 2 * w - 1)
    tc = skew[:, :, :, w - 1:2 * w - 1]
    cq = np.arange(w)[:, None]
    ck = np.arange(w)[None, :]
    cs = np.clip(cq - kw // 2, 0, w - kw)
    col_ok = (ck >= cs) & (ck < cs + kw)
    tc = jnp.where(col_ok[None, None], tc, NEG_BIG)
    tc = jnp.concatenate([tc, jnp.full((nh, 1, w, w), NEG_BIG, F32)], axis=1)
    tc2 = jnp.concatenate([tc, tc], axis=-1)
    blks = []
    tile_starts = (0, NA_ROWS, rows - NA_ROWS)
    for variant, r0 in enumerate(tile_starts):
        rb = int(np.clip(r0 - kh // 2, 0, rows - NA_BAND))
        rq = r0 + np.arange(NA_ROWS)[:, None]
        rk = rb + np.arange(NA_BAND)[None, :]
        rs = np.clip(rq - kh // 2, 0, rows - kh)
        row_ok = (rk >= rs) & (rk < rs + kh)
        band_blk = np.where(row_ok, rk - rq + kh - 1, n_dr)
        win_blk = []
        for hq, off in enumerate(_na_window_offsets(variant, len(tile_starts))):
            part = band_blk[hq * NA_ROWS // 2:(hq + 1) * NA_ROWS // 2]
            assert (np.delete(part, np.s_[off:off + NA_WIN], axis=1) == n_dr).all()
            win_blk.extend(part[:, off:off + NA_WIN].tolist())
        blks.append(win_blk)
    return pl.pallas_call(
        functools.partial(_na_bias_kernel, blks=blks),
        grid=(nh,),
        in_specs=[pl.BlockSpec((1, n_dr + 1, w, 2 * w), lambda h: (h, 0, 0, 0))],
        out_specs=pl.BlockSpec((1, len(blks), NA_ROWS * w, NA_WIN * w), lambda h: (h, 0, 0, 0)),
        out_shape=jax.ShapeDtypeStruct((nh, len(blks), NA_ROWS * w, NA_WIN * w), F32),
        compiler_params=_cparams(("parallel",)),
        name="na_bias_table",
    )(tc2)


def _na_bias_kernel(tc_ref, o_ref, *, blks):
    w = GRID_W
    low = lax.broadcasted_iota(jnp.int32, (w, 2 * w), 1) < w
    for v, blk in enumerate(blks):
        for rq, row in enumerate(blk):
            for pr in range(len(row) // 2):
                pair = jnp.where(low, tc_ref[0, row[2 * pr]], tc_ref[0, row[2 * pr + 1]])
                o_ref[0, v, rq * w:(rq + 1) * w, pr * 2 * w:(pr + 1) * 2 * w] = pair


def kernel(x, c, ctx, c_ctx, w_ada, b_ada, w_in, gqa_q_norm, gqa_k_norm, mla_q_norm, mla_kv_norm, mla_w_uq, mla_w_ukv, na_rpb, w_branch_a, w_branch_b, w_branch_c, w_out, ln1_g, ln1_b, w_router_group, b_router_group, w_router_expert, b_router_expert, w_expert_gate, w_expert_up, w_expert_down, ln2_g, ln2_b):
    bsz, seq, d = x.shape
    n_ctx = ctx.shape[1]
    depth = w_ada.shape[0]
    nx, nc = bsz * seq, bsz * n_ctx
    t = nx + nc
    assert seq % PROJ_TM == 0 and nc == PROJ_TM and seq % (NA_BAND * GRID_W) == 0 and nx % n_ctx == 0
    alpha = (2 * depth) ** 0.25

    def mod_row(i, tm):
        return jnp.minimum(i // (seq // tm), bsz)

    def rope_blk(i, tm):
        return jnp.where(i < nx // tm, i % (seq // tm), seq // tm)

    stream = (x.reshape(nx, d), ctx.reshape(nc, d), 0)
    cc = jnp.concatenate([c, c_ctx[None], jnp.zeros((8 - bsz - 1, d), F32)], axis=0)
    mod_all = _ada_call(cc, w_ada, b_ada).reshape(depth, 8, 6, d)
    cos_a, sin_a, cos_b, sin_b = _rope_tables(seq, ROW_TM)
    w_in_p = _permute_w_in(w_in)

    attn_kw = dict(bsz=bsz, seq=seq, n_ctx=n_ctx)
    for i in range(depth):
        last = i == depth - 1
        n_rows = nx if last else t
        mod = mod_all[i]
        p = _proj_call(stream, nx, t, mod, w_in_p, i, mod_row)

        qa, ka, qb, kb, vb = _prep_ab_call(
            p, (cos_a, sin_a, cos_b, sin_b), gqa_q_norm[i][None], gqa_k_norm[i][None],
            mla_q_norm[i][None], mla_kv_norm[i][None],
            _permute_w_uq(mla_w_uq[i]), _permute_w_ukv(mla_w_ukv[i]), rope_blk)
        a_kw = dict(n_heads=GQA_HEADS, dq=HEAD_DIM, q_col=lambda h: h, k_col=lambda h: h // GQA_GROUP,
                    v_col=lambda h: COL_AV // HEAD_DIM + h // GQA_GROUP, **attn_kw)
        o_a = _dense_attn_call(qa, ka, p, **a_kw)
        b_kw = dict(n_heads=MLA_HEADS, dq=MLA_PAD, q_col=lambda h: h, k_col=lambda h: h, v_col=lambda h: h, **attn_kw)
        o_b = _dense_attn_call(qb, kb, vb, **b_kw)
        o_c = _na_call(p, _na_bias_tables(na_rpb[i], seq), **attn_kw)
        o_lat = (o_a, o_b, o_c)
        if last:
            o_ctx = o_lat
        else:
            c_kw = dict(n_heads=NA_HEADS, dq=HEAD_DIM, q_col=lambda h: COL_CQ // HEAD_DIM + h,
                        k_col=lambda h: COL_CK // HEAD_DIM + h, v_col=lambda h: COL_CV // HEAD_DIM + h, **attn_kw)
            o_ctx = (_ctx_attn_call(qa, ka, p, **a_kw), _ctx_attn_call(qb, kb, vb, **b_kw),
                     _ctx_attn_call(p, p, p, **c_kw))

        mrg = _branch_call(o_lat, o_ctx, p, w_branch_a[i].astype(BF16), w_branch_b[i].astype(BF16),
                           w_branch_c[i].astype(BF16), n_rows)
        wr = jnp.concatenate([w_router_group[i], w_router_expert[i],
                              jnp.zeros((d, LANES - N_GROUPS - N_EXPERTS), F32)], axis=1)
        br = jnp.concatenate([b_router_group[i], b_router_expert[i],
                              jnp.zeros((LANES - N_GROUPS - N_EXPERTS,), F32)])[None]
        wr_cat = jnp.concatenate(_split_bf16(wr), axis=1)
        xs, u_f, route = _out_call(mrg, stream, nx, mod, w_out[i].astype(BF16), ln1_g[i][None], ln1_b[i][None],
                                   wr_cat, br, mod_row, n_rows, alpha)

        rank_slab, cnt = _rank_call(route)
        eid = route[:, :TOP_K].astype(jnp.int32)
        counts = cnt[0, :N_EXPERTS].astype(jnp.int32)
        padded = ((counts + MOE_BM - 1) // MOE_BM) * MOE_BM
        pends = jnp.cumsum(padded)
        pstarts = pends - padded
        pstarts_row = jnp.concatenate([pstarts.astype(F32), jnp.zeros((LANES - N_EXPERTS,), F32)])[None]
        dest = _dest_call(route, rank_slab, pstarts_row)[:, :TOP_K].astype(jnp.int32)
        nb = -(-(n_rows * TOP_K) // MOE_BM) + N_EXPERTS
        n_used = (pends[-1] // MOE_BM).astype(jnp.int32)
        blk = jnp.minimum(jnp.arange(nb, dtype=jnp.int32), n_used - 1)
        block_e = jnp.clip(jnp.searchsorted(pends, blk * MOE_BM, side="right"), 0, N_EXPERTS - 1).astype(jnp.int32)
        order = jnp.argsort(eid.reshape(-1), stable=True).astype(jnp.int32)
        starts = jnp.cumsum(counts) - counts
        slot = jnp.arange(nb * MOE_BM, dtype=jnp.int32).reshape(nb, MOE_BM)
        blk_shift = (starts - pstarts)[block_e][:, None]
        blk_limit = (pstarts + counts)[block_e][:, None]
        pos = jnp.clip(slot + blk_shift, 0, n_rows * TOP_K - 1)
        slot_tok = jnp.where(slot < blk_limit, order.at[pos].get(mode="promise_in_bounds") // TOP_K,
                             slot % n_rows).reshape(-1)
        u_sorted = u_f.at[slot_tok].get(mode="promise_in_bounds")
        jj = jnp.arange(nb, dtype=jnp.int32)
        first = ((jj == 0) | (block_e != jnp.roll(block_e, 1))).astype(jnp.int32)
        parity = (jnp.cumsum(first) - 1) % 2
        live = jnp.where(counts > 0, jnp.arange(N_EXPERTS, dtype=jnp.int32), N_EXPERTS)
        next_live = jnp.concatenate([lax.cummin(live, reverse=True)[1:], jnp.full((1,), N_EXPERTS, jnp.int32)])
        nxt = jnp.where(next_live < N_EXPERTS, next_live, -1)[block_e]
        y_sorted = _moe_call(block_e, first, nxt.astype(jnp.int32), parity.astype(jnp.int32), n_used[None],
                             u_sorted, w_expert_gate, w_expert_up, w_expert_down, i)
        y_pair = y_sorted.at[dest.T.reshape(-1)].get(mode="promise_in_bounds")
        xs = _combine_call(xs, y_pair, route, mod, ln2_g[i][None], ln2_b[i][None], mod_row, alpha)
        stream = (xs, xs, nx)
    return xs.reshape(bsz, seq, d)
```
